```python
import jax, jax.numpy as jnp
from jax import lax
import numpy as np

D_MODEL = 1024
BATCH = 16
SEQ = 2048
DEPTH = 1
DEC_BATCH = 32
DEC_SEQ = 4
PAST_LEN = 16384
PAGE_SIZE = 128

D_CONV = D_MODEL // 2
D_ATTN = D_MODEL - D_CONV
HEAD_DIM = 64
N_HEADS = D_ATTN // HEAD_DIM
N_KV = 2
GQA_R = N_HEADS // N_KV
N_BR = 3
CONV_W = 3
BLK = 64
N_SEL = 16
WINDOW = 512
WIN_QB = 128
NW = WINDOW // WIN_QB
SEL_QC = 32
CMP_HID = HEAD_DIM
D_FF = ((8 * D_MODEL // 3) + 127) // 128 * 128
KV_W = 2 * N_KV * HEAD_DIM
D_IN = 3 * D_CONV + D_ATTN + 3 * KV_W + N_BR * N_HEADS
ALPHA = (2.0 * DEPTH) ** 0.25
BETA = (8.0 * DEPTH) ** -0.25
SCALE = HEAD_DIM ** -0.5
LN_EPS = 1e-5
NEG = -1e30
FORCE = 1e4

kernel_name = 'hybrid_shortconv_nsa_macaron_deepnorm_step'


def layer_norm(x, g, b):
    xf = x.astype(jnp.float32)
    mu = xf.mean(-1, keepdims=True)
    var = jnp.square(xf - mu).mean(-1, keepdims=True)
    return ((xf - mu) * lax.rsqrt(var + LN_EPS) * g + b).astype(x.dtype)


def post_ln(x, sub, g, b):
    return layer_norm(ALPHA * x + sub, g, b)


def swiglu(x, w_gate, w_up, w_down):
    return (jax.nn.silu(x @ w_gate) * (x @ w_up)) @ w_down


def masked_softmax(s, mask):
    s = jnp.where(mask, s.astype(jnp.float32), NEG)
    return jnp.where(mask, jax.nn.softmax(s, axis=-1), 0.0)


def project(h, w_in, b_in):
    n, t = h.shape[:2]
    z = h @ w_in + b_in
    sizes = [D_CONV, D_CONV, D_CONV, D_ATTN, KV_W, KV_W, KV_W]
    cuts = []
    acc = 0
    for s in sizes:
        acc += s
        cuts.append(acc)
    cb, cc, ch, q, kv_cmp, kv_sel, kv_win, g = jnp.split(z, cuts, axis=-1)
    kvshape = (n, t, 2, N_KV, HEAD_DIM)
    gates = jax.nn.sigmoid(g.astype(jnp.float32)).reshape(n, t, N_BR, N_HEADS).astype(h.dtype)
    return (cb, cc, ch, q.reshape(n, t, N_HEADS, HEAD_DIM), kv_cmp.reshape(kvshape),
            kv_sel.reshape(kvshape), kv_win.reshape(kvshape), gates)


def short_conv(u, prev, w_conv):
    t = u.shape[1]
    ext = jnp.concatenate([prev, u], axis=1)
    z = sum(ext[:, i:i + t] * w_conv[i] for i in range(CONV_W))
    return z, ext[:, -(CONV_W - 1):]


def compress(kv, w1, w2, pe):
    n, L = kv.shape[:2]
    nb = L // BLK
    blk = kv[:, :nb * BLK].reshape(n, nb, BLK, 2, N_KV, HEAD_DIM) + pe[:, :, None, :]
    hid = jax.nn.silu(jnp.einsum('nbpcgd,cpdh->nbcgh', blk, w1))
    return jnp.einsum('nbcgh,chd->nbcgd', hid, w2)


def cmp_attn(q, q_pos, kc):
    n, tq = q.shape[:2]
    nb = kc.shape[1]
    qg = q.reshape(n, tq, N_KV, GQA_R, HEAD_DIM)
    s = jnp.einsum('nqgrd,nbgd->ngrqb', qg, kc[:, :, 0]) * SCALE
    blk_end = (jnp.arange(nb) + 1) * BLK - 1
    p = masked_softmax(s, blk_end[None, :] <= q_pos[:, None])
    o = jnp.einsum('ngrqb,nbgd->nqgrd', p.astype(kc.dtype), kc[:, :, 1])
    return o.reshape(n, tq, N_HEADS, HEAD_DIM), p.sum(axis=2)


def select_blocks(imp, q_pos, n_blocks):
    n_cand = max(n_blocks, N_SEL)
    imp = jnp.pad(imp, ((0, 0), (0, 0), (0, 0), (0, n_cand - imp.shape[-1])))
    j = jnp.arange(n_cand)[None, :]
    cur = (q_pos // BLK)[:, None]
    score = jnp.where(j > cur, NEG, imp)
    score = jnp.where((j == 0) | (j == cur) | (j == cur - 1), FORCE, score)
    _, idx = lax.top_k(score, N_SEL)
    return idx


def to_blocks(kv):
    n, t = kv.shape[:2]
    nb = -(-t // BLK)
    kv = jnp.pad(kv, ((0, 0), (0, nb * BLK - t), (0, 0), (0, 0), (0, 0)))
    return kv.reshape(n, nb, BLK, 2, N_KV, HEAD_DIM)


def gather_blocks(new_blocks, idx, n_past_blk, pool_blocks=None, page_table=None):
    n = new_blocks.shape[0]
    nidx = jnp.arange(n)[:, None, None, None]
    gidx = jnp.arange(N_KV)[None, :, None, None]
    j_new = jnp.clip(idx - n_past_blk, 0, new_blocks.shape[1] - 1)
    out = new_blocks[nidx, j_new, :, :, gidx]
    if pool_blocks is not None:
        bpp = PAGE_SIZE // BLK
        jp = jnp.clip(idx, 0, n_past_blk - 1)
        page = jnp.take_along_axis(page_table, (jp // bpp).reshape(n, -1), axis=1).reshape(jp.shape)
        past = pool_blocks[page * bpp + jp % bpp, :, :, gidx]
        out = jnp.where((idx < n_past_blk)[..., None, None, None], past, out)
    return out


def sel_attn(q, q_pos, g, idx):
    n, tq = q.shape[:2]
    qg = q.reshape(n, tq, N_KV, GQA_R, HEAD_DIM)
    k = g[..., 0, :]
    v = g[..., 1, :]
    s = jnp.einsum('nqgrd,ngqspd->ngrqsp', qg, k) * SCALE
    kpos = idx[..., None] * BLK + jnp.arange(BLK)
    mask = (kpos <= q_pos[:, None, None])[:, :, None]
    p = masked_softmax(s.reshape(n, N_KV, GQA_R, tq, -1), mask.reshape(n, N_KV, 1, tq, -1))
    o = jnp.einsum('ngrqk,ngqkd->nqgrd', p.astype(v.dtype), v.reshape(n, N_KV, tq, -1, HEAD_DIM))
    return o.reshape(n, tq, N_HEADS, HEAD_DIM)


def win_attn(q, k, v, q_pos, k_pos):
    n, tq = q.shape[:2]
    qg = q.reshape(n, tq, N_KV, GQA_R, HEAD_DIM)
    s = jnp.einsum('nqgrd,nkgd->ngrqk', qg, k) * SCALE
    d = q_pos[:, None] - k_pos[None, :]
    p = masked_softmax(s, (d >= 0) & (d < WINDOW) & (k_pos[None, :] >= 0))
    o = jnp.einsum('ngrqk,nkgd->nqgrd', p.astype(v.dtype), v)
    return o.reshape(n, tq, N_HEADS, HEAD_DIM)


def win_prompt(q, kv):
    n, t = q.shape[:2]
    nqb = t // WIN_QB
    kvp = jnp.pad(kv, ((0, 0), (WINDOW, 0), (0, 0), (0, 0), (0, 0)))
    kvb = kvp.reshape(n, nqb + NW, WIN_QB, 2, N_KV, HEAD_DIM)
    band = jnp.concatenate([kvb[:, o:o + nqb] for o in range(NW + 1)], axis=2)
    kp = jnp.arange(-WINDOW, t).reshape(nqb + NW, WIN_QB)
    kp = jnp.concatenate([kp[o:o + nqb] for o in range(NW + 1)], axis=1)
    qp = jnp.arange(t).reshape(nqb, WIN_QB)

    def one(args):
        qb, kb, qpb, kpb = args
        return win_attn(qb, kb[:, :, 0], kb[:, :, 1], qpb, kpb)

    o = lax.map(one, (q.reshape(n, nqb, WIN_QB, N_HEADS, HEAD_DIM).swapaxes(0, 1),
                      band.swapaxes(0, 1), qp, kp))
    return o.swapaxes(0, 1).reshape(n, t, N_HEADS, HEAD_DIM)


def merge(gates, o_cmp, o_sel, o_win, y_conv, w_out):
    n, t = y_conv.shape[:2]
    o = (gates[:, :, 0, :, None] * o_cmp + gates[:, :, 1, :, None] * o_sel
         + gates[:, :, 2, :, None] * o_win)
    return jnp.concatenate([o.reshape(n, t, D_ATTN), y_conv], axis=-1) @ w_out


def mix_prompt(h, w_in, b_in, w_conv, cmp_w1, cmp_w2, cmp_pe, w_out):
    n, t = h.shape[:2]
    cb, cc, ch, q, kv_cmp, kv_sel, kv_win, gates = project(h, w_in, b_in)
    z, conv_state = short_conv(cc * ch, jnp.zeros((n, CONV_W - 1, D_CONV), h.dtype), w_conv)
    y_conv = cb * z
    q_pos = jnp.arange(t)
    o_cmp, imp = cmp_attn(q, q_pos, compress(kv_cmp, cmp_w1, cmp_w2, cmp_pe))
    idx = select_blocks(imp, q_pos, -(-t // BLK))
    blocks = to_blocks(kv_sel)
    nc = t // SEL_QC

    def sel_chunk(args):
        qc, pc, ic = args
        return sel_attn(qc, pc, gather_blocks(blocks, ic, 0), ic)

    o_sel = lax.map(sel_chunk, (q.reshape(n, nc, SEL_QC, N_HEADS, HEAD_DIM).swapaxes(0, 1),
                                q_pos.reshape(nc, SEL_QC),
                                idx.reshape(n, N_KV, nc, SEL_QC, N_SEL).transpose(2, 0, 1, 3, 4)))
    o_sel = o_sel.swapaxes(0, 1).reshape(n, t, N_HEADS, HEAD_DIM)
    o_win = win_prompt(q, kv_win)
    out = merge(gates, o_cmp, o_sel, o_win, y_conv, w_out)
    return out, kv_cmp, kv_sel, kv_win[:, -min(WINDOW, t):], conv_state


def mix_sample(h, cmp_pool, sel_pool, win_buf, conv_buf, page_table,
               w_in, b_in, w_conv, cmp_w1, cmp_w2, cmp_pe, w_out):
    n, s = h.shape[:2]
    past = page_table.shape[1] * PAGE_SIZE
    cb, cc, ch, q, kv_cmp, kv_sel, kv_win, gates = project(h, w_in, b_in)
    z, conv_state = short_conv(cc * ch, conv_buf, w_conv)
    y_conv = cb * z
    q_pos = past + jnp.arange(s)
    past_cmp = cmp_pool[page_table].reshape(n, past, 2, N_KV, HEAD_DIM)
    ctx_cmp = jnp.concatenate([past_cmp, kv_cmp], axis=1)
    o_cmp, imp = cmp_attn(q, q_pos, compress(ctx_cmp, cmp_w1, cmp_w2, cmp_pe))
    idx = select_blocks(imp, q_pos, -(-(past + s) // BLK))
    pool_blocks = sel_pool.reshape(-1, BLK, 2, N_KV, HEAD_DIM)
    g = gather_blocks(to_blocks(kv_sel), idx, past // BLK, pool_blocks, page_table)
    o_sel = sel_attn(q, q_pos, g, idx)
    ctx_win = jnp.concatenate([win_buf, kv_win], axis=1)
    k_pos = jnp.arange(past - win_buf.shape[1], past + s)
    o_win = win_attn(q, ctx_win[:, :, 0], ctx_win[:, :, 1], q_pos, k_pos)
    out = merge(gates, o_cmp, o_sel, o_win, y_conv, w_out)
    return out, kv_cmp, kv_sel, ctx_win[:, -min(WINDOW, past + s):], conv_state


def setup_inputs(seed: int = 0) -> dict:
    key = jax.random.key(seed)
    ks = iter(jax.random.split(key, 40))

    def nrm(shape, scale):
        return jax.random.normal(next(ks), shape, jnp.float32) * scale

    n_pages = PAST_LEN // PAGE_SIZE
    n_phys = (5 * DEC_BATCH * n_pages + 3) // 4
    win_rows = min(WINDOW, PAST_LEN)
    d = D_MODEL
    inp = {}
    inp['x_prompt'] = nrm((BATCH, SEQ, d), 1.0)
    inp['x_sample'] = nrm((DEC_BATCH, DEC_SEQ, d), 1.0)
    inp['cache_cmp_kv'] = nrm((DEPTH, n_phys, PAGE_SIZE, 2, N_KV, HEAD_DIM), 1.0)
    inp['cache_sel_kv'] = nrm((DEPTH, n_phys, PAGE_SIZE, 2, N_KV, HEAD_DIM), 1.0)
    inp['state_win_kv'] = nrm((DEPTH, DEC_BATCH, win_rows, 2, N_KV, HEAD_DIM), 1.0)
    inp['state_conv'] = nrm((DEPTH, DEC_BATCH, CONV_W - 1, D_CONV), 1.0)
    perm = jax.random.permutation(next(ks), n_phys)
    inp['page_table'] = perm[:DEC_BATCH * n_pages].reshape(DEC_BATCH, n_pages).astype(jnp.int32)
    inp['ffa_gate'] = nrm((DEPTH, d, D_FF), d ** -0.5)
    inp['ffa_up'] = nrm((DEPTH, d, D_FF), d ** -0.5)
    inp['ffa_down'] = nrm((DEPTH, D_FF, d), BETA * D_FF ** -0.5)
    inp['ln1_g'] = 1.0 + nrm((DEPTH, d), 0.02)
    inp['ln1_b'] = nrm((DEPTH, d), 0.02)
    inp['w_in'] = nrm((DEPTH, d, D_IN), d ** -0.5)
    inp['b_in'] = nrm((DEPTH, D_IN), 0.02)
    inp['w_conv'] = nrm((DEPTH, CONV_W, D_CONV), CONV_W ** -0.5)
    inp['cmp_w1'] = nrm((DEPTH, 2, BLK, HEAD_DIM, CMP_HID), (BLK * HEAD_DIM) ** -0.5)
    inp['cmp_w2'] = nrm((DEPTH, 2, CMP_HID, HEAD_DIM), CMP_HID ** -0.5)
    inp['cmp_pe'] = nrm((DEPTH, BLK, 2, HEAD_DIM), 0.1)
    inp['w_out'] = nrm((DEPTH, d, d), BETA * d ** -0.5)
    inp['ln2_g'] = 1.0 + nrm((DEPTH, d), 0.02)
    inp['ln2_b'] = nrm((DEPTH, d), 0.02)
    inp['ffb_gate'] = nrm((DEPTH, d, D_FF), d ** -0.5)
    inp['ffb_up'] = nrm((DEPTH, d, D_FF), d ** -0.5)
    inp['ffb_down'] = nrm((DEPTH, D_FF, d), BETA * D_FF ** -0.5)
    inp['ln3_g'] = 1.0 + nrm((DEPTH, d), 0.02)
    inp['ln3_b'] = nrm((DEPTH, d), 0.02)
    return inp


def reference(x_prompt, x_sample, cache_cmp_kv, cache_sel_kv, state_win_kv, state_conv, page_table,
              ffa_gate, ffa_up, ffa_down, ln1_g, ln1_b, w_in, b_in, w_conv, cmp_w1, cmp_w2, cmp_pe,
              w_out, ln2_g, ln2_b, ffb_gate, ffb_up, ffb_down, ln3_g, ln3_b):
    yp, ys = x_prompt, x_sample
    pc, psl, pw, pcv, sc, ssl, sw, scv = [], [], [], [], [], [], [], []
    for l in range(DEPTH):
        yp = post_ln(yp, 0.5 * swiglu(yp, ffa_gate[l], ffa_up[l], ffa_down[l]), ln1_g[l], ln1_b[l])
        ys = post_ln(ys, 0.5 * swiglu(ys, ffa_gate[l], ffa_up[l], ffa_down[l]), ln1_g[l], ln1_b[l])
        mp, a, b, c, e = mix_prompt(yp, w_in[l], b_in[l], w_conv[l], cmp_w1[l], cmp_w2[l], cmp_pe[l], w_out[l])
        pc.append(a); psl.append(b); pw.append(c); pcv.append(e)
        ms, a, b, c, e = mix_sample(ys, cache_cmp_kv[l], cache_sel_kv[l], state_win_kv[l], state_conv[l],
                                    page_table, w_in[l], b_in[l], w_conv[l], cmp_w1[l], cmp_w2[l],
                                    cmp_pe[l], w_out[l])
        sc.append(a); ssl.append(b); sw.append(c); scv.append(e)
        yp = post_ln(yp, mp, ln2_g[l], ln2_b[l])
        ys = post_ln(ys, ms, ln2_g[l], ln2_b[l])
        yp = post_ln(yp, 0.5 * swiglu(yp, ffb_gate[l], ffb_up[l], ffb_down[l]), ln3_g[l], ln3_b[l])
        ys = post_ln(ys, 0.5 * swiglu(ys, ffb_gate[l], ffb_up[l], ffb_down[l]), ln3_g[l], ln3_b[l])
    return (yp, ys, jnp.stack(pc), jnp.stack(psl), jnp.stack(pw), jnp.stack(pcv),
            jnp.stack(sc), jnp.stack(ssl), jnp.stack(sw), jnp.stack(scv))
```

```python
import functools

import jax
import jax.numpy as jnp
from jax import lax
from jax.experimental import pallas as pl
from jax.experimental.pallas import tpu as pltpu

F32 = jnp.float32
BF16 = jnp.bfloat16

D_MODEL = 1024
D_CONV = 512
D_ATTN = 512
HEAD_DIM = 64
N_HEADS = 8
N_KV = 2
GQA_R = N_HEADS // N_KV
N_BR = 3
CONV_W = 3
BLK = 64
BLK_SHIFT = 6
N_SEL = 16
WINDOW = 512
PAGE_SIZE = 128
D_FF = 2816
KV_W = 2 * N_KV * HEAD_DIM
D_MAIN = 3 * D_CONV + D_ATTN + 3 * KV_W
N_GATE = N_BR * N_HEADS
ALPHA = 2.0 ** 0.25
SCALE = HEAD_DIM ** -0.5
LN_EPS = 1e-5
NEG = -1e30
FORCE = 1e4

LANES = 128
VMEM_LIMIT_BYTES = 56 * 1024 * 1024


def _cparams(sem):
    return pltpu.CompilerParams(dimension_semantics=sem, vmem_limit_bytes=VMEM_LIMIT_BYTES)


def _layer_norm(y, g, b):
    mu = jnp.mean(y, axis=-1, keepdims=True)
    yc = y - mu
    var = jnp.mean(yc * yc, axis=-1, keepdims=True)
    return yc * lax.rsqrt(var + LN_EPS) * g + b


def _dot(a, b):
    return jnp.dot(a, b, preferred_element_type=F32)


def _dot_nt(a, b):
    return lax.dot_general(a, b, (((1,), (1,)), ((), ())), preferred_element_type=F32)


FFN_CHUNK = 256


def _ffn_ln_body(x_ref, wg_ref, wu_ref, wd_ref, g_ref, b_ref, o_ref):
    x = x_ref[...]
    xb = x.astype(BF16)
    acc = jnp.zeros(x.shape, F32)
    for c in range(D_FF // FFN_CHUNK):
        sl = slice(c * FFN_CHUNK, (c + 1) * FFN_CHUNK)
        gg = _dot(xb, wg_ref[:, sl])
        uu = _dot(xb, wu_ref[:, sl])
        hh = (gg * jax.nn.sigmoid(gg) * uu).astype(BF16)
        acc = acc + _dot(hh, wd_ref[sl, :])
    o_ref[...] = _layer_norm(ALPHA * x + 0.5 * acc, g_ref[...], b_ref[...])


def _ffn_ln(x2d, wg, wu, wd, g, b, tm):
    rows = x2d.shape[0]
    const = lambda i: (0, 0)
    return pl.pallas_call(
        _ffn_ln_body,
        grid=(rows // tm,),
        in_specs=[
            pl.BlockSpec((tm, D_MODEL), lambda i: (i, 0)),
            pl.BlockSpec((D_MODEL, D_FF), const),
            pl.BlockSpec((D_MODEL, D_FF), const),
            pl.BlockSpec((D_FF, D_MODEL), const),
            pl.BlockSpec((1, D_MODEL), const),
            pl.BlockSpec((1, D_MODEL), const),
        ],
        out_specs=pl.BlockSpec((tm, D_MODEL), lambda i: (i, 0)),
        out_shape=jax.ShapeDtypeStruct((rows, D_MODEL), F32),
        compiler_params=_cparams(("arbitrary",)),
        name="ffn_ln",
    )(x2d, wg, wu, wd, g, b)


_C_CB, _C_CC, _C_CH, _C_Q = 0, D_CONV, 2 * D_CONV, 3 * D_CONV
_C_KVC = 3 * D_CONV + D_ATTN
_C_KVS = _C_KVC + KV_W
_C_KVW = _C_KVS + KV_W


def _proj_piece(hb, w_ref, b_ref, lo, width):
    return _dot(hb, w_ref[:, lo:lo + width]) + b_ref[:, lo:lo + width]


def _proj_prompt_body(h_ref, w_ref, b_ref, wgt_ref, bgt_ref, wc_ref,
                      q_ref, kvc_ref, kvs_ref, kvw_ref, gates_ref, yconv_ref, cstate_ref, ubuf):
    i = pl.program_id(1)
    tm = h_ref.shape[0]
    hb = h_ref[...].astype(BF16)
    q_ref[...] = (_proj_piece(hb, w_ref, b_ref, _C_Q, D_ATTN) * SCALE).astype(BF16)
    kvc_ref[...] = _proj_piece(hb, w_ref, b_ref, _C_KVC, KV_W)
    kvs_ref[...] = _proj_piece(hb, w_ref, b_ref, _C_KVS, KV_W)
    kvw_ref[...] = _proj_piece(hb, w_ref, b_ref, _C_KVW, KV_W)
    gates_ref[...] = jax.nn.sigmoid(_dot(hb, wgt_ref[...]) + bgt_ref[...])
    cb = _proj_piece(hb, w_ref, b_ref, _C_CB, D_CONV)
    u = _proj_piece(hb, w_ref, b_ref, _C_CC, D_CONV) * _proj_piece(hb, w_ref, b_ref, _C_CH, D_CONV)

    @pl.when(i == 0)
    def _():
        ubuf[0:8, :] = jnp.zeros((8, D_CONV), F32)

    ubuf[8:8 + tm, :] = u
    u1 = ubuf[7:7 + tm, :]
    u2 = ubuf[6:6 + tm, :]
    z = u2 * wc_ref[0:1, :] + u1 * wc_ref[1:2, :] + u * wc_ref[2:3, :]
    yconv_ref[...] = cb * z
    tail = u[tm - 8:tm, :]
    ubuf[0:8, :] = tail
    cstate_ref[...] = tail


def _proj_prompt(h, w_main, b_main, w_gate, b_gate, w_conv, tm):
    n, t, _ = h.shape
    const = lambda a, i: (0, 0)
    row = lambda a, i: (a, i, 0)

    def out(width, dtype=F32):
        return jax.ShapeDtypeStruct((n, t, width), dtype)

    return pl.pallas_call(
        _proj_prompt_body,
        grid=(n, t // tm),
        in_specs=[
            pl.BlockSpec((None, tm, D_MODEL), row),
            pl.BlockSpec((D_MODEL, D_MAIN), const),
            pl.BlockSpec((1, D_MAIN), const),
            pl.BlockSpec((D_MODEL, LANES), const),
            pl.BlockSpec((1, LANES), const),
            pl.BlockSpec((CONV_W, D_CONV), const),
        ],
        out_specs=[
            pl.BlockSpec((None, tm, D_ATTN), row),
            pl.BlockSpec((None, tm, KV_W), row),
            pl.BlockSpec((None, tm, KV_W), row),
            pl.BlockSpec((None, tm, KV_W), row),
            pl.BlockSpec((None, tm, LANES), row),
            pl.BlockSpec((None, tm, D_CONV), row),
            pl.BlockSpec((None, 8, D_CONV), lambda a, i: (a, 0, 0)),
        ],
        out_shape=[out(D_ATTN, BF16), out(KV_W), out(KV_W), out(KV_W), out(LANES), out(D_CONV),
                   jax.ShapeDtypeStruct((n, 8, D_CONV), F32)],
        scratch_shapes=[pltpu.VMEM((tm + 8, D_CONV), F32)],
        compiler_params=_cparams(("arbitrary", "arbitrary")),
        name="proj_prompt",
    )(h, w_main, b_main, w_gate, b_gate, w_conv)


def _proj_sample_body(h_ref, w_ref, b_ref, wgt_ref, bgt_ref, q_ref, kv_ref, gates_ref, cb_ref, u_ref):
    hb = h_ref[...].astype(BF16)
    q_ref[...] = (_proj_piece(hb, w_ref, b_ref, _C_Q, D_ATTN) * SCALE).astype(BF16)
    kv_ref[...] = _proj_piece(hb, w_ref, b_ref, _C_KVC, 3 * KV_W)
    gates_ref[...] = jax.nn.sigmoid(_dot(hb, wgt_ref[...]) + bgt_ref[...])
    cb_ref[...] = _proj_piece(hb, w_ref, b_ref, _C_CB, D_CONV)
    u_ref[...] = _proj_piece(hb, w_ref, b_ref, _C_CC, D_CONV) * _proj_piece(hb, w_ref, b_ref, _C_CH, D_CONV)


def _proj_sample(h2d, w_main, b_main, w_gate, b_gate):
    rows = h2d.shape[0]

    def out(width, dtype=F32):
        return jax.ShapeDtypeStruct((rows, width), dtype)

    return pl.pallas_call(
        _proj_sample_body,
        out_shape=[out(D_ATTN, BF16), out(3 * KV_W), out(LANES), out(D_CONV), out(D_CONV)],
        compiler_params=pltpu.CompilerParams(vmem_limit_bytes=VMEM_LIMIT_BYTES),
        name="proj_sample",
    )(h2d, w_main, b_main, w_gate, b_gate)


def _conv_sample_body(cb_ref, u_ref, prev_ref, wc_ref, y_ref):
    s = u_ref.shape[1]
    ext = [prev_ref[:, 0, :], prev_ref[:, 1, :]] + [u_ref[:, t, :] for t in range(s)]
    for t in range(s):
        z = ext[t] * wc_ref[0:1, :] + ext[t + 1] * wc_ref[1:2, :] + ext[t + 2] * wc_ref[2:3, :]
        y_ref[:, t, :] = cb_ref[:, t, :] * z


def _conv_sample(cb, u, prev, w_conv):
    return pl.pallas_call(
        _conv_sample_body,
        out_shape=jax.ShapeDtypeStruct(u.shape, F32),
        name="conv_sample",
    )(cb, u, prev, w_conv)


KV_HALF = N_KV * HEAD_DIM


def _compress_rows(load_pos, pe_ref, w1_ref, w2_ref):
    acc = [None, None]
    for p in range(BLK):
        xs = load_pos(p)
        for c in range(2):
            x = (xs[c] + pe_ref[p, c:c + 1, :]).astype(BF16)
            d = _dot(x, w1_ref[p, c])
            acc[c] = d if acc[c] is None else acc[c] + d
    out = [_dot((a * jax.nn.sigmoid(a)).astype(BF16), w2_ref[c]) for c, a in enumerate(acc)]
    return jnp.concatenate(out, axis=-1)


def _compress_prompt_body(k_ref, v_ref, pe_ref, w1_ref, w2_ref, kc_ref):
    ns, t, _ = k_ref.shape
    nb = t // BLK

    def load_pos(p):
        return [r[:, pl.ds(p, nb, stride=BLK), :].reshape(ns * nb, KV_HALF) for r in (k_ref, v_ref)]

    kc_ref[...] = _compress_rows(load_pos, pe_ref, w1_ref, w2_ref).reshape(ns, nb, KV_W)


def _compress_prompt(kvc, pe2, w1bd, w2bd, ns):
    n, t, _ = kvc.shape
    nb = t // BLK
    return pl.pallas_call(
        _compress_prompt_body,
        grid=(n // ns,),
        in_specs=[
            pl.BlockSpec((ns, t, KV_HALF), lambda i: (i, 0, 0)),
            pl.BlockSpec((ns, t, KV_HALF), lambda i: (i, 0, 1)),
            pl.BlockSpec((BLK, 2, KV_HALF), lambda i: (0, 0, 0)),
            pl.BlockSpec((BLK, 2, KV_HALF, KV_HALF), lambda i: (0, 0, 0, 0)),
            pl.BlockSpec((2, KV_HALF, KV_HALF), lambda i: (0, 0, 0)),
        ],
        out_specs=pl.BlockSpec((ns, nb, KV_W), lambda i: (i, 0, 0)),
        out_shape=jax.ShapeDtypeStruct((n, nb, KV_W), F32),
        compiler_params=_cparams(("arbitrary",)),
        name="compress_prompt",
    )(kvc, kvc, pe2, w1bd, w2bd)


DEC_PAGES_PER_STEP = 64


def _compress_decode_body(pt_ref, pool_ref, pe_ref, w1_ref, w2_ref, kc_ref, buf, sem, *, n_pages):
    steps_per_seq = n_pages // DEC_PAGES_PER_STEP
    step = pl.program_id(0) * steps_per_seq + pl.program_id(1)
    n_steps = pl.num_programs(0) * steps_per_seq

    def page_copy(stp, j, c, slot):
        seq = stp // steps_per_seq
        first = (stp % steps_per_seq) * DEC_PAGES_PER_STEP
        page = pt_ref[seq * n_pages + first + j]
        return pltpu.make_async_copy(pool_ref.at[page, :, pl.ds(c * KV_HALF, KV_HALF)],
                                     buf.at[slot, c, pl.ds(j * PAGE_SIZE, PAGE_SIZE)], sem.at[slot])

    def start(stp, slot):
        for j in range(DEC_PAGES_PER_STEP):
            for c in range(2):
                page_copy(stp, j, c, slot).start()

    slot = step % 2

    @pl.when(step == 0)
    def _():
        start(step, slot)

    @pl.when(step + 1 < n_steps)
    def _():
        start(step + 1, 1 - slot)

    for j in range(DEC_PAGES_PER_STEP):
        for c in range(2):
            page_copy(step, j, c, slot).wait()

    nblk = DEC_PAGES_PER_STEP * PAGE_SIZE // BLK

    def load_pos(p):
        return [buf[slot, c, pl.ds(p, nblk, stride=BLK), :] for c in range(2)]

    kc_ref[...] = _compress_rows(load_pos, pe_ref, w1_ref, w2_ref)


def _compress_decode(page_table, pool, pe2, w1bd, w2bd):
    nseq, n_pages = page_table.shape
    steps_per_seq = n_pages // DEC_PAGES_PER_STEP
    nblk = DEC_PAGES_PER_STEP * PAGE_SIZE // BLK
    grid_spec = pltpu.PrefetchScalarGridSpec(
        num_scalar_prefetch=1,
        grid=(nseq, steps_per_seq),
        in_specs=[
            pl.BlockSpec(memory_space=pl.ANY),
            pl.BlockSpec((BLK, 2, KV_HALF), lambda a, s, pt: (0, 0, 0)),
            pl.BlockSpec((BLK, 2, KV_HALF, KV_HALF), lambda a, s, pt: (0, 0, 0, 0)),
            pl.BlockSpec((2, KV_HALF, KV_HALF), lambda a, s, pt: (0, 0, 0)),
        ],
        out_specs=pl.BlockSpec((None, nblk, KV_W), lambda a, s, pt: (a, s, 0)),
        scratch_shapes=[pltpu.VMEM((2, 2, DEC_PAGES_PER_STEP * PAGE_SIZE, KV_HALF), F32),
                        pltpu.SemaphoreType.DMA((2,))],
    )
    return pl.pallas_call(
        functools.partial(_compress_decode_body, n_pages=n_pages),
        grid_spec=grid_spec,
        out_shape=jax.ShapeDtypeStruct((nseq, n_pages * PAGE_SIZE // BLK, KV_W), F32),
        compiler_params=_cparams(("arbitrary", "arbitrary")),
        name="compress_decode",
    )(page_table.reshape(-1), pool, pe2, w1bd, w2bd)


ATT_TQ = 256
ATT_TK = 256


def _softmax_update(s, valid, v, state):
    m, l, acc = state
    s = jnp.where(valid, s, NEG)
    m_new = jnp.maximum(m, jnp.max(s, axis=-1, keepdims=True))
    a = jnp.exp(m - m_new)
    e = jnp.exp(s - m_new)
    l = a * l + jnp.sum(e, axis=-1, keepdims=True)
    acc = a * acc + _dot(e.astype(BF16), v)
    return m_new, l, acc


def _attn_prompt_body(q_ref, kc_ref, kvs_ref, kvw_ref, gates_ref, o_ref):
    tq, tk = ATT_TQ, ATT_TK
    qi = pl.program_id(1)
    nb = kc_ref.shape[0]
    t0 = qi * tq
    tpos = t0 + lax.broadcasted_iota(jnp.int32, (tq, 1), 0)
    jblk = lax.broadcasted_iota(jnp.int32, (1, nb), 1)
    vis = ((jblk + 1) * BLK - 1) <= tpos
    cur = tpos >> BLK_SHIFT
    kcb = kc_ref[...].astype(BF16)
    gates = gates_ref[...]

    def init_state():
        return (jnp.full((tq, 1), NEG, F32), jnp.zeros((tq, 1), F32), jnp.zeros((tq, HEAD_DIM), F32))

    for g in range(N_KV):
        kg = kcb[:, g * HEAD_DIM:(g + 1) * HEAD_DIM]
        vg = kcb[:, (N_KV + g) * HEAD_DIM:(N_KV + g + 1) * HEAD_DIM]
        heads = [g * GQA_R + r for r in range(GQA_R)]
        qh = [q_ref[:, h * HEAD_DIM:(h + 1) * HEAD_DIM] for h in heads]

        imp = jnp.zeros((tq, nb), F32)
        o_cmp = []
        for r in range(GQA_R):
            s = jnp.where(vis, _dot_nt(qh[r], kg), NEG)
            m = jnp.max(s, axis=-1, keepdims=True)
            e = jnp.where(vis, jnp.exp(s - m), 0.0)
            l = jnp.sum(e, axis=-1, keepdims=True)
            p = e / jnp.where(l > 0.0, l, 1.0)
            imp = imp + p
            o_cmp.append(_dot(p.astype(BF16), vg))

        score = jnp.where(jblk > cur, NEG, imp)
        score = jnp.where((jblk == 0) | (jblk == cur) | (jblk == cur - 1), FORCE, score)
        rank = jnp.zeros((tq, nb), jnp.int32)
        for j2 in range(nb):
            col = score[:, j2:j2 + 1]
            beats = (col > score) | ((col == score) & (jblk > j2))
            rank = rank + beats.astype(jnp.int32)
        selb = jnp.where(rank < N_SEL, 1.0, 0.0).astype(BF16)

        kcol = slice(g * HEAD_DIM, (g + 1) * HEAD_DIM)
        vcol = slice((N_KV + g) * HEAD_DIM, (N_KV + g + 1) * HEAD_DIM)

        def sel_step(kt, states):
            ks = pl.multiple_of(kt * tk, tk)
            kpos = ks + lax.broadcasted_iota(jnp.int32, (1, tk), 1)
            expand = ((kpos >> BLK_SHIFT) == lax.broadcasted_iota(jnp.int32, (nb, 1), 0))
            expand = jnp.where(expand, 1.0, 0.0).astype(BF16)
            valid = (_dot(selb, expand) > 0.5) & (kpos <= tpos)
            kblk = kvs_ref[pl.ds(ks, tk), kcol].astype(BF16)
            vblk = kvs_ref[pl.ds(ks, tk), vcol].astype(BF16)
            return tuple(_softmax_update(_dot_nt(qh[r], kblk), valid, vblk, states[r])
                         for r in range(GQA_R))

        sel_states = lax.fori_loop(0, qi + 1, sel_step, tuple(init_state() for _ in range(GQA_R)))

        win_states = [init_state() for _ in range(GQA_R)]
        for back in range(WINDOW // tk + 1):
            kt = qi - back
            ks = pl.multiple_of(jnp.maximum(kt, 0) * tk, tk)
            kpos = kt * tk + lax.broadcasted_iota(jnp.int32, (1, tk), 1)
            dist = tpos - kpos
            valid = (dist >= 0) & (dist < WINDOW) & (kpos >= 0)
            kblk = kvw_ref[pl.ds(ks, tk), kcol].astype(BF16)
            vblk = kvw_ref[pl.ds(ks, tk), vcol].astype(BF16)
            win_states = [_softmax_update(_dot_nt(qh[r], kblk), valid, vblk, win_states[r])
                          for r in range(GQA_R)]

        for r, h in enumerate(heads):
            o_sel = sel_states[r][2] / sel_states[r][1]
            o_win = win_states[r][2] / win_states[r][1]
            o = (gates[:, h:h + 1] * o_cmp[r] + gates[:, N_HEADS + h:N_HEADS + h + 1] * o_sel
                 + gates[:, 2 * N_HEADS + h:2 * N_HEADS + h + 1] * o_win)
            o_ref[:, h * HEAD_DIM:(h + 1) * HEAD_DIM] = o


def _attn_prompt(q, kc, kvs, kvw, gates):
    n, t, _ = q.shape
    nb = kc.shape[1]
    row = lambda a, i: (a, i, 0)
    seq = lambda a, i: (a, 0, 0)
    return pl.pallas_call(
        _attn_prompt_body,
        grid=(n, t // ATT_TQ),
        in_specs=[
            pl.BlockSpec((None, ATT_TQ, D_ATTN), row),
            pl.BlockSpec((None, nb, KV_W), seq),
            pl.BlockSpec((None, t, KV_W), seq),
            pl.BlockSpec((None, t, KV_W), seq),
            pl.BlockSpec((None, ATT_TQ, LANES), row),
        ],
        out_specs=pl.BlockSpec((None, ATT_TQ, D_ATTN), row),
        out_shape=jax.ShapeDtypeStruct((n, t, D_ATTN), F32),
        compiler_params=_cparams(("arbitrary", "arbitrary")),
        name="attn_prompt",
    )(q, kc, kvs, kvw, gates)


Q_PAD = 16
DEC_SEQ_PER_STEP = 8


def _dec_cmp_body(q_ref, kc_ref, ocmp_ref, idx_ref, imp_buf, *, past, s_new):
    nseq, ncombo = q_ref.shape[0], q_ref.shape[1]
    nb = kc_ref.shape[1]
    ncand = nb + 1
    width = imp_buf.shape[1]
    imp_buf[...] = jnp.zeros(imp_buf.shape, F32)
    jblk = lax.broadcasted_iota(jnp.int32, (1, nb), 1)
    for a in range(nseq):
        kcb = kc_ref[a].astype(BF16)
        for c in range(ncombo):
            g, t = c // s_new, c % s_new
            kg = kcb[:, g * HEAD_DIM:(g + 1) * HEAD_DIM]
            vg = kcb[:, (N_KV + g) * HEAD_DIM:(N_KV + g + 1) * HEAD_DIM]
            vis = ((jblk + 1) * BLK - 1) <= (past + t)
            s = jnp.where(vis, _dot_nt(q_ref[a, c], kg), NEG)
            m = jnp.max(s, axis=-1, keepdims=True)
            e = jnp.where(vis, jnp.exp(s - m), 0.0)
            l = jnp.sum(e, axis=-1, keepdims=True)
            p = e / jnp.where(l > 0.0, l, 1.0)
            ocmp_ref[a, c] = _dot(p.astype(BF16), vg)
            row = a * ncombo + c
            imp_buf[row:row + 1, 0:nb] = jnp.sum(p[0:GQA_R, :], axis=0, keepdims=True)

    rows = nseq * ncombo
    imp = imp_buf[...]
    jc = lax.broadcasted_iota(jnp.int32, (1, width), 1)
    t_row = lax.broadcasted_iota(jnp.int32, (rows, 1), 0) & (s_new - 1)
    cur = (past + t_row) >> BLK_SHIFT
    score = jnp.where(jc > cur, NEG, imp)
    score = jnp.where((jc == 0) | (jc == cur) | (jc == cur - 1), FORCE, score)
    score = jnp.where(jc < ncand, score, -jnp.inf)
    jcf = jc.astype(F32)
    out_lane = lax.broadcasted_iota(jnp.int32, (1, LANES), 1)
    idx = jnp.zeros((rows, LANES), jnp.int32)
    for k in range(N_SEL):
        m = jnp.max(score, axis=-1, keepdims=True)
        first = jnp.min(jnp.where(score == m, jcf, 1e9), axis=-1, keepdims=True)
        idx = jnp.where(out_lane == k, first.astype(jnp.int32), idx)
        score = jnp.where(jcf == first, -jnp.inf, score)
    idx_ref[...] = idx


def _dec_cmp(q_pad, kc, past, s_new):
    nseq, ncombo = q_pad.shape[0], q_pad.shape[1]
    nb = kc.shape[1]
    ns = DEC_SEQ_PER_STEP
    width = (nb + 1 + LANES - 1) // LANES * LANES
    return pl.pallas_call(
        functools.partial(_dec_cmp_body, past=past, s_new=s_new),
        grid=(nseq // ns,),
        in_specs=[
            pl.BlockSpec((ns, ncombo, Q_PAD, HEAD_DIM), lambda i: (i, 0, 0, 0)),
            pl.BlockSpec((ns, nb, KV_W), lambda i: (i, 0, 0)),
        ],
        out_specs=[
            pl.BlockSpec((ns, ncombo, Q_PAD, HEAD_DIM), lambda i: (i, 0, 0, 0)),
            pl.BlockSpec((ns * ncombo, LANES), lambda i: (i, 0)),
        ],
        out_shape=[jax.ShapeDtypeStruct((nseq, ncombo, Q_PAD, HEAD_DIM), F32),
                   jax.ShapeDtypeStruct((nseq * ncombo, LANES), jnp.int32)],
        scratch_shapes=[pltpu.VMEM((ns * ncombo, width), F32)],
        compiler_params=_cparams(("arbitrary",)),
        name="dec_cmp_select",
    )(q_pad, kc)


def _dec_sel_win_body(idx_ref, pt_ref, q_ref, pool_ref, newblk_ref, win_ref, kvw_new_ref,
                      osel_ref, owin_ref, buf, sem, *, past, s_new, n_pages):
    ncombo = q_ref.shape[0]
    n_past_blk = past // BLK
    bpp = PAGE_SIZE // BLK
    n = pl.program_id(0)
    nseq = pl.num_programs(0)
    slot = n % 2

    def dst_rows(c, k, sl):
        return buf.at[sl, pl.ds((c * N_SEL + k) * BLK, BLK)]

    def new_copy(seq, c, k, sl):
        return pltpu.make_async_copy(newblk_ref.at[seq], dst_rows(c, k, sl), sem.at[sl])

    def past_copy(seq, j, c, k, sl):
        jp = jnp.clip(j, 0, n_past_blk - 1)
        page = pt_ref[seq * n_pages + jp // bpp]
        return pltpu.make_async_copy(pool_ref.at[page * bpp + jp % bpp], dst_rows(c, k, sl), sem.at[sl])

    def start(seq, sl):
        for c in range(ncombo):
            for k in range(N_SEL):
                j = idx_ref[(seq * ncombo + c) * N_SEL + k]

                @pl.when(j < n_past_blk)
                def _():
                    past_copy(seq, j, c, k, sl).start()

                @pl.when(j >= n_past_blk)
                def _():
                    new_copy(seq, c, k, sl).start()

    @pl.when(n == 0)
    def _():
        start(n, slot)

    @pl.when(n + 1 < nseq)
    def _():
        start(n + 1, 1 - slot)

    for c in range(ncombo):
        for k in range(N_SEL):
            new_copy(n, c, k, slot).wait()

    nkeys = N_SEL * BLK
    lane = lax.broadcasted_iota(jnp.int32, (1, nkeys), 1)
    for c in range(ncombo):
        g, t = c // s_new, c % s_new
        base = c * nkeys
        kblk = buf[slot, pl.ds(base, nkeys), g * HEAD_DIM:(g + 1) * HEAD_DIM].astype(BF16)
        vblk = buf[slot, pl.ds(base, nkeys),
                   (N_KV + g) * HEAD_DIM:(N_KV + g + 1) * HEAD_DIM].astype(BF16)
        kstart = jnp.zeros((1, nkeys), jnp.int32)
        for k in range(N_SEL):
            j = idx_ref[(n * ncombo + c) * N_SEL + k]
            kstart = jnp.where((lane >> BLK_SHIFT) == k, j * BLK, kstart)
        valid = (kstart + (lane & (BLK - 1))) <= (past + t)
        s = jnp.where(valid, _dot_nt(q_ref[c], kblk), NEG)
        m = jnp.max(s, axis=-1, keepdims=True)
        e = jnp.exp(s - m)
        l = jnp.sum(e, axis=-1, keepdims=True)
        osel_ref[c] = _dot(e.astype(BF16), vblk) / l

    nwin = win_ref.shape[0]
    npad = kvw_new_ref.shape[0]
    wpos = past - nwin + lax.broadcasted_iota(jnp.int32, (1, nwin), 1)
    npos = past + lax.broadcasted_iota(jnp.int32, (1, npad), 1)
    new_real = lax.broadcasted_iota(jnp.int32, (1, npad), 1) < s_new
    for c in range(ncombo):
        g, t = c // s_new, c % s_new
        kcol = slice(g * HEAD_DIM, (g + 1) * HEAD_DIM)
        vcol = slice((N_KV + g) * HEAD_DIM, (N_KV + g + 1) * HEAD_DIM)
        qc = q_ref[c]
        dw = (past + t) - wpos
        dn = (past + t) - npos
        valid_w = (dw >= 0) & (dw < WINDOW) & (wpos >= 0)
        valid_n = (dn >= 0) & (dn < WINDOW) & new_real
        sw = jnp.where(valid_w, _dot_nt(qc, win_ref[:, kcol].astype(BF16)), NEG)
        sn = jnp.where(valid_n, _dot_nt(qc, kvw_new_ref[:, kcol].astype(BF16)), NEG)
        m = jnp.maximum(jnp.max(sw, axis=-1, keepdims=True), jnp.max(sn, axis=-1, keepdims=True))
        ew = jnp.exp(sw - m)
        en = jnp.exp(sn - m)
        l = jnp.sum(ew, axis=-1, keepdims=True) + jnp.sum(en, axis=-1, keepdims=True)
        o = (_dot(ew.astype(BF16), win_ref[:, vcol].astype(BF16))
             + _dot(en.astype(BF16), kvw_new_ref[:, vcol].astype(BF16)))
        owin_ref[c] = o / l


def _dec_sel_win(idx_flat, pt_flat, q_pad, pool_blocks, new_blocks, win_buf, kvw_new_pad,
                 past, s_new, n_pages):
    nseq, ncombo = q_pad.shape[0], q_pad.shape[1]
    nwin = win_buf.shape[1]
    npad = kvw_new_pad.shape[1]
    per_seq = lambda a, idx, pt: (a, 0, 0, 0)
    grid_spec = pltpu.PrefetchScalarGridSpec(
        num_scalar_prefetch=2,
        grid=(nseq,),
        in_specs=[
            pl.BlockSpec((None, ncombo, Q_PAD, HEAD_DIM), per_seq),
            pl.BlockSpec(memory_space=pl.ANY),
            pl.BlockSpec(memory_space=pl.ANY),
            pl.BlockSpec((None, nwin, KV_W), lambda a, idx, pt: (a, 0, 0)),
            pl.BlockSpec((None, npad, KV_W), lambda a, idx, pt: (a, 0, 0)),
        ],
        out_specs=[
            pl.BlockSpec((None, ncombo, Q_PAD, HEAD_DIM), per_seq),
            pl.BlockSpec((None, ncombo, Q_PAD, HEAD_DIM), per_seq),
        ],
        scratch_shapes=[pltpu.VMEM((2, ncombo * N_SEL * BLK, KV_W), F32),
                        pltpu.SemaphoreType.DMA((2,))],
    )
    shape = jax.ShapeDtypeStruct((nseq, ncombo, Q_PAD, HEAD_DIM), F32)
    return pl.pallas_call(
        functools.partial(_dec_sel_win_body, past=past, s_new=s_new, n_pages=n_pages),
        grid_spec=grid_spec,
        out_shape=[shape, shape],
        compiler_params=_cparams(("arbitrary",)),
        name="dec_sel_win",
    )(idx_flat, pt_flat, q_pad, pool_blocks, new_blocks, win_buf, kvw_new_pad)


def _out_ln_body(*refs, gated):
    if gated:
        ocmp_ref, osel_ref, owin_ref, gates_ref, yconv_ref, h_ref, wo_ref, g_ref, b_ref, y_ref = refs
        gates = gates_ref[...]
        pieces = []
        for h in range(N_HEADS):
            col = slice(h * HEAD_DIM, (h + 1) * HEAD_DIM)
            pieces.append(gates[:, h:h + 1] * ocmp_ref[:, col]
                          + gates[:, N_HEADS + h:N_HEADS + h + 1] * osel_ref[:, col]
                          + gates[:, 2 * N_HEADS + h:2 * N_HEADS + h + 1] * owin_ref[:, col])
        o = jnp.concatenate(pieces, axis=-1)
    else:
        o_ref, yconv_ref, h_ref, wo_ref, g_ref, b_ref, y_ref = refs
        o = o_ref[...]
    mix = (_dot(o.astype(BF16), wo_ref[0:D_ATTN, :])
           + _dot(yconv_ref[...].astype(BF16), wo_ref[D_ATTN:D_MODEL, :]))
    y_ref[...] = _layer_norm(ALPHA * h_ref[...] + mix, g_ref[...], b_ref[...])


def _out_ln(branches, yconv, h2d, w_out, g, b, tm):
    rows = h2d.shape[0]
    gated = len(branches) > 1
    row = lambda i: (i, 0)
    const = lambda i: (0, 0)
    widths = [D_ATTN, D_ATTN, D_ATTN, LANES] if gated else [D_ATTN]
    return pl.pallas_call(
        functools.partial(_out_ln_body, gated=gated),
        grid=(rows // tm,),
        in_specs=[pl.BlockSpec((tm, w), row) for w in widths] + [
            pl.BlockSpec((tm, D_CONV), row),
            pl.BlockSpec((tm, D_MODEL), row),
            pl.BlockSpec((D_MODEL, D_MODEL), const),
            pl.BlockSpec((1, D_MODEL), const),
            pl.BlockSpec((1, D_MODEL), const),
        ],
        out_specs=pl.BlockSpec((tm, D_MODEL), row),
        out_shape=jax.ShapeDtypeStruct((rows, D_MODEL), F32),
        compiler_params=_cparams(("arbitrary",)),
        name="out_ln",
    )(*branches, yconv, h2d, w_out, g, b)


def _block_diag_groups(w):
    z = jnp.zeros_like(w)
    return jnp.concatenate([jnp.concatenate([w, z], axis=-1),
                            jnp.concatenate([z, w], axis=-1)], axis=-2).astype(BF16)


def _to_combo_layout(x, nseq, s_new):
    x = x.reshape(nseq, s_new, N_KV, GQA_R, HEAD_DIM).transpose(0, 2, 1, 3, 4)
    x = x.reshape(nseq, N_KV * s_new, GQA_R, HEAD_DIM)
    return jnp.pad(x, ((0, 0), (0, 0), (0, Q_PAD - GQA_R), (0, 0)))


def _from_combo_layout(x, nseq, s_new):
    x = x[:, :, :GQA_R, :].reshape(nseq, N_KV, s_new, GQA_R, HEAD_DIM).transpose(0, 2, 1, 3, 4)
    return x.reshape(nseq * s_new, D_ATTN)


def kernel(x_prompt, x_sample, cache_cmp_kv, cache_sel_kv, state_win_kv, state_conv, page_table,
           ffa_gate, ffa_up, ffa_down, ln1_g, ln1_b, w_in, b_in, w_conv, cmp_w1, cmp_w2, cmp_pe,
           w_out, ln2_g, ln2_b, ffb_gate, ffb_up, ffb_down, ln3_g, ln3_b):
    depth = ffa_gate.shape[0]
    assert depth == 1
    l = 0
    n, t, d = x_prompt.shape
    nseq, s_new, _ = x_sample.shape
    n_pages = page_table.shape[1]
    past = n_pages * PAGE_SIZE
    n_phys = cache_cmp_kv.shape[1]
    rows_p, rows_s = n * t, nseq * s_new
    kvshape = (2, N_KV, HEAD_DIM)

    bf = lambda w: w.astype(BF16)
    vec = lambda v: v.reshape(1, -1)
    w_main = bf(w_in[l][:, :D_MAIN])
    b_main = vec(b_in[l][:D_MAIN])
    w_gate = bf(jnp.pad(w_in[l][:, D_MAIN:], ((0, 0), (0, LANES - N_GATE))))
    b_gate = vec(jnp.pad(b_in[l][D_MAIN:], (0, LANES - N_GATE)))
    pe2 = jnp.concatenate([cmp_pe[l]] * N_KV, axis=-1)
    w1bd = _block_diag_groups(cmp_w1[l].transpose(1, 0, 2, 3))
    w2bd = _block_diag_groups(cmp_w2[l])
    w_out_b = bf(w_out[l])
    ffa = (bf(ffa_gate[l]), bf(ffa_up[l]), bf(ffa_down[l]), vec(ln1_g[l]), vec(ln1_b[l]))
    ffb = (bf(ffb_gate[l]), bf(ffb_up[l]), bf(ffb_down[l]), vec(ln3_g[l]), vec(ln3_b[l]))

    hp = _ffn_ln(x_prompt.reshape(rows_p, d), *ffa, tm=512)
    q, kvc, kvs, kvw, gates, yconv, cstate = _proj_prompt(
        hp.reshape(n, t, d), w_main, b_main, w_gate, b_gate, w_conv[l], tm=512)
    kc = _compress_prompt(kvc, pe2, w1bd, w2bd, ns=4)
    o = _attn_prompt(q, kc, kvs, kvw, gates)
    yp = _out_ln([o.reshape(rows_p, D_ATTN)], yconv.reshape(rows_p, D_CONV), hp, w_out_b,
                 vec(ln2_g[l]), vec(ln2_b[l]), tm=512)
    yp = _ffn_ln(yp, *ffb, tm=512).reshape(n, t, d)
    p_cmp = kvc.reshape(1, n, t, *kvshape)
    p_sel = kvs.reshape(1, n, t, *kvshape)
    p_win = kvw[:, t - min(WINDOW, t):].reshape(1, n, min(WINDOW, t), *kvshape)
    p_conv = cstate[:, 8 - (CONV_W - 1):].reshape(1, n, CONV_W - 1, D_CONV)

    hs = _ffn_ln(x_sample.reshape(rows_s, d), *ffa, tm=rows_s)
    qs, kv3, gates_s, cb_s, u_s = _proj_sample(hs, w_main, b_main, w_gate, b_gate)
    kvc_s, kvs_s, kvw_s = kv3[:, :KV_W], kv3[:, KV_W:2 * KV_W], kv3[:, 2 * KV_W:]
    u3 = u_s.reshape(nseq, s_new, D_CONV)
    yconv_s = _conv_sample(cb_s.reshape(nseq, s_new, D_CONV), u3, state_conv[l], w_conv[l])
    s_conv = jnp.concatenate([state_conv[l], u3], axis=1)[:, -(CONV_W - 1):]

    kc_s = _compress_decode(page_table, cache_cmp_kv[l].reshape(n_phys, PAGE_SIZE, KV_W),
                            pe2, w1bd, w2bd)
    q_pad = _to_combo_layout(qs, nseq, s_new)
    ocmp_c, idx_pad = _dec_cmp(q_pad, kc_s, past, s_new)
    idx_flat = idx_pad[:, :N_SEL].reshape(-1)
    new_blocks = jnp.pad(kvs_s.reshape(nseq, s_new, KV_W), ((0, 0), (0, BLK - s_new), (0, 0)))
    kvw_new_pad = jnp.pad(kvw_s.reshape(nseq, s_new, KV_W), ((0, 0), (0, Q_PAD - s_new), (0, 0)))
    win_buf = state_win_kv[l].reshape(nseq, -1, KV_W)
    osel_c, owin_c = _dec_sel_win(
        idx_flat, page_table.reshape(-1), q_pad,
        cache_sel_kv[l].reshape(n_phys * (PAGE_SIZE // BLK), BLK, KV_W), new_blocks,
        win_buf, kvw_new_pad, past, s_new, n_pages)
    branches = [_from_combo_layout(x, nseq, s_new) for x in (ocmp_c, osel_c, owin_c)]
    ys = _out_ln(branches + [gates_s], yconv_s.reshape(rows_s, D_CONV), hs, w_out_b,
                 vec(ln2_g[l]), vec(ln2_b[l]), tm=rows_s)
    ys = _ffn_ln(ys, *ffb, tm=rows_s).reshape(nseq, s_new, d)
    ctx_win = jnp.concatenate([win_buf, kvw_s.reshape(nseq, s_new, KV_W)], axis=1)
    n_keep = min(WINDOW, past + s_new)
    s_cmp = kvc_s.reshape(1, nseq, s_new, *kvshape)
    s_sel = kvs_s.reshape(1, nseq, s_new, *kvshape)
    s_win = ctx_win[:, -n_keep:].reshape(1, nseq, n_keep, *kvshape)
    return (yp, ys, p_cmp, p_sel, p_win, p_conv, s_cmp, s_sel, s_win,
            s_conv.reshape(1, nseq, CONV_W - 1, D_CONV))
```

```python
import functools

import jax
import jax.numpy as jnp
from jax import lax
from jax.experimental import pallas as pl
from jax.experimental.pallas import tpu as pltpu

F32 = jnp.float32
BF16 = jnp.bfloat16

D_MODEL = 1024
D_CONV = 512
D_ATTN = 512
HEAD_DIM = 64
N_HEADS = 8
N_KV = 2
GQA_R = N_HEADS // N_KV
N_BR = 3
CONV_W = 3
BLK = 64
BLK_SHIFT = 6
N_SEL = 16
WINDOW = 512
PAGE_SIZE = 128
PAGE_SHIFT = 7
D_FF = 2816
KV_W = 2 * N_KV * HEAD_DIM
D_MAIN = 3 * D_CONV + D_ATTN + 3 * KV_W
N_GATE = N_BR * N_HEADS
ALPHA = 2.0 ** 0.25
SCALE = HEAD_DIM ** -0.5
LN_EPS = 1e-5
NEG = -1e30
FORCE = 1e4

LANES = 128
VMEM_LIMIT_BYTES = 56 * 1024 * 1024


def _cparams(sem):
    return pltpu.CompilerParams(dimension_semantics=sem, vmem_limit_bytes=VMEM_LIMIT_BYTES)


def _layer_norm(y, g, b):
    mu = jnp.mean(y, axis=-1, keepdims=True)
    yc = y - mu
    var = jnp.mean(yc * yc, axis=-1, keepdims=True)
    return yc * lax.rsqrt(var + LN_EPS) * g + b


def _dot(a, b):
    return jnp.dot(a, b, preferred_element_type=F32)


def _dot_nt(a, b):
    return lax.dot_general(a, b, (((1,), (1,)), ((), ())), preferred_element_type=F32)


FFN_CHUNK = 256


def _ffn_ln_body(x_ref, wg_ref, wu_ref, wd_ref, g_ref, b_ref, o_ref):
    x = x_ref[...]
    xb = x.astype(BF16)
    acc = jnp.zeros(x.shape, F32)
    for c in range(D_FF // FFN_CHUNK):
        sl = slice(c * FFN_CHUNK, (c + 1) * FFN_CHUNK)
        gg = _dot(xb, wg_ref[:, sl])
        uu = _dot(xb, wu_ref[:, sl])
        hh = (gg * jax.nn.sigmoid(gg) * uu).astype(BF16)
        acc = acc + _dot(hh, wd_ref[sl, :])
    o_ref[...] = _layer_norm(ALPHA * x + 0.5 * acc, g_ref[...], b_ref[...])


def _ffn_ln(x2d, wg, wu, wd, g, b, tm):
    rows = x2d.shape[0]
    const = lambda i: (0, 0)
    return pl.pallas_call(
        _ffn_ln_body,
        grid=(rows // tm,),
        in_specs=[
            pl.BlockSpec((tm, D_MODEL), lambda i: (i, 0)),
            pl.BlockSpec((D_MODEL, D_FF), const),
            pl.BlockSpec((D_MODEL, D_FF), const),
            pl.BlockSpec((D_FF, D_MODEL), const),
            pl.BlockSpec((1, D_MODEL), const),
            pl.BlockSpec((1, D_MODEL), const),
        ],
        out_specs=pl.BlockSpec((tm, D_MODEL), lambda i: (i, 0)),
        out_shape=jax.ShapeDtypeStruct((rows, D_MODEL), F32),
        compiler_params=_cparams(("arbitrary",)),
        name="ffn_ln",
    )(x2d, wg, wu, wd, g, b)


_C_CB, _C_CC, _C_CH, _C_Q = 0, D_CONV, 2 * D_CONV, 3 * D_CONV
_C_KVC = 3 * D_CONV + D_ATTN
_C_KVS = _C_KVC + KV_W
_C_KVW = _C_KVS + KV_W


def _proj_piece(hb, w_ref, b_ref, lo, width):
    return _dot(hb, w_ref[:, lo:lo + width]) + b_ref[:, lo:lo + width]


def _proj_prompt_body(h_ref, w_ref, b_ref, wgt_ref, bgt_ref, wc_ref,
                      q_ref, kvc_ref, kvs_ref, kvw_ref, gates_ref, yconv_ref, cstate_ref, ubuf):
    i = pl.program_id(1)
    tm = h_ref.shape[0]
    hb = h_ref[...].astype(BF16)
    q_ref[...] = (_proj_piece(hb, w_ref, b_ref, _C_Q, D_ATTN) * SCALE).astype(BF16)
    kvc_ref[...] = _proj_piece(hb, w_ref, b_ref, _C_KVC, KV_W)
    kvs_ref[...] = _proj_piece(hb, w_ref, b_ref, _C_KVS, KV_W)
    kvw_ref[...] = _proj_piece(hb, w_ref, b_ref, _C_KVW, KV_W)
    gates_ref[...] = jax.nn.sigmoid(_dot(hb, wgt_ref[...]) + bgt_ref[...])
    cb = _proj_piece(hb, w_ref, b_ref, _C_CB, D_CONV)
    u = _proj_piece(hb, w_ref, b_ref, _C_CC, D_CONV) * _proj_piece(hb, w_ref, b_ref, _C_CH, D_CONV)

    @pl.when(i == 0)
    def _():
        ubuf[0:8, :] = jnp.zeros((8, D_CONV), F32)

    ubuf[8:8 + tm, :] = u
    u1 = ubuf[7:7 + tm, :]
    u2 = ubuf[6:6 + tm, :]
    z = u2 * wc_ref[0:1, :] + u1 * wc_ref[1:2, :] + u * wc_ref[2:3, :]
    yconv_ref[...] = cb * z
    tail = u[tm - 8:tm, :]
    ubuf[0:8, :] = tail
    cstate_ref[...] = tail


def _proj_prompt(h, w_main, b_main, w_gate, b_gate, w_conv, tm):
    n, t, _ = h.shape
    const = lambda a, i: (0, 0)
    row = lambda a, i: (a, i, 0)

    def out(width, dtype=F32):
        return jax.ShapeDtypeStruct((n, t, width), dtype)

    return pl.pallas_call(
        _proj_prompt_body,
        grid=(n, t // tm),
        in_specs=[
            pl.BlockSpec((None, tm, D_MODEL), row),
            pl.BlockSpec((D_MODEL, D_MAIN), const),
            pl.BlockSpec((1, D_MAIN), const),
            pl.BlockSpec((D_MODEL, LANES), const),
            pl.BlockSpec((1, LANES), const),
            pl.BlockSpec((CONV_W, D_CONV), const),
        ],
        out_specs=[
            pl.BlockSpec((None, tm, D_ATTN), row),
            pl.BlockSpec((None, tm, KV_W), row),
            pl.BlockSpec((None, tm, KV_W), row),
            pl.BlockSpec((None, tm, KV_W), row),
            pl.BlockSpec((None, tm, LANES), row),
            pl.BlockSpec((None, tm, D_CONV), row),
            pl.BlockSpec((None, 8, D_CONV), lambda a, i: (a, 0, 0)),
        ],
        out_shape=[out(D_ATTN, BF16), out(KV_W), out(KV_W), out(KV_W), out(LANES), out(D_CONV),
                   jax.ShapeDtypeStruct((n, 8, D_CONV), F32)],
        scratch_shapes=[pltpu.VMEM((tm + 8, D_CONV), F32)],
        compiler_params=_cparams(("arbitrary", "arbitrary")),
        name="proj_prompt",
    )(h, w_main, b_main, w_gate, b_gate, w_conv)


def _proj_sample_body(h_ref, w_ref, b_ref, wgt_ref, bgt_ref, q_ref, kv_ref, gates_ref, cb_ref, u_ref):
    hb = h_ref[...].astype(BF16)
    q_ref[...] = (_proj_piece(hb, w_ref, b_ref, _C_Q, D_ATTN) * SCALE).astype(BF16)
    kv_ref[...] = _proj_piece(hb, w_ref, b_ref, _C_KVC, 3 * KV_W)
    gates_ref[...] = jax.nn.sigmoid(_dot(hb, wgt_ref[...]) + bgt_ref[...])
    cb_ref[...] = _proj_piece(hb, w_ref, b_ref, _C_CB, D_CONV)
    u_ref[...] = _proj_piece(hb, w_ref, b_ref, _C_CC, D_CONV) * _proj_piece(hb, w_ref, b_ref, _C_CH, D_CONV)


def _proj_sample(h2d, w_main, b_main, w_gate, b_gate):
    rows = h2d.shape[0]

    def out(width, dtype=F32):
        return jax.ShapeDtypeStruct((rows, width), dtype)

    return pl.pallas_call(
        _proj_sample_body,
        out_shape=[out(D_ATTN, BF16), out(3 * KV_W), out(LANES), out(D_CONV), out(D_CONV)],
        compiler_params=pltpu.CompilerParams(vmem_limit_bytes=VMEM_LIMIT_BYTES),
        name="proj_sample",
    )(h2d, w_main, b_main, w_gate, b_gate)


def _conv_sample_body(cb_ref, u_ref, prev_ref, wc_ref, y_ref):
    s = u_ref.shape[1]
    ext = [prev_ref[:, 0, :], prev_ref[:, 1, :]] + [u_ref[:, t, :] for t in range(s)]
    for t in range(s):
        z = ext[t] * wc_ref[0:1, :] + ext[t + 1] * wc_ref[1:2, :] + ext[t + 2] * wc_ref[2:3, :]
        y_ref[:, t, :] = cb_ref[:, t, :] * z


def _conv_sample(cb, u, prev, w_conv):
    return pl.pallas_call(
        _conv_sample_body,
        out_shape=jax.ShapeDtypeStruct(u.shape, F32),
        name="conv_sample",
    )(cb, u, prev, w_conv)


KV_HALF = N_KV * HEAD_DIM


def _compress_rows(load_pos, pe_ref, w1_ref, w2_ref):
    acc = [None, None]
    for p in range(BLK):
        xs = load_pos(p)
        for c in range(2):
            x = (xs[c] + pe_ref[p, c:c + 1, :]).astype(BF16)
            d = _dot(x, w1_ref[p, c])
            acc[c] = d if acc[c] is None else acc[c] + d
    out = [_dot((a * jax.nn.sigmoid(a)).astype(BF16), w2_ref[c]) for c, a in enumerate(acc)]
    return jnp.concatenate(out, axis=-1)


def _compress_prompt_body(k_ref, v_ref, pe_ref, w1_ref, w2_ref, kc_ref):
    ns, t, _ = k_ref.shape
    nb = t // BLK

    def load_pos(p):
        return [r[:, pl.ds(p, nb, stride=BLK), :].reshape(ns * nb, KV_HALF) for r in (k_ref, v_ref)]

    kc_ref[...] = _compress_rows(load_pos, pe_ref, w1_ref, w2_ref).reshape(ns, nb, KV_W)


def _compress_prompt(kvc, pe2, w1bd, w2bd, ns):
    n, t, _ = kvc.shape
    nb = t // BLK
    return pl.pallas_call(
        _compress_prompt_body,
        grid=(n // ns,),
        in_specs=[
            pl.BlockSpec((ns, t, KV_HALF), lambda i: (i, 0, 0)),
            pl.BlockSpec((ns, t, KV_HALF), lambda i: (i, 0, 1)),
            pl.BlockSpec((BLK, 2, KV_HALF), lambda i: (0, 0, 0)),
            pl.BlockSpec((BLK, 2, KV_HALF, KV_HALF), lambda i: (0, 0, 0, 0)),
            pl.BlockSpec((2, KV_HALF, KV_HALF), lambda i: (0, 0, 0)),
        ],
        out_specs=pl.BlockSpec((ns, nb, KV_W), lambda i: (i, 0, 0)),
        out_shape=jax.ShapeDtypeStruct((n, nb, KV_W), F32),
        compiler_params=_cparams(("arbitrary",)),
        name="compress_prompt",
    )(kvc, kvc, pe2, w1bd, w2bd)


BPP = PAGE_SIZE // BLK
D_PAIRS = HEAD_DIM // 2


def _compress_decode_body(pt_ref, pool_ref, pe_ref, w1_ref, w2_ref, kc_ref, buf, sem, *, n_pages):
    n = pl.program_id(0)
    slot = n % 2

    def page_copy(seq, j, sl):
        page = pt_ref[seq * n_pages + j]
        return pltpu.make_async_copy(pool_ref.at[page], buf.at[sl, pl.ds(j * KV_W, KV_W)], sem.at[sl])

    def start(seq, sl):
        for j in range(n_pages):
            page_copy(seq, j, sl).start()

    @pl.when(n == 0)
    def _():
        start(n, slot)

    @pl.when(n + 1 < pl.num_programs(0))
    def _():
        start(n + 1, 1 - slot)

    for j in range(n_pages):
        page_copy(n, j, slot).wait()

    def feature_rows(f):
        return buf[slot, pl.ds(f, n_pages, stride=KV_W), :]

    for c in range(2):
        acc = None
        for dp in range(D_PAIRS):
            rows = []
            for g in range(N_KV):
                f0 = (c * N_KV + g) * HEAD_DIM + 2 * dp
                rows.append(jnp.concatenate(
                    [feature_rows(f0) + pe_ref[c, 2 * dp:2 * dp + 1, :],
                     feature_rows(f0 + 1) + pe_ref[c, 2 * dp + 1:2 * dp + 2, :]], axis=-1))
            x = jnp.concatenate(rows, axis=0).astype(BF16)
            d = _dot(x, w1_ref[c, dp])
            acc = d if acc is None else acc + d
        hid = (acc * jax.nn.sigmoid(acc)).astype(BF16)
        kc_ref[c] = _dot(hid, w2_ref[c])


def _compress_decode(page_table, pool_t, pe_t, w1_t, w2_t):
    nseq, n_pages = page_table.shape
    const3 = lambda a, pt: (0, 0, 0)
    grid_spec = pltpu.PrefetchScalarGridSpec(
        num_scalar_prefetch=1,
        grid=(nseq,),
        in_specs=[
            pl.BlockSpec(memory_space=pl.ANY),
            pl.BlockSpec((2, HEAD_DIM, PAGE_SIZE), const3),
            pl.BlockSpec((2, D_PAIRS, 2 * PAGE_SIZE, PAGE_SIZE), lambda a, pt: (0, 0, 0, 0)),
            pl.BlockSpec((2, PAGE_SIZE, PAGE_SIZE), const3),
        ],
        out_specs=pl.BlockSpec((None, 2, N_KV * n_pages, PAGE_SIZE), lambda a, pt: (a, 0, 0, 0)),
        scratch_shapes=[pltpu.VMEM((2, n_pages * KV_W, PAGE_SIZE), F32),
                        pltpu.SemaphoreType.DMA((2,))],
    )
    return pl.pallas_call(
        functools.partial(_compress_decode_body, n_pages=n_pages),
        grid_spec=grid_spec,
        out_shape=jax.ShapeDtypeStruct((nseq, 2, N_KV * n_pages, PAGE_SIZE), F32),
        compiler_params=_cparams(("arbitrary",)),
        name="compress_decode",
    )(page_table.reshape(-1), pool_t, pe_t, w1_t, w2_t)


ATT_TQ = 256
ATT_TK = 256


def _softmax_update(s, valid, v, state):
    m, l, acc = state
    s = jnp.where(valid, s, NEG)
    m_new = jnp.maximum(m, jnp.max(s, axis=-1, keepdims=True))
    a = jnp.exp(m - m_new)
    e = jnp.exp(s - m_new)
    l = a * l + jnp.sum(e, axis=-1, keepdims=True)
    acc = a * acc + _dot(e.astype(BF16), v)
    return m_new, l, acc


def _attn_prompt_body(q_ref, kc_ref, kvs_ref, kvw_ref, gates_ref, o_ref):
    tq, tk = ATT_TQ, ATT_TK
    qi = pl.program_id(1)
    nb = kc_ref.shape[0]
    t0 = qi * tq
    tpos = t0 + lax.broadcasted_iota(jnp.int32, (tq, 1), 0)
    jblk = lax.broadcasted_iota(jnp.int32, (1, nb), 1)
    vis = ((jblk + 1) * BLK - 1) <= tpos
    cur = tpos >> BLK_SHIFT
    kcb = kc_ref[...].astype(BF16)
    gates = gates_ref[...]

    def init_state():
        return (jnp.full((tq, 1), NEG, F32), jnp.zeros((tq, 1), F32), jnp.zeros((tq, HEAD_DIM), F32))

    for g in range(N_KV):
        kg = kcb[:, g * HEAD_DIM:(g + 1) * HEAD_DIM]
        vg = kcb[:, (N_KV + g) * HEAD_DIM:(N_KV + g + 1) * HEAD_DIM]
        heads = [g * GQA_R + r for r in range(GQA_R)]
        qh = [q_ref[:, h * HEAD_DIM:(h + 1) * HEAD_DIM] for h in heads]

        imp = jnp.zeros((tq, nb), F32)
        o_cmp = []
        for r in range(GQA_R):
            s = jnp.where(vis, _dot_nt(qh[r], kg), NEG)
            m = jnp.max(s, axis=-1, keepdims=True)
            e = jnp.where(vis, jnp.exp(s - m), 0.0)
            l = jnp.sum(e, axis=-1, keepdims=True)
            p = e / jnp.where(l > 0.0, l, 1.0)
            imp = imp + p
            o_cmp.append(_dot(p.astype(BF16), vg))

        score = jnp.where(jblk > cur, NEG, imp)
        score = jnp.where((jblk == 0) | (jblk == cur) | (jblk == cur - 1), FORCE, score)
        rank = jnp.zeros((tq, nb), jnp.int32)
        for j2 in range(nb):
            col = score[:, j2:j2 + 1]
            beats = (col > score) | ((col == score) & (jblk > j2))
            rank = rank + beats.astype(jnp.int32)
        selb = jnp.where(rank < N_SEL, 1.0, 0.0).astype(BF16)

        kcol = slice(g * HEAD_DIM, (g + 1) * HEAD_DIM)
        vcol = slice((N_KV + g) * HEAD_DIM, (N_KV + g + 1) * HEAD_DIM)

        def sel_step(kt, states):
            ks = pl.multiple_of(kt * tk, tk)
            kpos = ks + lax.broadcasted_iota(jnp.int32, (1, tk), 1)
            expand = ((kpos >> BLK_SHIFT) == lax.broadcasted_iota(jnp.int32, (nb, 1), 0))
            expand = jnp.where(expand, 1.0, 0.0).astype(BF16)
            valid = (_dot(selb, expand) > 0.5) & (kpos <= tpos)
            kblk = kvs_ref[pl.ds(ks, tk), kcol].astype(BF16)
            vblk = kvs_ref[pl.ds(ks, tk), vcol].astype(BF16)
            return tuple(_softmax_update(_dot_nt(qh[r], kblk), valid, vblk, states[r])
                         for r in range(GQA_R))

        sel_states = lax.fori_loop(0, qi + 1, sel_step, tuple(init_state() for _ in range(GQA_R)))

        win_states = [init_state() for _ in range(GQA_R)]
        for back in range(WINDOW // tk + 1):
            kt = qi - back
            ks = pl.multiple_of(jnp.maximum(kt, 0) * tk, tk)
            kpos = kt * tk + lax.broadcasted_iota(jnp.int32, (1, tk), 1)
            dist = tpos - kpos
            valid = (dist >= 0) & (dist < WINDOW) & (kpos >= 0)
            kblk = kvw_ref[pl.ds(ks, tk), kcol].astype(BF16)
            vblk = kvw_ref[pl.ds(ks, tk), vcol].astype(BF16)
            win_states = [_softmax_update(_dot_nt(qh[r], kblk), valid, vblk, win_states[r])
                          for r in range(GQA_R)]

        for r, h in enumerate(heads):
            o_sel = sel_states[r][2] / sel_states[r][1]
            o_win = win_states[r][2] / win_states[r][1]
            o = (gates[:, h:h + 1] * o_cmp[r] + gates[:, N_HEADS + h:N_HEADS + h + 1] * o_sel
                 + gates[:, 2 * N_HEADS + h:2 * N_HEADS + h + 1] * o_win)
            o_ref[:, h * HEAD_DIM:(h + 1) * HEAD_DIM] = o


def _attn_prompt(q, kc, kvs, kvw, gates):
    n, t, _ = q.shape
    nb = kc.shape[1]
    row = lambda a, i: (a, i, 0)
    seq = lambda a, i: (a, 0, 0)
    return pl.pallas_call(
        _attn_prompt_body,
        grid=(n, t // ATT_TQ),
        in_specs=[
            pl.BlockSpec((None, ATT_TQ, D_ATTN), row),
            pl.BlockSpec((None, nb, KV_W), seq),
            pl.BlockSpec((None, t, KV_W), seq),
            pl.BlockSpec((None, t, KV_W), seq),
            pl.BlockSpec((None, ATT_TQ, LANES), row),
        ],
        out_specs=pl.BlockSpec((None, ATT_TQ, D_ATTN), row),
        out_shape=jax.ShapeDtypeStruct((n, t, D_ATTN), F32),
        compiler_params=_cparams(("arbitrary", "arbitrary")),
        name="attn_prompt",
    )(q, kc, kvs, kvw, gates)


Q_PAD = 16
DEC_SEQ_PER_STEP = 8


def _dec_cmp_body(q_ref, kc_ref, ocmp_ref, idx_ref, imp_buf, *, past, s_new):
    nseq, ncombo = q_ref.shape[0], q_ref.shape[1]
    n_pages = kc_ref.shape[2] // N_KV
    nb = n_pages * BPP
    width = imp_buf.shape[1]
    imp_buf[...] = jnp.zeros(imp_buf.shape, F32)
    lane = lax.broadcasted_iota(jnp.int32, (1, width), 1)
    jc = jnp.where(lane < n_pages, BPP * lane, BPP * (lane - n_pages) + 1)
    jc = jnp.where(lane == nb, nb, jc)
    jblk = jc[:, 0:nb]
    for a in range(nseq):
        for c in range(ncombo):
            g, t = c // s_new, c % s_new
            kpg = kc_ref[a, 0, g * n_pages:(g + 1) * n_pages, :].astype(BF16)
            vpg = kc_ref[a, 1, g * n_pages:(g + 1) * n_pages, :].astype(BF16)
            qc = q_ref[a, c]
            vis = ((jblk + 1) * BLK - 1) <= (past + t)
            s = jnp.concatenate([_dot_nt(qc, kpg[:, h * HEAD_DIM:(h + 1) * HEAD_DIM])
                                 for h in range(BPP)], axis=-1)
            s = jnp.where(vis, s, NEG)
            m = jnp.max(s, axis=-1, keepdims=True)
            e = jnp.where(vis, jnp.exp(s - m), 0.0)
            l = jnp.sum(e, axis=-1, keepdims=True)
            p = e / jnp.where(l > 0.0, l, 1.0)
            pb = p.astype(BF16)
            ocmp_ref[a, c] = sum(_dot(pb[:, h * n_pages:(h + 1) * n_pages],
                                      vpg[:, h * HEAD_DIM:(h + 1) * HEAD_DIM]) for h in range(BPP))
            row = a * ncombo + c
            imp_buf[row:row + 1, 0:nb] = jnp.sum(p[0:GQA_R, :], axis=0, keepdims=True)

    rows = nseq * ncombo
    imp = imp_buf[...]
    t_row = lax.broadcasted_iota(jnp.int32, (rows, 1), 0) & (s_new - 1)
    cur = (past + t_row) >> BLK_SHIFT
    score = jnp.where(jc > cur, NEG, imp)
    score = jnp.where((jc == 0) | (jc == cur) | (jc == cur - 1), FORCE, score)
    score = jnp.where(lane <= nb, score, -jnp.inf)
    jcf = jc.astype(F32)
    out_lane = lax.broadcasted_iota(jnp.int32, (1, LANES), 1)
    idx = jnp.zeros((rows, LANES), jnp.int32)
    for k in range(N_SEL):
        m = jnp.max(score, axis=-1, keepdims=True)
        first = jnp.min(jnp.where(score == m, jcf, 1e9), axis=-1, keepdims=True)
        idx = jnp.where(out_lane == k, first.astype(jnp.int32), idx)
        score = jnp.where(jcf == first, -jnp.inf, score)
    idx_ref[...] = idx


def _dec_cmp(q_pad, kc, past, s_new):
    nseq, ncombo = q_pad.shape[0], q_pad.shape[1]
    rows_kc = kc.shape[2]
    nb = rows_kc // N_KV * BPP
    ns = DEC_SEQ_PER_STEP
    width = (nb + 1 + LANES - 1) // LANES * LANES
    return pl.pallas_call(
        functools.partial(_dec_cmp_body, past=past, s_new=s_new),
        grid=(nseq // ns,),
        in_specs=[
            pl.BlockSpec((ns, ncombo, Q_PAD, HEAD_DIM), lambda i: (i, 0, 0, 0)),
            pl.BlockSpec((ns, 2, rows_kc, PAGE_SIZE), lambda i: (i, 0, 0, 0)),
        ],
        out_specs=[
            pl.BlockSpec((ns, ncombo, Q_PAD, HEAD_DIM), lambda i: (i, 0, 0, 0)),
            pl.BlockSpec((ns * ncombo, LANES), lambda i: (i, 0)),
        ],
        out_shape=[jax.ShapeDtypeStruct((nseq, ncombo, Q_PAD, HEAD_DIM), F32),
                   jax.ShapeDtypeStruct((nseq * ncombo, LANES), jnp.int32)],
        scratch_shapes=[pltpu.VMEM((ns * ncombo, width), F32)],
        compiler_params=_cparams(("arbitrary",)),
        name="dec_cmp_select",
    )(q_pad, kc)


def _dec_sel_win_body(idx_ref, pt_ref, q_ref, pool_ref, newblk_ref, win_ref, kvw_new_ref,
                      osel_ref, owin_ref, kbuf, vbuf, sem, *, past, s_new, n_pages):
    ncombo = q_ref.shape[0]
    n_past_blk = past // BLK
    n = pl.program_id(0)
    nseq = pl.num_programs(0)
    slot = n % 2

    def feature_rows(c, part):
        g = c // s_new
        return pl.ds((part * N_KV + g) * HEAD_DIM, HEAD_DIM)

    def block_copy(src_page, c, k, part, sl):
        dst = (kbuf, vbuf)[part].at[sl, c, :, pl.ds(k * PAGE_SIZE, PAGE_SIZE)]
        return pltpu.make_async_copy(src_page.at[feature_rows(c, part)], dst, sem.at[sl])

    def start(seq, sl):
        for c in range(ncombo):
            for k in range(N_SEL):
                j = idx_ref[(seq * ncombo + c) * N_SEL + k]

                @pl.when(j < n_past_blk)
                def _():
                    page = pt_ref[seq * n_pages + jnp.clip(j, 0, n_past_blk - 1) // BPP]
                    for part in range(2):
                        block_copy(pool_ref.at[page], c, k, part, sl).start()

                @pl.when(j >= n_past_blk)
                def _():
                    for part in range(2):
                        block_copy(newblk_ref.at[seq], c, k, part, sl).start()

    @pl.when(n == 0)
    def _():
        start(n, slot)

    @pl.when(n + 1 < nseq)
    def _():
        start(n + 1, 1 - slot)

    for c in range(ncombo):
        for k in range(N_SEL):
            for part in range(2):
                block_copy(newblk_ref.at[n], c, k, part, slot).wait()

    nkeys = N_SEL * PAGE_SIZE
    lane = lax.broadcasted_iota(jnp.int32, (1, nkeys), 1)
    row_in_page = lane & (PAGE_SIZE - 1)
    for c in range(ncombo):
        t = c % s_new
        jv = jnp.zeros((1, nkeys), jnp.int32)
        for k in range(N_SEL):
            j = idx_ref[(n * ncombo + c) * N_SEL + k]
            jv = jnp.where((lane >> PAGE_SHIFT) == k, j, jv)
        kpos = jv * BLK + (row_in_page & (BLK - 1))
        half = jnp.where(jv >= n_past_blk, 0, jv & (BPP - 1))
        valid = ((row_in_page >> BLK_SHIFT) == half) & (kpos <= past + t)
        s = jnp.where(valid, _dot(q_ref[c], kbuf[slot, c].astype(BF16)), NEG)
        m = jnp.max(s, axis=-1, keepdims=True)
        e = jnp.exp(s - m)
        l = jnp.sum(e, axis=-1, keepdims=True)
        osel_ref[c] = _dot_nt(e.astype(BF16), vbuf[slot, c].astype(BF16)) / l

    nwin = win_ref.shape[1]
    npad = kvw_new_ref.shape[1]
    wpos = past - nwin + lax.broadcasted_iota(jnp.int32, (1, nwin), 1)
    npos = past + lax.broadcasted_iota(jnp.int32, (1, npad), 1)
    new_real = lax.broadcasted_iota(jnp.int32, (1, npad), 1) < s_new
    for c in range(ncombo):
        t = c % s_new
        krows, vrows = feature_rows(c, 0), feature_rows(c, 1)
        qc = q_ref[c]
        dw = (past + t) - wpos
        dn = (past + t) - npos
        valid_w = (dw >= 0) & (dw < WINDOW) & (wpos >= 0)
        valid_n = (dn >= 0) & (dn < WINDOW) & new_real
        sw = jnp.where(valid_w, _dot(qc, win_ref[krows, :].astype(BF16)), NEG)
        sn = jnp.where(valid_n, _dot(qc, kvw_new_ref[krows, :].astype(BF16)), NEG)
        m = jnp.maximum(jnp.max(sw, axis=-1, keepdims=True), jnp.max(sn, axis=-1, keepdims=True))
        ew = jnp.exp(sw - m)
        en = jnp.exp(sn - m)
        l = jnp.sum(ew, axis=-1, keepdims=True) + jnp.sum(en, axis=-1, keepdims=True)
        o = (_dot_nt(ew.astype(BF16), win_ref[vrows, :].astype(BF16))
             + _dot_nt(en.astype(BF16), kvw_new_ref[vrows, :].astype(BF16)))
        owin_ref[c] = o / l


def _dec_sel_win(idx_flat, pt_flat, q_pad, pool_t, newblk_t, win_t, kvw_new_t, past, s_new, n_pages):
    nseq, ncombo = q_pad.shape[0], q_pad.shape[1]
    nwin = win_t.shape[2]
    per_seq = lambda a, idx, pt: (a, 0, 0, 0)
    per_seq3 = lambda a, idx, pt: (a, 0, 0)
    grid_spec = pltpu.PrefetchScalarGridSpec(
        num_scalar_prefetch=2,
        grid=(nseq,),
        in_specs=[
            pl.BlockSpec((None, ncombo, Q_PAD, HEAD_DIM), per_seq),
            pl.BlockSpec(memory_space=pl.ANY),
            pl.BlockSpec(memory_space=pl.ANY),
            pl.BlockSpec((None, KV_W, nwin), per_seq3),
            pl.BlockSpec((None, KV_W, PAGE_SIZE), per_seq3),
        ],
        out_specs=[
            pl.BlockSpec((None, ncombo, Q_PAD, HEAD_DIM), per_seq),
            pl.BlockSpec((None, ncombo, Q_PAD, HEAD_DIM), per_seq),
        ],
        scratch_shapes=[pltpu.VMEM((2, ncombo, HEAD_DIM, N_SEL * PAGE_SIZE), F32),
                        pltpu.VMEM((2, ncombo, HEAD_DIM, N_SEL * PAGE_SIZE), F32),
                        pltpu.SemaphoreType.DMA((2,))],
    )
    shape = jax.ShapeDtypeStruct((nseq, ncombo, Q_PAD, HEAD_DIM), F32)
    return pl.pallas_call(
        functools.partial(_dec_sel_win_body, past=past, s_new=s_new, n_pages=n_pages),
        grid_spec=grid_spec,
        out_shape=[shape, shape],
        compiler_params=_cparams(("arbitrary",)),
        name="dec_sel_win",
    )(idx_flat, pt_flat, q_pad, pool_t, newblk_t, win_t, kvw_new_t)


def _out_ln_body(*refs, gated):
    if gated:
        ocmp_ref, osel_ref, owin_ref, gates_ref, yconv_ref, h_ref, wo_ref, g_ref, b_ref, y_ref = refs
        gates = gates_ref[...]
        pieces = []
        for h in range(N_HEADS):
            col = slice(h * HEAD_DIM, (h + 1) * HEAD_DIM)
            pieces.append(gates[:, h:h + 1] * ocmp_ref[:, col]
                          + gates[:, N_HEADS + h:N_HEADS + h + 1] * osel_ref[:, col]
                          + gates[:, 2 * N_HEADS + h:2 * N_HEADS + h + 1] * owin_ref[:, col])
        o = jnp.concatenate(pieces, axis=-1)
    else:
        o_ref, yconv_ref, h_ref, wo_ref, g_ref, b_ref, y_ref = refs
        o = o_ref[...]
    mix = (_dot(o.astype(BF16), wo_ref[0:D_ATTN, :])
           + _dot(yconv_ref[...].astype(BF16), wo_ref[D_ATTN:D_MODEL, :]))
    y_ref[...] = _layer_norm(ALPHA * h_ref[...] + mix, g_ref[...], b_ref[...])


def _out_ln(branches, yconv, h2d, w_out, g, b, tm):
    rows = h2d.shape[0]
    gated = len(branches) > 1
    row = lambda i: (i, 0)
    const = lambda i: (0, 0)
    widths = [D_ATTN, D_ATTN, D_ATTN, LANES] if gated else [D_ATTN]
    return pl.pallas_call(
        functools.partial(_out_ln_body, gated=gated),
        grid=(rows // tm,),
        in_specs=[pl.BlockSpec((tm, w), row) for w in widths] + [
            pl.BlockSpec((tm, D_CONV), row),
            pl.BlockSpec((tm, D_MODEL), row),
            pl.BlockSpec((D_MODEL, D_MODEL), const),
            pl.BlockSpec((1, D_MODEL), const),
            pl.BlockSpec((1, D_MODEL), const),
        ],
        out_specs=pl.BlockSpec((tm, D_MODEL), row),
        out_shape=jax.ShapeDtypeStruct((rows, D_MODEL), F32),
        compiler_params=_cparams(("arbitrary",)),
        name="out_ln",
    )(*branches, yconv, h2d, w_out, g, b)


def _block_diag_groups(w):
    z = jnp.zeros_like(w)
    return jnp.concatenate([jnp.concatenate([w, z], axis=-1),
                            jnp.concatenate([z, w], axis=-1)], axis=-2).astype(BF16)


def _to_combo_layout(x, nseq, s_new):
    x = x.reshape(nseq, s_new, N_KV, GQA_R, HEAD_DIM).transpose(0, 2, 1, 3, 4)
    x = x.reshape(nseq, N_KV * s_new, GQA_R, HEAD_DIM)
    return jnp.pad(x, ((0, 0), (0, 0), (0, Q_PAD - GQA_R), (0, 0)))


def _from_combo_layout(x, nseq, s_new):
    x = x[:, :, :GQA_R, :].reshape(nseq, N_KV, s_new, GQA_R, HEAD_DIM).transpose(0, 2, 1, 3, 4)
    return x.reshape(nseq * s_new, D_ATTN)


def kernel(x_prompt, x_sample, cache_cmp_kv, cache_sel_kv, state_win_kv, state_conv, page_table,
           ffa_gate, ffa_up, ffa_down, ln1_g, ln1_b, w_in, b_in, w_conv, cmp_w1, cmp_w2, cmp_pe,
           w_out, ln2_g, ln2_b, ffb_gate, ffb_up, ffb_down, ln3_g, ln3_b):
    depth = ffa_gate.shape[0]
    assert depth == 1
    l = 0
    n, t, d = x_prompt.shape
    nseq, s_new, _ = x_sample.shape
    n_pages = page_table.shape[1]
    past = n_pages * PAGE_SIZE
    n_phys = cache_cmp_kv.shape[1]
    rows_p, rows_s = n * t, nseq * s_new
    kvshape = (2, N_KV, HEAD_DIM)

    bf = lambda w: w.astype(BF16)
    vec = lambda v: v.reshape(1, -1)
    w_main = bf(w_in[l][:, :D_MAIN])
    b_main = vec(b_in[l][:D_MAIN])
    w_gate = bf(jnp.pad(w_in[l][:, D_MAIN:], ((0, 0), (0, LANES - N_GATE))))
    b_gate = vec(jnp.pad(b_in[l][D_MAIN:], (0, LANES - N_GATE)))
    pe2 = jnp.concatenate([cmp_pe[l]] * N_KV, axis=-1)
    w1bd = _block_diag_groups(cmp_w1[l].transpose(1, 0, 2, 3))
    w2bd = _block_diag_groups(cmp_w2[l])
    w_out_b = bf(w_out[l])
    ffa = (bf(ffa_gate[l]), bf(ffa_up[l]), bf(ffa_down[l]), vec(ln1_g[l]), vec(ln1_b[l]))
    ffb = (bf(ffb_gate[l]), bf(ffb_up[l]), bf(ffb_down[l]), vec(ln3_g[l]), vec(ln3_b[l]))

    hp = _ffn_ln(x_prompt.reshape(rows_p, d), *ffa, tm=512)
    q, kvc, kvs, kvw, gates, yconv, cstate = _proj_prompt(
        hp.reshape(n, t, d), w_main, b_main, w_gate, b_gate, w_conv[l], tm=512)
    kc = _compress_prompt(kvc, pe2, w1bd, w2bd, ns=4)
    o = _attn_prompt(q, kc, kvs, kvw, gates)
    yp = _out_ln([o.reshape(rows_p, D_ATTN)], yconv.reshape(rows_p, D_CONV), hp, w_out_b,
                 vec(ln2_g[l]), vec(ln2_b[l]), tm=512)
    yp = _ffn_ln(yp, *ffb, tm=512).reshape(n, t, d)
    p_cmp = kvc.reshape(1, n, t, *kvshape)
    p_sel = kvs.reshape(1, n, t, *kvshape)
    p_win = kvw[:, t - min(WINDOW, t):].reshape(1, n, min(WINDOW, t), *kvshape)
    p_conv = cstate[:, 8 - (CONV_W - 1):].reshape(1, n, CONV_W - 1, D_CONV)

    hs = _ffn_ln(x_sample.reshape(rows_s, d), *ffa, tm=rows_s)
    qs, kv3, gates_s, cb_s, u_s = _proj_sample(hs, w_main, b_main, w_gate, b_gate)
    kvc_s, kvs_s, kvw_s = kv3[:, :KV_W], kv3[:, KV_W:2 * KV_W], kv3[:, 2 * KV_W:]
    u3 = u_s.reshape(nseq, s_new, D_CONV)
    yconv_s = _conv_sample(cb_s.reshape(nseq, s_new, D_CONV), u3, state_conv[l], w_conv[l])
    s_conv = jnp.concatenate([state_conv[l], u3], axis=1)[:, -(CONV_W - 1):]

    def feature_major(x):
        return jnp.moveaxis(x, -4, -1).reshape(*x.shape[:-4], KV_W, x.shape[-4])

    def new_rows_t(x2d):
        xt = x2d.reshape(nseq, s_new, KV_W).transpose(0, 2, 1)
        return jnp.pad(xt, ((0, 0), (0, 0), (0, PAGE_SIZE - s_new)))

    pe_t = jnp.concatenate([cmp_pe[l].transpose(1, 2, 0)] * BPP, axis=-1)
    w1_t = _block_diag_groups(cmp_w1[l].transpose(0, 2, 1, 3)).reshape(
        2, D_PAIRS, 2 * PAGE_SIZE, PAGE_SIZE)
    kc_s = _compress_decode(page_table, feature_major(cache_cmp_kv[l]), pe_t, w1_t, w2bd)
    q_pad = _to_combo_layout(qs, nseq, s_new)
    ocmp_c, idx_pad = _dec_cmp(q_pad, kc_s, past, s_new)
    idx_flat = idx_pad[:, :N_SEL].reshape(-1)
    win_t = feature_major(state_win_kv[l])
    kvw_new_t = new_rows_t(kvw_s)
    osel_c, owin_c = _dec_sel_win(
        idx_flat, page_table.reshape(-1), q_pad, feature_major(cache_sel_kv[l]), new_rows_t(kvs_s),
        win_t, kvw_new_t, past, s_new, n_pages)
    branches = [_from_combo_layout(x, nseq, s_new) for x in (ocmp_c, osel_c, owin_c)]
    ys = _out_ln(branches + [gates_s], yconv_s.reshape(rows_s, D_CONV), hs, w_out_b,
                 vec(ln2_g[l]), vec(ln2_b[l]), tm=rows_s)
    ys = _ffn_ln(ys, *ffb, tm=rows_s).reshape(nseq, s_new, d)
    n_keep = min(WINDOW, past + s_new)
    ctx_t = jnp.concatenate([win_t, kvw_new_t[:, :, :s_new]], axis=-1)[:, :, -n_keep:]
    s_cmp = kvc_s.reshape(1, nseq, s_new, *kvshape)
    s_sel = kvs_s.reshape(1, nseq, s_new, *kvshape)
    s_win = jnp.moveaxis(ctx_t.reshape(nseq, *kvshape, n_keep), -1, 1)[None]
    return (yp, ys, p_cmp, p_sel, p_win, p_conv, s_cmp, s_sel, s_win,
            s_conv.reshape(1, nseq, CONV_W - 1, D_CONV))
```

```python
import functools

import jax
import jax.numpy as jnp
from jax import lax
from jax.experimental import pallas as pl
from jax.experimental.pallas import tpu as pltpu

F32 = jnp.float32
BF16 = jnp.bfloat16

D_MODEL = 1024
D_CONV = 512
D_ATTN = 512
HEAD_DIM = 64
N_HEADS = 8
N_KV = 2
GQA_R = N_HEADS // N_KV
N_BR = 3
CONV_W = 3
BLK = 64
BLK_SHIFT = 6
N_SEL = 16
WINDOW = 512
PAGE_SIZE = 128
PAGE_SHIFT = 7
D_FF = 2816
KV_W = 2 * N_KV * HEAD_DIM
D_MAIN = 3 * D_CONV + D_ATTN + 3 * KV_W
N_GATE = N_BR * N_HEADS
ALPHA = 2.0 ** 0.25
SCALE = HEAD_DIM ** -0.5
LN_EPS = 1e-5
NEG = -1e30
FORCE = 1e4

LANES = 128
VMEM_LIMIT_BYTES = 56 * 1024 * 1024


def _cparams(sem):
    return pltpu.CompilerParams(dimension_semantics=sem, vmem_limit_bytes=VMEM_LIMIT_BYTES)


def _layer_norm(y, g, b):
    mu = jnp.mean(y, axis=-1, keepdims=True)
    yc = y - mu
    var = jnp.mean(yc * yc, axis=-1, keepdims=True)
    return yc * lax.rsqrt(var + LN_EPS) * g + b


def _dot(a, b):
    return jnp.dot(a, b, preferred_element_type=F32)


def _dot_nt(a, b):
    return lax.dot_general(a, b, (((1,), (1,)), ((), ())), preferred_element_type=F32)


FFN_CHUNK = 256


def _ffn_ln_body(x_ref, wg_ref, wu_ref, wd_ref, g_ref, b_ref, o_ref):
    x = x_ref[...]
    xb = x.astype(BF16)
    acc = jnp.zeros(x.shape, F32)
    for c in range(D_FF // FFN_CHUNK):
        sl = slice(c * FFN_CHUNK, (c + 1) * FFN_CHUNK)
        gg = _dot(xb, wg_ref[:, sl])
        uu = _dot(xb, wu_ref[:, sl])
        hh = (gg * jax.nn.sigmoid(gg) * uu).astype(BF16)
        acc = acc + _dot(hh, wd_ref[sl, :])
    o_ref[...] = _layer_norm(ALPHA * x + 0.5 * acc, g_ref[...], b_ref[...])


def _ffn_ln(x2d, wg, wu, wd, g, b, tm):
    rows = x2d.shape[0]
    const = lambda i: (0, 0)
    return pl.pallas_call(
        _ffn_ln_body,
        grid=(rows // tm,),
        in_specs=[
            pl.BlockSpec((tm, D_MODEL), lambda i: (i, 0)),
            pl.BlockSpec((D_MODEL, D_FF), const),
            pl.BlockSpec((D_MODEL, D_FF), const),
            pl.BlockSpec((D_FF, D_MODEL), const),
            pl.BlockSpec((1, D_MODEL), const),
            pl.BlockSpec((1, D_MODEL), const),
        ],
        out_specs=pl.BlockSpec((tm, D_MODEL), lambda i: (i, 0)),
        out_shape=jax.ShapeDtypeStruct((rows, D_MODEL), F32),
        compiler_params=_cparams(("arbitrary",)),
        name="ffn_ln",
    )(x2d, wg, wu, wd, g, b)


_C_CB, _C_CC, _C_CH, _C_Q = 0, D_CONV, 2 * D_CONV, 3 * D_CONV
_C_KVC = 3 * D_CONV + D_ATTN
_C_KVS = _C_KVC + KV_W
_C_KVW = _C_KVS + KV_W


def _proj_piece(hb, w_ref, b_ref, lo, width):
    return _dot(hb, w_ref[:, lo:lo + width]) + b_ref[:, lo:lo + width]


_P_KS = 3 * D_CONV + D_ATTN
_P_KW = _P_KS + KV_W
P_ROW_COLS = _P_KW + KV_W
GATE_ROWS = 32
P_T_ROWS = 3 * KV_W + GATE_ROWS


def _proj_prompt_body(h_ref, w_ref, b_ref, wt_ref, bt_ref, wc_ref,
                      q_ref, ks_ref, kw_ref, kvct_ref, kvst_ref, kvwt_ref, gates_ref,
                      yconv_ref, cstate_ref, ubuf):
    i = pl.program_id(1)
    tm = h_ref.shape[0]
    hb = h_ref[...].astype(BF16)
    q_ref[...] = (_proj_piece(hb, w_ref, b_ref, _C_Q, D_ATTN) * SCALE).astype(BF16)
    ks_ref[...] = _proj_piece(hb, w_ref, b_ref, _P_KS, KV_W).astype(BF16)
    kw_ref[...] = _proj_piece(hb, w_ref, b_ref, _P_KW, KV_W).astype(BF16)
    for k, ref in enumerate((kvct_ref, kvst_ref, kvwt_ref)):
        rows = slice(k * KV_W, (k + 1) * KV_W)
        ref[...] = _dot_nt(wt_ref[rows, :], hb) + bt_ref[rows, :]
    rows = slice(3 * KV_W, P_T_ROWS)
    gates_ref[...] = jax.nn.sigmoid(_dot_nt(wt_ref[rows, :], hb) + bt_ref[rows, :])
    cb = _proj_piece(hb, w_ref, b_ref, _C_CB, D_CONV)
    u = _proj_piece(hb, w_ref, b_ref, _C_CC, D_CONV) * _proj_piece(hb, w_ref, b_ref, _C_CH, D_CONV)

    @pl.when(i == 0)
    def _():
        ubuf[0:8, :] = jnp.zeros((8, D_CONV), F32)

    ubuf[8:8 + tm, :] = u
    u1 = ubuf[7:7 + tm, :]
    u2 = ubuf[6:6 + tm, :]
    z = u2 * wc_ref[0:1, :] + u1 * wc_ref[1:2, :] + u * wc_ref[2:3, :]
    yconv_ref[...] = cb * z
    tail = u[tm - 8:tm, :]
    ubuf[0:8, :] = tail
    cstate_ref[...] = tail


def _proj_prompt(h, w_rows, b_rows, w_t, b_t, w_conv, tm):
    n, t, _ = h.shape
    const = lambda a, i: (0, 0)
    row = lambda a, i: (a, i, 0)
    col = lambda a, i: (a, 0, i)

    def out(width, dtype=F32):
        return jax.ShapeDtypeStruct((n, t, width), dtype)

    def out_t(rows):
        return jax.ShapeDtypeStruct((n, rows, t), F32)

    return pl.pallas_call(
        _proj_prompt_body,
        grid=(n, t // tm),
        in_specs=[
            pl.BlockSpec((None, tm, D_MODEL), row),
            pl.BlockSpec((D_MODEL, P_ROW_COLS), const),
            pl.BlockSpec((1, P_ROW_COLS), const),
            pl.BlockSpec((P_T_ROWS, D_MODEL), const),
            pl.BlockSpec((P_T_ROWS, 1), const),
            pl.BlockSpec((CONV_W, D_CONV), const),
        ],
        out_specs=[
            pl.BlockSpec((None, tm, D_ATTN), row),
            pl.BlockSpec((None, tm, KV_W), row),
            pl.BlockSpec((None, tm, KV_W), row),
            pl.BlockSpec((None, KV_W, tm), col),
            pl.BlockSpec((None, KV_W, tm), col),
            pl.BlockSpec((None, KV_W, tm), col),
            pl.BlockSpec((None, GATE_ROWS, tm), col),
            pl.BlockSpec((None, tm, D_CONV), row),
            pl.BlockSpec((None, 8, D_CONV), lambda a, i: (a, 0, 0)),
        ],
        out_shape=[out(D_ATTN, BF16), out(KV_W, BF16), out(KV_W, BF16), out_t(KV_W), out_t(KV_W),
                   out_t(KV_W), out_t(GATE_ROWS), out(D_CONV), jax.ShapeDtypeStruct((n, 8, D_CONV), F32)],
        scratch_shapes=[pltpu.VMEM((tm + 8, D_CONV), F32)],
        compiler_params=_cparams(("arbitrary", "arbitrary")),
        name="proj_prompt",
    )(h, w_rows, b_rows, w_t, b_t, w_conv)


def _proj_sample_body(h_ref, w_ref, b_ref, wgt_ref, bgt_ref, q_ref, kv_ref, gates_ref, cb_ref, u_ref):
    hb = h_ref[...].astype(BF16)
    q_ref[...] = (_proj_piece(hb, w_ref, b_ref, _C_Q, D_ATTN) * SCALE).astype(BF16)
    kv_ref[...] = _proj_piece(hb, w_ref, b_ref, _C_KVC, 3 * KV_W)
    gates_ref[...] = jax.nn.sigmoid(_dot(hb, wgt_ref[...]) + bgt_ref[...])
    cb_ref[...] = _proj_piece(hb, w_ref, b_ref, _C_CB, D_CONV)
    u_ref[...] = _proj_piece(hb, w_ref, b_ref, _C_CC, D_CONV) * _proj_piece(hb, w_ref, b_ref, _C_CH, D_CONV)


def _proj_sample(h2d, w_main, b_main, w_gate, b_gate):
    rows = h2d.shape[0]

    def out(width, dtype=F32):
        return jax.ShapeDtypeStruct((rows, width), dtype)

    return pl.pallas_call(
        _proj_sample_body,
        out_shape=[out(D_ATTN, BF16), out(3 * KV_W), out(LANES), out(D_CONV), out(D_CONV)],
        compiler_params=pltpu.CompilerParams(vmem_limit_bytes=VMEM_LIMIT_BYTES),
        name="proj_sample",
    )(h2d, w_main, b_main, w_gate, b_gate)


def _conv_sample_body(cb_ref, u_ref, prev_ref, wc_ref, y_ref):
    s = u_ref.shape[1]
    ext = [prev_ref[:, 0, :], prev_ref[:, 1, :]] + [u_ref[:, t, :] for t in range(s)]
    for t in range(s):
        z = ext[t] * wc_ref[0:1, :] + ext[t + 1] * wc_ref[1:2, :] + ext[t + 2] * wc_ref[2:3, :]
        y_ref[:, t, :] = cb_ref[:, t, :] * z


def _conv_sample(cb, u, prev, w_conv):
    return pl.pallas_call(
        _conv_sample_body,
        out_shape=jax.ShapeDtypeStruct(u.shape, F32),
        name="conv_sample",
    )(cb, u, prev, w_conv)


BPP = PAGE_SIZE // BLK
D_PAIRS = HEAD_DIM // 2


def _compress_body(pt_ref, src_ref, pe_ref, w1_ref, w2_ref, kc_ref, buf, sem, *, n_pages, pages_per_seq):
    n = pl.program_id(0)
    slot = n % 2

    def page_copy(step, j, sl):
        dst = buf.at[sl, pl.ds(j * KV_W, KV_W)]
        if pages_per_seq is None:
            src = src_ref.at[pt_ref[step * n_pages + j]]
        else:
            seq = step * (n_pages // pages_per_seq) + j // pages_per_seq
            src = src_ref.at[seq, :, pl.ds((j % pages_per_seq) * PAGE_SIZE, PAGE_SIZE)]
        return pltpu.make_async_copy(src, dst, sem.at[sl])

    def start(step, sl):
        for j in range(n_pages):
            page_copy(step, j, sl).start()

    @pl.when(n == 0)
    def _():
        start(n, slot)

    @pl.when(n + 1 < pl.num_programs(0))
    def _():
        start(n + 1, 1 - slot)

    for j in range(n_pages):
        page_copy(n, j, slot).wait()

    def feature_rows(f):
        return buf[slot, pl.ds(f, n_pages, stride=KV_W), :]

    for c in range(2):
        acc = None
        for dp in range(D_PAIRS):
            rows = []
            for g in range(N_KV):
                f0 = (c * N_KV + g) * HEAD_DIM + 2 * dp
                rows.append(jnp.concatenate(
                    [feature_rows(f0) + pe_ref[c, 2 * dp:2 * dp + 1, :],
                     feature_rows(f0 + 1) + pe_ref[c, 2 * dp + 1:2 * dp + 2, :]], axis=-1))
            x = jnp.concatenate(rows, axis=0).astype(BF16)
            d = _dot(x, w1_ref[c, dp])
            acc = d if acc is None else acc + d
        hid = (acc * jax.nn.sigmoid(acc)).astype(BF16)
        kc_ref[c] = _dot(hid, w2_ref[c])


def _compress(pt_flat, src, pe_t, w1_t, w2_t, n_steps, n_pages, pages_per_seq, name):
    const3 = lambda a, pt: (0, 0, 0)
    grid_spec = pltpu.PrefetchScalarGridSpec(
        num_scalar_prefetch=1,
        grid=(n_steps,),
        in_specs=[
            pl.BlockSpec(memory_space=pl.ANY),
            pl.BlockSpec((2, HEAD_DIM, PAGE_SIZE), const3),
            pl.BlockSpec((2, D_PAIRS, 2 * PAGE_SIZE, PAGE_SIZE), lambda a, pt: (0, 0, 0, 0)),
            pl.BlockSpec((2, PAGE_SIZE, PAGE_SIZE), const3),
        ],
        out_specs=pl.BlockSpec((None, 2, N_KV * n_pages, PAGE_SIZE), lambda a, pt: (a, 0, 0, 0)),
        scratch_shapes=[pltpu.VMEM((2, n_pages * KV_W, PAGE_SIZE), F32),
                        pltpu.SemaphoreType.DMA((2,))],
    )
    return pl.pallas_call(
        functools.partial(_compress_body, n_pages=n_pages, pages_per_seq=pages_per_seq),
        grid_spec=grid_spec,
        out_shape=jax.ShapeDtypeStruct((n_steps, 2, N_KV * n_pages, PAGE_SIZE), F32),
        compiler_params=_cparams(("arbitrary",)),
        name=name,
    )(pt_flat, src, pe_t, w1_t, w2_t)


ATT_TQ = 128
ATT_TK = 256
ATT_TK_SHIFT = 8
WIN_TILES = WINDOW // ATT_TK + 1


def _attn_update(s_t, v_t, state):
    m, l, acc = state
    m_new = jnp.maximum(m, jnp.max(s_t, axis=0, keepdims=True))
    a = jnp.exp(m - m_new)
    e = jnp.exp(s_t - m_new)
    l = a * l + jnp.sum(e, axis=0, keepdims=True)
    acc = a * acc + _dot(v_t, e.astype(BF16))
    return m_new, l, acc


def _attn_prompt_body(q_ref, ks_ref, kw_ref, vs_ref, vw_ref, kc_ref, vct_ref, gates_ref, o_ref,
                      selbias, winbias):
    tq, tk = ATT_TQ, ATT_TK
    qi = pl.program_id(1)
    nb = kc_ref.shape[1]
    t0 = qi * tq
    kd = t0 >> ATT_TK_SHIFT
    tpos = t0 + lax.broadcasted_iota(jnp.int32, (1, tq), 1)
    jrow = lax.broadcasted_iota(jnp.int32, (nb, 1), 0)
    krow = lax.broadcasted_iota(jnp.int32, (tk, 1), 0)
    vis = ((jrow + 1) * BLK - 1) <= tpos
    cur = tpos >> BLK_SHIFT

    for back in range(WIN_TILES):
        kpos = (kd - back) * tk + krow
        dist = tpos - kpos
        ok = (dist >= 0) & (dist < WINDOW) & (kpos >= 0)
        winbias[back] = jnp.where(ok, 0.0, NEG)

    o_cmp = []
    for g in range(N_KV):
        kcg = kc_ref[g]
        vctg = vct_ref[g]
        imp = jnp.zeros((nb, tq), F32)
        for r in range(GQA_R):
            s = jnp.where(vis, _dot_nt(kcg, q_ref[:, r * LANES:(r + 1) * LANES]), NEG)
            m = jnp.max(s, axis=0, keepdims=True)
            e = jnp.where(vis, jnp.exp(s - m), 0.0)
            l = jnp.sum(e, axis=0, keepdims=True)
            p = e / jnp.where(l > 0.0, l, 1.0)
            imp = imp + p
            o_cmp.append(_dot(vctg, p.astype(BF16)))

        score = jnp.where(jrow > cur, NEG, imp)
        score = jnp.where((jrow == 0) | (jrow == cur) | (jrow == cur - 1), FORCE, score)
        rank = jnp.zeros((nb, tq), F32)
        for j2 in range(nb):
            row = score[j2:j2 + 1, :]
            tie = jnp.where(jrow > j2, 1.0, 0.0)
            rank = rank + jnp.where(row > score, 1.0, jnp.where(row == score, tie, 0.0))
        selbias[g] = jnp.where(rank < N_SEL, 0.0, NEG)

    def key_tile(k_ref, kt, g):
        return k_ref[pl.ds(pl.multiple_of(kt * tk, tk), tk), g * LANES:(g + 1) * LANES]

    def val_tile(v_ref, kt, g):
        return v_ref[g * HEAD_DIM:(g + 1) * HEAD_DIM, pl.ds(pl.multiple_of(kt * tk, tk), tk)].astype(BF16)

    def sel_bias_tile(kt, g):
        rows = [jnp.broadcast_to(selbias[g, pl.ds(kt * (tk // BLK) + b, 1), :], (BLK, tq))
                for b in range(tk // BLK)]
        return jnp.concatenate(rows, axis=0)

    init = (jnp.full((1, tq), NEG, F32), jnp.zeros((1, tq), F32), jnp.zeros((HEAD_DIM, tq), F32))
    gates = gates_ref[...]
    outs = []
    for h in range(N_HEADS):
        g, r = h // GQA_R, h % GQA_R
        qcol = q_ref[:, r * LANES:(r + 1) * LANES]

        s = _dot_nt(key_tile(ks_ref, kd, g), qcol) + sel_bias_tile(kd, g) + winbias[0]
        state = _attn_update(s, val_tile(vs_ref, kd, g), init)

        def sel_step(kt, st):
            s = _dot_nt(key_tile(ks_ref, kt, g), qcol) + sel_bias_tile(kt, g)
            return _attn_update(s, val_tile(vs_ref, kt, g), st)

        state = lax.fori_loop(0, kd, sel_step, state)
        o_sel = state[2] * (1.0 / state[1])

        state = init
        for back in range(WIN_TILES):
            kt = jnp.maximum(kd - back, 0)
            s = _dot_nt(key_tile(kw_ref, kt, g), qcol) + winbias[back]
            state = _attn_update(s, val_tile(vw_ref, kt, g), state)
        o_win = state[2] * (1.0 / state[1])

        outs.append(gates[h:h + 1, :] * o_cmp[h] + gates[N_HEADS + h:N_HEADS + h + 1, :] * o_sel
                    + gates[2 * N_HEADS + h:2 * N_HEADS + h + 1, :] * o_win)
    o_ref[...] = jnp.concatenate(outs, axis=0).T


def _attn_prompt(q, k_sel, k_win, kvs_t, kvw_t, kc_pad, vc_t, gates_t):
    n, t, _ = q.shape
    nb = kc_pad.shape[2]
    row = lambda a, i: (a, i, 0)
    seq = lambda a, i: (a, 0, 0)
    seq4 = lambda a, i: (a, 0, 0, 0)
    v_rows = lambda a, i: (a, 1, 0)
    return pl.pallas_call(
        _attn_prompt_body,
        grid=(n, t // ATT_TQ),
        in_specs=[
            pl.BlockSpec((None, ATT_TQ, D_ATTN), row),
            pl.BlockSpec((None, t, KV_W), seq),
            pl.BlockSpec((None, t, KV_W), seq),
            pl.BlockSpec((None, N_KV * HEAD_DIM, t), v_rows),
            pl.BlockSpec((None, N_KV * HEAD_DIM, t), v_rows),
            pl.BlockSpec((None, N_KV, nb, LANES), seq4),
            pl.BlockSpec((None, N_KV, HEAD_DIM, nb), seq4),
            pl.BlockSpec((None, GATE_ROWS, ATT_TQ), lambda a, i: (a, 0, i)),
        ],
        out_specs=pl.BlockSpec((None, ATT_TQ, D_ATTN), row),
        out_shape=jax.ShapeDtypeStruct((n, t, D_ATTN), F32),
        scratch_shapes=[pltpu.VMEM((N_KV, nb, ATT_TQ), F32),
                        pltpu.VMEM((WIN_TILES, ATT_TK, ATT_TQ), F32)],
        compiler_params=_cparams(("arbitrary", "arbitrary")),
        name="attn_prompt",
    )(q, k_sel, k_win, kvs_t, kvw_t, kc_pad, vc_t, gates_t)


Q_PAD = 16
DEC_SEQ_PER_STEP = 8


def _dec_cmp_body(q_ref, kc_ref, ocmp_ref, idx_ref, imp_buf, *, past, s_new):
    nseq, ncombo = q_ref.shape[0], q_ref.shape[1]
    n_pages = kc_ref.shape[2] // N_KV
    nb = n_pages * BPP
    width = imp_buf.shape[1]
    imp_buf[...] = jnp.zeros(imp_buf.shape, F32)
    lane = lax.broadcasted_iota(jnp.int32, (1, width), 1)
    jc = jnp.where(lane < n_pages, BPP * lane, BPP * (lane - n_pages) + 1)
    jc = jnp.where(lane == nb, nb, jc)
    jblk = jc[:, 0:nb]
    for a in range(nseq):
        for c in range(ncombo):
            g, t = c // s_new, c % s_new
            kpg = kc_ref[a, 0, g * n_pages:(g + 1) * n_pages, :].astype(BF16)
            vpg = kc_ref[a, 1, g * n_pages:(g + 1) * n_pages, :].astype(BF16)
            qc = q_ref[a, c]
            vis = ((jblk + 1) * BLK - 1) <= (past + t)
            s = jnp.concatenate([_dot_nt(qc, kpg[:, h * HEAD_DIM:(h + 1) * HEAD_DIM])
                                 for h in range(BPP)], axis=-1)
            s = jnp.where(vis, s, NEG)
            m = jnp.max(s, axis=-1, keepdims=True)
            e = jnp.where(vis, jnp.exp(s - m), 0.0)
            l = jnp.sum(e, axis=-1, keepdims=True)
            p = e / jnp.where(l > 0.0, l, 1.0)
            pb = p.astype(BF16)
            ocmp_ref[a, c] = sum(_dot(pb[:, h * n_pages:(h + 1) * n_pages],
                                      vpg[:, h * HEAD_DIM:(h + 1) * HEAD_DIM]) for h in range(BPP))
            row = a * ncombo + c
            imp_buf[row:row + 1, 0:nb] = jnp.sum(p[0:GQA_R, :], axis=0, keepdims=True)

    rows = nseq * ncombo
    imp = imp_buf[...]
    t_row = lax.broadcasted_iota(jnp.int32, (rows, 1), 0) & (s_new - 1)
    cur = (past + t_row) >> BLK_SHIFT
    score = jnp.where(jc > cur, NEG, imp)
    score = jnp.where((jc == 0) | (jc == cur) | (jc == cur - 1), FORCE, score)
    score = jnp.where(lane <= nb, score, -jnp.inf)
    jcf = jc.astype(F32)
    out_lane = lax.broadcasted_iota(jnp.int32, (1, LANES), 1)
    idx = jnp.zeros((rows, LANES), jnp.int32)
    for k in range(N_SEL):
        m = jnp.max(score, axis=-1, keepdims=True)
        first = jnp.min(jnp.where(score == m, jcf, 1e9), axis=-1, keepdims=True)
        idx = jnp.where(out_lane == k, first.astype(jnp.int32), idx)
        score = jnp.where(jcf == first, -jnp.inf, score)
    idx_ref[...] = idx


def _dec_cmp(q_pad, kc, past, s_new):
    nseq, ncombo = q_pad.shape[0], q_pad.shape[1]
    rows_kc = kc.shape[2]
    nb = rows_kc // N_KV * BPP
    ns = DEC_SEQ_PER_STEP
    width = (nb + 1 + LANES - 1) // LANES * LANES
    return pl.pallas_call(
        functools.partial(_dec_cmp_body, past=past, s_new=s_new),
        grid=(nseq // ns,),
        in_specs=[
            pl.BlockSpec((ns, ncombo, Q_PAD, HEAD_DIM), lambda i: (i, 0, 0, 0)),
            pl.BlockSpec((ns, 2, rows_kc, PAGE_SIZE), lambda i: (i, 0, 0, 0)),
        ],
        out_specs=[
            pl.BlockSpec((ns, ncombo, Q_PAD, HEAD_DIM), lambda i: (i, 0, 0, 0)),
            pl.BlockSpec((ns * ncombo, LANES), lambda i: (i, 0)),
        ],
        out_shape=[jax.ShapeDtypeStruct((nseq, ncombo, Q_PAD, HEAD_DIM), F32),
                   jax.ShapeDtypeStruct((nseq * ncombo, LANES), jnp.int32)],
        scratch_shapes=[pltpu.VMEM((ns * ncombo, width), F32)],
        compiler_params=_cparams(("arbitrary",)),
        name="dec_cmp_select",
    )(q_pad, kc)


def _dec_sel_win_body(idx_ref, pt_ref, q_ref, pool_ref, newblk_ref, win_ref, kvw_new_ref,
                      osel_ref, owin_ref, kbuf, vbuf, sem, *, past, s_new, n_pages):
    ncombo = q_ref.shape[0]
    n_past_blk = past // BLK
    n = pl.program_id(0)
    nseq = pl.num_programs(0)
    slot = n % 2

    def feature_rows(c, part):
        g = c // s_new
        return pl.ds((part * N_KV + g) * HEAD_DIM, HEAD_DIM)

    def block_copy(src_page, c, k, part, sl):
        dst = (kbuf, vbuf)[part].at[sl, c, :, pl.ds(k * PAGE_SIZE, PAGE_SIZE)]
        return pltpu.make_async_copy(src_page.at[feature_rows(c, part)], dst, sem.at[sl])

    def start(seq, sl):
        for c in range(ncombo):
            for k in range(N_SEL):
                j = idx_ref[(seq * ncombo + c) * N_SEL + k]

                @pl.when(j < n_past_blk)
                def _():
                    page = pt_ref[seq * n_pages + jnp.clip(j, 0, n_past_blk - 1) // BPP]
                    for part in range(2):
                        block_copy(pool_ref.at[page], c, k, part, sl).start()

                @pl.when(j >= n_past_blk)
                def _():
                    for part in range(2):
                        block_copy(newblk_ref.at[seq], c, k, part, sl).start()

    @pl.when(n == 0)
    def _():
        start(n, slot)

    @pl.when(n + 1 < nseq)
    def _():
        start(n + 1, 1 - slot)

    for c in range(ncombo):
        for k in range(N_SEL):
            for part in range(2):
                block_copy(newblk_ref.at[n], c, k, part, slot).wait()

    nkeys = N_SEL * PAGE_SIZE
    lane = lax.broadcasted_iota(jnp.int32, (1, nkeys), 1)
    row_in_page = lane & (PAGE_SIZE - 1)
    for c in range(ncombo):
        t = c % s_new
        jv = jnp.zeros((1, nkeys), jnp.int32)
        for k in range(N_SEL):
            j = idx_ref[(n * ncombo + c) * N_SEL + k]
            jv = jnp.where((lane >> PAGE_SHIFT) == k, j, jv)
        kpos = jv * BLK + (row_in_page & (BLK - 1))
        half = jnp.where(jv >= n_past_blk, 0, jv & (BPP - 1))
        valid = ((row_in_page >> BLK_SHIFT) == half) & (kpos <= past + t)
        s = jnp.where(valid, _dot(q_ref[c], kbuf[slot, c].astype(BF16)), NEG)
        m = jnp.max(s, axis=-1, keepdims=True)
        e = jnp.exp(s - m)
        l = jnp.sum(e, axis=-1, keepdims=True)
        osel_ref[c] = _dot_nt(e.astype(BF16), vbuf[slot, c].astype(BF16)) / l

    nwin = win_ref.shape[1]
    npad = kvw_new_ref.shape[1]
    wpos = past - nwin + lax.broadcasted_iota(jnp.int32, (1, nwin), 1)
    npos = past + lax.broadcasted_iota(jnp.int32, (1, npad), 1)
    new_real = lax.broadcasted_iota(jnp.int32, (1, npad), 1) < s_new
    for c in range(ncombo):
        t = c % s_new
        krows, vrows = feature_rows(c, 0), feature_rows(c, 1)
        qc = q_ref[c]
        dw = (past + t) - wpos
        dn = (past + t) - npos
        valid_w = (dw >= 0) & (dw < WINDOW) & (wpos >= 0)
        valid_n = (dn >= 0) & (dn < WINDOW) & new_real
        sw = jnp.where(valid_w, _dot(qc, win_ref[krows, :].astype(BF16)), NEG)
        sn = jnp.where(valid_n, _dot(qc, kvw_new_ref[krows, :].astype(BF16)), NEG)
        m = jnp.maximum(jnp.max(sw, axis=-1, keepdims=True), jnp.max(sn, axis=-1, keepdims=True))
        ew = jnp.exp(sw - m)
        en = jnp.exp(sn - m)
        l = jnp.sum(ew, axis=-1, keepdims=True) + jnp.sum(en, axis=-1, keepdims=True)
        o = (_dot_nt(ew.astype(BF16), win_ref[vrows, :].astype(BF16))
             + _dot_nt(en.astype(BF16), kvw_new_ref[vrows, :].astype(BF16)))
        owin_ref[c] = o / l


def _dec_sel_win(idx_flat, pt_flat, q_pad, pool_t, newblk_t, win_t, kvw_new_t, past, s_new, n_pages):
    nseq, ncombo = q_pad.shape[0], q_pad.shape[1]
    nwin = win_t.shape[2]
    per_seq = lambda a, idx, pt: (a, 0, 0, 0)
    per_seq3 = lambda a, idx, pt: (a, 0, 0)
    grid_spec = pltpu.PrefetchScalarGridSpec(
        num_scalar_prefetch=2,
        grid=(nseq,),
        in_specs=[
            pl.BlockSpec((None, ncombo, Q_PAD, HEAD_DIM), per_seq),
            pl.BlockSpec(memory_space=pl.ANY),
            pl.BlockSpec(memory_space=pl.ANY),
            pl.BlockSpec((None, KV_W, nwin), per_seq3),
            pl.BlockSpec((None, KV_W, PAGE_SIZE), per_seq3),
        ],
        out_specs=[
            pl.BlockSpec((None, ncombo, Q_PAD, HEAD_DIM), per_seq),
            pl.BlockSpec((None, ncombo, Q_PAD, HEAD_DIM), per_seq),
        ],
        scratch_shapes=[pltpu.VMEM((2, ncombo, HEAD_DIM, N_SEL * PAGE_SIZE), F32),
                        pltpu.VMEM((2, ncombo, HEAD_DIM, N_SEL * PAGE_SIZE), F32),
                        pltpu.SemaphoreType.DMA((2,))],
    )
    shape = jax.ShapeDtypeStruct((nseq, ncombo, Q_PAD, HEAD_DIM), F32)
    return pl.pallas_call(
        functools.partial(_dec_sel_win_body, past=past, s_new=s_new, n_pages=n_pages),
        grid_spec=grid_spec,
        out_shape=[shape, shape],
        compiler_params=_cparams(("arbitrary",)),
        name="dec_sel_win",
    )(idx_flat, pt_flat, q_pad, pool_t, newblk_t, win_t, kvw_new_t)


def _out_ln_body(*refs, gated):
    if gated:
        ocmp_ref, osel_ref, owin_ref, gates_ref, yconv_ref, h_ref, wo_ref, g_ref, b_ref, y_ref = refs
        gates = gates_ref[...]
        pieces = []
        for h in range(N_HEADS):
            col = slice(h * HEAD_DIM, (h + 1) * HEAD_DIM)
            pieces.append(gates[:, h:h + 1] * ocmp_ref[:, col]
                          + gates[:, N_HEADS + h:N_HEADS + h + 1] * osel_ref[:, col]
                          + gates[:, 2 * N_HEADS + h:2 * N_HEADS + h + 1] * owin_ref[:, col])
        o = jnp.concatenate(pieces, axis=-1)
    else:
        o_ref, yconv_ref, h_ref, wo_ref, g_ref, b_ref, y_ref = refs
        o = o_ref[...]
    mix = (_dot(o.astype(BF16), wo_ref[0:D_ATTN, :])
           + _dot(yconv_ref[...].astype(BF16), wo_ref[D_ATTN:D_MODEL, :]))
    y_ref[...] = _layer_norm(ALPHA * h_ref[...] + mix, g_ref[...], b_ref[...])


def _out_ln(branches, yconv, h2d, w_out, g, b, tm):
    rows = h2d.shape[0]
    gated = len(branches) > 1
    row = lambda i: (i, 0)
    const = lambda i: (0, 0)
    widths = [D_ATTN, D_ATTN, D_ATTN, LANES] if gated else [D_ATTN]
    return pl.pallas_call(
        functools.partial(_out_ln_body, gated=gated),
        grid=(rows // tm,),
        in_specs=[pl.BlockSpec((tm, w), row) for w in widths] + [
            pl.BlockSpec((tm, D_CONV), row),
            pl.BlockSpec((tm, D_MODEL), row),
            pl.BlockSpec((D_MODEL, D_MODEL), const),
            pl.BlockSpec((1, D_MODEL), const),
            pl.BlockSpec((1, D_MODEL), const),
        ],
        out_specs=pl.BlockSpec((tm, D_MODEL), row),
        out_shape=jax.ShapeDtypeStruct((rows, D_MODEL), F32),
        compiler_params=_cparams(("arbitrary",)),
        name="out_ln",
    )(*branches, yconv, h2d, w_out, g, b)


def _block_diag_groups(w):
    z = jnp.zeros_like(w)
    return jnp.concatenate([jnp.concatenate([w, z], axis=-1),
                            jnp.concatenate([z, w], axis=-1)], axis=-2).astype(BF16)


def _to_combo_layout(x, nseq, s_new):
    x = x.reshape(nseq, s_new, N_KV, GQA_R, HEAD_DIM).transpose(0, 2, 1, 3, 4)
    x = x.reshape(nseq, N_KV * s_new, GQA_R, HEAD_DIM)
    return jnp.pad(x, ((0, 0), (0, 0), (0, Q_PAD - GQA_R), (0, 0)))


def _from_combo_layout(x, nseq, s_new):
    x = x[:, :, :GQA_R, :].reshape(nseq, N_KV, s_new, GQA_R, HEAD_DIM).transpose(0, 2, 1, 3, 4)
    return x.reshape(nseq * s_new, D_ATTN)


def kernel(x_prompt, x_sample, cache_cmp_kv, cache_sel_kv, state_win_kv, state_conv, page_table,
           ffa_gate, ffa_up, ffa_down, ln1_g, ln1_b, w_in, b_in, w_conv, cmp_w1, cmp_w2, cmp_pe,
           w_out, ln2_g, ln2_b, ffb_gate, ffb_up, ffb_down, ln3_g, ln3_b):
    depth = ffa_gate.shape[0]
    assert depth == 1
    l = 0
    n, t, d = x_prompt.shape
    nseq, s_new, _ = x_sample.shape
    n_pages = page_table.shape[1]
    past = n_pages * PAGE_SIZE
    rows_p, rows_s = n * t, nseq * s_new
    kvshape = (2, N_KV, HEAD_DIM)

    bf = lambda w: w.astype(BF16)
    vec = lambda v: v.reshape(1, -1)
    w_main = bf(w_in[l][:, :D_MAIN])
    b_main = vec(b_in[l][:D_MAIN])
    w_gate = bf(jnp.pad(w_in[l][:, D_MAIN:], ((0, 0), (0, LANES - N_GATE))))
    b_gate = vec(jnp.pad(b_in[l][D_MAIN:], (0, LANES - N_GATE)))
    pe_t = jnp.concatenate([cmp_pe[l].transpose(1, 2, 0)] * BPP, axis=-1)
    w1_t = _block_diag_groups(cmp_w1[l].transpose(0, 2, 1, 3)).reshape(
        2, D_PAIRS, 2 * PAGE_SIZE, PAGE_SIZE)
    w2_t = _block_diag_groups(cmp_w2[l])
    w_out_b = bf(w_out[l])
    ffa = (bf(ffa_gate[l]), bf(ffa_up[l]), bf(ffa_down[l]), vec(ln1_g[l]), vec(ln1_b[l]))
    ffb = (bf(ffb_gate[l]), bf(ffb_up[l]), bf(ffb_down[l]), vec(ln3_g[l]), vec(ln3_b[l]))

    def head_pairs(x):
        x = x.reshape(*x.shape[:-1], N_KV, GQA_R, HEAD_DIM)
        return jnp.swapaxes(x, -3, -2).reshape(*x.shape[:-3], D_ATTN)

    def k_pad(x, lo):
        z = jnp.zeros_like(x[..., :HEAD_DIM])
        return jnp.concatenate([x[..., lo:lo + HEAD_DIM], z, z, x[..., lo + HEAD_DIM:lo + 2 * HEAD_DIM]],
                               axis=-1)

    def prompt_cols(x):
        return jnp.concatenate([x[..., :_C_Q], head_pairs(x[..., _C_Q:_C_KVC]),
                                k_pad(x, _C_KVS), k_pad(x, _C_KVW)], axis=-1)

    def prompt_rows(x):
        return jnp.pad(x[..., _C_KVC:], [(0, 0)] * (x.ndim - 1) + [(0, GATE_ROWS - N_GATE)])

    w_rows, b_rows = bf(prompt_cols(w_in[l])), vec(prompt_cols(b_in[l]))
    w_tr, b_tr = bf(prompt_rows(w_in[l]).T), prompt_rows(b_in[l]).reshape(-1, 1)

    pps = t // PAGE_SIZE
    cmp_seqs = 4
    hp = _ffn_ln(x_prompt.reshape(rows_p, d), *ffa, tm=512)
    q, k_sel, k_win, kvc_t, kvs_t, kvw_t, gates_t, yconv, cstate = _proj_prompt(
        hp.reshape(n, t, d), w_rows, b_rows, w_tr, b_tr, w_conv[l], tm=512)
    kc = _compress(jnp.zeros((1,), jnp.int32), kvc_t, pe_t, w1_t, w2_t, n_steps=n // cmp_seqs,
                   n_pages=cmp_seqs * pps, pages_per_seq=pps, name="compress_prompt")
    kc = kc.reshape(n // cmp_seqs, 2, N_KV, cmp_seqs, pps, BPP, HEAD_DIM)
    kc = kc.transpose(0, 3, 2, 1, 4, 5, 6).reshape(n, N_KV, 2, pps * BPP, HEAD_DIM)
    zc = jnp.zeros_like(kc[:, 0, 0])
    kc_pad = bf(jnp.stack([jnp.concatenate([kc[:, 0, 0], zc], axis=-1),
                           jnp.concatenate([zc, kc[:, 1, 0]], axis=-1)], axis=1))
    vc_t = bf(jnp.swapaxes(kc[:, :, 1], -1, -2))
    o = _attn_prompt(q, k_sel, k_win, kvs_t, kvw_t, kc_pad, vc_t, gates_t)
    yp = _out_ln([o.reshape(rows_p, D_ATTN)], yconv.reshape(rows_p, D_CONV), hp, w_out_b,
                 vec(ln2_g[l]), vec(ln2_b[l]), tm=512)
    yp = _ffn_ln(yp, *ffb, tm=512).reshape(n, t, d)

    def from_feature_major(x_t):
        return jnp.moveaxis(x_t.reshape(x_t.shape[0], *kvshape, x_t.shape[-1]), -1, 1)[None]

    p_cmp = from_feature_major(kvc_t)
    p_sel = from_feature_major(kvs_t)
    p_win = from_feature_major(kvw_t[:, :, t - min(WINDOW, t):])
    p_conv = cstate[:, 8 - (CONV_W - 1):].reshape(1, n, CONV_W - 1, D_CONV)

    hs = _ffn_ln(x_sample.reshape(rows_s, d), *ffa, tm=rows_s)
    qs, kv3, gates_s, cb_s, u_s = _proj_sample(hs, w_main, b_main, w_gate, b_gate)
    kvc_s, kvs_s, kvw_s = kv3[:, :KV_W], kv3[:, KV_W:2 * KV_W], kv3[:, 2 * KV_W:]
    u3 = u_s.reshape(nseq, s_new, D_CONV)
    yconv_s = _conv_sample(cb_s.reshape(nseq, s_new, D_CONV), u3, state_conv[l], w_conv[l])
    s_conv = jnp.concatenate([state_conv[l], u3], axis=1)[:, -(CONV_W - 1):]

    def feature_major(x):
        return jnp.moveaxis(x, -4, -1).reshape(*x.shape[:-4], KV_W, x.shape[-4])

    def new_rows_t(x2d):
        xt = x2d.reshape(nseq, s_new, KV_W).transpose(0, 2, 1)
        return jnp.pad(xt, ((0, 0), (0, 0), (0, PAGE_SIZE - s_new)))

    kc_s = _compress(page_table.reshape(-1), feature_major(cache_cmp_kv[l]), pe_t, w1_t, w2_t,
                     n_steps=nseq, n_pages=n_pages, pages_per_seq=None, name="compress_decode")
    q_pad = _to_combo_layout(qs, nseq, s_new)
    ocmp_c, idx_pad = _dec_cmp(q_pad, kc_s, past, s_new)
    idx_flat = idx_pad[:, :N_SEL].reshape(-1)
    win_t = feature_major(state_win_kv[l])
    kvw_new_t = new_rows_t(kvw_s)
    osel_c, owin_c = _dec_sel_win(
        idx_flat, page_table.reshape(-1), q_pad, feature_major(cache_sel_kv[l]), new_rows_t(kvs_s),
        win_t, kvw_new_t, past, s_new, n_pages)
    branches = [_from_combo_layout(x, nseq, s_new) for x in (ocmp_c, osel_c, owin_c)]
    ys = _out_ln(branches + [gates_s], yconv_s.reshape(rows_s, D_CONV), hs, w_out_b,
                 vec(ln2_g[l]), vec(ln2_b[l]), tm=rows_s)
    ys = _ffn_ln(ys, *ffb, tm=rows_s).reshape(nseq, s_new, d)
    n_keep = min(WINDOW, past + s_new)
    ctx_t = jnp.concatenate([win_t, kvw_new_t[:, :, :s_new]], axis=-1)[:, :, -n_keep:]
    s_cmp = kvc_s.reshape(1, nseq, s_new, *kvshape)
    s_sel = kvs_s.reshape(1, nseq, s_new, *kvshape)
    s_win = jnp.moveaxis(ctx_t.reshape(nseq, *kvshape, n_keep), -1, 1)[None]
    return (yp, ys, p_cmp, p_sel, p_win, p_conv, s_cmp, s_sel, s_win,
            s_conv.reshape(1, nseq, CONV_W - 1, D_CONV))
```

```python
import functools

import jax
import jax.numpy as jnp
from jax import lax
from jax.experimental import pallas as pl
from jax.experimental.pallas import tpu as pltpu

F32 = jnp.float32
BF16 = jnp.bfloat16

D_MODEL = 1024
D_CONV = 512
D_ATTN = 512
HEAD_DIM = 64
N_HEADS = 8
N_KV = 2
GQA_R = N_HEADS // N_KV
N_BR = 3
CONV_W = 3
BLK = 64
BLK_SHIFT = 6
N_SEL = 16
WINDOW = 512
PAGE_SIZE = 128
PAGE_SHIFT = 7
D_FF = 2816
KV_W = 2 * N_KV * HEAD_DIM
D_MAIN = 3 * D_CONV + D_ATTN + 3 * KV_W
N_GATE = N_BR * N_HEADS
ALPHA = 2.0 ** 0.25
SCALE = HEAD_DIM ** -0.5
LN_EPS = 1e-5
NEG = -1e30
FORCE = 1e4

LANES = 128
VMEM_LIMIT_BYTES = 56 * 1024 * 1024


def _cparams(sem):
    return pltpu.CompilerParams(dimension_semantics=sem, vmem_limit_bytes=VMEM_LIMIT_BYTES)


def _layer_norm(y, g, b):
    mu = jnp.mean(y, axis=-1, keepdims=True)
    yc = y - mu
    var = jnp.mean(yc * yc, axis=-1, keepdims=True)
    return yc * lax.rsqrt(var + LN_EPS) * g + b


def _dot(a, b):
    return jnp.dot(a, b, preferred_element_type=F32)


def _dot_nt(a, b):
    return lax.dot_general(a, b, (((1,), (1,)), ((), ())), preferred_element_type=F32)


FFN_CHUNK = 256


def _ffn_ln_body(x_ref, wg_ref, wu_ref, wd_ref, g_ref, b_ref, o_ref):
    x = x_ref[...]
    xb = x.astype(BF16)
    acc = jnp.zeros(x.shape, F32)
    for c in range(D_FF // FFN_CHUNK):
        sl = slice(c * FFN_CHUNK, (c + 1) * FFN_CHUNK)
        gg = _dot(xb, wg_ref[:, sl])
        uu = _dot(xb, wu_ref[:, sl])
        hh = (gg * jax.nn.sigmoid(gg) * uu).astype(BF16)
        acc = acc + _dot(hh, wd_ref[sl, :])
    o_ref[...] = _layer_norm(ALPHA * x + 0.5 * acc, g_ref[...], b_ref[...])


def _ffn_ln(x2d, wg, wu, wd, g, b, tm):
    rows = x2d.shape[0]
    const = lambda i: (0, 0)
    return pl.pallas_call(
        _ffn_ln_body,
        grid=(rows // tm,),
        in_specs=[
            pl.BlockSpec((tm, D_MODEL), lambda i: (i, 0)),
            pl.BlockSpec((D_MODEL, D_FF), const),
            pl.BlockSpec((D_MODEL, D_FF), const),
            pl.BlockSpec((D_FF, D_MODEL), const),
            pl.BlockSpec((1, D_MODEL), const),
            pl.BlockSpec((1, D_MODEL), const),
        ],
        out_specs=pl.BlockSpec((tm, D_MODEL), lambda i: (i, 0)),
        out_shape=jax.ShapeDtypeStruct((rows, D_MODEL), F32),
        compiler_params=_cparams(("arbitrary",)),
        name="ffn_ln",
    )(x2d, wg, wu, wd, g, b)


_C_CB, _C_CC, _C_CH, _C_Q = 0, D_CONV, 2 * D_CONV, 3 * D_CONV
_C_KVC = 3 * D_CONV + D_ATTN
_C_KVS = _C_KVC + KV_W
_C_KVW = _C_KVS + KV_W


def _proj_piece(hb, w_ref, b_ref, lo, width):
    return _dot(hb, w_ref[:, lo:lo + width]) + b_ref[:, lo:lo + width]


_P_KS = 3 * D_CONV
_P_KW = _P_KS + KV_W
P_ROW_COLS = _P_KW + KV_W
GATE_ROWS = 32
_T_KV = D_ATTN
_T_GATE = _T_KV + 3 * KV_W
P_T_ROWS = _T_GATE + GATE_ROWS


def _proj_prompt_body(h_ref, w_ref, b_ref, wt_ref, bt_ref, wc_ref,
                      qt_ref, ks_ref, kw_ref, kvct_ref, kvst_ref, kvwt_ref, gates_ref,
                      yconv_ref, cstate_ref, ubuf):
    i = pl.program_id(1)
    tm = h_ref.shape[0]
    hb = h_ref[...].astype(BF16)

    def piece_t(lo, height):
        return _dot_nt(wt_ref[lo:lo + height, :], hb) + bt_ref[lo:lo + height, :]

    qt_ref[...] = (piece_t(0, D_ATTN) * SCALE).astype(BF16)
    ks_ref[...] = _proj_piece(hb, w_ref, b_ref, _P_KS, KV_W).astype(BF16)
    kw_ref[...] = _proj_piece(hb, w_ref, b_ref, _P_KW, KV_W).astype(BF16)
    for k, ref in enumerate((kvct_ref, kvst_ref, kvwt_ref)):
        ref[...] = piece_t(_T_KV + k * KV_W, KV_W)
    gates_ref[...] = jax.nn.sigmoid(piece_t(_T_GATE, GATE_ROWS))
    cb = _proj_piece(hb, w_ref, b_ref, _C_CB, D_CONV)
    u = _proj_piece(hb, w_ref, b_ref, _C_CC, D_CONV) * _proj_piece(hb, w_ref, b_ref, _C_CH, D_CONV)

    @pl.when(i == 0)
    def _():
        ubuf[0:8, :] = jnp.zeros((8, D_CONV), F32)

    ubuf[8:8 + tm, :] = u
    u1 = ubuf[7:7 + tm, :]
    u2 = ubuf[6:6 + tm, :]
    z = u2 * wc_ref[0:1, :] + u1 * wc_ref[1:2, :] + u * wc_ref[2:3, :]
    yconv_ref[...] = cb * z
    tail = u[tm - 8:tm, :]
    ubuf[0:8, :] = tail
    cstate_ref[...] = tail


def _proj_prompt(h, w_rows, b_rows, w_t, b_t, w_conv, tm):
    n, t, _ = h.shape
    const = lambda a, i: (0, 0)
    row = lambda a, i: (a, i, 0)
    col = lambda a, i: (a, 0, i)

    def out(width, dtype=F32):
        return jax.ShapeDtypeStruct((n, t, width), dtype)

    def out_t(rows, dtype=F32):
        return jax.ShapeDtypeStruct((n, rows, t), dtype)

    return pl.pallas_call(
        _proj_prompt_body,
        grid=(n, t // tm),
        in_specs=[
            pl.BlockSpec((None, tm, D_MODEL), row),
            pl.BlockSpec((D_MODEL, P_ROW_COLS), const),
            pl.BlockSpec((1, P_ROW_COLS), const),
            pl.BlockSpec((P_T_ROWS, D_MODEL), const),
            pl.BlockSpec((P_T_ROWS, 1), const),
            pl.BlockSpec((CONV_W, D_CONV), const),
        ],
        out_specs=[
            pl.BlockSpec((None, D_ATTN, tm), col),
            pl.BlockSpec((None, tm, KV_W), row),
            pl.BlockSpec((None, tm, KV_W), row),
            pl.BlockSpec((None, KV_W, tm), col),
            pl.BlockSpec((None, KV_W, tm), col),
            pl.BlockSpec((None, KV_W, tm), col),
            pl.BlockSpec((None, GATE_ROWS, tm), col),
            pl.BlockSpec((None, tm, D_CONV), row),
            pl.BlockSpec((None, 8, D_CONV), lambda a, i: (a, 0, 0)),
        ],
        out_shape=[out_t(D_ATTN, BF16), out(KV_W, BF16), out(KV_W, BF16), out_t(KV_W), out_t(KV_W),
                   out_t(KV_W), out_t(GATE_ROWS), out(D_CONV), jax.ShapeDtypeStruct((n, 8, D_CONV), F32)],
        scratch_shapes=[pltpu.VMEM((tm + 8, D_CONV), F32)],
        compiler_params=_cparams(("arbitrary", "arbitrary")),
        name="proj_prompt",
    )(h, w_rows, b_rows, w_t, b_t, w_conv)


def _proj_sample_body(h_ref, w_ref, b_ref, wgt_ref, bgt_ref, q_ref, kv_ref, gates_ref, cb_ref, u_ref):
    hb = h_ref[...].astype(BF16)
    q_ref[...] = (_proj_piece(hb, w_ref, b_ref, _C_Q, D_ATTN) * SCALE).astype(BF16)
    kv_ref[...] = _proj_piece(hb, w_ref, b_ref, _C_KVC, 3 * KV_W)
    gates_ref[...] = jax.nn.sigmoid(_dot(hb, wgt_ref[...]) + bgt_ref[...])
    cb_ref[...] = _proj_piece(hb, w_ref, b_ref, _C_CB, D_CONV)
    u_ref[...] = _proj_piece(hb, w_ref, b_ref, _C_CC, D_CONV) * _proj_piece(hb, w_ref, b_ref, _C_CH, D_CONV)


def _proj_sample(h2d, w_main, b_main, w_gate, b_gate):
    rows = h2d.shape[0]

    def out(width, dtype=F32):
        return jax.ShapeDtypeStruct((rows, width), dtype)

    return pl.pallas_call(
        _proj_sample_body,
        out_shape=[out(D_ATTN, BF16), out(3 * KV_W), out(LANES), out(D_CONV), out(D_CONV)],
        compiler_params=pltpu.CompilerParams(vmem_limit_bytes=VMEM_LIMIT_BYTES),
        name="proj_sample",
    )(h2d, w_main, b_main, w_gate, b_gate)


def _conv_sample_body(cb_ref, u_ref, prev_ref, wc_ref, y_ref):
    s = u_ref.shape[1]
    ext = [prev_ref[:, 0, :], prev_ref[:, 1, :]] + [u_ref[:, t, :] for t in range(s)]
    for t in range(s):
        z = ext[t] * wc_ref[0:1, :] + ext[t + 1] * wc_ref[1:2, :] + ext[t + 2] * wc_ref[2:3, :]
        y_ref[:, t, :] = cb_ref[:, t, :] * z


def _conv_sample(cb, u, prev, w_conv):
    return pl.pallas_call(
        _conv_sample_body,
        out_shape=jax.ShapeDtypeStruct(u.shape, F32),
        name="conv_sample",
    )(cb, u, prev, w_conv)


BPP = PAGE_SIZE // BLK
D_PAIRS = HEAD_DIM // 2


def _compress_body(pt_ref, src_ref, pe_ref, w1_ref, w2_ref, kc_ref, buf, sem, *, n_pages, pages_per_seq):
    n = pl.program_id(0)
    slot = n % 2

    def page_copy(step, j, sl):
        dst = buf.at[sl, pl.ds(j * KV_W, KV_W)]
        if pages_per_seq is None:
            src = src_ref.at[pt_ref[step * n_pages + j]]
        else:
            seq = step * (n_pages // pages_per_seq) + j // pages_per_seq
            src = src_ref.at[seq, :, pl.ds((j % pages_per_seq) * PAGE_SIZE, PAGE_SIZE)]
        return pltpu.make_async_copy(src, dst, sem.at[sl])

    def start(step, sl):
        for j in range(n_pages):
            page_copy(step, j, sl).start()

    @pl.when(n == 0)
    def _():
        start(n, slot)

    @pl.when(n + 1 < pl.num_programs(0))
    def _():
        start(n + 1, 1 - slot)

    for j in range(n_pages):
        page_copy(n, j, slot).wait()

    def feature_rows(f):
        return buf[slot, pl.ds(f, n_pages, stride=KV_W), :]

    for c in range(2):
        acc = None
        for dp in range(D_PAIRS):
            rows = []
            for g in range(N_KV):
                f0 = (c * N_KV + g) * HEAD_DIM + 2 * dp
                rows.append(jnp.concatenate(
                    [feature_rows(f0) + pe_ref[c, 2 * dp:2 * dp + 1, :],
                     feature_rows(f0 + 1) + pe_ref[c, 2 * dp + 1:2 * dp + 2, :]], axis=-1))
            x = jnp.concatenate(rows, axis=0).astype(BF16)
            d = _dot(x, w1_ref[c, dp])
            acc = d if acc is None else acc + d
        hid = (acc * jax.nn.sigmoid(acc)).astype(BF16)
        kc_ref[c] = _dot(hid, w2_ref[c])


def _compress(pt_flat, src, pe_t, w1_t, w2_t, n_steps, n_pages, pages_per_seq, name):
    const3 = lambda a, pt: (0, 0, 0)
    grid_spec = pltpu.PrefetchScalarGridSpec(
        num_scalar_prefetch=1,
        grid=(n_steps,),
        in_specs=[
            pl.BlockSpec(memory_space=pl.ANY),
            pl.BlockSpec((2, HEAD_DIM, PAGE_SIZE), const3),
            pl.BlockSpec((2, D_PAIRS, 2 * PAGE_SIZE, PAGE_SIZE), lambda a, pt: (0, 0, 0, 0)),
            pl.BlockSpec((2, PAGE_SIZE, PAGE_SIZE), const3),
        ],
        out_specs=pl.BlockSpec((None, 2, N_KV * n_pages, PAGE_SIZE), lambda a, pt: (a, 0, 0, 0)),
        scratch_shapes=[pltpu.VMEM((2, n_pages * KV_W, PAGE_SIZE), F32),
                        pltpu.SemaphoreType.DMA((2,))],
    )
    return pl.pallas_call(
        functools.partial(_compress_body, n_pages=n_pages, pages_per_seq=pages_per_seq),
        grid_spec=grid_spec,
        out_shape=jax.ShapeDtypeStruct((n_steps, 2, N_KV * n_pages, PAGE_SIZE), F32),
        compiler_params=_cparams(("arbitrary",)),
        name=name,
    )(pt_flat, src, pe_t, w1_t, w2_t)


ATT_TQ = 256
ATT_TK = 256
WIN_TILES = WINDOW // ATT_TK + 1


def _attn_update(s_t, v_t, state):
    m, l, acc = state
    m_new = jnp.maximum(m, jnp.max(s_t, axis=0, keepdims=True))
    a = jnp.exp(m - m_new)
    e = jnp.exp(s_t - m_new)
    l = a * l + jnp.sum(e, axis=0, keepdims=True)
    acc = a * acc + _dot(v_t, e.astype(BF16))
    return m_new, l, acc


def _attn_prompt_body(qt_ref, ks_ref, kw_ref, vs_ref, vw_ref, kc_ref, vct_ref, gates_ref, o_ref,
                      selbias, winbias):
    tq, tk = ATT_TQ, ATT_TK
    qi = pl.program_id(1)
    nb = kc_ref.shape[1]
    t0 = qi * tq
    kd = qi
    tpos = t0 + lax.broadcasted_iota(jnp.int32, (1, tq), 1)
    jrow = lax.broadcasted_iota(jnp.int32, (nb, 1), 0)
    krow = lax.broadcasted_iota(jnp.int32, (tk, 1), 0)
    vis = ((jrow + 1) * BLK - 1) <= tpos
    cur = tpos >> BLK_SHIFT

    for back in range(WIN_TILES):
        kpos = (kd - back) * tk + krow
        dist = tpos - kpos
        ok = (dist >= 0) & (dist < WINDOW) & (kpos >= 0)
        winbias[back] = jnp.where(ok, 0.0, NEG)

    def heads_of(x):
        return jnp.concatenate([x] * GQA_R, axis=1)

    def group_queries(g):
        pieces = []
        for r in range(GQA_R):
            lo = (g * GQA_R + r - g) * HEAD_DIM
            pieces.append(qt_ref[lo:lo + 2 * HEAD_DIM, :])
        return jnp.concatenate(pieces, axis=1)

    q4 = [group_queries(g) for g in range(N_KV)]
    vis4 = heads_of(vis)

    o_cmp = []
    for g in range(N_KV):
        s = jnp.where(vis4, _dot(kc_ref[g], q4[g]), NEG)
        m = jnp.max(s, axis=0, keepdims=True)
        e = jnp.where(vis4, jnp.exp(s - m), 0.0)
        l = jnp.sum(e, axis=0, keepdims=True)
        p = e / jnp.where(l > 0.0, l, 1.0)
        imp = sum(p[:, r * tq:(r + 1) * tq] for r in range(GQA_R))
        o_cmp.append(_dot(vct_ref[g], p.astype(BF16)))

        score = jnp.where(jrow > cur, NEG, imp)
        score = jnp.where((jrow == 0) | (jrow == cur) | (jrow == cur - 1), FORCE, score)
        rank = jnp.zeros((nb, tq), F32)
        for j2 in range(nb):
            row = score[j2:j2 + 1, :]
            tie = jnp.where(jrow > j2, 1.0, 0.0)
            rank = rank + jnp.where(row > score, 1.0, jnp.where(row == score, tie, 0.0))
        selbias[g] = jnp.where(rank < N_SEL, 0.0, NEG)

    def key_tile(k_ref, kt, g):
        return k_ref[pl.ds(pl.multiple_of(kt * tk, tk), tk), g * LANES:(g + 1) * LANES]

    def val_tile(v_ref, kt, g):
        return v_ref[g * HEAD_DIM:(g + 1) * HEAD_DIM, pl.ds(pl.multiple_of(kt * tk, tk), tk)].astype(BF16)

    def sel_bias_tile(kt, g):
        rows = [jnp.broadcast_to(selbias[g, pl.ds(kt * (tk // BLK) + b, 1), :], (BLK, tq))
                for b in range(tk // BLK)]
        return jnp.concatenate(rows, axis=0)

    wide = GQA_R * tq
    init = (jnp.full((1, wide), NEG, F32), jnp.zeros((1, wide), F32), jnp.zeros((HEAD_DIM, wide), F32))

    win = []
    for g in range(N_KV):
        state = init
        for back in range(WIN_TILES):
            kt = jnp.maximum(kd - back, 0)
            s = _dot(key_tile(kw_ref, kt, g), q4[g]) + heads_of(winbias[back])
            state = _attn_update(s, val_tile(vw_ref, kt, g), state)
        win.append(state)

    def sel_tile(kt, g, extra):
        bias = sel_bias_tile(kt, g)
        if extra is not None:
            bias = bias + extra
        s = _dot(key_tile(ks_ref, kt, g), q4[g]) + heads_of(bias)
        return s, val_tile(vs_ref, kt, g)

    def sel_step(kt, states):
        return tuple(_attn_update(*sel_tile(kt, g, None), states[g]) for g in range(N_KV))

    sel = tuple(_attn_update(*sel_tile(kd, g, winbias[0]), init) for g in range(N_KV))
    sel = lax.fori_loop(0, kd, sel_step, sel)

    gates = gates_ref[...]
    outs = []
    for h in range(N_HEADS):
        g, r = h // GQA_R, h % GQA_R
        cols = slice(r * tq, (r + 1) * tq)
        o_sel = sel[g][2][:, cols] * (1.0 / sel[g][1][:, cols])
        o_win = win[g][2][:, cols] * (1.0 / win[g][1][:, cols])
        outs.append(gates[h:h + 1, :] * o_cmp[g][:, cols] + gates[N_HEADS + h:N_HEADS + h + 1, :] * o_sel
                    + gates[2 * N_HEADS + h:2 * N_HEADS + h + 1, :] * o_win)
    o_ref[...] = jnp.concatenate(outs, axis=0).T


def _attn_prompt(q_t, k_sel, k_win, kvs_t, kvw_t, kc_pad, vc_t, gates_t):
    n, _, t = q_t.shape
    nb = kc_pad.shape[2]
    row = lambda a, i: (a, i, 0)
    seq = lambda a, i: (a, 0, 0)
    seq4 = lambda a, i: (a, 0, 0, 0)
    v_rows = lambda a, i: (a, 1, 0)
    return pl.pallas_call(
        _attn_prompt_body,
        grid=(n, t // ATT_TQ),
        in_specs=[
            pl.BlockSpec((None, D_ATTN, ATT_TQ), lambda a, i: (a, 0, i)),
            pl.BlockSpec((None, t, KV_W), seq),
            pl.BlockSpec((None, t, KV_W), seq),
            pl.BlockSpec((None, N_KV * HEAD_DIM, t), v_rows),
            pl.BlockSpec((None, N_KV * HEAD_DIM, t), v_rows),
            pl.BlockSpec((None, N_KV, nb, LANES), seq4),
            pl.BlockSpec((None, N_KV, HEAD_DIM, nb), seq4),
            pl.BlockSpec((None, GATE_ROWS, ATT_TQ), lambda a, i: (a, 0, i)),
        ],
        out_specs=pl.BlockSpec((None, ATT_TQ, D_ATTN), row),
        out_shape=jax.ShapeDtypeStruct((n, t, D_ATTN), F32),
        scratch_shapes=[pltpu.VMEM((N_KV, nb, ATT_TQ), F32),
                        pltpu.VMEM((WIN_TILES, ATT_TK, ATT_TQ), F32)],
        compiler_params=_cparams(("arbitrary", "arbitrary")),
        name="attn_prompt",
    )(q_t, k_sel, k_win, kvs_t, kvw_t, kc_pad, vc_t, gates_t)


Q_PAD = 16
DEC_SEQ_PER_STEP = 8


def _dec_cmp_body(q_ref, kc_ref, ocmp_ref, idx_ref, imp_buf, *, past, s_new):
    nseq, ncombo = q_ref.shape[0], q_ref.shape[1]
    n_pages = kc_ref.shape[2] // N_KV
    nb = n_pages * BPP
    width = imp_buf.shape[1]
    imp_buf[...] = jnp.zeros(imp_buf.shape, F32)
    lane = lax.broadcasted_iota(jnp.int32, (1, width), 1)
    jc = jnp.where(lane < n_pages, BPP * lane, BPP * (lane - n_pages) + 1)
    jc = jnp.where(lane == nb, nb, jc)
    jblk = jc[:, 0:nb]
    for a in range(nseq):
        for c in range(ncombo):
            g, t = c // s_new, c % s_new
            kpg = kc_ref[a, 0, g * n_pages:(g + 1) * n_pages, :].astype(BF16)
            vpg = kc_ref[a, 1, g * n_pages:(g + 1) * n_pages, :].astype(BF16)
            qc = q_ref[a, c]
            vis = ((jblk + 1) * BLK - 1) <= (past + t)
            s = jnp.concatenate([_dot_nt(qc, kpg[:, h * HEAD_DIM:(h + 1) * HEAD_DIM])
                                 for h in range(BPP)], axis=-1)
            s = jnp.where(vis, s, NEG)
            m = jnp.max(s, axis=-1, keepdims=True)
            e = jnp.where(vis, jnp.exp(s - m), 0.0)
            l = jnp.sum(e, axis=-1, keepdims=True)
            p = e / jnp.where(l > 0.0, l, 1.0)
            pb = p.astype(BF16)
            ocmp_ref[a, c] = sum(_dot(pb[:, h * n_pages:(h + 1) * n_pages],
                                      vpg[:, h * HEAD_DIM:(h + 1) * HEAD_DIM]) for h in range(BPP))
            row = a * ncombo + c
            imp_buf[row:row + 1, 0:nb] = jnp.sum(p[0:GQA_R, :], axis=0, keepdims=True)

    rows = nseq * ncombo
    imp = imp_buf[...]
    t_row = lax.broadcasted_iota(jnp.int32, (rows, 1), 0) & (s_new - 1)
    cur = (past + t_row) >> BLK_SHIFT
    score = jnp.where(jc > cur, NEG, imp)
    score = jnp.where((jc == 0) | (jc == cur) | (jc == cur - 1), FORCE, score)
    score = jnp.where(lane <= nb, score, -jnp.inf)
    jcf = jc.astype(F32)
    out_lane = lax.broadcasted_iota(jnp.int32, (1, LANES), 1)
    idx = jnp.zeros((rows, LANES), jnp.int32)
    for k in range(N_SEL):
        m = jnp.max(score, axis=-1, keepdims=True)
        first = jnp.min(jnp.where(score == m, jcf, 1e9), axis=-1, keepdims=True)
        idx = jnp.where(out_lane == k, first.astype(jnp.int32), idx)
        score = jnp.where(jcf == first, -jnp.inf, score)
    idx_ref[...] = idx


def _dec_cmp(q_pad, kc, past, s_new):
    nseq, ncombo = q_pad.shape[0], q_pad.shape[1]
    rows_kc = kc.shape[2]
    nb = rows_kc // N_KV * BPP
    ns = DEC_SEQ_PER_STEP
    width = (nb + 1 + LANES - 1) // LANES * LANES
    return pl.pallas_call(
        functools.partial(_dec_cmp_body, past=past, s_new=s_new),
        grid=(nseq // ns,),
        in_specs=[
            pl.BlockSpec((ns, ncombo, Q_PAD, HEAD_DIM), lambda i: (i, 0, 0, 0)),
            pl.BlockSpec((ns, 2, rows_kc, PAGE_SIZE), lambda i: (i, 0, 0, 0)),
        ],
        out_specs=[
            pl.BlockSpec((ns, ncombo, Q_PAD, HEAD_DIM), lambda i: (i, 0, 0, 0)),
            pl.BlockSpec((ns * ncombo, LANES), lambda i: (i, 0)),
        ],
        out_shape=[jax.ShapeDtypeStruct((nseq, ncombo, Q_PAD, HEAD_DIM), F32),
                   jax.ShapeDtypeStruct((nseq * ncombo, LANES), jnp.int32)],
        scratch_shapes=[pltpu.VMEM((ns * ncombo, width), F32)],
        compiler_params=_cparams(("arbitrary",)),
        name="dec_cmp_select",
    )(q_pad, kc)


def _dec_sel_win_body(idx_ref, pt_ref, q_ref, pool_ref, newblk_ref, win_ref, kvw_new_ref,
                      osel_ref, owin_ref, kbuf, vbuf, sem, *, past, s_new, n_pages):
    ncombo = q_ref.shape[0]
    n_past_blk = past // BLK
    n = pl.program_id(0)
    nseq = pl.num_programs(0)
    slot = n % 2

    def feature_rows(c, part):
        g = c // s_new
        return pl.ds((part * N_KV + g) * HEAD_DIM, HEAD_DIM)

    def block_copy(src_page, c, k, part, sl):
        dst = (kbuf, vbuf)[part].at[sl, c, :, pl.ds(k * PAGE_SIZE, PAGE_SIZE)]
        return pltpu.make_async_copy(src_page.at[feature_rows(c, part)], dst, sem.at[sl])

    def start(seq, sl):
        for c in range(ncombo):
            for k in range(N_SEL):
                j = idx_ref[(seq * ncombo + c) * N_SEL + k]

                @pl.when(j < n_past_blk)
                def _():
                    page = pt_ref[seq * n_pages + jnp.clip(j, 0, n_past_blk - 1) // BPP]
                    for part in range(2):
                        block_copy(pool_ref.at[page], c, k, part, sl).start()

                @pl.when(j >= n_past_blk)
                def _():
                    for part in range(2):
                        block_copy(newblk_ref.at[seq], c, k, part, sl).start()

    @pl.when(n == 0)
    def _():
        start(n, slot)

    @pl.when(n + 1 < nseq)
    def _():
        start(n + 1, 1 - slot)

    for c in range(ncombo):
        for k in range(N_SEL):
            for part in range(2):
                block_copy(newblk_ref.at[n], c, k, part, slot).wait()

    nkeys = N_SEL * PAGE_SIZE
    lane = lax.broadcasted_iota(jnp.int32, (1, nkeys), 1)
    row_in_page = lane & (PAGE_SIZE - 1)
    for c in range(ncombo):
        t = c % s_new
        jv = jnp.zeros((1, nkeys), jnp.int32)
        for k in range(N_SEL):
            j = idx_ref[(n * ncombo + c) * N_SEL + k]
            jv = jnp.where((lane >> PAGE_SHIFT) == k, j, jv)
        kpos = jv * BLK + (row_in_page & (BLK - 1))
        half = jnp.where(jv >= n_past_blk, 0, jv & (BPP - 1))
        valid = ((row_in_page >> BLK_SHIFT) == half) & (kpos <= past + t)
        s = jnp.where(valid, _dot(q_ref[c], kbuf[slot, c].astype(BF16)), NEG)
        m = jnp.max(s, axis=-1, keepdims=True)
        e = jnp.exp(s - m)
        l = jnp.sum(e, axis=-1, keepdims=True)
        osel_ref[c] = _dot_nt(e.astype(BF16), vbuf[slot, c].astype(BF16)) / l

    nwin = win_ref.shape[1]
    npad = kvw_new_ref.shape[1]
    wpos = past - nwin + lax.broadcasted_iota(jnp.int32, (1, nwin), 1)
    npos = past + lax.broadcasted_iota(jnp.int32, (1, npad), 1)
    new_real = lax.broadcasted_iota(jnp.int32, (1, npad), 1) < s_new
    for c in range(ncombo):
        t = c % s_new
        krows, vrows = feature_rows(c, 0), feature_rows(c, 1)
        qc = q_ref[c]
        dw = (past + t) - wpos
        dn = (past + t) - npos
        valid_w = (dw >= 0) & (dw < WINDOW) & (wpos >= 0)
        valid_n = (dn >= 0) & (dn < WINDOW) & new_real
        sw = jnp.where(valid_w, _dot(qc, win_ref[krows, :].astype(BF16)), NEG)
        sn = jnp.where(valid_n, _dot(qc, kvw_new_ref[krows, :].astype(BF16)), NEG)
        m = jnp.maximum(jnp.max(sw, axis=-1, keepdims=True), jnp.max(sn, axis=-1, keepdims=True))
        ew = jnp.exp(sw - m)
        en = jnp.exp(sn - m)
        l = jnp.sum(ew, axis=-1, keepdims=True) + jnp.sum(en, axis=-1, keepdims=True)
        o = (_dot_nt(ew.astype(BF16), win_ref[vrows, :].astype(BF16))
             + _dot_nt(en.astype(BF16), kvw_new_ref[vrows, :].astype(BF16)))
        owin_ref[c] = o / l


def _dec_sel_win(idx_flat, pt_flat, q_pad, pool_t, newblk_t, win_t, kvw_new_t, past, s_new, n_pages):
    nseq, ncombo = q_pad.shape[0], q_pad.shape[1]
    nwin = win_t.shape[2]
    per_seq = lambda a, idx, pt: (a, 0, 0, 0)
    per_seq3 = lambda a, idx, pt: (a, 0, 0)
    grid_spec = pltpu.PrefetchScalarGridSpec(
        num_scalar_prefetch=2,
        grid=(nseq,),
        in_specs=[
            pl.BlockSpec((None, ncombo, Q_PAD, HEAD_DIM), per_seq),
            pl.BlockSpec(memory_space=pl.ANY),
            pl.BlockSpec(memory_space=pl.ANY),
            pl.BlockSpec((None, KV_W, nwin), per_seq3),
            pl.BlockSpec((None, KV_W, PAGE_SIZE), per_seq3),
        ],
        out_specs=[
            pl.BlockSpec((None, ncombo, Q_PAD, HEAD_DIM), per_seq),
            pl.BlockSpec((None, ncombo, Q_PAD, HEAD_DIM), per_seq),
        ],
        scratch_shapes=[pltpu.VMEM((2, ncombo, HEAD_DIM, N_SEL * PAGE_SIZE), F32),
                        pltpu.VMEM((2, ncombo, HEAD_DIM, N_SEL * PAGE_SIZE), F32),
                        pltpu.SemaphoreType.DMA((2,))],
    )
    shape = jax.ShapeDtypeStruct((nseq, ncombo, Q_PAD, HEAD_DIM), F32)
    return pl.pallas_call(
        functools.partial(_dec_sel_win_body, past=past, s_new=s_new, n_pages=n_pages),
        grid_spec=grid_spec,
        out_shape=[shape, shape],
        compiler_params=_cparams(("arbitrary",)),
        name="dec_sel_win",
    )(idx_flat, pt_flat, q_pad, pool_t, newblk_t, win_t, kvw_new_t)


def _out_ln_body(*refs, gated):
    if gated:
        ocmp_ref, osel_ref, owin_ref, gates_ref, yconv_ref, h_ref, wo_ref, g_ref, b_ref, y_ref = refs
        gates = gates_ref[...]
        pieces = []
        for h in range(N_HEADS):
            col = slice(h * HEAD_DIM, (h + 1) * HEAD_DIM)
            pieces.append(gates[:, h:h + 1] * ocmp_ref[:, col]
                          + gates[:, N_HEADS + h:N_HEADS + h + 1] * osel_ref[:, col]
                          + gates[:, 2 * N_HEADS + h:2 * N_HEADS + h + 1] * owin_ref[:, col])
        o = jnp.concatenate(pieces, axis=-1)
    else:
        o_ref, yconv_ref, h_ref, wo_ref, g_ref, b_ref, y_ref = refs
        o = o_ref[...]
    mix = (_dot(o.astype(BF16), wo_ref[0:D_ATTN, :])
           + _dot(yconv_ref[...].astype(BF16), wo_ref[D_ATTN:D_MODEL, :]))
    y_ref[...] = _layer_norm(ALPHA * h_ref[...] + mix, g_ref[...], b_ref[...])


def _out_ln(branches, yconv, h2d, w_out, g, b, tm):
    rows = h2d.shape[0]
    gated = len(branches) > 1
    row = lambda i: (i, 0)
    const = lambda i: (0, 0)
    widths = [D_ATTN, D_ATTN, D_ATTN, LANES] if gated else [D_ATTN]
    return pl.pallas_call(
        functools.partial(_out_ln_body, gated=gated),
        grid=(rows // tm,),
        in_specs=[pl.BlockSpec((tm, w), row) for w in widths] + [
            pl.BlockSpec((tm, D_CONV), row),
            pl.BlockSpec((tm, D_MODEL), row),
            pl.BlockSpec((D_MODEL, D_MODEL), const),
            pl.BlockSpec((1, D_MODEL), const),
            pl.BlockSpec((1, D_MODEL), const),
        ],
        out_specs=pl.BlockSpec((tm, D_MODEL), row),
        out_shape=jax.ShapeDtypeStruct((rows, D_MODEL), F32),
        compiler_params=_cparams(("arbitrary",)),
        name="out_ln",
    )(*branches, yconv, h2d, w_out, g, b)


def _block_diag_groups(w):
    z = jnp.zeros_like(w)
    return jnp.concatenate([jnp.concatenate([w, z], axis=-1),
                            jnp.concatenate([z, w], axis=-1)], axis=-2).astype(BF16)


def _to_combo_layout(x, nseq, s_new):
    x = x.reshape(nseq, s_new, N_KV, GQA_R, HEAD_DIM).transpose(0, 2, 1, 3, 4)
    x = x.reshape(nseq, N_KV * s_new, GQA_R, HEAD_DIM)
    return jnp.pad(x, ((0, 0), (0, 0), (0, Q_PAD - GQA_R), (0, 0)))


def _from_combo_layout(x, nseq, s_new):
    x = x[:, :, :GQA_R, :].reshape(nseq, N_KV, s_new, GQA_R, HEAD_DIM).transpose(0, 2, 1, 3, 4)
    return x.reshape(nseq * s_new, D_ATTN)


def kernel(x_prompt, x_sample, cache_cmp_kv, cache_sel_kv, state_win_kv, state_conv, page_table,
           ffa_gate, ffa_up, ffa_down, ln1_g, ln1_b, w_in, b_in, w_conv, cmp_w1, cmp_w2, cmp_pe,
           w_out, ln2_g, ln2_b, ffb_gate, ffb_up, ffb_down, ln3_g, ln3_b):
    depth = ffa_gate.shape[0]
    assert depth == 1
    l = 0
    n, t, d = x_prompt.shape
    nseq, s_new, _ = x_sample.shape
    n_pages = page_table.shape[1]
    past = n_pages * PAGE_SIZE
    rows_p, rows_s = n * t, nseq * s_new
    kvshape = (2, N_KV, HEAD_DIM)

    bf = lambda w: w.astype(BF16)
    vec = lambda v: v.reshape(1, -1)
    w_main = bf(w_in[l][:, :D_MAIN])
    b_main = vec(b_in[l][:D_MAIN])
    w_gate = bf(jnp.pad(w_in[l][:, D_MAIN:], ((0, 0), (0, LANES - N_GATE))))
    b_gate = vec(jnp.pad(b_in[l][D_MAIN:], (0, LANES - N_GATE)))
    pe_t = jnp.concatenate([cmp_pe[l].transpose(1, 2, 0)] * BPP, axis=-1)
    w1_t = _block_diag_groups(cmp_w1[l].transpose(0, 2, 1, 3)).reshape(
        2, D_PAIRS, 2 * PAGE_SIZE, PAGE_SIZE)
    w2_t = _block_diag_groups(cmp_w2[l])
    w_out_b = bf(w_out[l])
    ffa = (bf(ffa_gate[l]), bf(ffa_up[l]), bf(ffa_down[l]), vec(ln1_g[l]), vec(ln1_b[l]))
    ffb = (bf(ffb_gate[l]), bf(ffb_up[l]), bf(ffb_down[l]), vec(ln3_g[l]), vec(ln3_b[l]))

    def k_pad(x, lo):
        z = jnp.zeros_like(x[..., :HEAD_DIM])
        return jnp.concatenate([x[..., lo:lo + HEAD_DIM], z, z, x[..., lo + HEAD_DIM:lo + 2 * HEAD_DIM]],
                               axis=-1)

    def prompt_cols(x):
        return jnp.concatenate([x[..., :_C_Q], k_pad(x, _C_KVS), k_pad(x, _C_KVW)], axis=-1)

    def prompt_rows(x):
        return jnp.pad(x[..., _C_Q:], [(0, 0)] * (x.ndim - 1) + [(0, GATE_ROWS - N_GATE)])

    w_rows, b_rows = bf(prompt_cols(w_in[l])), vec(prompt_cols(b_in[l]))
    w_tr, b_tr = bf(prompt_rows(w_in[l]).T), prompt_rows(b_in[l]).reshape(-1, 1)

    pps = t // PAGE_SIZE
    cmp_seqs = 4
    hp = _ffn_ln(x_prompt.reshape(rows_p, d), *ffa, tm=512)
    q, k_sel, k_win, kvc_t, kvs_t, kvw_t, gates_t, yconv, cstate = _proj_prompt(
        hp.reshape(n, t, d), w_rows, b_rows, w_tr, b_tr, w_conv[l], tm=512)
    kc = _compress(jnp.zeros((1,), jnp.int32), kvc_t, pe_t, w1_t, w2_t, n_steps=n // cmp_seqs,
                   n_pages=cmp_seqs * pps, pages_per_seq=pps, name="compress_prompt")
    kc = kc.reshape(n // cmp_seqs, 2, N_KV, cmp_seqs, pps, BPP, HEAD_DIM)
    kc = kc.transpose(0, 3, 2, 1, 4, 5, 6).reshape(n, N_KV, 2, pps * BPP, HEAD_DIM)
    zc = jnp.zeros_like(kc[:, 0, 0])
    kc_pad = bf(jnp.stack([jnp.concatenate([kc[:, 0, 0], zc], axis=-1),
                           jnp.concatenate([zc, kc[:, 1, 0]], axis=-1)], axis=1))
    vc_t = bf(jnp.swapaxes(kc[:, :, 1], -1, -2))
    o = _attn_prompt(q, k_sel, k_win, kvs_t, kvw_t, kc_pad, vc_t, gates_t)
    yp = _out_ln([o.reshape(rows_p, D_ATTN)], yconv.reshape(rows_p, D_CONV), hp, w_out_b,
                 vec(ln2_g[l]), vec(ln2_b[l]), tm=512)
    yp = _ffn_ln(yp, *ffb, tm=512).reshape(n, t, d)

    def from_feature_major(x_t):
        return jnp.moveaxis(x_t.reshape(x_t.shape[0], *kvshape, x_t.shape[-1]), -1, 1)[None]

    p_cmp = from_feature_major(kvc_t)
    p_sel = from_feature_major(kvs_t)
    p_win = from_feature_major(kvw_t[:, :, t - min(WINDOW, t):])
    p_conv = cstate[:, 8 - (CONV_W - 1):].reshape(1, n, CONV_W - 1, D_CONV)

    hs = _ffn_ln(x_sample.reshape(rows_s, d), *ffa, tm=rows_s)
    qs, kv3, gates_s, cb_s, u_s = _proj_sample(hs, w_main, b_main, w_gate, b_gate)
    kvc_s, kvs_s, kvw_s = kv3[:, :KV_W], kv3[:, KV_W:2 * KV_W], kv3[:, 2 * KV_W:]
    u3 = u_s.reshape(nseq, s_new, D_CONV)
    yconv_s = _conv_sample(cb_s.reshape(nseq, s_new, D_CONV), u3, state_conv[l], w_conv[l])
    s_conv = jnp.concatenate([state_conv[l], u3], axis=1)[:, -(CONV_W - 1):]

    def feature_major(x):
        return jnp.moveaxis(x, -4, -1).reshape(*x.shape[:-4], KV_W, x.shape[-4])

    def new_rows_t(x2d):
        xt = x2d.reshape(nseq, s_new, KV_W).transpose(0, 2, 1)
        return jnp.pad(xt, ((0, 0), (0, 0), (0, PAGE_SIZE - s_new)))

    kc_s = _compress(page_table.reshape(-1), feature_major(cache_cmp_kv[l]), pe_t, w1_t, w2_t,
                     n_steps=nseq, n_pages=n_pages, pages_per_seq=None, name="compress_decode")
    q_pad = _to_combo_layout(qs, nseq, s_new)
    ocmp_c, idx_pad = _dec_cmp(q_pad, kc_s, past, s_new)
    idx_flat = idx_pad[:, :N_SEL].reshape(-1)
    win_t = feature_major(state_win_kv[l])
    kvw_new_t = new_rows_t(kvw_s)
    osel_c, owin_c = _dec_sel_win(
        idx_flat, page_table.reshape(-1), q_pad, feature_major(cache_sel_kv[l]), new_rows_t(kvs_s),
        win_t, kvw_new_t, past, s_new, n_pages)
    branches = [_from_combo_layout(x, nseq, s_new) for x in (ocmp_c, osel_c, owin_c)]
    ys = _out_ln(branches + [gates_s], yconv_s.reshape(rows_s, D_CONV), hs, w_out_b,
                 vec(ln2_g[l]), vec(ln2_b[l]), tm=rows_s)
    ys = _ffn_ln(ys, *ffb, tm=rows_s).reshape(nseq, s_new, d)
    n_keep = min(WINDOW, past + s_new)
    ctx_t = jnp.concatenate([win_t, kvw_new_t[:, :, :s_new]], axis=-1)[:, :, -n_keep:]
    s_cmp = kvc_s.reshape(1, nseq, s_new, *kvshape)
    s_sel = kvs_s.reshape(1, nseq, s_new, *kvshape)
    s_win = jnp.moveaxis(ctx_t.reshape(nseq, *kvshape, n_keep), -1, 1)[None]
    return (yp, ys, p_cmp, p_sel, p_win, p_conv, s_cmp, s_sel, s_win,
            s_conv.reshape(1, nseq, CONV_W - 1, D_CONV))
```

```python
import functools

import jax
import jax.numpy as jnp
from jax import lax
from jax.experimental import pallas as pl
from jax.experimental.pallas import tpu as pltpu

F32 = jnp.float32
BF16 = jnp.bfloat16

D_MODEL = 1024
D_CONV = 512
D_ATTN = 512
HEAD_DIM = 64
N_HEADS = 8
N_KV = 2
GQA_R = N_HEADS // N_KV
N_BR = 3
CONV_W = 3
BLK = 64
BLK_SHIFT = 6
N_SEL = 16
WINDOW = 512
PAGE_SIZE = 128
PAGE_SHIFT = 7
D_FF = 2816
KV_W = 2 * N_KV * HEAD_DIM
D_MAIN = 3 * D_CONV + D_ATTN + 3 * KV_W
N_GATE = N_BR * N_HEADS
ALPHA = 2.0 ** 0.25
SCALE = HEAD_DIM ** -0.5
LN_EPS = 1e-5
NEG = -1e30
FORCE = 1e4

LANES = 128
VMEM_LIMIT_BYTES = 56 * 1024 * 1024


def _cparams(sem):
    return pltpu.CompilerParams(dimension_semantics=sem, vmem_limit_bytes=VMEM_LIMIT_BYTES)


def _layer_norm(y, g, b):
    mu = jnp.mean(y, axis=-1, keepdims=True)
    yc = y - mu
    var = jnp.mean(yc * yc, axis=-1, keepdims=True)
    return yc * lax.rsqrt(var + LN_EPS) * g + b


def _dot(a, b):
    return jnp.dot(a, b, preferred_element_type=F32)


def _dot_nt(a, b):
    return lax.dot_general(a, b, (((1,), (1,)), ((), ())), preferred_element_type=F32)


FFN_CHUNK = 256


def _ffn_ln_body(x_ref, wg_ref, wu_ref, wd_ref, g_ref, b_ref, o_ref):
    x = x_ref[...]
    xb = x.astype(BF16)
    acc = jnp.zeros(x.shape, F32)
    for c in range(D_FF // FFN_CHUNK):
        sl = slice(c * FFN_CHUNK, (c + 1) * FFN_CHUNK)
        gg = _dot(xb, wg_ref[:, sl])
        uu = _dot(xb, wu_ref[:, sl])
        hh = (gg * jax.nn.sigmoid(gg) * uu).astype(BF16)
        acc = acc + _dot(hh, wd_ref[sl, :])
    o_ref[...] = _layer_norm(ALPHA * x + 0.5 * acc, g_ref[...], b_ref[...])


def _ffn_ln(x2d, wg, wu, wd, g, b, tm):
    rows = x2d.shape[0]
    const = lambda i: (0, 0)
    return pl.pallas_call(
        _ffn_ln_body,
        grid=(rows // tm,),
        in_specs=[
            pl.BlockSpec((tm, D_MODEL), lambda i: (i, 0)),
            pl.BlockSpec((D_MODEL, D_FF), const),
            pl.BlockSpec((D_MODEL, D_FF), const),
            pl.BlockSpec((D_FF, D_MODEL), const),
            pl.BlockSpec((1, D_MODEL), const),
            pl.BlockSpec((1, D_MODEL), const),
        ],
        out_specs=pl.BlockSpec((tm, D_MODEL), lambda i: (i, 0)),
        out_shape=jax.ShapeDtypeStruct((rows, D_MODEL), F32),
        compiler_params=_cparams(("arbitrary",)),
        name="ffn_ln",
    )(x2d, wg, wu, wd, g, b)


_C_CB, _C_CC, _C_CH, _C_Q = 0, D_CONV, 2 * D_CONV, 3 * D_CONV
_C_KVC = 3 * D_CONV + D_ATTN
_C_KVS = _C_KVC + KV_W
_C_KVW = _C_KVS + KV_W


def _proj_piece(hb, w_ref, b_ref, lo, width):
    return _dot(hb, w_ref[:, lo:lo + width]) + b_ref[:, lo:lo + width]


_P_KS = 3 * D_CONV
_P_KW = _P_KS + KV_W
P_ROW_COLS = _P_KW + KV_W
GATE_ROWS = 32
_T_KV = D_ATTN
_T_GATE = _T_KV + 3 * KV_W
P_T_ROWS = _T_GATE + GATE_ROWS


def _proj_prompt_body(h_ref, w_ref, b_ref, wt_ref, bt_ref, wc_ref,
                      qt_ref, ks_ref, kw_ref, kvct_ref, kvst_ref, kvwt_ref, gates_ref,
                      yconv_ref, cstate_ref, ubuf):
    i = pl.program_id(1)
    tm = h_ref.shape[0]
    hb = h_ref[...].astype(BF16)

    def piece_t(lo, height):
        return _dot_nt(wt_ref[lo:lo + height, :], hb) + bt_ref[lo:lo + height, :]

    qt_ref[...] = (piece_t(0, D_ATTN) * (SCALE * LOG2E)).astype(BF16)
    lane = lax.broadcasted_iota(jnp.int32, (1, KV_W), 1)
    blk = (i * tm + lax.broadcasted_iota(jnp.int32, (tm, 1), 0)) >> BLK_SHIFT
    aux0, aux1 = HEAD_DIM, 2 * HEAD_DIM
    ks = _proj_piece(hb, w_ref, b_ref, _P_KS, KV_W)
    ks_ref[...] = jnp.where((lane == aux0 + blk) | (lane == aux1 + blk), 1.0, ks).astype(BF16)
    kw = _proj_piece(hb, w_ref, b_ref, _P_KW, KV_W)
    kw_ref[...] = jnp.where((lane == aux0) | (lane == aux1), 1.0, kw).astype(BF16)
    for k, ref in enumerate((kvct_ref, kvst_ref, kvwt_ref)):
        ref[...] = piece_t(_T_KV + k * KV_W, KV_W)
    gates_ref[...] = jax.nn.sigmoid(piece_t(_T_GATE, GATE_ROWS))
    cb = _proj_piece(hb, w_ref, b_ref, _C_CB, D_CONV)
    u = _proj_piece(hb, w_ref, b_ref, _C_CC, D_CONV) * _proj_piece(hb, w_ref, b_ref, _C_CH, D_CONV)

    @pl.when(i == 0)
    def _():
        ubuf[0:8, :] = jnp.zeros((8, D_CONV), F32)

    ubuf[8:8 + tm, :] = u
    u1 = ubuf[7:7 + tm, :]
    u2 = ubuf[6:6 + tm, :]
    z = u2 * wc_ref[0:1, :] + u1 * wc_ref[1:2, :] + u * wc_ref[2:3, :]
    yconv_ref[...] = cb * z
    tail = u[tm - 8:tm, :]
    ubuf[0:8, :] = tail
    cstate_ref[...] = tail


def _proj_prompt(h, w_rows, b_rows, w_t, b_t, w_conv, tm):
    n, t, _ = h.shape
    const = lambda a, i: (0, 0)
    row = lambda a, i: (a, i, 0)
    col = lambda a, i: (a, 0, i)

    def out(width, dtype=F32):
        return jax.ShapeDtypeStruct((n, t, width), dtype)

    def out_t(rows, dtype=F32):
        return jax.ShapeDtypeStruct((n, rows, t), dtype)

    return pl.pallas_call(
        _proj_prompt_body,
        grid=(n, t // tm),
        in_specs=[
            pl.BlockSpec((None, tm, D_MODEL), row),
            pl.BlockSpec((D_MODEL, P_ROW_COLS), const),
            pl.BlockSpec((1, P_ROW_COLS), const),
            pl.BlockSpec((P_T_ROWS, D_MODEL), const),
            pl.BlockSpec((P_T_ROWS, 1), const),
            pl.BlockSpec((CONV_W, D_CONV), const),
        ],
        out_specs=[
            pl.BlockSpec((None, D_ATTN, tm), col),
            pl.BlockSpec((None, tm, KV_W), row),
            pl.BlockSpec((None, tm, KV_W), row),
            pl.BlockSpec((None, KV_W, tm), col),
            pl.BlockSpec((None, KV_W, tm), col),
            pl.BlockSpec((None, KV_W, tm), col),
            pl.BlockSpec((None, GATE_ROWS, tm), col),
            pl.BlockSpec((None, tm, D_CONV), row),
            pl.BlockSpec((None, 8, D_CONV), lambda a, i: (a, 0, 0)),
        ],
        out_shape=[out_t(D_ATTN, BF16), out(KV_W, BF16), out(KV_W, BF16), out_t(KV_W), out_t(KV_W),
                   out_t(KV_W), out_t(GATE_ROWS), out(D_CONV), jax.ShapeDtypeStruct((n, 8, D_CONV), F32)],
        scratch_shapes=[pltpu.VMEM((tm + 8, D_CONV), F32)],
        compiler_params=_cparams(("arbitrary", "arbitrary")),
        name="proj_prompt",
    )(h, w_rows, b_rows, w_t, b_t, w_conv)


def _proj_sample_body(h_ref, w_ref, b_ref, wgt_ref, bgt_ref, q_ref, kv_ref, gates_ref, cb_ref, u_ref):
    hb = h_ref[...].astype(BF16)
    q_ref[...] = (_proj_piece(hb, w_ref, b_ref, _C_Q, D_ATTN) * SCALE).astype(BF16)
    kv_ref[...] = _proj_piece(hb, w_ref, b_ref, _C_KVC, 3 * KV_W)
    gates_ref[...] = jax.nn.sigmoid(_dot(hb, wgt_ref[...]) + bgt_ref[...])
    cb_ref[...] = _proj_piece(hb, w_ref, b_ref, _C_CB, D_CONV)
    u_ref[...] = _proj_piece(hb, w_ref, b_ref, _C_CC, D_CONV) * _proj_piece(hb, w_ref, b_ref, _C_CH, D_CONV)


def _proj_sample(h2d, w_main, b_main, w_gate, b_gate):
    rows = h2d.shape[0]

    def out(width, dtype=F32):
        return jax.ShapeDtypeStruct((rows, width), dtype)

    return pl.pallas_call(
        _proj_sample_body,
        out_shape=[out(D_ATTN, BF16), out(3 * KV_W), out(LANES), out(D_CONV), out(D_CONV)],
        compiler_params=pltpu.CompilerParams(vmem_limit_bytes=VMEM_LIMIT_BYTES),
        name="proj_sample",
    )(h2d, w_main, b_main, w_gate, b_gate)


def _conv_sample_body(cb_ref, u_ref, prev_ref, wc_ref, y_ref):
    s = u_ref.shape[1]
    ext = [prev_ref[:, 0, :], prev_ref[:, 1, :]] + [u_ref[:, t, :] for t in range(s)]
    for t in range(s):
        z = ext[t] * wc_ref[0:1, :] + ext[t + 1] * wc_ref[1:2, :] + ext[t + 2] * wc_ref[2:3, :]
        y_ref[:, t, :] = cb_ref[:, t, :] * z


def _conv_sample(cb, u, prev, w_conv):
    return pl.pallas_call(
        _conv_sample_body,
        out_shape=jax.ShapeDtypeStruct(u.shape, F32),
        name="conv_sample",
    )(cb, u, prev, w_conv)


BPP = PAGE_SIZE // BLK
D_PAIRS = HEAD_DIM // 2


def _compress_body(pt_ref, src_ref, pe_ref, w1_ref, w2_ref, kc_ref, buf, sem, *, n_pages, pages_per_seq):
    n = pl.program_id(0)
    slot = n % 2

    def page_copy(step, j, sl):
        dst = buf.at[sl, :, j, :]
        if pages_per_seq is None:
            src = src_ref.at[pt_ref[step * n_pages + j]]
        else:
            seq = step * (n_pages // pages_per_seq) + j // pages_per_seq
            src = src_ref.at[seq, :, pl.ds((j % pages_per_seq) * PAGE_SIZE, PAGE_SIZE)]
        return pltpu.make_async_copy(src, dst, sem.at[sl])

    def start(step, sl):
        for j in range(n_pages):
            page_copy(step, j, sl).start()

    @pl.when(n == 0)
    def _():
        start(n, slot)

    @pl.when(n + 1 < pl.num_programs(0))
    def _():
        start(n + 1, 1 - slot)

    for j in range(n_pages):
        page_copy(n, j, slot).wait()

    def feature_rows(f):
        return buf[slot, f]

    for c in range(2):
        acc = None
        for dp in range(D_PAIRS):
            rows = []
            for g in range(N_KV):
                f0 = (c * N_KV + g) * HEAD_DIM + 2 * dp
                rows.append(jnp.concatenate(
                    [feature_rows(f0) + pe_ref[c, 2 * dp:2 * dp + 1, :],
                     feature_rows(f0 + 1) + pe_ref[c, 2 * dp + 1:2 * dp + 2, :]], axis=-1))
            x = jnp.concatenate(rows, axis=0).astype(BF16)
            d = _dot(x, w1_ref[c, dp])
            acc = d if acc is None else acc + d
        hid = (acc * jax.nn.sigmoid(acc)).astype(BF16)
        kc_ref[c] = _dot(hid, w2_ref[c])


def _compress(pt_flat, src, pe_t, w1_t, w2_t, n_steps, n_pages, pages_per_seq, name):
    const3 = lambda a, pt: (0, 0, 0)
    grid_spec = pltpu.PrefetchScalarGridSpec(
        num_scalar_prefetch=1,
        grid=(n_steps,),
        in_specs=[
            pl.BlockSpec(memory_space=pl.ANY),
            pl.BlockSpec((2, HEAD_DIM, PAGE_SIZE), const3),
            pl.BlockSpec((2, D_PAIRS, 2 * PAGE_SIZE, PAGE_SIZE), lambda a, pt: (0, 0, 0, 0)),
            pl.BlockSpec((2, PAGE_SIZE, PAGE_SIZE), const3),
        ],
        out_specs=pl.BlockSpec((None, 2, N_KV * n_pages, PAGE_SIZE), lambda a, pt: (a, 0, 0, 0)),
        scratch_shapes=[pltpu.VMEM((2, KV_W, n_pages, PAGE_SIZE), F32),
                        pltpu.SemaphoreType.DMA((2,))],
    )
    return pl.pallas_call(
        functools.partial(_compress_body, n_pages=n_pages, pages_per_seq=pages_per_seq),
        grid_spec=grid_spec,
        out_shape=jax.ShapeDtypeStruct((n_steps, 2, N_KV * n_pages, PAGE_SIZE), F32),
        compiler_params=_cparams(("arbitrary",)),
        name=name,
    )(pt_flat, src, pe_t, w1_t, w2_t)


ATT_TQ = 256
ATT_TK = 256
WIN_TILES = WINDOW // ATT_TK + 1
K_AUX = 32
LOG2E = 1.4426950408889634


V_ROWS = HEAD_DIM + 16


def _attn_update(s_t, v_t, state):
    m, acc = state
    m_new = jnp.maximum(m, jnp.max(s_t, axis=0, keepdims=True))
    e = jnp.exp2(s_t - m_new)
    acc = jnp.exp2(m - m_new) * acc + _dot(v_t, e.astype(BF16))
    return m_new, acc


def _attn_prompt_body(qt_ref, ks_ref, kw_ref, vs_ref, vw_ref, kc_ref, vct_ref, gates_ref, o_ref,
                      winbias):
    tq, tk = ATT_TQ, ATT_TK
    qi = pl.program_id(1)
    nb = kc_ref.shape[1]
    t0 = qi * tq
    kd = qi
    tpos = t0 + lax.broadcasted_iota(jnp.int32, (1, tq), 1)
    jrow = lax.broadcasted_iota(jnp.int32, (nb, 1), 0)
    krow = lax.broadcasted_iota(jnp.int32, (tk, 1), 0)
    vis = ((jrow + 1) * BLK - 1) <= tpos
    cur = tpos >> BLK_SHIFT

    for slot, back in enumerate((0, WIN_TILES - 1)):
        kpos = (kd - back) * tk + krow
        dist = tpos - kpos
        ok = (dist >= 0) & (dist < WINDOW) & (kpos >= 0)
        winbias[slot] = jnp.where(ok, 0.0, NEG)

    def heads_of(x):
        return jnp.concatenate([x] * GQA_R, axis=1)

    def group_queries(g, aux):
        side = jnp.concatenate([aux, jnp.zeros((HEAD_DIM - K_AUX, tq), BF16)], axis=0)
        pieces = []
        for r in range(GQA_R):
            h = g * GQA_R + r
            qh = qt_ref[h * HEAD_DIM:(h + 1) * HEAD_DIM, :]
            pieces.append(jnp.concatenate([qh, side] if g == 0 else [side, qh], axis=0))
        return jnp.concatenate(pieces, axis=1)

    no_aux = jnp.zeros((K_AUX, tq), BF16)
    q4 = [group_queries(g, no_aux) for g in range(N_KV)]
    vis4 = heads_of(vis)

    o_cmp, q4_sel = [], []
    for g in range(N_KV):
        s = jnp.where(vis4, _dot(kc_ref[g], q4[g]), NEG)
        m = jnp.max(s, axis=0, keepdims=True)
        e = jnp.where(vis4, jnp.exp2(s - m), 0.0)
        l = jnp.sum(e, axis=0, keepdims=True)
        p = e / jnp.where(l > 0.0, l, 1.0)
        imp = sum(p[:, r * tq:(r + 1) * tq] for r in range(GQA_R))
        o_cmp.append(_dot(vct_ref[g], p.astype(BF16)))

        score = jnp.where(jrow > cur, NEG, imp)
        score = jnp.where((jrow == 0) | (jrow == cur) | (jrow == cur - 1), FORCE, score)
        rank = jnp.zeros((nb, tq), F32)
        for j2 in range(nb):
            row = score[j2:j2 + 1, :]
            tie = jnp.where(jrow > j2, 1.0, 0.0)
            rank = rank + jnp.where(row > score, 1.0, jnp.where(row == score, tie, 0.0))
        q4_sel.append(group_queries(g, jnp.where(rank < N_SEL, 0.0, NEG).astype(BF16)))

    def key_tile(k_ref, kt, g):
        return k_ref[pl.ds(pl.multiple_of(kt * tk, tk), tk), g * LANES:(g + 1) * LANES]

    ones_rows = jnp.ones((V_ROWS - HEAD_DIM, tk), BF16)

    def val_tile(v_ref, kt, g):
        v = v_ref[g * HEAD_DIM:(g + 1) * HEAD_DIM, pl.ds(pl.multiple_of(kt * tk, tk), tk)]
        return jnp.concatenate([v.astype(BF16), ones_rows], axis=0)

    wide = GQA_R * tq
    init = (jnp.full((1, wide), NEG, F32), jnp.zeros((V_ROWS, wide), F32))

    def win_scores(back, g):
        kt = jnp.maximum(kd - back, 0)
        if back in (0, WIN_TILES - 1):
            return _dot(key_tile(kw_ref, kt, g), q4[g]) + heads_of(winbias[min(back, 1)])
        exists = jnp.broadcast_to(jnp.where(kd >= back, 0.0, NEG), (K_AUX, tq)).astype(BF16)
        return _dot(key_tile(kw_ref, kt, g), group_queries(g, exists))

    s_diag = [_dot(key_tile(ks_ref, kd, g), q4_sel[g]) + heads_of(winbias[0]) for g in range(N_KV)]
    s_win = [[win_scores(back, g) for g in range(N_KV)] for back in range(WIN_TILES)]
    sel = tuple(_attn_update(s_diag[g], val_tile(vs_ref, kd, g), init) for g in range(N_KV))
    win = [init] * N_KV
    for back in range(WIN_TILES):
        kt = jnp.maximum(kd - back, 0)
        win = [_attn_update(s_win[back][g], val_tile(vw_ref, kt, g), win[g]) for g in range(N_KV)]

    def sel_step(kt, states):
        s = [_dot(key_tile(ks_ref, kt, g), q4_sel[g]) for g in range(N_KV)]
        return tuple(_attn_update(s[g], val_tile(vs_ref, kt, g), states[g]) for g in range(N_KV))

    sel = lax.fori_loop(0, kd, sel_step, sel)

    gates = gates_ref[...]
    outs = []
    for h in range(N_HEADS):
        g, r = h // GQA_R, h % GQA_R
        cols = slice(r * tq, (r + 1) * tq)
        o_sel = sel[g][1][:HEAD_DIM, cols] * (1.0 / sel[g][1][HEAD_DIM:HEAD_DIM + 1, cols])
        o_win = win[g][1][:HEAD_DIM, cols] * (1.0 / win[g][1][HEAD_DIM:HEAD_DIM + 1, cols])
        outs.append(gates[h:h + 1, :] * o_cmp[g][:, cols] + gates[N_HEADS + h:N_HEADS + h + 1, :] * o_sel
                    + gates[2 * N_HEADS + h:2 * N_HEADS + h + 1, :] * o_win)
    o_ref[...] = jnp.concatenate(outs, axis=0).T


def _attn_prompt(q_t, k_sel, k_win, kvs_t, kvw_t, kc_pad, vc_t, gates_t):
    n, _, t = q_t.shape
    nb = kc_pad.shape[2]
    row = lambda a, i: (a, i, 0)
    seq = lambda a, i: (a, 0, 0)
    seq4 = lambda a, i: (a, 0, 0, 0)
    v_rows = lambda a, i: (a, 1, 0)
    return pl.pallas_call(
        _attn_prompt_body,
        grid=(n, t // ATT_TQ),
        in_specs=[
            pl.BlockSpec((None, D_ATTN, ATT_TQ), lambda a, i: (a, 0, i)),
            pl.BlockSpec((None, t, KV_W), seq),
            pl.BlockSpec((None, t, KV_W), seq),
            pl.BlockSpec((None, N_KV * HEAD_DIM, t), v_rows),
            pl.BlockSpec((None, N_KV * HEAD_DIM, t), v_rows),
            pl.BlockSpec((None, N_KV, nb, LANES), seq4),
            pl.BlockSpec((None, N_KV, HEAD_DIM, nb), seq4),
            pl.BlockSpec((None, GATE_ROWS, ATT_TQ), lambda a, i: (a, 0, i)),
        ],
        out_specs=pl.BlockSpec((None, ATT_TQ, D_ATTN), row),
        out_shape=jax.ShapeDtypeStruct((n, t, D_ATTN), F32),
        scratch_shapes=[pltpu.VMEM((2, ATT_TK, ATT_TQ), F32)],
        compiler_params=_cparams(("arbitrary", "arbitrary")),
        name="attn_prompt",
    )(q_t, k_sel, k_win, kvs_t, kvw_t, kc_pad, vc_t, gates_t)


Q_PAD = 16
DEC_SEQ_PER_STEP = 8


def _dec_cmp_body(q_ref, kc_ref, ocmp_ref, idx_ref, imp_buf, *, past, s_new):
    nseq, ncombo = q_ref.shape[0], q_ref.shape[1]
    n_pages = kc_ref.shape[2] // N_KV
    nb = n_pages * BPP
    width = imp_buf.shape[1]
    imp_buf[...] = jnp.zeros(imp_buf.shape, F32)
    lane = lax.broadcasted_iota(jnp.int32, (1, width), 1)
    jc = jnp.where(lane < n_pages, BPP * lane, BPP * (lane - n_pages) + 1)
    jc = jnp.where(lane == nb, nb, jc)
    jblk = jc[:, 0:nb]
    for a in range(nseq):
        for c in range(ncombo):
            g, t = c // s_new, c % s_new
            kpg = kc_ref[a, 0, g * n_pages:(g + 1) * n_pages, :].astype(BF16)
            vpg = kc_ref[a, 1, g * n_pages:(g + 1) * n_pages, :].astype(BF16)
            qc = q_ref[a, c]
            vis = ((jblk + 1) * BLK - 1) <= (past + t)
            s = jnp.concatenate([_dot_nt(qc, kpg[:, h * HEAD_DIM:(h + 1) * HEAD_DIM])
                                 for h in range(BPP)], axis=-1)
            s = jnp.where(vis, s, NEG)
            m = jnp.max(s, axis=-1, keepdims=True)
            e = jnp.where(vis, jnp.exp(s - m), 0.0)
            l = jnp.sum(e, axis=-1, keepdims=True)
            p = e / jnp.where(l > 0.0, l, 1.0)
            pb = p.astype(BF16)
            ocmp_ref[a, c] = sum(_dot(pb[:, h * n_pages:(h + 1) * n_pages],
                                      vpg[:, h * HEAD_DIM:(h + 1) * HEAD_DIM]) for h in range(BPP))
            row = a * ncombo + c
            imp_buf[row:row + 1, 0:nb] = jnp.sum(p[0:GQA_R, :], axis=0, keepdims=True)

    rows = nseq * ncombo
    imp = imp_buf[...]
    t_row = lax.broadcasted_iota(jnp.int32, (rows, 1), 0) & (s_new - 1)
    cur = (past + t_row) >> BLK_SHIFT
    score = jnp.where(jc > cur, NEG, imp)
    score = jnp.where((jc == 0) | (jc == cur) | (jc == cur - 1), FORCE, score)
    score = jnp.where(lane <= nb, score, -jnp.inf)
    jcf = jc.astype(F32)
    out_lane = lax.broadcasted_iota(jnp.int32, (1, LANES), 1)
    idx = jnp.zeros((rows, LANES), jnp.int32)
    for k in range(N_SEL):
        m = jnp.max(score, axis=-1, keepdims=True)
        first = jnp.min(jnp.where(score == m, jcf, 1e9), axis=-1, keepdims=True)
        idx = jnp.where(out_lane == k, first.astype(jnp.int32), idx)
        score = jnp.where(jcf == first, -jnp.inf, score)
    idx_ref[...] = idx


def _dec_cmp(q_pad, kc, past, s_new):
    nseq, ncombo = q_pad.shape[0], q_pad.shape[1]
    rows_kc = kc.shape[2]
    nb = rows_kc // N_KV * BPP
    ns = DEC_SEQ_PER_STEP
    width = (nb + 1 + LANES - 1) // LANES * LANES
    return pl.pallas_call(
        functools.partial(_dec_cmp_body, past=past, s_new=s_new),
        grid=(nseq // ns,),
        in_specs=[
            pl.BlockSpec((ns, ncombo, Q_PAD, HEAD_DIM), lambda i: (i, 0, 0, 0)),
            pl.BlockSpec((ns, 2, rows_kc, PAGE_SIZE), lambda i: (i, 0, 0, 0)),
        ],
        out_specs=[
            pl.BlockSpec((ns, ncombo, Q_PAD, HEAD_DIM), lambda i: (i, 0, 0, 0)),
            pl.BlockSpec((ns * ncombo, LANES), lambda i: (i, 0)),
        ],
        out_shape=[jax.ShapeDtypeStruct((nseq, ncombo, Q_PAD, HEAD_DIM), F32),
                   jax.ShapeDtypeStruct((nseq * ncombo, LANES), jnp.int32)],
        scratch_shapes=[pltpu.VMEM((ns * ncombo, width), F32)],
        compiler_params=_cparams(("arbitrary",)),
        name="dec_cmp_select",
    )(q_pad, kc)


def _dec_sel_win_body(idx_ref, pt_ref, q_ref, pool_ref, newblk_ref, win_ref, kvw_new_ref,
                      osel_ref, owin_ref, kbuf, vbuf, sem, *, past, s_new, n_pages):
    ncombo = q_ref.shape[0]
    n_past_blk = past // BLK
    n = pl.program_id(0)
    nseq = pl.num_programs(0)
    slot = n % 2

    def feature_rows(c, part):
        g = c // s_new
        return pl.ds((part * N_KV + g) * HEAD_DIM, HEAD_DIM)

    def block_copy(src_page, c, k, part, sl):
        dst = (kbuf, vbuf)[part].at[sl, c, :, pl.ds(k * PAGE_SIZE, PAGE_SIZE)]
        return pltpu.make_async_copy(src_page.at[feature_rows(c, part)], dst, sem.at[sl])

    def start(seq, sl):
        for c in range(ncombo):
            for k in range(N_SEL):
                j = idx_ref[(seq * ncombo + c) * N_SEL + k]

                @pl.when(j < n_past_blk)
                def _():
                    page = pt_ref[seq * n_pages + jnp.clip(j, 0, n_past_blk - 1) // BPP]
                    for part in range(2):
                        block_copy(pool_ref.at[page], c, k, part, sl).start()

                @pl.when(j >= n_past_blk)
                def _():
                    for part in range(2):
                        block_copy(newblk_ref.at[seq], c, k, part, sl).start()

    @pl.when(n == 0)
    def _():
        start(n, slot)

    @pl.when(n + 1 < nseq)
    def _():
        start(n + 1, 1 - slot)

    for c in range(ncombo):
        for k in range(N_SEL):
            for part in range(2):
                block_copy(newblk_ref.at[n], c, k, part, slot).wait()

    nkeys = N_SEL * PAGE_SIZE
    lane = lax.broadcasted_iota(jnp.int32, (1, nkeys), 1)
    row_in_page = lane & (PAGE_SIZE - 1)
    for c in range(ncombo):
        t = c % s_new
        jv = jnp.zeros((1, nkeys), jnp.int32)
        for k in range(N_SEL):
            j = idx_ref[(n * ncombo + c) * N_SEL + k]
            jv = jnp.where((lane >> PAGE_SHIFT) == k, j, jv)
        kpos = jv * BLK + (row_in_page & (BLK - 1))
        half = jnp.where(jv >= n_past_blk, 0, jv & (BPP - 1))
        valid = ((row_in_page >> BLK_SHIFT) == half) & (kpos <= past + t)
        s = jnp.where(valid, _dot(q_ref[c], kbuf[slot, c].astype(BF16)), NEG)
        m = jnp.max(s, axis=-1, keepdims=True)
        e = jnp.exp(s - m)
        l = jnp.sum(e, axis=-1, keepdims=True)
        osel_ref[c] = _dot_nt(e.astype(BF16), vbuf[slot, c].astype(BF16)) / l

    nwin = win_ref.shape[1]
    npad = kvw_new_ref.shape[1]
    wpos = past - nwin + lax.broadcasted_iota(jnp.int32, (1, nwin), 1)
    npos = past + lax.broadcasted_iota(jnp.int32, (1, npad), 1)
    new_real = lax.broadcasted_iota(jnp.int32, (1, npad), 1) < s_new
    for c in range(ncombo):
        t = c % s_new
        krows, vrows = feature_rows(c, 0), feature_rows(c, 1)
        qc = q_ref[c]
        dw = (past + t) - wpos
        dn = (past + t) - npos
        valid_w = (dw >= 0) & (dw < WINDOW) & (wpos >= 0)
        valid_n = (dn >= 0) & (dn < WINDOW) & new_real
        sw = jnp.where(valid_w, _dot(qc, win_ref[krows, :].astype(BF16)), NEG)
        sn = jnp.where(valid_n, _dot(qc, kvw_new_ref[krows, :].astype(BF16)), NEG)
        m = jnp.maximum(jnp.max(sw, axis=-1, keepdims=True), jnp.max(sn, axis=-1, keepdims=True))
        ew = jnp.exp(sw - m)
        en = jnp.exp(sn - m)
        l = jnp.sum(ew, axis=-1, keepdims=True) + jnp.sum(en, axis=-1, keepdims=True)
        o = (_dot_nt(ew.astype(BF16), win_ref[vrows, :].astype(BF16))
             + _dot_nt(en.astype(BF16), kvw_new_ref[vrows, :].astype(BF16)))
        owin_ref[c] = o / l


def _dec_sel_win(idx_flat, pt_flat, q_pad, pool_t, newblk_t, win_t, kvw_new_t, past, s_new, n_pages):
    nseq, ncombo = q_pad.shape[0], q_pad.shape[1]
    nwin = win_t.shape[2]
    per_seq = lambda a, idx, pt: (a, 0, 0, 0)
    per_seq3 = lambda a, idx, pt: (a, 0, 0)
    grid_spec = pltpu.PrefetchScalarGridSpec(
        num_scalar_prefetch=2,
        grid=(nseq,),
        in_specs=[
            pl.BlockSpec((None, ncombo, Q_PAD, HEAD_DIM), per_seq),
            pl.BlockSpec(memory_space=pl.ANY),
            pl.BlockSpec(memory_space=pl.ANY),
            pl.BlockSpec((None, KV_W, nwin), per_seq3),
            pl.BlockSpec((None, KV_W, PAGE_SIZE), per_seq3),
        ],
        out_specs=[
            pl.BlockSpec((None, ncombo, Q_PAD, HEAD_DIM), per_seq),
            pl.BlockSpec((None, ncombo, Q_PAD, HEAD_DIM), per_seq),
        ],
        scratch_shapes=[pltpu.VMEM((2, ncombo, HEAD_DIM, N_SEL * PAGE_SIZE), F32),
                        pltpu.VMEM((2, ncombo, HEAD_DIM, N_SEL * PAGE_SIZE), F32),
                        pltpu.SemaphoreType.DMA((2,))],
    )
    shape = jax.ShapeDtypeStruct((nseq, ncombo, Q_PAD, HEAD_DIM), F32)
    return pl.pallas_call(
        functools.partial(_dec_sel_win_body, past=past, s_new=s_new, n_pages=n_pages),
        grid_spec=grid_spec,
        out_shape=[shape, shape],
        compiler_params=_cparams(("arbitrary",)),
        name="dec_sel_win",
    )(idx_flat, pt_flat, q_pad, pool_t, newblk_t, win_t, kvw_new_t)


def _out_ln_body(*refs, gated):
    if gated:
        ocmp_ref, osel_ref, owin_ref, gates_ref, yconv_ref, h_ref, wo_ref, g_ref, b_ref, y_ref = refs
        gates = gates_ref[...]
        pieces = []
        for h in range(N_HEADS):
            col = slice(h * HEAD_DIM, (h + 1) * HEAD_DIM)
            pieces.append(gates[:, h:h + 1] * ocmp_ref[:, col]
                          + gates[:, N_HEADS + h:N_HEADS + h + 1] * osel_ref[:, col]
                          + gates[:, 2 * N_HEADS + h:2 * N_HEADS + h + 1] * owin_ref[:, col])
        o = jnp.concatenate(pieces, axis=-1)
    else:
        o_ref, yconv_ref, h_ref, wo_ref, g_ref, b_ref, y_ref = refs
        o = o_ref[...]
    mix = (_dot(o.astype(BF16), wo_ref[0:D_ATTN, :])
           + _dot(yconv_ref[...].astype(BF16), wo_ref[D_ATTN:D_MODEL, :]))
    y_ref[...] = _layer_norm(ALPHA * h_ref[...] + mix, g_ref[...], b_ref[...])


def _out_ln(branches, yconv, h2d, w_out, g, b, tm):
    rows = h2d.shape[0]
    gated = len(branches) > 1
    row = lambda i: (i, 0)
    const = lambda i: (0, 0)
    widths = [D_ATTN, D_ATTN, D_ATTN, LANES] if gated else [D_ATTN]
    return pl.pallas_call(
        functools.partial(_out_ln_body, gated=gated),
        grid=(rows // tm,),
        in_specs=[pl.BlockSpec((tm, w), row) for w in widths] + [
            pl.BlockSpec((tm, D_CONV), row),
            pl.BlockSpec((tm, D_MODEL), row),
            pl.BlockSpec((D_MODEL, D_MODEL), const),
            pl.BlockSpec((1, D_MODEL), const),
            pl.BlockSpec((1, D_MODEL), const),
        ],
        out_specs=pl.BlockSpec((tm, D_MODEL), row),
        out_shape=jax.ShapeDtypeStruct((rows, D_MODEL), F32),
        compiler_params=_cparams(("arbitrary",)),
        name="out_ln",
    )(*branches, yconv, h2d, w_out, g, b)


def _block_diag_groups(w):
    z = jnp.zeros_like(w)
    return jnp.concatenate([jnp.concatenate([w, z], axis=-1),
                            jnp.concatenate([z, w], axis=-1)], axis=-2).astype(BF16)


def _to_combo_layout(x, nseq, s_new):
    x = x.reshape(nseq, s_new, N_KV, GQA_R, HEAD_DIM).transpose(0, 2, 1, 3, 4)
    x = x.reshape(nseq, N_KV * s_new, GQA_R, HEAD_DIM)
    return jnp.pad(x, ((0, 0), (0, 0), (0, Q_PAD - GQA_R), (0, 0)))


def _from_combo_layout(x, nseq, s_new):
    x = x[:, :, :GQA_R, :].reshape(nseq, N_KV, s_new, GQA_R, HEAD_DIM).transpose(0, 2, 1, 3, 4)
    return x.reshape(nseq * s_new, D_ATTN)


def kernel(x_prompt, x_sample, cache_cmp_kv, cache_sel_kv, state_win_kv, state_conv, page_table,
           ffa_gate, ffa_up, ffa_down, ln1_g, ln1_b, w_in, b_in, w_conv, cmp_w1, cmp_w2, cmp_pe,
           w_out, ln2_g, ln2_b, ffb_gate, ffb_up, ffb_down, ln3_g, ln3_b):
    depth = ffa_gate.shape[0]
    assert depth == 1
    l = 0
    n, t, d = x_prompt.shape
    nseq, s_new, _ = x_sample.shape
    n_pages = page_table.shape[1]
    past = n_pages * PAGE_SIZE
    rows_p, rows_s = n * t, nseq * s_new
    kvshape = (2, N_KV, HEAD_DIM)

    bf = lambda w: w.astype(BF16)
    vec = lambda v: v.reshape(1, -1)
    w_main = bf(w_in[l][:, :D_MAIN])
    b_main = vec(b_in[l][:D_MAIN])
    w_gate = bf(jnp.pad(w_in[l][:, D_MAIN:], ((0, 0), (0, LANES - N_GATE))))
    b_gate = vec(jnp.pad(b_in[l][D_MAIN:], (0, LANES - N_GATE)))
    pe_t = jnp.concatenate([cmp_pe[l].transpose(1, 2, 0)] * BPP, axis=-1)
    w1_t = _block_diag_groups(cmp_w1[l].transpose(0, 2, 1, 3)).reshape(
        2, D_PAIRS, 2 * PAGE_SIZE, PAGE_SIZE)
    w2_t = _block_diag_groups(cmp_w2[l])
    w_out_b = bf(w_out[l])
    ffa = (bf(ffa_gate[l]), bf(ffa_up[l]), bf(ffa_down[l]), vec(ln1_g[l]), vec(ln1_b[l]))
    ffb = (bf(ffb_gate[l]), bf(ffb_up[l]), bf(ffb_down[l]), vec(ln3_g[l]), vec(ln3_b[l]))

    def k_pad(x, lo):
        z = jnp.zeros_like(x[..., :HEAD_DIM])
        return jnp.concatenate([x[..., lo:lo + HEAD_DIM], z, z, x[..., lo + HEAD_DIM:lo + 2 * HEAD_DIM]],
                               axis=-1)

    def prompt_cols(x):
        return jnp.concatenate([x[..., :_C_Q], k_pad(x, _C_KVS), k_pad(x, _C_KVW)], axis=-1)

    def prompt_rows(x):
        return jnp.pad(x[..., _C_Q:], [(0, 0)] * (x.ndim - 1) + [(0, GATE_ROWS - N_GATE)])

    w_rows, b_rows = bf(prompt_cols(w_in[l])), vec(prompt_cols(b_in[l]))
    w_tr, b_tr = bf(prompt_rows(w_in[l]).T), prompt_rows(b_in[l]).reshape(-1, 1)

    pps = t // PAGE_SIZE
    cmp_seqs = 4
    hp = _ffn_ln(x_prompt.reshape(rows_p, d), *ffa, tm=512)
    q, k_sel, k_win, kvc_t, kvs_t, kvw_t, gates_t, yconv, cstate = _proj_prompt(
        hp.reshape(n, t, d), w_rows, b_rows, w_tr, b_tr, w_conv[l], tm=512)
    kc = _compress(jnp.zeros((1,), jnp.int32), kvc_t, pe_t, w1_t, w2_t, n_steps=n // cmp_seqs,
                   n_pages=cmp_seqs * pps, pages_per_seq=pps, name="compress_prompt")
    kc = kc.reshape(n // cmp_seqs, 2, N_KV, cmp_seqs, pps, BPP, HEAD_DIM)
    kc = kc.transpose(0, 3, 2, 1, 4, 5, 6).reshape(n, N_KV, 2, pps * BPP, HEAD_DIM)
    zc = jnp.zeros_like(kc[:, 0, 0])
    kc_pad = bf(jnp.stack([jnp.concatenate([kc[:, 0, 0], zc], axis=-1),
                           jnp.concatenate([zc, kc[:, 1, 0]], axis=-1)], axis=1))
    vc_t = bf(jnp.swapaxes(kc[:, :, 1], -1, -2))
    o = _attn_prompt(q, k_sel, k_win, kvs_t, kvw_t, kc_pad, vc_t, gates_t)
    yp = _out_ln([o.reshape(rows_p, D_ATTN)], yconv.reshape(rows_p, D_CONV), hp, w_out_b,
                 vec(ln2_g[l]), vec(ln2_b[l]), tm=512)
    yp = _ffn_ln(yp, *ffb, tm=512).reshape(n, t, d)

    def from_feature_major(x_t):
        return jnp.moveaxis(x_t.reshape(x_t.shape[0], *kvshape, x_t.shape[-1]), -1, 1)[None]

    p_cmp = from_feature_major(kvc_t)
    p_sel = from_feature_major(kvs_t)
    p_win = from_feature_major(kvw_t[:, :, t - min(WINDOW, t):])
    p_conv = cstate[:, 8 - (CONV_W - 1):].reshape(1, n, CONV_W - 1, D_CONV)

    hs = _ffn_ln(x_sample.reshape(rows_s, d), *ffa, tm=rows_s)
    qs, kv3, gates_s, cb_s, u_s = _proj_sample(hs, w_main, b_main, w_gate, b_gate)
    kvc_s, kvs_s, kvw_s = kv3[:, :KV_W], kv3[:, KV_W:2 * KV_W], kv3[:, 2 * KV_W:]
    u3 = u_s.reshape(nseq, s_new, D_CONV)
    yconv_s = _conv_sample(cb_s.reshape(nseq, s_new, D_CONV), u3, state_conv[l], w_conv[l])
    s_conv = jnp.concatenate([state_conv[l], u3], axis=1)[:, -(CONV_W - 1):]

    def feature_major(x):
        return jnp.moveaxis(x, -4, -1).reshape(*x.shape[:-4], KV_W, x.shape[-4])

    def new_rows_t(x2d):
        xt = x2d.reshape(nseq, s_new, KV_W).transpose(0, 2, 1)
        return jnp.pad(xt, ((0, 0), (0, 0), (0, PAGE_SIZE - s_new)))

    kc_s = _compress(page_table.reshape(-1), feature_major(cache_cmp_kv[l]), pe_t, w1_t, w2_t,
                     n_steps=nseq, n_pages=n_pages, pages_per_seq=None, name="compress_decode")
    q_pad = _to_combo_layout(qs, nseq, s_new)
    ocmp_c, idx_pad = _dec_cmp(q_pad, kc_s, past, s_new)
    idx_flat = idx_pad[:, :N_SEL].reshape(-1)
    win_t = feature_major(state_win_kv[l])
    kvw_new_t = new_rows_t(kvw_s)
    osel_c, owin_c = _dec_sel_win(
        idx_flat, page_table.reshape(-1), q_pad, feature_major(cache_sel_kv[l]), new_rows_t(kvs_s),
        win_t, kvw_new_t, past, s_new, n_pages)
    branches = [_from_combo_layout(x, nseq, s_new) for x in (ocmp_c, osel_c, owin_c)]
    ys = _out_ln(branches + [gates_s], yconv_s.reshape(rows_s, D_CONV), hs, w_out_b,
                 vec(ln2_g[l]), vec(ln2_b[l]), tm=rows_s)
    ys = _ffn_ln(ys, *ffb, tm=rows_s).reshape(nseq, s_new, d)
    n_keep = min(WINDOW, past + s_new)
    ctx_t = jnp.concatenate([win_t, kvw_new_t[:, :, :s_new]], axis=-1)[:, :, -n_keep:]
    s_cmp = kvc_s.reshape(1, nseq, s_new, *kvshape)
    s_sel = kvs_s.reshape(1, nseq, s_new, *kvshape)
    s_win = jnp.moveaxis(ctx_t.reshape(nseq, *kvshape, n_keep), -1, 1)[None]
    return (yp, ys, p_cmp, p_sel, p_win, p_conv, s_cmp, s_sel, s_win,
            s_conv.reshape(1, nseq, CONV_W - 1, D_CONV))
```

```python
import functools

import jax
import jax.numpy as jnp
from jax import lax
from jax.experimental import pallas as pl
from jax.experimental.pallas import tpu as pltpu

F32 = jnp.float32
BF16 = jnp.bfloat16

D_MODEL = 1024
D_CONV = 512
D_ATTN = 512
HEAD_DIM = 64
N_HEADS = 8
N_KV = 2
GQA_R = N_HEADS // N_KV
N_BR = 3
CONV_W = 3
BLK = 64
BLK_SHIFT = 6
N_SEL = 16
WINDOW = 512
PAGE_SIZE = 128
PAGE_SHIFT = 7
D_FF = 2816
KV_W = 2 * N_KV * HEAD_DIM
D_MAIN = 3 * D_CONV + D_ATTN + 3 * KV_W
N_GATE = N_BR * N_HEADS
ALPHA = 2.0 ** 0.25
SCALE = HEAD_DIM ** -0.5
LN_EPS = 1e-5
NEG = -1e30
FORCE = 1e4

LANES = 128
VMEM_LIMIT_BYTES = 56 * 1024 * 1024


def _cparams(sem):
    return pltpu.CompilerParams(dimension_semantics=sem, vmem_limit_bytes=VMEM_LIMIT_BYTES)


def _layer_norm(y, g, b):
    mu = jnp.mean(y, axis=-1, keepdims=True)
    yc = y - mu
    var = jnp.mean(yc * yc, axis=-1, keepdims=True)
    return yc * lax.rsqrt(var + LN_EPS) * g + b


def _dot(a, b):
    return jnp.dot(a, b, preferred_element_type=F32)


def _dot_nt(a, b):
    return lax.dot_general(a, b, (((1,), (1,)), ((), ())), preferred_element_type=F32)


FFN_CHUNK = 256


def _ffn_ln_body(x_ref, wg_ref, wu_ref, wd_ref, g_ref, b_ref, o_ref):
    x = x_ref[...]
    xb = x.astype(BF16)
    acc = jnp.zeros(x.shape, F32)
    for c in range(D_FF // FFN_CHUNK):
        sl = slice(c * FFN_CHUNK, (c + 1) * FFN_CHUNK)
        gg = _dot(xb, wg_ref[:, sl])
        uu = _dot(xb, wu_ref[:, sl])
        hh = (gg * jax.nn.sigmoid(gg) * uu).astype(BF16)
        acc = acc + _dot(hh, wd_ref[sl, :])
    o_ref[...] = _layer_norm(ALPHA * x + 0.5 * acc, g_ref[...], b_ref[...])


def _ffn_ln(x2d, wg, wu, wd, g, b, tm):
    rows = x2d.shape[0]
    const = lambda i: (0, 0)
    return pl.pallas_call(
        _ffn_ln_body,
        grid=(rows // tm,),
        in_specs=[
            pl.BlockSpec((tm, D_MODEL), lambda i: (i, 0)),
            pl.BlockSpec((D_MODEL, D_FF), const),
            pl.BlockSpec((D_MODEL, D_FF), const),
            pl.BlockSpec((D_FF, D_MODEL), const),
            pl.BlockSpec((1, D_MODEL), const),
            pl.BlockSpec((1, D_MODEL), const),
        ],
        out_specs=pl.BlockSpec((tm, D_MODEL), lambda i: (i, 0)),
        out_shape=jax.ShapeDtypeStruct((rows, D_MODEL), F32),
        compiler_params=_cparams(("arbitrary",)),
        name="ffn_ln",
    )(x2d, wg, wu, wd, g, b)


_C_CB, _C_CC, _C_CH, _C_Q = 0, D_CONV, 2 * D_CONV, 3 * D_CONV
_C_KVC = 3 * D_CONV + D_ATTN
_C_KVS = _C_KVC + KV_W
_C_KVW = _C_KVS + KV_W


def _proj_piece(hb, w_ref, b_ref, lo, width):
    return _dot(hb, w_ref[:, lo:lo + width]) + b_ref[:, lo:lo + width]


_P_KS = 3 * D_CONV
_P_KW = _P_KS + KV_W
P_ROW_COLS = _P_KW + KV_W
GATE_ROWS = 32
_T_KV = D_ATTN
_T_GATE = _T_KV + 3 * KV_W
P_T_ROWS = _T_GATE + GATE_ROWS


def _proj_prompt_body(h_ref, w_ref, b_ref, wt_ref, bt_ref, wc_ref,
                      qt_ref, ks_ref, kw_ref, kvct_ref, kvst_ref, kvwt_ref, gates_ref,
                      yconv_ref, cstate_ref, ubuf):
    i = pl.program_id(1)
    tm = h_ref.shape[0]
    hb = h_ref[...].astype(BF16)

    def piece_t(lo, height):
        return _dot_nt(wt_ref[lo:lo + height, :], hb) + bt_ref[lo:lo + height, :]

    qt_ref[...] = (piece_t(0, D_ATTN) * (SCALE * LOG2E)).astype(BF16)
    lane = lax.broadcasted_iota(jnp.int32, (1, KV_W), 1)
    blk = (i * tm + lax.broadcasted_iota(jnp.int32, (tm, 1), 0)) >> BLK_SHIFT
    aux0, aux1 = HEAD_DIM, 2 * HEAD_DIM
    one = (lane == aux0 + AUX_BLOCKS) | (lane == aux1 + AUX_BLOCKS)
    ks = _proj_piece(hb, w_ref, b_ref, _P_KS, KV_W)
    ks_ref[...] = jnp.where(one | (lane == aux0 + blk) | (lane == aux1 + blk), 1.0, ks).astype(BF16)
    kw = _proj_piece(hb, w_ref, b_ref, _P_KW, KV_W)
    kw_ref[...] = jnp.where(one, 1.0, kw).astype(BF16)
    for k, ref in enumerate((kvct_ref, kvst_ref, kvwt_ref)):
        ref[...] = piece_t(_T_KV + k * KV_W, KV_W)
    gates_ref[...] = jax.nn.sigmoid(piece_t(_T_GATE, GATE_ROWS))
    cb = _proj_piece(hb, w_ref, b_ref, _C_CB, D_CONV)
    u = _proj_piece(hb, w_ref, b_ref, _C_CC, D_CONV) * _proj_piece(hb, w_ref, b_ref, _C_CH, D_CONV)

    @pl.when(i == 0)
    def _():
        ubuf[0:8, :] = jnp.zeros((8, D_CONV), F32)

    ubuf[8:8 + tm, :] = u
    u1 = ubuf[7:7 + tm, :]
    u2 = ubuf[6:6 + tm, :]
    z = u2 * wc_ref[0:1, :] + u1 * wc_ref[1:2, :] + u * wc_ref[2:3, :]
    yconv_ref[...] = cb * z
    tail = u[tm - 8:tm, :]
    ubuf[0:8, :] = tail
    cstate_ref[...] = tail


def _proj_prompt(h, w_rows, b_rows, w_t, b_t, w_conv, tm):
    n, t, _ = h.shape
    const = lambda a, i: (0, 0)
    row = lambda a, i: (a, i, 0)
    col = lambda a, i: (a, 0, i)

    def out(width, dtype=F32):
        return jax.ShapeDtypeStruct((n, t, width), dtype)

    def out_t(rows, dtype=F32):
        return jax.ShapeDtypeStruct((n, rows, t), dtype)

    return pl.pallas_call(
        _proj_prompt_body,
        grid=(n, t // tm),
        in_specs=[
            pl.BlockSpec((None, tm, D_MODEL), row),
            pl.BlockSpec((D_MODEL, P_ROW_COLS), const),
            pl.BlockSpec((1, P_ROW_COLS), const),
            pl.BlockSpec((P_T_ROWS, D_MODEL), const),
            pl.BlockSpec((P_T_ROWS, 1), const),
            pl.BlockSpec((CONV_W, D_CONV), const),
        ],
        out_specs=[
            pl.BlockSpec((None, D_ATTN, tm), col),
            pl.BlockSpec((None, tm, KV_W), row),
            pl.BlockSpec((None, tm, KV_W), row),
            pl.BlockSpec((None, KV_W, tm), col),
            pl.BlockSpec((None, KV_W, tm), col),
            pl.BlockSpec((None, KV_W, tm), col),
            pl.BlockSpec((None, GATE_ROWS, tm), col),
            pl.BlockSpec((None, tm, D_CONV), row),
            pl.BlockSpec((None, 8, D_CONV), lambda a, i: (a, 0, 0)),
        ],
        out_shape=[out_t(D_ATTN, BF16), out(KV_W, BF16), out(KV_W, BF16), out_t(KV_W), out_t(KV_W),
                   out_t(KV_W), out_t(GATE_ROWS), out(D_CONV), jax.ShapeDtypeStruct((n, 8, D_CONV), F32)],
        scratch_shapes=[pltpu.VMEM((tm + 8, D_CONV), F32)],
        compiler_params=_cparams(("arbitrary", "arbitrary")),
        name="proj_prompt",
    )(h, w_rows, b_rows, w_t, b_t, w_conv)


def _proj_sample_body(h_ref, w_ref, b_ref, wgt_ref, bgt_ref, q_ref, kv_ref, gates_ref, cb_ref, u_ref):
    hb = h_ref[...].astype(BF16)
    q_ref[...] = (_proj_piece(hb, w_ref, b_ref, _C_Q, D_ATTN) * SCALE).astype(BF16)
    kv_ref[...] = _proj_piece(hb, w_ref, b_ref, _C_KVC, 3 * KV_W)
    gates_ref[...] = jax.nn.sigmoid(_dot(hb, wgt_ref[...]) + bgt_ref[...])
    cb_ref[...] = _proj_piece(hb, w_ref, b_ref, _C_CB, D_CONV)
    u_ref[...] = _proj_piece(hb, w_ref, b_ref, _C_CC, D_CONV) * _proj_piece(hb, w_ref, b_ref, _C_CH, D_CONV)


def _proj_sample(h2d, w_main, b_main, w_gate, b_gate):
    rows = h2d.shape[0]

    def out(width, dtype=F32):
        return jax.ShapeDtypeStruct((rows, width), dtype)

    return pl.pallas_call(
        _proj_sample_body,
        out_shape=[out(D_ATTN, BF16), out(3 * KV_W), out(LANES), out(D_CONV), out(D_CONV)],
        compiler_params=pltpu.CompilerParams(vmem_limit_bytes=VMEM_LIMIT_BYTES),
        name="proj_sample",
    )(h2d, w_main, b_main, w_gate, b_gate)


def _conv_sample_body(cb_ref, u_ref, prev_ref, wc_ref, y_ref):
    s = u_ref.shape[1]
    ext = [prev_ref[:, 0, :], prev_ref[:, 1, :]] + [u_ref[:, t, :] for t in range(s)]
    for t in range(s):
        z = ext[t] * wc_ref[0:1, :] + ext[t + 1] * wc_ref[1:2, :] + ext[t + 2] * wc_ref[2:3, :]
        y_ref[:, t, :] = cb_ref[:, t, :] * z


def _conv_sample(cb, u, prev, w_conv):
    return pl.pallas_call(
        _conv_sample_body,
        out_shape=jax.ShapeDtypeStruct(u.shape, F32),
        name="conv_sample",
    )(cb, u, prev, w_conv)


BPP = PAGE_SIZE // BLK
D_PAIRS = HEAD_DIM // 2


def _compress_body(pt_ref, src_ref, pe_ref, w1_ref, w2_ref, kc_ref, buf, sem, *, n_pages, pages_per_seq):
    n = pl.program_id(0)
    slot = n % 2

    def page_copy(step, j, sl):
        dst = buf.at[sl, :, j, :]
        if pages_per_seq is None:
            src = src_ref.at[pt_ref[step * n_pages + j]]
        else:
            seq = step * (n_pages // pages_per_seq) + j // pages_per_seq
            src = src_ref.at[seq, :, pl.ds((j % pages_per_seq) * PAGE_SIZE, PAGE_SIZE)]
        return pltpu.make_async_copy(src, dst, sem.at[sl])

    def start(step, sl):
        for j in range(n_pages):
            page_copy(step, j, sl).start()

    @pl.when(n == 0)
    def _():
        start(n, slot)

    @pl.when(n + 1 < pl.num_programs(0))
    def _():
        start(n + 1, 1 - slot)

    for j in range(n_pages):
        page_copy(n, j, slot).wait()

    def feature_rows(f):
        return buf[slot, f]

    for c in range(2):
        acc = None
        for dp in range(D_PAIRS):
            rows = []
            for g in range(N_KV):
                f0 = (c * N_KV + g) * HEAD_DIM + 2 * dp
                rows.append(jnp.concatenate(
                    [feature_rows(f0) + pe_ref[c, 2 * dp:2 * dp + 1, :],
                     feature_rows(f0 + 1) + pe_ref[c, 2 * dp + 1:2 * dp + 2, :]], axis=-1))
            x = jnp.concatenate(rows, axis=0).astype(BF16)
            d = _dot(x, w1_ref[c, dp])
            acc = d if acc is None else acc + d
        hid = (acc * jax.nn.sigmoid(acc)).astype(BF16)
        kc_ref[c] = _dot(hid, w2_ref[c])


def _compress(pt_flat, src, pe_t, w1_t, w2_t, n_steps, n_pages, pages_per_seq, name):
    const3 = lambda a, pt: (0, 0, 0)
    grid_spec = pltpu.PrefetchScalarGridSpec(
        num_scalar_prefetch=1,
        grid=(n_steps,),
        in_specs=[
            pl.BlockSpec(memory_space=pl.ANY),
            pl.BlockSpec((2, HEAD_DIM, PAGE_SIZE), const3),
            pl.BlockSpec((2, D_PAIRS, 2 * PAGE_SIZE, PAGE_SIZE), lambda a, pt: (0, 0, 0, 0)),
            pl.BlockSpec((2, PAGE_SIZE, PAGE_SIZE), const3),
        ],
        out_specs=pl.BlockSpec((None, 2, N_KV * n_pages, PAGE_SIZE), lambda a, pt: (a, 0, 0, 0)),
        scratch_shapes=[pltpu.VMEM((2, KV_W, n_pages, PAGE_SIZE), F32),
                        pltpu.SemaphoreType.DMA((2,))],
    )
    return pl.pallas_call(
        functools.partial(_compress_body, n_pages=n_pages, pages_per_seq=pages_per_seq),
        grid_spec=grid_spec,
        out_shape=jax.ShapeDtypeStruct((n_steps, 2, N_KV * n_pages, PAGE_SIZE), F32),
        compiler_params=_cparams(("arbitrary",)),
        name=name,
    )(pt_flat, src, pe_t, w1_t, w2_t)


ATT_TQ = 256
ATT_TK = 256
WIN_TILES = WINDOW // ATT_TK + 1
AUX_BLOCKS = 32
LOG2E = 1.4426950408889634


V_ROWS = HEAD_DIM + 16


def _attn_update(s_t, v_t, state):
    m, acc = state
    m_new = jnp.maximum(m, jnp.max(s_t, axis=0, keepdims=True))
    e = jnp.exp2(s_t - m_new)
    acc = jnp.exp2(m - m_new) * acc + _dot(v_t, e.astype(BF16))
    return m_new, acc


def _attn_prompt_body(qt_ref, ks_ref, kw_ref, vs_ref, vw_ref, kc_ref, vct_ref, gates_ref, o_ref,
                      winbias):
    tq, tk = ATT_TQ, ATT_TK
    qi = pl.program_id(1)
    nb = kc_ref.shape[1]
    t0 = qi * tq
    kd = qi
    tpos = t0 + lax.broadcasted_iota(jnp.int32, (1, tq), 1)
    jrow = lax.broadcasted_iota(jnp.int32, (nb, 1), 0)
    krow = lax.broadcasted_iota(jnp.int32, (tk, 1), 0)
    vis = ((jrow + 1) * BLK - 1) <= tpos
    cur = tpos >> BLK_SHIFT

    for slot, back in enumerate((0, WIN_TILES - 1)):
        kpos = (kd - back) * tk + krow
        dist = tpos - kpos
        ok = (dist >= 0) & (dist < WINDOW) & (kpos >= 0)
        winbias[slot] = jnp.where(ok, 0.0, NEG)

    def heads_of(x):
        return jnp.concatenate([x] * GQA_R, axis=1)

    def group_queries(g, block_bias=None, switch=None):
        zeros = jnp.zeros((AUX_BLOCKS, tq), BF16)
        top = zeros if block_bias is None else block_bias
        low = zeros if switch is None else jnp.broadcast_to(switch, (AUX_BLOCKS, tq)).astype(BF16)
        aux = jnp.concatenate([top, low], axis=0)
        pieces = []
        for r in range(GQA_R):
            h = g * GQA_R + r
            qh = qt_ref[h * HEAD_DIM:(h + 1) * HEAD_DIM, :]
            pieces.append(jnp.concatenate([qh, aux] if g == 0 else [aux, qh], axis=0))
        return jnp.concatenate(pieces, axis=1)

    q4 = [group_queries(g) for g in range(N_KV)]
    vis4 = heads_of(vis)

    kt_rem = jnp.maximum(kd - 1, 0)
    rem_switch = jnp.where((kd & 1) == 1, 0.0, NEG)

    o_cmp, q4_sel, q4_rem = [], [], []
    for g in range(N_KV):
        s = jnp.where(vis4, _dot(kc_ref[g], q4[g]), NEG)
        m = jnp.max(s, axis=0, keepdims=True)
        e = jnp.where(vis4, jnp.exp2(s - m), 0.0)
        l = jnp.sum(e, axis=0, keepdims=True)
        p = e / jnp.where(l > 0.0, l, 1.0)
        imp = sum(p[:, r * tq:(r + 1) * tq] for r in range(GQA_R))
        o_cmp.append(_dot(vct_ref[g], p.astype(BF16)))

        score = jnp.where(jrow > cur, NEG, imp)
        score = jnp.where((jrow == 0) | (jrow == cur) | (jrow == cur - 1), FORCE, score)
        rank = jnp.zeros((nb, tq), F32)
        for j2 in range(nb):
            row = score[j2:j2 + 1, :]
            tie = jnp.where(jrow > j2, 1.0, 0.0)
            rank = rank + jnp.where(row > score, 1.0, jnp.where(row == score, tie, 0.0))
        block_bias = jnp.where(rank < N_SEL, 0.0, NEG).astype(BF16)
        q4_sel.append(group_queries(g, block_bias))
        q4_rem.append(group_queries(g, block_bias, rem_switch))

    def key_tile(k_ref, kt, g):
        return k_ref[pl.ds(pl.multiple_of(kt * tk, tk), tk), g * LANES:(g + 1) * LANES]

    ones_rows = jnp.ones((V_ROWS - HEAD_DIM, tk), BF16)

    def val_tile(v_ref, kt, g):
        v = v_ref[g * HEAD_DIM:(g + 1) * HEAD_DIM, pl.ds(pl.multiple_of(kt * tk, tk), tk)]
        return jnp.concatenate([v.astype(BF16), ones_rows], axis=0)

    wide = GQA_R * tq
    init = (jnp.full((1, wide), NEG, F32), jnp.zeros((V_ROWS, wide), F32))

    def win_scores(back, g):
        kt = jnp.maximum(kd - back, 0)
        if back in (0, WIN_TILES - 1):
            return _dot(key_tile(kw_ref, kt, g), q4[g]) + heads_of(winbias[min(back, 1)])
        q_sw = group_queries(g, None, jnp.where(kd >= back, 0.0, NEG))
        return _dot(key_tile(kw_ref, kt, g), q_sw)

    groups = range(N_KV)
    s_diag = [_dot(key_tile(ks_ref, kd, g), q4_sel[g]) + heads_of(winbias[0]) for g in groups]
    s_rem = [_dot(key_tile(ks_ref, kt_rem, g), q4_rem[g]) for g in groups]
    s_win = [[win_scores(back, g) for g in groups] for back in range(WIN_TILES)]
    sel = tuple(_attn_update(s_diag[g], val_tile(vs_ref, kd, g), init) for g in groups)
    sel = tuple(_attn_update(s_rem[g], val_tile(vs_ref, kt_rem, g), sel[g]) for g in groups)
    win = [init] * N_KV
    for back in range(WIN_TILES):
        kt = jnp.maximum(kd - back, 0)
        win = [_attn_update(s_win[back][g], val_tile(vw_ref, kt, g), win[g]) for g in groups]

    def sel_pair(j, states):
        kts = (2 * j, 2 * j + 1)
        s = [[_dot(key_tile(ks_ref, kt, g), q4_sel[g]) for g in groups] for kt in kts]
        for i, kt in enumerate(kts):
            states = tuple(_attn_update(s[i][g], val_tile(vs_ref, kt, g), states[g]) for g in groups)
        return states

    sel = lax.fori_loop(0, kd >> 1, sel_pair, sel)

    gates = gates_ref[...]
    outs = []
    for h in range(N_HEADS):
        g, r = h // GQA_R, h % GQA_R
        cols = slice(r * tq, (r + 1) * tq)
        o_sel = sel[g][1][:HEAD_DIM, cols] * (1.0 / sel[g][1][HEAD_DIM:HEAD_DIM + 1, cols])
        o_win = win[g][1][:HEAD_DIM, cols] * (1.0 / win[g][1][HEAD_DIM:HEAD_DIM + 1, cols])
        outs.append(gates[h:h + 1, :] * o_cmp[g][:, cols] + gates[N_HEADS + h:N_HEADS + h + 1, :] * o_sel
                    + gates[2 * N_HEADS + h:2 * N_HEADS + h + 1, :] * o_win)
    o_ref[...] = jnp.concatenate(outs, axis=0).T


def _attn_prompt(q_t, k_sel, k_win, kvs_t, kvw_t, kc_pad, vc_t, gates_t):
    n, _, t = q_t.shape
    nb = kc_pad.shape[2]
    row = lambda a, i: (a, i, 0)
    seq = lambda a, i: (a, 0, 0)
    seq4 = lambda a, i: (a, 0, 0, 0)
    v_rows = lambda a, i: (a, 1, 0)
    return pl.pallas_call(
        _attn_prompt_body,
        grid=(n, t // ATT_TQ),
        in_specs=[
            pl.BlockSpec((None, D_ATTN, ATT_TQ), lambda a, i: (a, 0, i)),
            pl.BlockSpec((None, t, KV_W), seq),
            pl.BlockSpec((None, t, KV_W), seq),
            pl.BlockSpec((None, N_KV * HEAD_DIM, t), v_rows),
            pl.BlockSpec((None, N_KV * HEAD_DIM, t), v_rows),
            pl.BlockSpec((None, N_KV, nb, LANES), seq4),
            pl.BlockSpec((None, N_KV, HEAD_DIM, nb), seq4),
            pl.BlockSpec((None, GATE_ROWS, ATT_TQ), lambda a, i: (a, 0, i)),
        ],
        out_specs=pl.BlockSpec((None, ATT_TQ, D_ATTN), row),
        out_shape=jax.ShapeDtypeStruct((n, t, D_ATTN), F32),
        scratch_shapes=[pltpu.VMEM((2, ATT_TK, ATT_TQ), F32)],
        compiler_params=_cparams(("arbitrary", "arbitrary")),
        name="attn_prompt",
    )(q_t, k_sel, k_win, kvs_t, kvw_t, kc_pad, vc_t, gates_t)


Q_PAD = 16
DEC_SEQ_PER_STEP = 8


def _dec_cmp_body(q_ref, kc_ref, ocmp_ref, idx_ref, imp_buf, *, past, s_new):
    nseq, ncombo = q_ref.shape[0], q_ref.shape[1]
    n_pages = kc_ref.shape[2] // N_KV
    nb = n_pages * BPP
    width = imp_buf.shape[1]
    imp_buf[...] = jnp.zeros(imp_buf.shape, F32)
    lane = lax.broadcasted_iota(jnp.int32, (1, width), 1)
    jc = jnp.where(lane < n_pages, BPP * lane, BPP * (lane - n_pages) + 1)
    jc = jnp.where(lane == nb, nb, jc)
    jblk = jc[:, 0:nb]
    for a in range(nseq):
        for c in range(ncombo):
            g, t = c // s_new, c % s_new
            kpg = kc_ref[a, 0, g * n_pages:(g + 1) * n_pages, :].astype(BF16)
            vpg = kc_ref[a, 1, g * n_pages:(g + 1) * n_pages, :].astype(BF16)
            qc = q_ref[a, c]
            vis = ((jblk + 1) * BLK - 1) <= (past + t)
            s = jnp.concatenate([_dot_nt(qc, kpg[:, h * HEAD_DIM:(h + 1) * HEAD_DIM])
                                 for h in range(BPP)], axis=-1)
            s = jnp.where(vis, s, NEG)
            m = jnp.max(s, axis=-1, keepdims=True)
            e = jnp.where(vis, jnp.exp(s - m), 0.0)
            l = jnp.sum(e, axis=-1, keepdims=True)
            p = e / jnp.where(l > 0.0, l, 1.0)
            pb = p.astype(BF16)
            ocmp_ref[a, c] = sum(_dot(pb[:, h * n_pages:(h + 1) * n_pages],
                                      vpg[:, h * HEAD_DIM:(h + 1) * HEAD_DIM]) for h in range(BPP))
            row = a * ncombo + c
            imp_buf[row:row + 1, 0:nb] = jnp.sum(p[0:GQA_R, :], axis=0, keepdims=True)

    rows = nseq * ncombo
    imp = imp_buf[...]
    t_row = lax.broadcasted_iota(jnp.int32, (rows, 1), 0) & (s_new - 1)
    cur = (past + t_row) >> BLK_SHIFT
    score = jnp.where(jc > cur, NEG, imp)
    score = jnp.where((jc == 0) | (jc == cur) | (jc == cur - 1), FORCE, score)
    score = jnp.where(lane <= nb, score, -jnp.inf)
    jcf = jc.astype(F32)
    out_lane = lax.broadcasted_iota(jnp.int32, (1, LANES), 1)
    idx = jnp.zeros((rows, LANES), jnp.int32)
    for k in range(N_SEL):
        m = jnp.max(score, axis=-1, keepdims=True)
        first = jnp.min(jnp.where(score == m, jcf, 1e9), axis=-1, keepdims=True)
        idx = jnp.where(out_lane == k, first.astype(jnp.int32), idx)
        score = jnp.where(jcf == first, -jnp.inf, score)
    idx_ref[...] = idx


def _dec_cmp(q_pad, kc, past, s_new):
    nseq, ncombo = q_pad.shape[0], q_pad.shape[1]
    rows_kc = kc.shape[2]
    nb = rows_kc // N_KV * BPP
    ns = DEC_SEQ_PER_STEP
    width = (nb + 1 + LANES - 1) // LANES * LANES
    return pl.pallas_call(
        functools.partial(_dec_cmp_body, past=past, s_new=s_new),
        grid=(nseq // ns,),
        in_specs=[
            pl.BlockSpec((ns, ncombo, Q_PAD, HEAD_DIM), lambda i: (i, 0, 0, 0)),
            pl.BlockSpec((ns, 2, rows_kc, PAGE_SIZE), lambda i: (i, 0, 0, 0)),
        ],
        out_specs=[
            pl.BlockSpec((ns, ncombo, Q_PAD, HEAD_DIM), lambda i: (i, 0, 0, 0)),
            pl.BlockSpec((ns * ncombo, LANES), lambda i: (i, 0)),
        ],
        out_shape=[jax.ShapeDtypeStruct((nseq, ncombo, Q_PAD, HEAD_DIM), F32),
                   jax.ShapeDtypeStruct((nseq * ncombo, LANES), jnp.int32)],
        scratch_shapes=[pltpu.VMEM((ns * ncombo, width), F32)],
        compiler_params=_cparams(("arbitrary",)),
        name="dec_cmp_select",
    )(q_pad, kc)


def _dec_sel_win_body(idx_ref, pt_ref, q_ref, pool_ref, newblk_ref, win_ref, kvw_new_ref,
                      osel_ref, owin_ref, kbuf, vbuf, sem, *, past, s_new, n_pages):
    ncombo = q_ref.shape[0]
    n_past_blk = past // BLK
    n = pl.program_id(0)
    nseq = pl.num_programs(0)
    slot = n % 2

    def feature_rows(c, part):
        g = c // s_new
        return pl.ds((part * N_KV + g) * HEAD_DIM, HEAD_DIM)

    def block_copy(src_page, c, k, part, sl):
        dst = (kbuf, vbuf)[part].at[sl, c, :, pl.ds(k * PAGE_SIZE, PAGE_SIZE)]
        return pltpu.make_async_copy(src_page.at[feature_rows(c, part)], dst, sem.at[sl])

    def start(seq, sl):
        for c in range(ncombo):
            for k in range(N_SEL):
                j = idx_ref[(seq * ncombo + c) * N_SEL + k]

                @pl.when(j < n_past_blk)
                def _():
                    page = pt_ref[seq * n_pages + jnp.clip(j, 0, n_past_blk - 1) // BPP]
                    for part in range(2):
                        block_copy(pool_ref.at[page], c, k, part, sl).start()

                @pl.when(j >= n_past_blk)
                def _():
                    for part in range(2):
                        block_copy(newblk_ref.at[seq], c, k, part, sl).start()

    @pl.when(n == 0)
    def _():
        start(n, slot)

    @pl.when(n + 1 < nseq)
    def _():
        start(n + 1, 1 - slot)

    for c in range(ncombo):
        for k in range(N_SEL):
            for part in range(2):
                block_copy(newblk_ref.at[n], c, k, part, slot).wait()

    nkeys = N_SEL * PAGE_SIZE
    lane = lax.broadcasted_iota(jnp.int32, (1, nkeys), 1)
    row_in_page = lane & (PAGE_SIZE - 1)
    for c in range(ncombo):
        t = c % s_new
        jv = jnp.zeros((1, nkeys), jnp.int32)
        for k in range(N_SEL):
            j = idx_ref[(n * ncombo + c) * N_SEL + k]
            jv = jnp.where((lane >> PAGE_SHIFT) == k, j, jv)
        kpos = jv * BLK + (row_in_page & (BLK - 1))
        half = jnp.where(jv >= n_past_blk, 0, jv & (BPP - 1))
        valid = ((row_in_page >> BLK_SHIFT) == half) & (kpos <= past + t)
        s = jnp.where(valid, _dot(q_ref[c], kbuf[slot, c].astype(BF16)), NEG)
        m = jnp.max(s, axis=-1, keepdims=True)
        e = jnp.exp(s - m)
        l = jnp.sum(e, axis=-1, keepdims=True)
        osel_ref[c] = _dot_nt(e.astype(BF16), vbuf[slot, c].astype(BF16)) / l

    nwin = win_ref.shape[1]
    npad = kvw_new_ref.shape[1]
    wpos = past - nwin + lax.broadcasted_iota(jnp.int32, (1, nwin), 1)
    npos = past + lax.broadcasted_iota(jnp.int32, (1, npad), 1)
    new_real = lax.broadcasted_iota(jnp.int32, (1, npad), 1) < s_new
    for c in range(ncombo):
        t = c % s_new
        krows, vrows = feature_rows(c, 0), feature_rows(c, 1)
        qc = q_ref[c]
        dw = (past + t) - wpos
        dn = (past + t) - npos
        valid_w = (dw >= 0) & (dw < WINDOW) & (wpos >= 0)
        valid_n = (dn >= 0) & (dn < WINDOW) & new_real
        sw = jnp.where(valid_w, _dot(qc, win_ref[krows, :].astype(BF16)), NEG)
        sn = jnp.where(valid_n, _dot(qc, kvw_new_ref[krows, :].astype(BF16)), NEG)
        m = jnp.maximum(jnp.max(sw, axis=-1, keepdims=True), jnp.max(sn, axis=-1, keepdims=True))
        ew = jnp.exp(sw - m)
        en = jnp.exp(sn - m)
        l = jnp.sum(ew, axis=-1, keepdims=True) + jnp.sum(en, axis=-1, keepdims=True)
        o = (_dot_nt(ew.astype(BF16), win_ref[vrows, :].astype(BF16))
             + _dot_nt(en.astype(BF16), kvw_new_ref[vrows, :].astype(BF16)))
        owin_ref[c] = o / l


def _dec_sel_win(idx_flat, pt_flat, q_pad, pool_t, newblk_t, win_t, kvw_new_t, past, s_new, n_pages):
    nseq, ncombo = q_pad.shape[0], q_pad.shape[1]
    nwin = win_t.shape[2]
    per_seq = lambda a, idx, pt: (a, 0, 0, 0)
    per_seq3 = lambda a, idx, pt: (a, 0, 0)
    grid_spec = pltpu.PrefetchScalarGridSpec(
        num_scalar_prefetch=2,
        grid=(nseq,),
        in_specs=[
            pl.BlockSpec((None, ncombo, Q_PAD, HEAD_DIM), per_seq),
            pl.BlockSpec(memory_space=pl.ANY),
            pl.BlockSpec(memory_space=pl.ANY),
            pl.BlockSpec((None, KV_W, nwin), per_seq3),
            pl.BlockSpec((None, KV_W, PAGE_SIZE), per_seq3),
        ],
        out_specs=[
            pl.BlockSpec((None, ncombo, Q_PAD, HEAD_DIM), per_seq),
            pl.BlockSpec((None, ncombo, Q_PAD, HEAD_DIM), per_seq),
        ],
        scratch_shapes=[pltpu.VMEM((2, ncombo, HEAD_DIM, N_SEL * PAGE_SIZE), F32),
                        pltpu.VMEM((2, ncombo, HEAD_DIM, N_SEL * PAGE_SIZE), F32),
                        pltpu.SemaphoreType.DMA((2,))],
    )
    shape = jax.ShapeDtypeStruct((nseq, ncombo, Q_PAD, HEAD_DIM), F32)
    return pl.pallas_call(
        functools.partial(_dec_sel_win_body, past=past, s_new=s_new, n_pages=n_pages),
        grid_spec=grid_spec,
        out_shape=[shape, shape],
        compiler_params=_cparams(("arbitrary",)),
        name="dec_sel_win",
    )(idx_flat, pt_flat, q_pad, pool_t, newblk_t, win_t, kvw_new_t)


def _out_ln_body(*refs, gated):
    if gated:
        ocmp_ref, osel_ref, owin_ref, gates_ref, yconv_ref, h_ref, wo_ref, g_ref, b_ref, y_ref = refs
        gates = gates_ref[...]
        pieces = []
        for h in range(N_HEADS):
            col = slice(h * HEAD_DIM, (h + 1) * HEAD_DIM)
            pieces.append(gates[:, h:h + 1] * ocmp_ref[:, col]
                          + gates[:, N_HEADS + h:N_HEADS + h + 1] * osel_ref[:, col]
                          + gates[:, 2 * N_HEADS + h:2 * N_HEADS + h + 1] * owin_ref[:, col])
        o = jnp.concatenate(pieces, axis=-1)
    else:
        o_ref, yconv_ref, h_ref, wo_ref, g_ref, b_ref, y_ref = refs
        o = o_ref[...]
    mix = (_dot(o.astype(BF16), wo_ref[0:D_ATTN, :])
           + _dot(yconv_ref[...].astype(BF16), wo_ref[D_ATTN:D_MODEL, :]))
    y_ref[...] = _layer_norm(ALPHA * h_ref[...] + mix, g_ref[...], b_ref[...])


def _out_ln(branches, yconv, h2d, w_out, g, b, tm):
    rows = h2d.shape[0]
    gated = len(branches) > 1
    row = lambda i: (i, 0)
    const = lambda i: (0, 0)
    widths = [D_ATTN, D_ATTN, D_ATTN, LANES] if gated else [D_ATTN]
    return pl.pallas_call(
        functools.partial(_out_ln_body, gated=gated),
        grid=(rows // tm,),
        in_specs=[pl.BlockSpec((tm, w), row) for w in widths] + [
            pl.BlockSpec((tm, D_CONV), row),
            pl.BlockSpec((tm, D_MODEL), row),
            pl.BlockSpec((D_MODEL, D_MODEL), const),
            pl.BlockSpec((1, D_MODEL), const),
            pl.BlockSpec((1, D_MODEL), const),
        ],
        out_specs=pl.BlockSpec((tm, D_MODEL), row),
        out_shape=jax.ShapeDtypeStruct((rows, D_MODEL), F32),
        compiler_params=_cparams(("arbitrary",)),
        name="out_ln",
    )(*branches, yconv, h2d, w_out, g, b)


def _block_diag_groups(w):
    z = jnp.zeros_like(w)
    return jnp.concatenate([jnp.concatenate([w, z], axis=-1),
                            jnp.concatenate([z, w], axis=-1)], axis=-2).astype(BF16)


def _to_combo_layout(x, nseq, s_new):
    x = x.reshape(nseq, s_new, N_KV, GQA_R, HEAD_DIM).transpose(0, 2, 1, 3, 4)
    x = x.reshape(nseq, N_KV * s_new, GQA_R, HEAD_DIM)
    return jnp.pad(x, ((0, 0), (0, 0), (0, Q_PAD - GQA_R), (0, 0)))


def _from_combo_layout(x, nseq, s_new):
    x = x[:, :, :GQA_R, :].reshape(nseq, N_KV, s_new, GQA_R, HEAD_DIM).transpose(0, 2, 1, 3, 4)
    return x.reshape(nseq * s_new, D_ATTN)


def kernel(x_prompt, x_sample, cache_cmp_kv, cache_sel_kv, state_win_kv, state_conv, page_table,
           ffa_gate, ffa_up, ffa_down, ln1_g, ln1_b, w_in, b_in, w_conv, cmp_w1, cmp_w2, cmp_pe,
           w_out, ln2_g, ln2_b, ffb_gate, ffb_up, ffb_down, ln3_g, ln3_b):
    depth = ffa_gate.shape[0]
    assert depth == 1
    l = 0
    n, t, d = x_prompt.shape
    nseq, s_new, _ = x_sample.shape
    n_pages = page_table.shape[1]
    past = n_pages * PAGE_SIZE
    rows_p, rows_s = n * t, nseq * s_new
    kvshape = (2, N_KV, HEAD_DIM)

    bf = lambda w: w.astype(BF16)
    vec = lambda v: v.reshape(1, -1)
    w_main = bf(w_in[l][:, :D_MAIN])
    b_main = vec(b_in[l][:D_MAIN])
    w_gate = bf(jnp.pad(w_in[l][:, D_MAIN:], ((0, 0), (0, LANES - N_GATE))))
    b_gate = vec(jnp.pad(b_in[l][D_MAIN:], (0, LANES - N_GATE)))
    pe_t = jnp.concatenate([cmp_pe[l].transpose(1, 2, 0)] * BPP, axis=-1)
    w1_t = _block_diag_groups(cmp_w1[l].transpose(0, 2, 1, 3)).reshape(
        2, D_PAIRS, 2 * PAGE_SIZE, PAGE_SIZE)
    w2_t = _block_diag_groups(cmp_w2[l])
    w_out_b = bf(w_out[l])
    ffa = (bf(ffa_gate[l]), bf(ffa_up[l]), bf(ffa_down[l]), vec(ln1_g[l]), vec(ln1_b[l]))
    ffb = (bf(ffb_gate[l]), bf(ffb_up[l]), bf(ffb_down[l]), vec(ln3_g[l]), vec(ln3_b[l]))

    def k_pad(x, lo):
        z = jnp.zeros_like(x[..., :HEAD_DIM])
        return jnp.concatenate([x[..., lo:lo + HEAD_DIM], z, z, x[..., lo + HEAD_DIM:lo + 2 * HEAD_DIM]],
                               axis=-1)

    def prompt_cols(x):
        return jnp.concatenate([x[..., :_C_Q], k_pad(x, _C_KVS), k_pad(x, _C_KVW)], axis=-1)

    def prompt_rows(x):
        return jnp.pad(x[..., _C_Q:], [(0, 0)] * (x.ndim - 1) + [(0, GATE_ROWS - N_GATE)])

    w_rows, b_rows = bf(prompt_cols(w_in[l])), vec(prompt_cols(b_in[l]))
    w_tr, b_tr = bf(prompt_rows(w_in[l]).T), prompt_rows(b_in[l]).reshape(-1, 1)

    pps = t // PAGE_SIZE
    cmp_seqs = 4
    hp = _ffn_ln(x_prompt.reshape(rows_p, d), *ffa, tm=512)
    q, k_sel, k_win, kvc_t, kvs_t, kvw_t, gates_t, yconv, cstate = _proj_prompt(
        hp.reshape(n, t, d), w_rows, b_rows, w_tr, b_tr, w_conv[l], tm=512)
    kc = _compress(jnp.zeros((1,), jnp.int32), kvc_t, pe_t, w1_t, w2_t, n_steps=n // cmp_seqs,
                   n_pages=cmp_seqs * pps, pages_per_seq=pps, name="compress_prompt")
    kc = kc.reshape(n // cmp_seqs, 2, N_KV, cmp_seqs, pps, BPP, HEAD_DIM)
    kc = kc.transpose(0, 3, 2, 1, 4, 5, 6).reshape(n, N_KV, 2, pps * BPP, HEAD_DIM)
    zc = jnp.zeros_like(kc[:, 0, 0])
    kc_pad = bf(jnp.stack([jnp.concatenate([kc[:, 0, 0], zc], axis=-1),
                           jnp.concatenate([zc, kc[:, 1, 0]], axis=-1)], axis=1))
    vc_t = bf(jnp.swapaxes(kc[:, :, 1], -1, -2))
    o = _attn_prompt(q, k_sel, k_win, kvs_t, kvw_t, kc_pad, vc_t, gates_t)
    yp = _out_ln([o.reshape(rows_p, D_ATTN)], yconv.reshape(rows_p, D_CONV), hp, w_out_b,
                 vec(ln2_g[l]), vec(ln2_b[l]), tm=512)
    yp = _ffn_ln(yp, *ffb, tm=512).reshape(n, t, d)

    def from_feature_major(x_t):
        return jnp.moveaxis(x_t.reshape(x_t.shape[0], *kvshape, x_t.shape[-1]), -1, 1)[None]

    p_cmp = from_feature_major(kvc_t)
    p_sel = from_feature_major(kvs_t)
    p_win = from_feature_major(kvw_t[:, :, t - min(WINDOW, t):])
    p_conv = cstate[:, 8 - (CONV_W - 1):].reshape(1, n, CONV_W - 1, D_CONV)

    hs = _ffn_ln(x_sample.reshape(rows_s, d), *ffa, tm=rows_s)
    qs, kv3, gates_s, cb_s, u_s = _proj_sample(hs, w_main, b_main, w_gate, b_gate)
    kvc_s, kvs_s, kvw_s = kv3[:, :KV_W], kv3[:, KV_W:2 * KV_W], kv3[:, 2 * KV_W:]
    u3 = u_s.reshape(nseq, s_new, D_CONV)
    yconv_s = _conv_sample(cb_s.reshape(nseq, s_new, D_CONV), u3, state_conv[l], w_conv[l])
    s_conv = jnp.concatenate([state_conv[l], u3], axis=1)[:, -(CONV_W - 1):]

    def feature_major(x):
        return jnp.moveaxis(x, -4, -1).reshape(*x.shape[:-4], KV_W, x.shape[-4])

    def new_rows_t(x2d):
        xt = x2d.reshape(nseq, s_new, KV_W).transpose(0, 2, 1)
        return jnp.pad(xt, ((0, 0), (0, 0), (0, PAGE_SIZE - s_new)))

    kc_s = _compress(page_table.reshape(-1), feature_major(cache_cmp_kv[l]), pe_t, w1_t, w2_t,
                     n_steps=nseq, n_pages=n_pages, pages_per_seq=None, name="compress_decode")
    q_pad = _to_combo_layout(qs, nseq, s_new)
    ocmp_c, idx_pad = _dec_cmp(q_pad, kc_s, past, s_new)
    idx_flat = idx_pad[:, :N_SEL].reshape(-1)
    win_t = feature_major(state_win_kv[l])
    kvw_new_t = new_rows_t(kvw_s)
    osel_c, owin_c = _dec_sel_win(
        idx_flat, page_table.reshape(-1), q_pad, feature_major(cache_sel_kv[l]), new_rows_t(kvs_s),
        win_t, kvw_new_t, past, s_new, n_pages)
    branches = [_from_combo_layout(x, nseq, s_new) for x in (ocmp_c, osel_c, owin_c)]
    ys = _out_ln(branches + [gates_s], yconv_s.reshape(rows_s, D_CONV), hs, w_out_b,
                 vec(ln2_g[l]), vec(ln2_b[l]), tm=rows_s)
    ys = _ffn_ln(ys, *ffb, tm=rows_s).reshape(nseq, s_new, d)
    n_keep = min(WINDOW, past + s_new)
    ctx_t = jnp.concatenate([win_t, kvw_new_t[:, :, :s_new]], axis=-1)[:, :, -n_keep:]
    s_cmp = kvc_s.reshape(1, nseq, s_new, *kvshape)
    s_sel = kvs_s.reshape(1, nseq, s_new, *kvshape)
    s_win = jnp.moveaxis(ctx_t.reshape(nseq, *kvshape, n_keep), -1, 1)[None]
    return (yp, ys, p_cmp, p_sel, p_win, p_conv, s_cmp, s_sel, s_win,
            s_conv.reshape(1, nseq, CONV_W - 1, D_CONV))
```

```python
import functools

import jax
import jax.numpy as jnp
from jax import lax
from jax.experimental import pallas as pl
from jax.experimental.pallas import tpu as pltpu

F32 = jnp.float32
BF16 = jnp.bfloat16

D_MODEL = 1024
D_CONV = 512
D_ATTN = 512
HEAD_DIM = 64
N_HEADS = 8
N_KV = 2
GQA_R = N_HEADS // N_KV
N_BR = 3
CONV_W = 3
BLK = 64
BLK_SHIFT = 6
N_SEL = 16
WINDOW = 512
PAGE_SIZE = 128
PAGE_SHIFT = 7
D_FF = 2816
KV_W = 2 * N_KV * HEAD_DIM
D_MAIN = 3 * D_CONV + D_ATTN + 3 * KV_W
N_GATE = N_BR * N_HEADS
ALPHA = 2.0 ** 0.25
SCALE = HEAD_DIM ** -0.5
LN_EPS = 1e-5
NEG = -1e30
FORCE = 1e4

LANES = 128
VMEM_LIMIT_BYTES = 56 * 1024 * 1024


def _cparams(sem):
    return pltpu.CompilerParams(dimension_semantics=sem, vmem_limit_bytes=VMEM_LIMIT_BYTES)


def _layer_norm(y, g, b):
    mu = jnp.mean(y, axis=-1, keepdims=True)
    yc = y - mu
    var = jnp.mean(yc * yc, axis=-1, keepdims=True)
    return yc * lax.rsqrt(var + LN_EPS) * g + b


def _dot(a, b):
    return jnp.dot(a, b, preferred_element_type=F32)


def _dot_nt(a, b):
    return lax.dot_general(a, b, (((1,), (1,)), ((), ())), preferred_element_type=F32)


FFN_CHUNK = 256


def _ffn_ln_body(x_ref, wg_ref, wu_ref, wd_ref, g_ref, b_ref, o_ref):
    x = x_ref[...]
    xb = x.astype(BF16)
    acc = jnp.zeros(x.shape, F32)
    for c in range(D_FF // FFN_CHUNK):
        sl = slice(c * FFN_CHUNK, (c + 1) * FFN_CHUNK)
        gg = _dot(xb, wg_ref[:, sl])
        uu = _dot(xb, wu_ref[:, sl])
        hh = (gg * jax.nn.sigmoid(gg) * uu).astype(BF16)
        acc = acc + _dot(hh, wd_ref[sl, :])
    o_ref[...] = _layer_norm(ALPHA * x + 0.5 * acc, g_ref[...], b_ref[...])


def _ffn_ln(x2d, wg, wu, wd, g, b, tm):
    rows = x2d.shape[0]
    const = lambda i: (0, 0)
    return pl.pallas_call(
        _ffn_ln_body,
        grid=(rows // tm,),
        in_specs=[
            pl.BlockSpec((tm, D_MODEL), lambda i: (i, 0)),
            pl.BlockSpec((D_MODEL, D_FF), const),
            pl.BlockSpec((D_MODEL, D_FF), const),
            pl.BlockSpec((D_FF, D_MODEL), const),
            pl.BlockSpec((1, D_MODEL), const),
            pl.BlockSpec((1, D_MODEL), const),
        ],
        out_specs=pl.BlockSpec((tm, D_MODEL), lambda i: (i, 0)),
        out_shape=jax.ShapeDtypeStruct((rows, D_MODEL), F32),
        compiler_params=_cparams(("arbitrary",)),
        name="ffn_ln",
    )(x2d, wg, wu, wd, g, b)


_C_CB, _C_CC, _C_CH, _C_Q = 0, D_CONV, 2 * D_CONV, 3 * D_CONV
_C_KVC = 3 * D_CONV + D_ATTN
_C_KVS = _C_KVC + KV_W
_C_KVW = _C_KVS + KV_W


def _proj_piece(hb, w_ref, b_ref, lo, width):
    return _dot(hb, w_ref[:, lo:lo + width]) + b_ref[:, lo:lo + width]


_P_KS = 3 * D_CONV
_P_KW = _P_KS + KV_W
P_ROW_COLS = _P_KW + KV_W
GATE_ROWS = 32
_T_KV = D_ATTN
_T_GATE = _T_KV + 3 * KV_W
P_T_ROWS = _T_GATE + GATE_ROWS


def _proj_prompt_body(h_ref, w_ref, b_ref, wt_ref, bt_ref, wc_ref,
                      qt_ref, ks_ref, kw_ref, kvct_ref, kvst_ref, kvwt_ref, gates_ref,
                      yconv_ref, cstate_ref, ubuf):
    i = pl.program_id(1)
    tm = h_ref.shape[0]
    hb = h_ref[...].astype(BF16)

    def piece_t(lo, height):
        return _dot_nt(wt_ref[lo:lo + height, :], hb) + bt_ref[lo:lo + height, :]

    qt_ref[...] = (piece_t(0, D_ATTN) * (SCALE * LOG2E)).astype(BF16)
    lane = lax.broadcasted_iota(jnp.int32, (1, KV_W), 1)
    blk = (i * tm + lax.broadcasted_iota(jnp.int32, (tm, 1), 0)) >> BLK_SHIFT
    aux0, aux1 = HEAD_DIM, 2 * HEAD_DIM
    one = (lane == aux0 + AUX_BLOCKS) | (lane == aux1 + AUX_BLOCKS)
    ks = _proj_piece(hb, w_ref, b_ref, _P_KS, KV_W)
    ks_ref[...] = jnp.where(one | (lane == aux0 + blk) | (lane == aux1 + blk), 1.0, ks).astype(BF16)
    kw = _proj_piece(hb, w_ref, b_ref, _P_KW, KV_W)
    kw_ref[...] = jnp.where(one, 1.0, kw).astype(BF16)
    for k, ref in enumerate((kvct_ref, kvst_ref, kvwt_ref)):
        ref[...] = piece_t(_T_KV + k * KV_W, KV_W)
    gates_ref[...] = jax.nn.sigmoid(piece_t(_T_GATE, GATE_ROWS))
    cb = _proj_piece(hb, w_ref, b_ref, _C_CB, D_CONV)
    u = _proj_piece(hb, w_ref, b_ref, _C_CC, D_CONV) * _proj_piece(hb, w_ref, b_ref, _C_CH, D_CONV)

    @pl.when(i == 0)
    def _():
        ubuf[0:8, :] = jnp.zeros((8, D_CONV), F32)

    ubuf[8:8 + tm, :] = u
    u1 = ubuf[7:7 + tm, :]
    u2 = ubuf[6:6 + tm, :]
    z = u2 * wc_ref[0:1, :] + u1 * wc_ref[1:2, :] + u * wc_ref[2:3, :]
    yconv_ref[...] = cb * z
    tail = u[tm - 8:tm, :]
    ubuf[0:8, :] = tail
    cstate_ref[...] = tail


def _proj_prompt(h, w_rows, b_rows, w_t, b_t, w_conv, tm):
    n, t, _ = h.shape
    const = lambda a, i: (0, 0)
    row = lambda a, i: (a, i, 0)
    col = lambda a, i: (a, 0, i)

    def out(width, dtype=F32):
        return jax.ShapeDtypeStruct((n, t, width), dtype)

    def out_t(rows, dtype=F32):
        return jax.ShapeDtypeStruct((n, rows, t), dtype)

    return pl.pallas_call(
        _proj_prompt_body,
        grid=(n, t // tm),
        in_specs=[
            pl.BlockSpec((None, tm, D_MODEL), row),
            pl.BlockSpec((D_MODEL, P_ROW_COLS), const),
            pl.BlockSpec((1, P_ROW_COLS), const),
            pl.BlockSpec((P_T_ROWS, D_MODEL), const),
            pl.BlockSpec((P_T_ROWS, 1), const),
            pl.BlockSpec((CONV_W, D_CONV), const),
        ],
        out_specs=[
            pl.BlockSpec((None, D_ATTN, tm), col),
            pl.BlockSpec((None, tm, KV_W), row),
            pl.BlockSpec((None, tm, KV_W), row),
            pl.BlockSpec((None, KV_W, tm), col),
            pl.BlockSpec((None, KV_W, tm), col),
            pl.BlockSpec((None, KV_W, tm), col),
            pl.BlockSpec((None, GATE_ROWS, tm), col),
            pl.BlockSpec((None, tm, D_CONV), row),
            pl.BlockSpec((None, 8, D_CONV), lambda a, i: (a, 0, 0)),
        ],
        out_shape=[out_t(D_ATTN, BF16), out(KV_W, BF16), out(KV_W, BF16), out_t(KV_W), out_t(KV_W),
                   out_t(KV_W), out_t(GATE_ROWS), out(D_CONV), jax.ShapeDtypeStruct((n, 8, D_CONV), F32)],
        scratch_shapes=[pltpu.VMEM((tm + 8, D_CONV), F32)],
        compiler_params=_cparams(("arbitrary", "arbitrary")),
        name="proj_prompt",
    )(h, w_rows, b_rows, w_t, b_t, w_conv)


def _proj_sample_body(h_ref, w_ref, b_ref, wgt_ref, bgt_ref, q_ref, kv_ref, gates_ref, cb_ref, u_ref):
    hb = h_ref[...].astype(BF16)
    q_ref[...] = (_proj_piece(hb, w_ref, b_ref, _C_Q, D_ATTN) * SCALE).astype(BF16)
    kv_ref[...] = _proj_piece(hb, w_ref, b_ref, _C_KVC, 3 * KV_W)
    gates_ref[...] = jax.nn.sigmoid(_dot(hb, wgt_ref[...]) + bgt_ref[...])
    cb_ref[...] = _proj_piece(hb, w_ref, b_ref, _C_CB, D_CONV)
    u_ref[...] = _proj_piece(hb, w_ref, b_ref, _C_CC, D_CONV) * _proj_piece(hb, w_ref, b_ref, _C_CH, D_CONV)


def _proj_sample(h2d, w_main, b_main, w_gate, b_gate):
    rows = h2d.shape[0]

    def out(width, dtype=F32):
        return jax.ShapeDtypeStruct((rows, width), dtype)

    return pl.pallas_call(
        _proj_sample_body,
        out_shape=[out(D_ATTN, BF16), out(3 * KV_W), out(LANES), out(D_CONV), out(D_CONV)],
        compiler_params=pltpu.CompilerParams(vmem_limit_bytes=VMEM_LIMIT_BYTES),
        name="proj_sample",
    )(h2d, w_main, b_main, w_gate, b_gate)


def _conv_sample_body(cb_ref, u_ref, prev_ref, wc_ref, y_ref):
    s = u_ref.shape[1]
    ext = [prev_ref[:, 0, :], prev_ref[:, 1, :]] + [u_ref[:, t, :] for t in range(s)]
    for t in range(s):
        z = ext[t] * wc_ref[0:1, :] + ext[t + 1] * wc_ref[1:2, :] + ext[t + 2] * wc_ref[2:3, :]
        y_ref[:, t, :] = cb_ref[:, t, :] * z


def _conv_sample(cb, u, prev, w_conv):
    return pl.pallas_call(
        _conv_sample_body,
        out_shape=jax.ShapeDtypeStruct(u.shape, F32),
        name="conv_sample",
    )(cb, u, prev, w_conv)


BPP = PAGE_SIZE // BLK
D_PAIRS = HEAD_DIM // 2


def _compress_body(pt_ref, src_ref, pe_ref, w1_ref, w2_ref, kc_ref, buf, sem, *, n_pages, pages_per_seq):
    n = pl.program_id(0)
    slot = n % 2

    def page_copy(step, j, sl):
        dst = buf.at[sl, :, j, :]
        if pages_per_seq is None:
            src = src_ref.at[pt_ref[step * n_pages + j]]
        else:
            seq = step * (n_pages // pages_per_seq) + j // pages_per_seq
            src = src_ref.at[seq, :, pl.ds((j % pages_per_seq) * PAGE_SIZE, PAGE_SIZE)]
        return pltpu.make_async_copy(src, dst, sem.at[sl])

    def start(step, sl):
        for j in range(n_pages):
            page_copy(step, j, sl).start()

    @pl.when(n == 0)
    def _():
        start(n, slot)

    @pl.when(n + 1 < pl.num_programs(0))
    def _():
        start(n + 1, 1 - slot)

    for j in range(n_pages):
        page_copy(n, j, slot).wait()

    def feature_rows(f):
        return buf[slot, f]

    for c in range(2):
        acc = None
        for dp in range(D_PAIRS):
            rows = []
            for g in range(N_KV):
                f0 = (c * N_KV + g) * HEAD_DIM + 2 * dp
                rows.append(jnp.concatenate(
                    [feature_rows(f0) + pe_ref[c, 2 * dp:2 * dp + 1, :],
                     feature_rows(f0 + 1) + pe_ref[c, 2 * dp + 1:2 * dp + 2, :]], axis=-1))
            x = jnp.concatenate(rows, axis=0).astype(BF16)
            d = _dot(x, w1_ref[c, dp])
            acc = d if acc is None else acc + d
        hid = (acc * jax.nn.sigmoid(acc)).astype(BF16)
        kc_ref[c] = _dot(hid, w2_ref[c])


def _compress(pt_flat, src, pe_t, w1_t, w2_t, n_steps, n_pages, pages_per_seq, name):
    const3 = lambda a, pt: (0, 0, 0)
    grid_spec = pltpu.PrefetchScalarGridSpec(
        num_scalar_prefetch=1,
        grid=(n_steps,),
        in_specs=[
            pl.BlockSpec(memory_space=pl.ANY),
            pl.BlockSpec((2, HEAD_DIM, PAGE_SIZE), const3),
            pl.BlockSpec((2, D_PAIRS, 2 * PAGE_SIZE, PAGE_SIZE), lambda a, pt: (0, 0, 0, 0)),
            pl.BlockSpec((2, PAGE_SIZE, PAGE_SIZE), const3),
        ],
        out_specs=pl.BlockSpec((None, 2, N_KV * n_pages, PAGE_SIZE), lambda a, pt: (a, 0, 0, 0)),
        scratch_shapes=[pltpu.VMEM((2, KV_W, n_pages, PAGE_SIZE), F32),
                        pltpu.SemaphoreType.DMA((2,))],
    )
    return pl.pallas_call(
        functools.partial(_compress_body, n_pages=n_pages, pages_per_seq=pages_per_seq),
        grid_spec=grid_spec,
        out_shape=jax.ShapeDtypeStruct((n_steps, 2, N_KV * n_pages, PAGE_SIZE), F32),
        compiler_params=_cparams(("arbitrary",)),
        name=name,
    )(pt_flat, src, pe_t, w1_t, w2_t)


ATT_TQ = 256
ATT_TK = 256
WIN_TILES = WINDOW // ATT_TK + 1
AUX_BLOCKS = 32
LOG2E = 1.4426950408889634


V_ROWS = HEAD_DIM + 16


def _attn_update(s_t, v_t, state):
    m, acc = state
    m_new = jnp.maximum(m, jnp.max(s_t, axis=0, keepdims=True))
    e = jnp.exp2(s_t - m_new)
    acc = jnp.exp2(m - m_new) * acc + _dot(v_t, e.astype(BF16))
    return m_new, acc


def _attn_prompt_body(qt_ref, ks_ref, kw_ref, vs_ref, vw_ref, kc_ref, vct_ref, gates_ref, o_ref,
                      winbias):
    tq, tk = ATT_TQ, ATT_TK
    qi = pl.program_id(1)
    nb = kc_ref.shape[1]
    t0 = qi * tq
    kd = qi
    tpos = t0 + lax.broadcasted_iota(jnp.int32, (1, tq), 1)
    jrow = lax.broadcasted_iota(jnp.int32, (nb, 1), 0)
    krow = lax.broadcasted_iota(jnp.int32, (tk, 1), 0)
    vis = ((jrow + 1) * BLK - 1) <= tpos
    cur = tpos >> BLK_SHIFT

    for slot, back in enumerate((0, WIN_TILES - 1)):
        kpos = (kd - back) * tk + krow
        dist = tpos - kpos
        ok = (dist >= 0) & (dist < WINDOW) & (kpos >= 0)
        winbias[slot] = jnp.where(ok, 0.0, NEG)

    def heads_of(x):
        return jnp.concatenate([x] * GQA_R, axis=1)

    def group_queries(g, block_bias=None, switch=None):
        zeros = jnp.zeros((AUX_BLOCKS, tq), BF16)
        top = zeros if block_bias is None else block_bias
        low = zeros if switch is None else jnp.broadcast_to(switch, (AUX_BLOCKS, tq)).astype(BF16)
        aux = jnp.concatenate([top, low], axis=0)
        pieces = []
        for r in range(GQA_R):
            h = g * GQA_R + r
            qh = qt_ref[h * HEAD_DIM:(h + 1) * HEAD_DIM, :]
            pieces.append(jnp.concatenate([qh, aux] if g == 0 else [aux, qh], axis=0))
        return jnp.concatenate(pieces, axis=1)

    q4 = [group_queries(g) for g in range(N_KV)]
    vis4 = heads_of(vis)

    kt_rem = jnp.maximum(kd - 1, 0)
    rem_switch = jnp.where((kd & 1) == 1, 0.0, NEG)

    o_cmp, q4_sel, q4_rem = [], [], []
    for g in range(N_KV):
        s = jnp.where(vis4, _dot(kc_ref[g], q4[g]), NEG)
        m = jnp.max(s, axis=0, keepdims=True)
        e = jnp.where(vis4, jnp.exp2(s - m), 0.0)
        l = jnp.sum(e, axis=0, keepdims=True)
        p = e / jnp.where(l > 0.0, l, 1.0)
        imp = sum(p[:, r * tq:(r + 1) * tq] for r in range(GQA_R))
        o_cmp.append(_dot(vct_ref[g], p.astype(BF16)))

        score = jnp.where(jrow > cur, NEG, imp)
        score = jnp.where((jrow == 0) | (jrow == cur) | (jrow == cur - 1), FORCE, score)
        rank = jnp.zeros((nb, tq), F32)
        for j2 in range(nb):
            row = score[j2:j2 + 1, :]
            tie = jnp.where(jrow > j2, 1.0, 0.0)
            rank = rank + jnp.where(row > score, 1.0, jnp.where(row == score, tie, 0.0))
        block_bias = jnp.where(rank < N_SEL, 0.0, NEG).astype(BF16)
        q4_sel.append(group_queries(g, block_bias))
        q4_rem.append(group_queries(g, block_bias, rem_switch))

    def key_tile(k_ref, kt, g):
        return k_ref[pl.ds(pl.multiple_of(kt * tk, tk), tk), g * LANES:(g + 1) * LANES]

    ones_rows = jnp.ones((V_ROWS - HEAD_DIM, tk), BF16)

    def val_tile(v_ref, kt, g):
        v = v_ref[g * HEAD_DIM:(g + 1) * HEAD_DIM, pl.ds(pl.multiple_of(kt * tk, tk), tk)]
        return jnp.concatenate([v.astype(BF16), ones_rows], axis=0)

    wide = GQA_R * tq
    init = (jnp.full((1, wide), NEG, F32), jnp.zeros((V_ROWS, wide), F32))

    def win_scores(back, g):
        kt = jnp.maximum(kd - back, 0)
        if back in (0, WIN_TILES - 1):
            return _dot(key_tile(kw_ref, kt, g), q4[g]) + heads_of(winbias[min(back, 1)])
        q_sw = group_queries(g, None, jnp.where(kd >= back, 0.0, NEG))
        return _dot(key_tile(kw_ref, kt, g), q_sw)

    groups = range(N_KV)
    s_diag = [_dot(key_tile(ks_ref, kd, g), q4_sel[g]) + heads_of(winbias[0]) for g in groups]
    s_rem = [_dot(key_tile(ks_ref, kt_rem, g), q4_rem[g]) for g in groups]
    s_win = [[win_scores(back, g) for g in groups] for back in range(WIN_TILES)]
    sel = tuple(_attn_update(s_diag[g], val_tile(vs_ref, kd, g), init) for g in groups)
    sel = tuple(_attn_update(s_rem[g], val_tile(vs_ref, kt_rem, g), sel[g]) for g in groups)
    win = [init] * N_KV
    for back in range(WIN_TILES):
        kt = jnp.maximum(kd - back, 0)
        win = [_attn_update(s_win[back][g], val_tile(vw_ref, kt, g), win[g]) for g in groups]

    def sel_pair(j, states):
        kts = (2 * j, 2 * j + 1)
        s = [[_dot(key_tile(ks_ref, kt, g), q4_sel[g]) for g in groups] for kt in kts]
        for i, kt in enumerate(kts):
            states = tuple(_attn_update(s[i][g], val_tile(vs_ref, kt, g), states[g]) for g in groups)
        return states

    sel = lax.fori_loop(0, kd >> 1, sel_pair, sel)

    gates = gates_ref[...]
    outs = []
    for h in range(N_HEADS):
        g, r = h // GQA_R, h % GQA_R
        cols = slice(r * tq, (r + 1) * tq)
        o_sel = sel[g][1][:HEAD_DIM, cols] * (1.0 / sel[g][1][HEAD_DIM:HEAD_DIM + 1, cols])
        o_win = win[g][1][:HEAD_DIM, cols] * (1.0 / win[g][1][HEAD_DIM:HEAD_DIM + 1, cols])
        outs.append(gates[h:h + 1, :] * o_cmp[g][:, cols] + gates[N_HEADS + h:N_HEADS + h + 1, :] * o_sel
                    + gates[2 * N_HEADS + h:2 * N_HEADS + h + 1, :] * o_win)
    o_ref[...] = jnp.concatenate(outs, axis=0).T


def _attn_prompt(q_t, k_sel, k_win, kvs_t, kvw_t, kc_pad, vc_t, gates_t):
    n, _, t = q_t.shape
    nb = kc_pad.shape[2]
    row = lambda a, i: (a, i, 0)
    seq = lambda a, i: (a, 0, 0)
    seq4 = lambda a, i: (a, 0, 0, 0)
    v_rows = lambda a, i: (a, 1, 0)
    return pl.pallas_call(
        _attn_prompt_body,
        grid=(n, t // ATT_TQ),
        in_specs=[
            pl.BlockSpec((None, D_ATTN, ATT_TQ), lambda a, i: (a, 0, i)),
            pl.BlockSpec((None, t, KV_W), seq),
            pl.BlockSpec((None, t, KV_W), seq),
            pl.BlockSpec((None, N_KV * HEAD_DIM, t), v_rows),
            pl.BlockSpec((None, N_KV * HEAD_DIM, t), v_rows),
            pl.BlockSpec((None, N_KV, nb, LANES), seq4),
            pl.BlockSpec((None, N_KV, HEAD_DIM, nb), seq4),
            pl.BlockSpec((None, GATE_ROWS, ATT_TQ), lambda a, i: (a, 0, i)),
        ],
        out_specs=pl.BlockSpec((None, ATT_TQ, D_ATTN), row),
        out_shape=jax.ShapeDtypeStruct((n, t, D_ATTN), F32),
        scratch_shapes=[pltpu.VMEM((2, ATT_TK, ATT_TQ), F32)],
        compiler_params=_cparams(("arbitrary", "arbitrary")),
        name="attn_prompt",
    )(q_t, k_sel, k_win, kvs_t, kvw_t, kc_pad, vc_t, gates_t)


DEC_SEQ_PER_STEP = 8


def _dec_cmp_body(q2_ref, kc_ref, ocmp_ref, idx_ref, p_buf, imp_buf, *, past, s_new):
    assert BPP == 2
    ns = q2_ref.shape[0]
    qrows = s_new * GQA_R
    n_pages = kc_ref.shape[2] // N_KV
    nb = n_pages * BPP
    width = imp_buf.shape[1]
    imp_buf[...] = jnp.zeros(imp_buf.shape, F32)
    lane = lax.broadcasted_iota(jnp.int32, (1, width), 1)
    jc = jnp.where(lane < n_pages, BPP * lane, BPP * (lane - n_pages) + 1)
    jc = jnp.where(lane == nb, nb, jc)

    row2 = lax.broadcasted_iota(jnp.int32, (2 * qrows, 1), 0)
    half = jnp.where(row2 >= qrows, 1, 0)
    blk = BPP * lax.broadcasted_iota(jnp.int32, (1, n_pages), 1) + half
    tok = (row2 & (qrows - 1)) >> (GQA_R.bit_length() - 1)
    vis = ((blk + 1) * BLK - 1) <= (past + tok)

    def both(x):
        return jnp.concatenate([x, x], axis=0)

    def pages(a, c, g):
        return kc_ref[a, c, g * n_pages:(g + 1) * n_pages, :].astype(BF16)

    pairs = [(a, g) for a in range(ns) for g in range(N_KV)]
    scores = [_dot_nt(q2_ref[a, g], pages(a, 0, g)) for a, g in pairs]
    for i, (a, g) in enumerate(pairs):
        s = jnp.where(vis, scores[i], NEG)
        m = jnp.max(s, axis=-1, keepdims=True)
        e = jnp.where(vis, jnp.exp(s - both(jnp.maximum(m[:qrows], m[qrows:]))), 0.0)
        l = jnp.sum(e, axis=-1, keepdims=True)
        l = l[:qrows] + l[qrows:]
        p = e / both(jnp.where(l > 0.0, l, 1.0))
        o2 = _dot(p.astype(BF16), pages(a, 1, g))
        ocmp_ref[a, g] = o2[:qrows, :HEAD_DIM] + o2[qrows:, HEAD_DIM:]
        p_buf[i] = p
        imp = sum(p_buf[i, pl.ds(r, 2 * s_new, stride=GQA_R), :] for r in range(GQA_R))
        row0 = i * s_new
        imp_buf[row0:row0 + s_new, 0:n_pages] = imp[:s_new]
        imp_buf[row0:row0 + s_new, n_pages:nb] = imp[s_new:]

    rows = ns * N_KV * s_new
    imp = imp_buf[...]
    t_row = lax.broadcasted_iota(jnp.int32, (rows, 1), 0) & (s_new - 1)
    cur = (past + t_row) >> BLK_SHIFT
    score = jnp.where(jc > cur, NEG, imp)
    score = jnp.where((jc == 0) | (jc == cur) | (jc == cur - 1), FORCE, score)
    score = jnp.where(lane <= nb, score, -jnp.inf)
    jcf = jc.astype(F32)
    out_lane = lax.broadcasted_iota(jnp.int32, (1, LANES), 1)
    idx = jnp.zeros((rows, LANES), jnp.int32)
    for k in range(N_SEL):
        m = jnp.max(score, axis=-1, keepdims=True)
        first = jnp.min(jnp.where(score == m, jcf, 1e9), axis=-1, keepdims=True)
        idx = jnp.where(out_lane == k, first.astype(jnp.int32), idx)
        score = jnp.where(jcf == first, -jnp.inf, score)
    idx_ref[...] = idx


def _dec_cmp(q2, kc, past, s_new):
    nseq = q2.shape[0]
    qrows = s_new * GQA_R
    rows_kc = kc.shape[2]
    n_pages = rows_kc // N_KV
    nb = n_pages * BPP
    ns = DEC_SEQ_PER_STEP
    width = (nb + 1 + LANES - 1) // LANES * LANES
    seq4 = lambda i: (i, 0, 0, 0)
    return pl.pallas_call(
        functools.partial(_dec_cmp_body, past=past, s_new=s_new),
        grid=(nseq // ns,),
        in_specs=[
            pl.BlockSpec((ns, N_KV, 2 * qrows, 2 * HEAD_DIM), seq4),
            pl.BlockSpec((ns, 2, rows_kc, PAGE_SIZE), seq4),
        ],
        out_specs=[
            pl.BlockSpec((ns, N_KV, qrows, HEAD_DIM), seq4),
            pl.BlockSpec((ns * N_KV * s_new, LANES), lambda i: (i, 0)),
        ],
        out_shape=[jax.ShapeDtypeStruct((nseq, N_KV, qrows, HEAD_DIM), F32),
                   jax.ShapeDtypeStruct((nseq * N_KV * s_new, LANES), jnp.int32)],
        scratch_shapes=[pltpu.VMEM((ns * N_KV, 2 * qrows, n_pages), F32),
                        pltpu.VMEM((ns * N_KV * s_new, width), F32)],
        compiler_params=_cparams(("arbitrary",)),
        name="dec_cmp_select",
    )(q2, kc)


def _dec_sel_win_body(idx_ref, pt_ref, q_ref, idx2_ref, expand_ref, pool_ref, newblk_ref, win_ref,
                      kvw_new_ref, osel_ref, owin_ref, kvbuf, sem, *, past, s_new, n_pages):
    n_past_blk = past // BLK
    n = pl.program_id(0)
    nseq = pl.num_programs(0)
    slot = n % 2
    rows = s_new * GQA_R
    seg_lanes = N_SEL * PAGE_SIZE
    nkeys = s_new * seg_lanes

    def block_copy(seq, c, k, sl):
        g, t = c // s_new, c % s_new
        j = jnp.minimum(idx_ref[(seq * N_KV * s_new + c) * N_SEL + k], n_past_blk - 1)
        page = pt_ref[seq * n_pages + (j >> 1)]
        dst = kvbuf.at[sl, :, g, :, pl.ds((t * N_SEL + k) * PAGE_SIZE, PAGE_SIZE)]
        return pltpu.make_async_copy(pool_ref.at[page, :, g], dst, sem.at[sl])

    def start(seq, sl):
        for c in range(N_KV * s_new):
            for k in range(N_SEL):
                block_copy(seq, c, k, sl).start()

    @pl.when(n == 0)
    def _():
        start(n, slot)

    @pl.when(n + 1 < nseq)
    def _():
        start(n + 1, 1 - slot)

    for c in range(N_KV * s_new):
        for k in range(N_SEL):
            block_copy(n, c, k, slot).wait()

    t_row = lax.broadcasted_iota(jnp.int32, (rows, 1), 0) >> (GQA_R.bit_length() - 1)
    qpos = past + t_row
    lane = lax.broadcasted_iota(jnp.int32, (1, nkeys), 1)
    own = (lane >> (seg_lanes.bit_length() - 1)) == t_row
    row_in_page = lane & (PAGE_SIZE - 1)
    nlane = lax.broadcasted_iota(jnp.int32, (1, PAGE_SIZE), 1)
    nwin = win_ref.shape[1]
    wpos = past - nwin + lax.broadcasted_iota(jnp.int32, (1, nwin), 1)
    groups = range(N_KV)

    def krows(g):
        return pl.ds(g * HEAD_DIM, HEAD_DIM)

    def vrows(g):
        return pl.ds((N_KV + g) * HEAD_DIM, HEAD_DIM)

    q = [q_ref[g] for g in groups]
    jv2 = _dot(idx2_ref[...], expand_ref[...])
    s_sel = [_dot(q[g], kvbuf[slot, 0, g].astype(BF16)) for g in groups]
    s_new_sel = [_dot(q[g], newblk_ref[krows(g), :].astype(BF16)) for g in groups]
    s_win = [_dot(q[g], win_ref[krows(g), :].astype(BF16)) for g in groups]
    s_new_win = [_dot(q[g], kvw_new_ref[krows(g), :].astype(BF16)) for g in groups]

    def two_piece_softmax(sa, va, sb, vb):
        m = jnp.maximum(jnp.max(sa, axis=-1, keepdims=True), jnp.max(sb, axis=-1, keepdims=True))
        ea, eb = jnp.exp(sa - m), jnp.exp(sb - m)
        l = jnp.sum(ea, axis=-1, keepdims=True) + jnp.sum(eb, axis=-1, keepdims=True)
        return (_dot_nt(ea.astype(BF16), va) + _dot_nt(eb.astype(BF16), vb)) / l

    for g in groups:
        jv = jv2[g:g + 1, :].astype(jnp.int32)
        kpos = jv * BLK + (row_in_page & (BLK - 1))
        valid = (own & (jv < n_past_blk) & ((row_in_page >> BLK_SHIFT) == (jv & (BPP - 1)))
                 & (kpos <= qpos))
        has_new = jnp.max(jnp.where(own & (jv >= n_past_blk), 1.0, 0.0), axis=-1, keepdims=True)
        valid_n = (has_new > 0.5) & (nlane < BLK) & (n_past_blk * BLK + nlane <= qpos)
        osel_ref[g] = two_piece_softmax(
            jnp.where(valid, s_sel[g], NEG), kvbuf[slot, 1, g].astype(BF16),
            jnp.where(valid_n, s_new_sel[g], NEG), newblk_ref[vrows(g), :].astype(BF16))

        dw = qpos - wpos
        dn = qpos - (past + nlane)
        valid_w = (dw >= 0) & (dw < WINDOW) & (wpos >= 0)
        valid_n = (dn >= 0) & (dn < WINDOW) & (nlane < s_new)
        owin_ref[g] = two_piece_softmax(
            jnp.where(valid_w, s_win[g], NEG), win_ref[vrows(g), :].astype(BF16),
            jnp.where(valid_n, s_new_win[g], NEG), kvw_new_ref[vrows(g), :].astype(BF16))


def _dec_sel_win(idx_flat, pt_flat, q_g, idx2, pool_t, newblk_t, win_t, kvw_new_t, past, s_new, n_pages):
    nseq = q_g.shape[0]
    rows = s_new * GQA_R
    nwin = win_t.shape[2]
    nkeys = s_new * N_SEL * PAGE_SIZE
    n_phys = pool_t.shape[0]
    expand = (jnp.arange(nkeys)[None, :] // PAGE_SIZE == jnp.arange(s_new * N_SEL)[:, None]).astype(BF16)
    per_seq = lambda a, idx, pt: (a, 0, 0, 0)
    per_seq3 = lambda a, idx, pt: (a, 0, 0)
    grid_spec = pltpu.PrefetchScalarGridSpec(
        num_scalar_prefetch=2,
        grid=(nseq,),
        in_specs=[
            pl.BlockSpec((None, N_KV, rows, HEAD_DIM), per_seq),
            pl.BlockSpec((None, rows, s_new * N_SEL), per_seq3),
            pl.BlockSpec((s_new * N_SEL, nkeys), lambda a, idx, pt: (0, 0)),
            pl.BlockSpec(memory_space=pl.ANY),
            pl.BlockSpec((None, KV_W, PAGE_SIZE), per_seq3),
            pl.BlockSpec((None, KV_W, nwin), per_seq3),
            pl.BlockSpec((None, KV_W, PAGE_SIZE), per_seq3),
        ],
        out_specs=[
            pl.BlockSpec((None, N_KV, rows, HEAD_DIM), per_seq),
            pl.BlockSpec((None, N_KV, rows, HEAD_DIM), per_seq),
        ],
        scratch_shapes=[pltpu.VMEM((2, 2, N_KV, HEAD_DIM, nkeys), F32),
                        pltpu.SemaphoreType.DMA((2,))],
    )
    shape = jax.ShapeDtypeStruct((nseq, N_KV, rows, HEAD_DIM), F32)
    return pl.pallas_call(
        functools.partial(_dec_sel_win_body, past=past, s_new=s_new, n_pages=n_pages),
        grid_spec=grid_spec,
        out_shape=[shape, shape],
        compiler_params=_cparams(("arbitrary",)),
        name="dec_sel_win",
    )(idx_flat, pt_flat, q_g, idx2, expand, pool_t.reshape(n_phys, 2, N_KV, HEAD_DIM, PAGE_SIZE),
      newblk_t, win_t, kvw_new_t)


def _out_ln_body(*refs, gated):
    if gated:
        ocmp_ref, osel_ref, owin_ref, gates_ref, yconv_ref, h_ref, wo_ref, g_ref, b_ref, y_ref = refs
        gates = gates_ref[...]
        pieces = []
        for h in range(N_HEADS):
            col = slice(h * HEAD_DIM, (h + 1) * HEAD_DIM)
            pieces.append(gates[:, h:h + 1] * ocmp_ref[:, col]
                          + gates[:, N_HEADS + h:N_HEADS + h + 1] * osel_ref[:, col]
                          + gates[:, 2 * N_HEADS + h:2 * N_HEADS + h + 1] * owin_ref[:, col])
        o = jnp.concatenate(pieces, axis=-1)
    else:
        o_ref, yconv_ref, h_ref, wo_ref, g_ref, b_ref, y_ref = refs
        o = o_ref[...]
    mix = (_dot(o.astype(BF16), wo_ref[0:D_ATTN, :])
           + _dot(yconv_ref[...].astype(BF16), wo_ref[D_ATTN:D_MODEL, :]))
    y_ref[...] = _layer_norm(ALPHA * h_ref[...] + mix, g_ref[...], b_ref[...])


def _out_ln(branches, yconv, h2d, w_out, g, b, tm):
    rows = h2d.shape[0]
    gated = len(branches) > 1
    row = lambda i: (i, 0)
    const = lambda i: (0, 0)
    widths = [D_ATTN, D_ATTN, D_ATTN, LANES] if gated else [D_ATTN]
    return pl.pallas_call(
        functools.partial(_out_ln_body, gated=gated),
        grid=(rows // tm,),
        in_specs=[pl.BlockSpec((tm, w), row) for w in widths] + [
            pl.BlockSpec((tm, D_CONV), row),
            pl.BlockSpec((tm, D_MODEL), row),
            pl.BlockSpec((D_MODEL, D_MODEL), const),
            pl.BlockSpec((1, D_MODEL), const),
            pl.BlockSpec((1, D_MODEL), const),
        ],
        out_specs=pl.BlockSpec((tm, D_MODEL), row),
        out_shape=jax.ShapeDtypeStruct((rows, D_MODEL), F32),
        compiler_params=_cparams(("arbitrary",)),
        name="out_ln",
    )(*branches, yconv, h2d, w_out, g, b)


def _block_diag_groups(w):
    z = jnp.zeros_like(w)
    return jnp.concatenate([jnp.concatenate([w, z], axis=-1),
                            jnp.concatenate([z, w], axis=-1)], axis=-2).astype(BF16)


def _to_group_layout(x, nseq, s_new):
    x = x.reshape(nseq, s_new, N_KV, GQA_R, HEAD_DIM).transpose(0, 2, 1, 3, 4)
    return x.reshape(nseq, N_KV, s_new * GQA_R, HEAD_DIM)


def _from_group_layout(x, nseq, s_new):
    x = x.reshape(nseq, N_KV, s_new, GQA_R, HEAD_DIM).transpose(0, 2, 1, 3, 4)
    return x.reshape(nseq * s_new, D_ATTN)


def kernel(x_prompt, x_sample, cache_cmp_kv, cache_sel_kv, state_win_kv, state_conv, page_table,
           ffa_gate, ffa_up, ffa_down, ln1_g, ln1_b, w_in, b_in, w_conv, cmp_w1, cmp_w2, cmp_pe,
           w_out, ln2_g, ln2_b, ffb_gate, ffb_up, ffb_down, ln3_g, ln3_b):
    depth = ffa_gate.shape[0]
    assert depth == 1
    l = 0
    n, t, d = x_prompt.shape
    nseq, s_new, _ = x_sample.shape
    n_pages = page_table.shape[1]
    past = n_pages * PAGE_SIZE
    rows_p, rows_s = n * t, nseq * s_new
    kvshape = (2, N_KV, HEAD_DIM)

    bf = lambda w: w.astype(BF16)
    vec = lambda v: v.reshape(1, -1)
    w_main = bf(w_in[l][:, :D_MAIN])
    b_main = vec(b_in[l][:D_MAIN])
    w_gate = bf(jnp.pad(w_in[l][:, D_MAIN:], ((0, 0), (0, LANES - N_GATE))))
    b_gate = vec(jnp.pad(b_in[l][D_MAIN:], (0, LANES - N_GATE)))
    pe_t = jnp.concatenate([cmp_pe[l].transpose(1, 2, 0)] * BPP, axis=-1)
    w1_t = _block_diag_groups(cmp_w1[l].transpose(0, 2, 1, 3)).reshape(
        2, D_PAIRS, 2 * PAGE_SIZE, PAGE_SIZE)
    w2_t = _block_diag_groups(cmp_w2[l])
    w_out_b = bf(w_out[l])
    ffa = (bf(ffa_gate[l]), bf(ffa_up[l]), bf(ffa_down[l]), vec(ln1_g[l]), vec(ln1_b[l]))
    ffb = (bf(ffb_gate[l]), bf(ffb_up[l]), bf(ffb_down[l]), vec(ln3_g[l]), vec(ln3_b[l]))

    def k_pad(x, lo):
        z = jnp.zeros_like(x[..., :HEAD_DIM])
        return jnp.concatenate([x[..., lo:lo + HEAD_DIM], z, z, x[..., lo + HEAD_DIM:lo + 2 * HEAD_DIM]],
                               axis=-1)

    def prompt_cols(x):
        return jnp.concatenate([x[..., :_C_Q], k_pad(x, _C_KVS), k_pad(x, _C_KVW)], axis=-1)

    def prompt_rows(x):
        return jnp.pad(x[..., _C_Q:], [(0, 0)] * (x.ndim - 1) + [(0, GATE_ROWS - N_GATE)])

    w_rows, b_rows = bf(prompt_cols(w_in[l])), vec(prompt_cols(b_in[l]))
    w_tr, b_tr = bf(prompt_rows(w_in[l]).T), prompt_rows(b_in[l]).reshape(-1, 1)

    pps = t // PAGE_SIZE
    cmp_seqs = 4
    hp = _ffn_ln(x_prompt.reshape(rows_p, d), *ffa, tm=512)
    q, k_sel, k_win, kvc_t, kvs_t, kvw_t, gates_t, yconv, cstate = _proj_prompt(
        hp.reshape(n, t, d), w_rows, b_rows, w_tr, b_tr, w_conv[l], tm=512)
    kc = _compress(jnp.zeros((1,), jnp.int32), kvc_t, pe_t, w1_t, w2_t, n_steps=n // cmp_seqs,
                   n_pages=cmp_seqs * pps, pages_per_seq=pps, name="compress_prompt")
    kc = kc.reshape(n // cmp_seqs, 2, N_KV, cmp_seqs, pps, BPP, HEAD_DIM)
    kc = kc.transpose(0, 3, 2, 1, 4, 5, 6).reshape(n, N_KV, 2, pps * BPP, HEAD_DIM)
    zc = jnp.zeros_like(kc[:, 0, 0])
    kc_pad = bf(jnp.stack([jnp.concatenate([kc[:, 0, 0], zc], axis=-1),
                           jnp.concatenate([zc, kc[:, 1, 0]], axis=-1)], axis=1))
    vc_t = bf(jnp.swapaxes(kc[:, :, 1], -1, -2))
    o = _attn_prompt(q, k_sel, k_win, kvs_t, kvw_t, kc_pad, vc_t, gates_t)
    yp = _out_ln([o.reshape(rows_p, D_ATTN)], yconv.reshape(rows_p, D_CONV), hp, w_out_b,
                 vec(ln2_g[l]), vec(ln2_b[l]), tm=512)
    yp = _ffn_ln(yp, *ffb, tm=512).reshape(n, t, d)

    def from_feature_major(x_t):
        return jnp.moveaxis(x_t.reshape(x_t.shape[0], *kvshape, x_t.shape[-1]), -1, 1)[None]

    p_cmp = from_feature_major(kvc_t)
    p_sel = from_feature_major(kvs_t)
    p_win = from_feature_major(kvw_t[:, :, t - min(WINDOW, t):])
    p_conv = cstate[:, 8 - (CONV_W - 1):].reshape(1, n, CONV_W - 1, D_CONV)

    hs = _ffn_ln(x_sample.reshape(rows_s, d), *ffa, tm=rows_s)
    qs, kv3, gates_s, cb_s, u_s = _proj_sample(hs, w_main, b_main, w_gate, b_gate)
    kvc_s, kvs_s, kvw_s = kv3[:, :KV_W], kv3[:, KV_W:2 * KV_W], kv3[:, 2 * KV_W:]
    u3 = u_s.reshape(nseq, s_new, D_CONV)
    yconv_s = _conv_sample(cb_s.reshape(nseq, s_new, D_CONV), u3, state_conv[l], w_conv[l])
    s_conv = jnp.concatenate([state_conv[l], u3], axis=1)[:, -(CONV_W - 1):]

    def feature_major(x):
        return jnp.moveaxis(x, -4, -1).reshape(*x.shape[:-4], KV_W, x.shape[-4])

    def new_rows_t(x2d):
        xt = x2d.reshape(nseq, s_new, KV_W).transpose(0, 2, 1)
        return jnp.pad(xt, ((0, 0), (0, 0), (0, PAGE_SIZE - s_new)))

    kc_s = _compress(page_table.reshape(-1), feature_major(cache_cmp_kv[l]), pe_t, w1_t, w2_t,
                     n_steps=nseq, n_pages=n_pages, pages_per_seq=None, name="compress_decode")
    q_g = _to_group_layout(qs, nseq, s_new)
    zq = jnp.zeros_like(q_g)
    q2 = jnp.concatenate([jnp.concatenate([q_g, zq], axis=-1), jnp.concatenate([zq, q_g], axis=-1)], axis=-2)
    ocmp_c, idx_pad = _dec_cmp(q2, kc_s, past, s_new)
    idx = idx_pad[:, :N_SEL]
    idx2 = jnp.pad(bf(idx.reshape(nseq, N_KV, s_new * N_SEL)), ((0, 0), (0, s_new * GQA_R - N_KV), (0, 0)))
    win_t = feature_major(state_win_kv[l])
    kvw_new_t = new_rows_t(kvw_s)
    osel_c, owin_c = _dec_sel_win(
        idx.reshape(-1), page_table.reshape(-1), q_g, idx2, feature_major(cache_sel_kv[l]),
        new_rows_t(kvs_s), win_t, kvw_new_t, past, s_new, n_pages)
    branches = [_from_group_layout(x, nseq, s_new) for x in (ocmp_c, osel_c, owin_c)]
    ys = _out_ln(branches + [gates_s], yconv_s.reshape(rows_s, D_CONV), hs, w_out_b,
                 vec(ln2_g[l]), vec(ln2_b[l]), tm=rows_s)
    ys = _ffn_ln(ys, *ffb, tm=rows_s).reshape(nseq, s_new, d)
    n_keep = min(WINDOW, past + s_new)
    ctx_t = jnp.concatenate([win_t, kvw_new_t[:, :, :s_new]], axis=-1)[:, :, -n_keep:]
    s_cmp = kvc_s.reshape(1, nseq, s_new, *kvshape)
    s_sel = kvs_s.reshape(1, nseq, s_new, *kvshape)
    s_win = jnp.moveaxis(ctx_t.reshape(nseq, *kvshape, n_keep), -1, 1)[None]
    return (yp, ys, p_cmp, p_sel, p_win, p_conv, s_cmp, s_sel, s_win,
            s_conv.reshape(1, nseq, CONV_W - 1, D_CONV))
```

```python
import functools

import jax
import jax.numpy as jnp
from jax import lax
from jax.experimental import pallas as pl
from jax.experimental.pallas import tpu as pltpu

F32 = jnp.float32
BF16 = jnp.bfloat16

D_MODEL = 1024
D_CONV = 512
D_ATTN = 512
HEAD_DIM = 64
N_HEADS = 8
N_KV = 2
GQA_R = N_HEADS // N_KV
N_BR = 3
CONV_W = 3
BLK = 64
BLK_SHIFT = 6
N_SEL = 16
WINDOW = 512
PAGE_SIZE = 128
PAGE_SHIFT = 7
D_FF = 2816
KV_W = 2 * N_KV * HEAD_DIM
D_MAIN = 3 * D_CONV + D_ATTN + 3 * KV_W
N_GATE = N_BR * N_HEADS
ALPHA = 2.0 ** 0.25
SCALE = HEAD_DIM ** -0.5
LN_EPS = 1e-5
NEG = -1e30
FORCE = 1e4

LANES = 128
VMEM_LIMIT_BYTES = 56 * 1024 * 1024


def _cparams(sem):
    return pltpu.CompilerParams(dimension_semantics=sem, vmem_limit_bytes=VMEM_LIMIT_BYTES)


def _layer_norm(y, g, b):
    mu = jnp.mean(y, axis=-1, keepdims=True)
    yc = y - mu
    var = jnp.mean(yc * yc, axis=-1, keepdims=True)
    return yc * lax.rsqrt(var + LN_EPS) * g + b


def _dot(a, b):
    return jnp.dot(a, b, preferred_element_type=F32)


def _dot_nt(a, b):
    return lax.dot_general(a, b, (((1,), (1,)), ((), ())), preferred_element_type=F32)


FFN_CHUNK = 256


def _ffn_ln_body(x_ref, wg_ref, wu_ref, wd_ref, g_ref, b_ref, o_ref):
    x = x_ref[...]
    xb = x.astype(BF16)
    acc = jnp.zeros(x.shape, F32)
    for c in range(D_FF // FFN_CHUNK):
        sl = slice(c * FFN_CHUNK, (c + 1) * FFN_CHUNK)
        gg = _dot(xb, wg_ref[:, sl])
        uu = _dot(xb, wu_ref[:, sl])
        hh = (gg * jax.nn.sigmoid(gg) * uu).astype(BF16)
        acc = acc + _dot(hh, wd_ref[sl, :])
    o_ref[...] = _layer_norm(ALPHA * x + 0.5 * acc, g_ref[...], b_ref[...])


def _ffn_ln(x2d, wg, wu, wd, g, b, tm):
    rows = x2d.shape[0]
    const = lambda i: (0, 0)
    return pl.pallas_call(
        _ffn_ln_body,
        grid=(rows // tm,),
        in_specs=[
            pl.BlockSpec((tm, D_MODEL), lambda i: (i, 0)),
            pl.BlockSpec((D_MODEL, D_FF), const),
            pl.BlockSpec((D_MODEL, D_FF), const),
            pl.BlockSpec((D_FF, D_MODEL), const),
            pl.BlockSpec((1, D_MODEL), const),
            pl.BlockSpec((1, D_MODEL), const),
        ],
        out_specs=pl.BlockSpec((tm, D_MODEL), lambda i: (i, 0)),
        out_shape=jax.ShapeDtypeStruct((rows, D_MODEL), F32),
        compiler_params=_cparams(("arbitrary",)),
        name="ffn_ln",
    )(x2d, wg, wu, wd, g, b)


_C_CB, _C_CC, _C_CH, _C_Q = 0, D_CONV, 2 * D_CONV, 3 * D_CONV
_C_KVC = 3 * D_CONV + D_ATTN
_C_KVS = _C_KVC + KV_W
_C_KVW = _C_KVS + KV_W


def _proj_piece(hb, w_ref, b_ref, lo, width):
    return _dot(hb, w_ref[:, lo:lo + width]) + b_ref[:, lo:lo + width]


_P_K = 3 * D_CONV
K_HALF = N_KV * HEAD_DIM
P_ROW_COLS = _P_K + 2 * K_HALF
GATE_ROWS = 32
_T_KV = D_ATTN
_T_GATE = _T_KV + 3 * KV_W
P_T_ROWS = _T_GATE + GATE_ROWS


def _proj_prompt_body(h_ref, w_ref, b_ref, wt_ref, bt_ref, wc_ref,
                      qt_ref, ks_ref, kw_ref, kvct_ref, kvst_ref, kvwt_ref, gates_ref,
                      yconv_ref, cstate_ref, ubuf):
    i = pl.program_id(1)
    tm = h_ref.shape[0]
    hb = h_ref[...].astype(BF16)

    def piece_t(lo, height):
        return _dot_nt(wt_ref[lo:lo + height, :], hb) + bt_ref[lo:lo + height, :]

    qt_ref[...] = (piece_t(0, D_ATTN) * (SCALE * LOG2E)).astype(BF16)
    lane = lax.broadcasted_iota(jnp.int32, (1, KV_W), 1)
    blk = (i * tm + lax.broadcasted_iota(jnp.int32, (tm, 1), 0)) >> BLK_SHIFT
    aux0, aux1 = HEAD_DIM, 2 * HEAD_DIM
    one = (lane == aux0 + AUX_BLOCKS) | (lane == aux1 + AUX_BLOCKS)
    is_k = (lane < aux0) | (lane >= aux1 + HEAD_DIM)
    k4 = _proj_piece(hb, w_ref, b_ref, _P_K, 2 * K_HALF)
    for ref, lo, ones in ((ks_ref, 0, one | (lane == aux0 + blk) | (lane == aux1 + blk)),
                          (kw_ref, K_HALF, one)):
        k = k4[:, lo:lo + K_HALF]
        ref[...] = jnp.where(is_k, jnp.concatenate([k, k], axis=1),
                             jnp.where(ones, 1.0, 0.0)).astype(BF16)
    for k, ref in enumerate((kvct_ref, kvst_ref, kvwt_ref)):
        ref[...] = piece_t(_T_KV + k * KV_W, KV_W)
    gates_ref[...] = jax.nn.sigmoid(piece_t(_T_GATE, GATE_ROWS))
    cb = _proj_piece(hb, w_ref, b_ref, _C_CB, D_CONV)
    u = _proj_piece(hb, w_ref, b_ref, _C_CC, D_CONV) * _proj_piece(hb, w_ref, b_ref, _C_CH, D_CONV)

    @pl.when(i == 0)
    def _():
        ubuf[0:8, :] = jnp.zeros((8, D_CONV), F32)

    ubuf[8:8 + tm, :] = u
    u1 = ubuf[7:7 + tm, :]
    u2 = ubuf[6:6 + tm, :]
    z = u2 * wc_ref[0:1, :] + u1 * wc_ref[1:2, :] + u * wc_ref[2:3, :]
    yconv_ref[...] = cb * z
    tail = u[tm - 8:tm, :]
    ubuf[0:8, :] = tail
    cstate_ref[...] = tail


def _proj_prompt(h, w_rows, b_rows, w_t, b_t, w_conv, tm):
    n, t, _ = h.shape
    const = lambda a, i: (0, 0)
    row = lambda a, i: (a, i, 0)
    col = lambda a, i: (a, 0, i)

    def out(width, dtype=F32):
        return jax.ShapeDtypeStruct((n, t, width), dtype)

    def out_t(rows, dtype=F32):
        return jax.ShapeDtypeStruct((n, rows, t), dtype)

    return pl.pallas_call(
        _proj_prompt_body,
        grid=(n, t // tm),
        in_specs=[
            pl.BlockSpec((None, tm, D_MODEL), row),
            pl.BlockSpec((D_MODEL, P_ROW_COLS), const),
            pl.BlockSpec((1, P_ROW_COLS), const),
            pl.BlockSpec((P_T_ROWS, D_MODEL), const),
            pl.BlockSpec((P_T_ROWS, 1), const),
            pl.BlockSpec((CONV_W, D_CONV), const),
        ],
        out_specs=[
            pl.BlockSpec((None, D_ATTN, tm), col),
            pl.BlockSpec((None, tm, KV_W), row),
            pl.BlockSpec((None, tm, KV_W), row),
            pl.BlockSpec((None, KV_W, tm), col),
            pl.BlockSpec((None, KV_W, tm), col),
            pl.BlockSpec((None, KV_W, tm), col),
            pl.BlockSpec((None, GATE_ROWS, tm), col),
            pl.BlockSpec((None, tm, D_CONV), row),
            pl.BlockSpec((None, 8, D_CONV), lambda a, i: (a, 0, 0)),
        ],
        out_shape=[out_t(D_ATTN, BF16), out(KV_W, BF16), out(KV_W, BF16), out_t(KV_W), out_t(KV_W),
                   out_t(KV_W), out_t(GATE_ROWS), out(D_CONV), jax.ShapeDtypeStruct((n, 8, D_CONV), F32)],
        scratch_shapes=[pltpu.VMEM((tm + 8, D_CONV), F32)],
        compiler_params=_cparams(("arbitrary", "arbitrary")),
        name="proj_prompt",
    )(h, w_rows, b_rows, w_t, b_t, w_conv)


def _proj_sample_body(h_ref, w_ref, b_ref, wgt_ref, bgt_ref, q_ref, kv_ref, gates_ref, cb_ref, u_ref):
    hb = h_ref[...].astype(BF16)
    q_ref[...] = (_proj_piece(hb, w_ref, b_ref, _C_Q, D_ATTN) * SCALE).astype(BF16)
    kv_ref[...] = _proj_piece(hb, w_ref, b_ref, _C_KVC, 3 * KV_W)
    gates_ref[...] = jax.nn.sigmoid(_dot(hb, wgt_ref[...]) + bgt_ref[...])
    cb_ref[...] = _proj_piece(hb, w_ref, b_ref, _C_CB, D_CONV)
    u_ref[...] = _proj_piece(hb, w_ref, b_ref, _C_CC, D_CONV) * _proj_piece(hb, w_ref, b_ref, _C_CH, D_CONV)


def _proj_sample(h2d, w_main, b_main, w_gate, b_gate):
    rows = h2d.shape[0]

    def out(width, dtype=F32):
        return jax.ShapeDtypeStruct((rows, width), dtype)

    return pl.pallas_call(
        _proj_sample_body,
        out_shape=[out(D_ATTN, BF16), out(3 * KV_W), out(LANES), out(D_CONV), out(D_CONV)],
        compiler_params=pltpu.CompilerParams(vmem_limit_bytes=VMEM_LIMIT_BYTES),
        name="proj_sample",
    )(h2d, w_main, b_main, w_gate, b_gate)


def _conv_sample_body(cb_ref, u_ref, prev_ref, wc_ref, y_ref):
    s = u_ref.shape[1]
    ext = [prev_ref[:, 0, :], prev_ref[:, 1, :]] + [u_ref[:, t, :] for t in range(s)]
    for t in range(s):
        z = ext[t] * wc_ref[0:1, :] + ext[t + 1] * wc_ref[1:2, :] + ext[t + 2] * wc_ref[2:3, :]
        y_ref[:, t, :] = cb_ref[:, t, :] * z


def _conv_sample(cb, u, prev, w_conv):
    return pl.pallas_call(
        _conv_sample_body,
        out_shape=jax.ShapeDtypeStruct(u.shape, F32),
        name="conv_sample",
    )(cb, u, prev, w_conv)


BPP = PAGE_SIZE // BLK
D_PAIRS = HEAD_DIM // 2


def _compress_body(pt_ref, src_ref, pe_ref, w1_ref, w2_ref, kc_ref, buf, sem, *, n_pages, pages_per_seq):
    n = pl.program_id(0)
    slot = n % 2

    def page_copy(step, j, sl):
        dst = buf.at[sl, :, j, :]
        if pages_per_seq is None:
            src = src_ref.at[pt_ref[step * n_pages + j]]
        else:
            seq = step * (n_pages // pages_per_seq) + j // pages_per_seq
            src = src_ref.at[seq, :, pl.ds((j % pages_per_seq) * PAGE_SIZE, PAGE_SIZE)]
        return pltpu.make_async_copy(src, dst, sem.at[sl])

    def start(step, sl):
        for j in range(n_pages):
            page_copy(step, j, sl).start()

    @pl.when(n == 0)
    def _():
        start(n, slot)

    @pl.when(n + 1 < pl.num_programs(0))
    def _():
        start(n + 1, 1 - slot)

    for j in range(n_pages):
        page_copy(n, j, slot).wait()

    def feature_rows(f):
        return buf[slot, f]

    for c in range(2):
        acc = None
        for dp in range(D_PAIRS):
            rows = []
            for g in range(N_KV):
                f0 = (c * N_KV + g) * HEAD_DIM + 2 * dp
                rows.append(jnp.concatenate(
                    [feature_rows(f0) + pe_ref[c, 2 * dp:2 * dp + 1, :],
                     feature_rows(f0 + 1) + pe_ref[c, 2 * dp + 1:2 * dp + 2, :]], axis=-1))
            x = jnp.concatenate(rows, axis=0).astype(BF16)
            d = _dot(x, w1_ref[c, dp])
            acc = d if acc is None else acc + d
        hid = (acc * jax.nn.sigmoid(acc)).astype(BF16)
        kc_ref[c] = _dot(hid, w2_ref[c])


def _compress(pt_flat, src, pe_t, w1_t, w2_t, n_steps, n_pages, pages_per_seq, name):
    const3 = lambda a, pt: (0, 0, 0)
    grid_spec = pltpu.PrefetchScalarGridSpec(
        num_scalar_prefetch=1,
        grid=(n_steps,),
        in_specs=[
            pl.BlockSpec(memory_space=pl.ANY),
            pl.BlockSpec((2, HEAD_DIM, PAGE_SIZE), const3),
            pl.BlockSpec((2, D_PAIRS, 2 * PAGE_SIZE, PAGE_SIZE), lambda a, pt: (0, 0, 0, 0)),
            pl.BlockSpec((2, PAGE_SIZE, PAGE_SIZE), const3),
        ],
        out_specs=pl.BlockSpec((None, 2, N_KV * n_pages, PAGE_SIZE), lambda a, pt: (a, 0, 0, 0)),
        scratch_shapes=[pltpu.VMEM((2, KV_W, n_pages, PAGE_SIZE), F32),
                        pltpu.SemaphoreType.DMA((2,))],
    )
    return pl.pallas_call(
        functools.partial(_compress_body, n_pages=n_pages, pages_per_seq=pages_per_seq),
        grid_spec=grid_spec,
        out_shape=jax.ShapeDtypeStruct((n_steps, 2, N_KV * n_pages, PAGE_SIZE), F32),
        compiler_params=_cparams(("arbitrary",)),
        name=name,
    )(pt_flat, src, pe_t, w1_t, w2_t)


ATT_TQ = 256
ATT_TK = 256
WIN_TILES = WINDOW // ATT_TK + 1
AUX_BLOCKS = 32
LOG2E = 1.4426950408889634


V_ROWS = HEAD_DIM + 16


def _attn_update(s_t, v_t, state):
    m, acc = state
    m_new = jnp.maximum(m, jnp.max(s_t, axis=0, keepdims=True))
    e = jnp.exp2(s_t - m_new)
    acc = jnp.exp2(m - m_new) * acc + _dot(v_t, e.astype(BF16))
    return m_new, acc


def _attn_prompt_body(qt_ref, ks_ref, kw_ref, vs_ref, vw_ref, kc_ref, vct_ref, gates_ref, wb_ref, o_ref):
    tq, tk = ATT_TQ, ATT_TK
    qi = pl.program_id(1)
    nb = kc_ref.shape[1]
    t0 = qi * tq
    kd = qi
    tpos = t0 + lax.broadcasted_iota(jnp.int32, (1, tq), 1)
    jrow = lax.broadcasted_iota(jnp.int32, (nb, 1), 0)
    vis = ((jrow + 1) * BLK - 1) <= tpos
    cur = tpos >> BLK_SHIFT

    def heads_of(x):
        return jnp.concatenate([x] * GQA_R, axis=1)

    def group_queries(g, block_bias=None, switch=None):
        zeros = jnp.zeros((AUX_BLOCKS, tq), BF16)
        top = zeros if block_bias is None else block_bias
        low = zeros if switch is None else jnp.broadcast_to(switch, (AUX_BLOCKS, tq)).astype(BF16)
        aux = jnp.concatenate([top, low], axis=0)
        pieces = []
        for r in range(GQA_R):
            h = g * GQA_R + r
            qh = qt_ref[h * HEAD_DIM:(h + 1) * HEAD_DIM, :]
            pieces.append(jnp.concatenate([qh, aux] if g == 0 else [aux, qh], axis=0))
        return jnp.concatenate(pieces, axis=1)

    q4 = [group_queries(g) for g in range(N_KV)]
    vis4 = heads_of(vis)

    o_cmp, q4_sel = [], []
    for g in range(N_KV):
        s = jnp.where(vis4, _dot(kc_ref[g], q4[g]), NEG)
        m = jnp.max(s, axis=0, keepdims=True)
        e = jnp.where(vis4, jnp.exp2(s - m), 0.0)
        l = jnp.sum(e, axis=0, keepdims=True)
        p = e / jnp.where(l > 0.0, l, 1.0)
        imp = sum(p[:, r * tq:(r + 1) * tq] for r in range(GQA_R))
        o_cmp.append(_dot(vct_ref[g], p.astype(BF16)))

        score = jnp.where(jrow > cur, NEG, imp)
        score = jnp.where((jrow == 0) | (jrow == cur) | (jrow == cur - 1), FORCE, score)
        rank = jnp.zeros((nb, tq), F32)
        for j2 in range(nb):
            row = score[j2:j2 + 1, :]
            tie = jnp.where(jrow > j2, 1.0, 0.0)
            rank = rank + jnp.where(row > score, 1.0, jnp.where(row == score, tie, 0.0))
        q4_sel.append(group_queries(g, jnp.where(rank < N_SEL, 0.0, NEG).astype(BF16)))

    def key_tile(k_ref, kt, g):
        return k_ref[pl.ds(pl.multiple_of(kt * tk, tk), tk), g * LANES:(g + 1) * LANES]

    ones_rows = jnp.ones((V_ROWS - HEAD_DIM, tk), BF16)

    def val_tile(v_ref, kt, g):
        v = v_ref[g * HEAD_DIM:(g + 1) * HEAD_DIM, pl.ds(pl.multiple_of(kt * tk, tk), tk)]
        return jnp.concatenate([v.astype(BF16), ones_rows], axis=0)

    wide = GQA_R * tq
    init = (jnp.full((1, wide), NEG, F32), jnp.zeros((V_ROWS, wide), F32))

    def win_scores(back, g):
        kt = jnp.maximum(kd - back, 0)
        qw = q4[g] if back == 0 else group_queries(g, None, jnp.where(kd >= back, 0.0, NEG))
        s = _dot(key_tile(kw_ref, kt, g), qw)
        if back in (0, WIN_TILES - 1):
            s = s + heads_of(wb_ref[min(back, 1)])
        return s

    groups = range(N_KV)
    s_diag = [_dot(key_tile(ks_ref, kd, g), q4_sel[g]) + heads_of(wb_ref[0]) for g in groups]
    s_win = [[win_scores(back, g) for g in groups] for back in range(WIN_TILES)]
    sel = tuple(_attn_update(s_diag[g], val_tile(vs_ref, kd, g), init) for g in groups)
    win = [init] * N_KV
    for back in range(WIN_TILES):
        kt = jnp.maximum(kd - back, 0)
        win = [_attn_update(s_win[back][g], val_tile(vw_ref, kt, g), win[g]) for g in groups]

    def sel_tiles(kts, states):
        s = [[_dot(key_tile(ks_ref, kt, g), q4_sel[g]) for g in groups] for kt in kts]
        for i, kt in enumerate(kts):
            states = tuple(_attn_update(s[i][g], val_tile(vs_ref, kt, g), states[g]) for g in groups)
        return states

    sel = lax.cond((kd & 1) == 1, lambda st: sel_tiles((kd - 1,), st), lambda st: st, sel)
    sel = lax.fori_loop(0, kd >> 1, lambda j, st: sel_tiles((2 * j, 2 * j + 1), st), sel)

    gates = gates_ref[...]
    outs = []
    for h in range(N_HEADS):
        g, r = h // GQA_R, h % GQA_R
        cols = slice(r * tq, (r + 1) * tq)
        o_sel = sel[g][1][:HEAD_DIM, cols] * (1.0 / sel[g][1][HEAD_DIM:HEAD_DIM + 1, cols])
        o_win = win[g][1][:HEAD_DIM, cols] * (1.0 / win[g][1][HEAD_DIM:HEAD_DIM + 1, cols])
        outs.append(gates[h:h + 1, :] * o_cmp[g][:, cols] + gates[N_HEADS + h:N_HEADS + h + 1, :] * o_sel
                    + gates[2 * N_HEADS + h:2 * N_HEADS + h + 1, :] * o_win)
    o_ref[...] = jnp.concatenate(outs, axis=0).T


def _attn_prompt(q_t, k_sel, k_win, kvs_t, kvw_t, kc_pad, vc_t, gates_t):
    n, _, t = q_t.shape
    nb = kc_pad.shape[2]
    row = lambda a, i: (a, i, 0)
    seq = lambda a, i: (a, 0, 0)
    seq4 = lambda a, i: (a, 0, 0, 0)
    v_rows = lambda a, i: (a, 1, 0)

    def win_bias(back):
        dist = back * ATT_TK + jnp.arange(ATT_TQ)[None, :] - jnp.arange(ATT_TK)[:, None]
        return jnp.where((dist >= 0) & (dist < WINDOW), 0.0, NEG).astype(F32)

    wb = jnp.stack([win_bias(0), win_bias(WIN_TILES - 1)])
    return pl.pallas_call(
        _attn_prompt_body,
        grid=(n, t // ATT_TQ),
        in_specs=[
            pl.BlockSpec((None, D_ATTN, ATT_TQ), lambda a, i: (a, 0, i)),
            pl.BlockSpec((None, t, KV_W), seq),
            pl.BlockSpec((None, t, KV_W), seq),
            pl.BlockSpec((None, N_KV * HEAD_DIM, t), v_rows),
            pl.BlockSpec((None, N_KV * HEAD_DIM, t), v_rows),
            pl.BlockSpec((None, N_KV, nb, LANES), seq4),
            pl.BlockSpec((None, N_KV, HEAD_DIM, nb), seq4),
            pl.BlockSpec((None, GATE_ROWS, ATT_TQ), lambda a, i: (a, 0, i)),
            pl.BlockSpec((2, ATT_TK, ATT_TQ), lambda a, i: (0, 0, 0)),
        ],
        out_specs=pl.BlockSpec((None, ATT_TQ, D_ATTN), row),
        out_shape=jax.ShapeDtypeStruct((n, t, D_ATTN), F32),
        compiler_params=_cparams(("arbitrary", "arbitrary")),
        name="attn_prompt",
    )(q_t, k_sel, k_win, kvs_t, kvw_t, kc_pad, vc_t, gates_t, wb)


DEC_SEQ_PER_STEP = 8


def _dec_cmp_body(q2_ref, kc_ref, ocmp_ref, idx_ref, p_buf, imp_buf, *, past, s_new):
    assert BPP == 2
    ns = q2_ref.shape[0]
    qrows = s_new * GQA_R
    n_pages = kc_ref.shape[2] // N_KV
    nb = n_pages * BPP
    width = imp_buf.shape[1]
    imp_buf[...] = jnp.zeros(imp_buf.shape, F32)
    lane = lax.broadcasted_iota(jnp.int32, (1, width), 1)
    jc = jnp.where(lane < n_pages, BPP * lane, BPP * (lane - n_pages) + 1)
    jc = jnp.where(lane == nb, nb, jc)

    row2 = lax.broadcasted_iota(jnp.int32, (2 * qrows, 1), 0)
    half = jnp.where(row2 >= qrows, 1, 0)
    blk = BPP * lax.broadcasted_iota(jnp.int32, (1, n_pages), 1) + half
    tok = (row2 & (qrows - 1)) >> (GQA_R.bit_length() - 1)
    vis = ((blk + 1) * BLK - 1) <= (past + tok)

    def both(x):
        return jnp.concatenate([x, x], axis=0)

    def pages(a, c, g):
        return kc_ref[a, c, g * n_pages:(g + 1) * n_pages, :].astype(BF16)

    pairs = [(a, g) for a in range(ns) for g in range(N_KV)]
    scores = [_dot_nt(q2_ref[a, g], pages(a, 0, g)) for a, g in pairs]
    for i, (a, g) in enumerate(pairs):
        s = jnp.where(vis, scores[i], NEG)
        m = jnp.max(s, axis=-1, keepdims=True)
        e = jnp.where(vis, jnp.exp(s - both(jnp.maximum(m[:qrows], m[qrows:]))), 0.0)
        l = jnp.sum(e, axis=-1, keepdims=True)
        l = l[:qrows] + l[qrows:]
        p = e / both(jnp.where(l > 0.0, l, 1.0))
        o2 = _dot(p.astype(BF16), pages(a, 1, g))
        ocmp_ref[a, g] = o2[:qrows, :HEAD_DIM] + o2[qrows:, HEAD_DIM:]
        p_buf[i] = p
        imp = sum(p_buf[i, pl.ds(r, 2 * s_new, stride=GQA_R), :] for r in range(GQA_R))
        row0 = i * s_new
        imp_buf[row0:row0 + s_new, 0:n_pages] = imp[:s_new]
        imp_buf[row0:row0 + s_new, n_pages:nb] = imp[s_new:]

    rows = ns * N_KV * s_new
    imp = imp_buf[...]
    t_row = lax.broadcasted_iota(jnp.int32, (rows, 1), 0) & (s_new - 1)
    cur = (past + t_row) >> BLK_SHIFT
    score = jnp.where(jc > cur, NEG, imp)
    score = jnp.where((jc == 0) | (jc == cur) | (jc == cur - 1), FORCE, score)
    score = jnp.where(lane <= nb, score, -jnp.inf)
    jcf = jc.astype(F32)
    out_lane = lax.broadcasted_iota(jnp.int32, (1, LANES), 1)
    idx = jnp.zeros((rows, LANES), jnp.int32)
    for k in range(N_SEL):
        m = jnp.max(score, axis=-1, keepdims=True)
        first = jnp.min(jnp.where(score == m, jcf, 1e9), axis=-1, keepdims=True)
        idx = jnp.where(out_lane == k, first.astype(jnp.int32), idx)
        score = jnp.where(jcf == first, -jnp.inf, score)
    idx_ref[...] = idx


def _dec_cmp(q2, kc, past, s_new):
    nseq = q2.shape[0]
    qrows = s_new * GQA_R
    rows_kc = kc.shape[2]
    n_pages = rows_kc // N_KV
    nb = n_pages * BPP
    ns = DEC_SEQ_PER_STEP
    width = (nb + 1 + LANES - 1) // LANES * LANES
    seq4 = lambda i: (i, 0, 0, 0)
    return pl.pallas_call(
        functools.partial(_dec_cmp_body, past=past, s_new=s_new),
        grid=(nseq // ns,),
        in_specs=[
            pl.BlockSpec((ns, N_KV, 2 * qrows, 2 * HEAD_DIM), seq4),
            pl.BlockSpec((ns, 2, rows_kc, PAGE_SIZE), seq4),
        ],
        out_specs=[
            pl.BlockSpec((ns, N_KV, qrows, HEAD_DIM), seq4),
            pl.BlockSpec((ns * N_KV * s_new, LANES), lambda i: (i, 0)),
        ],
        out_shape=[jax.ShapeDtypeStruct((nseq, N_KV, qrows, HEAD_DIM), F32),
                   jax.ShapeDtypeStruct((nseq * N_KV * s_new, LANES), jnp.int32)],
        scratch_shapes=[pltpu.VMEM((ns * N_KV, 2 * qrows, n_pages), F32),
                        pltpu.VMEM((ns * N_KV * s_new, width), F32)],
        compiler_params=_cparams(("arbitrary",)),
        name="dec_cmp_select",
    )(q2, kc)


def _dec_sel_win_body(idx_ref, pt_ref, q_ref, idx2_ref, expand_ref, pool_ref, newblk_ref, win_ref,
                      kvw_new_ref, osel_ref, owin_ref, kvbuf, sem, *, past, s_new, n_pages):
    n_past_blk = past // BLK
    n = pl.program_id(0)
    nseq = pl.num_programs(0)
    slot = n % 2
    rows = s_new * GQA_R
    seg_lanes = N_SEL * PAGE_SIZE
    nkeys = s_new * seg_lanes

    def block_copy(seq, c, k, sl):
        g, t = c // s_new, c % s_new
        j = jnp.minimum(idx_ref[(seq * N_KV * s_new + c) * N_SEL + k], n_past_blk - 1)
        page = pt_ref[seq * n_pages + (j >> 1)]
        dst = kvbuf.at[sl, :, g, :, pl.ds((t * N_SEL + k) * PAGE_SIZE, PAGE_SIZE)]
        return pltpu.make_async_copy(pool_ref.at[page, :, g], dst, sem.at[sl])

    def start(seq, sl):
        for c in range(N_KV * s_new):
            for k in range(N_SEL):
                block_copy(seq, c, k, sl).start()

    @pl.when(n == 0)
    def _():
        start(n, slot)

    @pl.when(n + 1 < nseq)
    def _():
        start(n + 1, 1 - slot)

    for c in range(N_KV * s_new):
        for k in range(N_SEL):
            block_copy(n, c, k, slot).wait()

    t_row = lax.broadcasted_iota(jnp.int32, (rows, 1), 0) >> (GQA_R.bit_length() - 1)
    qpos = past + t_row
    lane = lax.broadcasted_iota(jnp.int32, (1, nkeys), 1)
    own = (lane >> (seg_lanes.bit_length() - 1)) == t_row
    row_in_page = lane & (PAGE_SIZE - 1)
    nlane = lax.broadcasted_iota(jnp.int32, (1, PAGE_SIZE), 1)
    nwin = win_ref.shape[1]
    wpos = past - nwin + lax.broadcasted_iota(jnp.int32, (1, nwin), 1)
    groups = range(N_KV)

    def krows(g):
        return pl.ds(g * HEAD_DIM, HEAD_DIM)

    def vrows(g):
        return pl.ds((N_KV + g) * HEAD_DIM, HEAD_DIM)

    q = [q_ref[g] for g in groups]
    jv2 = _dot(idx2_ref[...], expand_ref[...])
    s_sel = [_dot(q[g], kvbuf[slot, 0, g].astype(BF16)) for g in groups]
    s_new_sel = [_dot(q[g], newblk_ref[krows(g), :].astype(BF16)) for g in groups]
    s_win = [_dot(q[g], win_ref[krows(g), :].astype(BF16)) for g in groups]
    s_new_win = [_dot(q[g], kvw_new_ref[krows(g), :].astype(BF16)) for g in groups]

    def two_piece_softmax(sa, va, sb, vb):
        m = jnp.maximum(jnp.max(sa, axis=-1, keepdims=True), jnp.max(sb, axis=-1, keepdims=True))
        ea, eb = jnp.exp(sa - m), jnp.exp(sb - m)
        l = jnp.sum(ea, axis=-1, keepdims=True) + jnp.sum(eb, axis=-1, keepdims=True)
        return (_dot_nt(ea.astype(BF16), va) + _dot_nt(eb.astype(BF16), vb)) / l

    for g in groups:
        jv = jv2[g:g + 1, :].astype(jnp.int32)
        kpos = jv * BLK + (row_in_page & (BLK - 1))
        valid = (own & (jv < n_past_blk) & ((row_in_page >> BLK_SHIFT) == (jv & (BPP - 1)))
                 & (kpos <= qpos))
        has_new = jnp.max(jnp.where(own & (jv >= n_past_blk), 1.0, 0.0), axis=-1, keepdims=True)
        valid_n = (has_new > 0.5) & (nlane < BLK) & (n_past_blk * BLK + nlane <= qpos)
        osel_ref[g] = two_piece_softmax(
            jnp.where(valid, s_sel[g], NEG), kvbuf[slot, 1, g].astype(BF16),
            jnp.where(valid_n, s_new_sel[g], NEG), newblk_ref[vrows(g), :].astype(BF16))

        dw = qpos - wpos
        dn = qpos - (past + nlane)
        valid_w = (dw >= 0) & (dw < WINDOW) & (wpos >= 0)
        valid_n = (dn >= 0) & (dn < WINDOW) & (nlane < s_new)
        owin_ref[g] = two_piece_softmax(
            jnp.where(valid_w, s_win[g], NEG), win_ref[vrows(g), :].astype(BF16),
            jnp.where(valid_n, s_new_win[g], NEG), kvw_new_ref[vrows(g), :].astype(BF16))


def _dec_sel_win(idx_flat, pt_flat, q_g, idx2, pool_t, newblk_t, win_t, kvw_new_t, past, s_new, n_pages):
    nseq = q_g.shape[0]
    rows = s_new * GQA_R
    nwin = win_t.shape[2]
    nkeys = s_new * N_SEL * PAGE_SIZE
    n_phys = pool_t.shape[0]
    expand = (jnp.arange(nkeys)[None, :] // PAGE_SIZE == jnp.arange(s_new * N_SEL)[:, None]).astype(BF16)
    per_seq = lambda a, idx, pt: (a, 0, 0, 0)
    per_seq3 = lambda a, idx, pt: (a, 0, 0)
    grid_spec = pltpu.PrefetchScalarGridSpec(
        num_scalar_prefetch=2,
        grid=(nseq,),
        in_specs=[
            pl.BlockSpec((None, N_KV, rows, HEAD_DIM), per_seq),
            pl.BlockSpec((None, rows, s_new * N_SEL), per_seq3),
            pl.BlockSpec((s_new * N_SEL, nkeys), lambda a, idx, pt: (0, 0)),
            pl.BlockSpec(memory_space=pl.ANY),
            pl.BlockSpec((None, KV_W, PAGE_SIZE), per_seq3),
            pl.BlockSpec((None, KV_W, nwin), per_seq3),
            pl.BlockSpec((None, KV_W, PAGE_SIZE), per_seq3),
        ],
        out_specs=[
            pl.BlockSpec((None, N_KV, rows, HEAD_DIM), per_seq),
            pl.BlockSpec((None, N_KV, rows, HEAD_DIM), per_seq),
        ],
        scratch_shapes=[pltpu.VMEM((2, 2, N_KV, HEAD_DIM, nkeys), F32),
                        pltpu.SemaphoreType.DMA((2,))],
    )
    shape = jax.ShapeDtypeStruct((nseq, N_KV, rows, HEAD_DIM), F32)
    return pl.pallas_call(
        functools.partial(_dec_sel_win_body, past=past, s_new=s_new, n_pages=n_pages),
        grid_spec=grid_spec,
        out_shape=[shape, shape],
        compiler_params=_cparams(("arbitrary",)),
        name="dec_sel_win",
    )(idx_flat, pt_flat, q_g, idx2, expand, pool_t.reshape(n_phys, 2, N_KV, HEAD_DIM, PAGE_SIZE),
      newblk_t, win_t, kvw_new_t)


def _out_ln_body(*refs, gated):
    if gated:
        ocmp_ref, osel_ref, owin_ref, gates_ref, yconv_ref, h_ref, wo_ref, g_ref, b_ref, y_ref = refs
        gates = gates_ref[...]
        pieces = []
        for h in range(N_HEADS):
            col = slice(h * HEAD_DIM, (h + 1) * HEAD_DIM)
            pieces.append(gates[:, h:h + 1] * ocmp_ref[:, col]
                          + gates[:, N_HEADS + h:N_HEADS + h + 1] * osel_ref[:, col]
                          + gates[:, 2 * N_HEADS + h:2 * N_HEADS + h + 1] * owin_ref[:, col])
        o = jnp.concatenate(pieces, axis=-1)
    else:
        o_ref, yconv_ref, h_ref, wo_ref, g_ref, b_ref, y_ref = refs
        o = o_ref[...]
    mixed = jnp.concatenate([o, yconv_ref[...]], axis=-1).astype(BF16)
    y_ref[...] = _layer_norm(ALPHA * h_ref[...] + _dot(mixed, wo_ref[...]), g_ref[...], b_ref[...])


def _out_ln(branches, yconv, h2d, w_out, g, b, tm):
    rows = h2d.shape[0]
    gated = len(branches) > 1
    row = lambda i: (i, 0)
    const = lambda i: (0, 0)
    widths = [D_ATTN, D_ATTN, D_ATTN, LANES] if gated else [D_ATTN]
    return pl.pallas_call(
        functools.partial(_out_ln_body, gated=gated),
        grid=(rows // tm,),
        in_specs=[pl.BlockSpec((tm, w), row) for w in widths] + [
            pl.BlockSpec((tm, D_CONV), row),
            pl.BlockSpec((tm, D_MODEL), row),
            pl.BlockSpec((D_MODEL, D_MODEL), const),
            pl.BlockSpec((1, D_MODEL), const),
            pl.BlockSpec((1, D_MODEL), const),
        ],
        out_specs=pl.BlockSpec((tm, D_MODEL), row),
        out_shape=jax.ShapeDtypeStruct((rows, D_MODEL), F32),
        compiler_params=_cparams(("arbitrary",)),
        name="out_ln",
    )(*branches, yconv, h2d, w_out, g, b)


def _block_diag_groups(w):
    z = jnp.zeros_like(w)
    return jnp.concatenate([jnp.concatenate([w, z], axis=-1),
                            jnp.concatenate([z, w], axis=-1)], axis=-2).astype(BF16)


def _to_group_layout(x, nseq, s_new):
    x = x.reshape(nseq, s_new, N_KV, GQA_R, HEAD_DIM).transpose(0, 2, 1, 3, 4)
    return x.reshape(nseq, N_KV, s_new * GQA_R, HEAD_DIM)


def _from_group_layout(x, nseq, s_new):
    x = x.reshape(nseq, N_KV, s_new, GQA_R, HEAD_DIM).transpose(0, 2, 1, 3, 4)
    return x.reshape(nseq * s_new, D_ATTN)


def kernel(x_prompt, x_sample, cache_cmp_kv, cache_sel_kv, state_win_kv, state_conv, page_table,
           ffa_gate, ffa_up, ffa_down, ln1_g, ln1_b, w_in, b_in, w_conv, cmp_w1, cmp_w2, cmp_pe,
           w_out, ln2_g, ln2_b, ffb_gate, ffb_up, ffb_down, ln3_g, ln3_b):
    depth = ffa_gate.shape[0]
    assert depth == 1
    l = 0
    n, t, d = x_prompt.shape
    nseq, s_new, _ = x_sample.shape
    n_pages = page_table.shape[1]
    past = n_pages * PAGE_SIZE
    rows_p, rows_s = n * t, nseq * s_new
    kvshape = (2, N_KV, HEAD_DIM)

    bf = lambda w: w.astype(BF16)
    vec = lambda v: v.reshape(1, -1)
    w_main = bf(w_in[l][:, :D_MAIN])
    b_main = vec(b_in[l][:D_MAIN])
    w_gate = bf(jnp.pad(w_in[l][:, D_MAIN:], ((0, 0), (0, LANES - N_GATE))))
    b_gate = vec(jnp.pad(b_in[l][D_MAIN:], (0, LANES - N_GATE)))
    pe_t = jnp.concatenate([cmp_pe[l].transpose(1, 2, 0)] * BPP, axis=-1)
    w1_t = _block_diag_groups(cmp_w1[l].transpose(0, 2, 1, 3)).reshape(
        2, D_PAIRS, 2 * PAGE_SIZE, PAGE_SIZE)
    w2_t = _block_diag_groups(cmp_w2[l])
    w_out_b = bf(w_out[l])
    ffa = (bf(ffa_gate[l]), bf(ffa_up[l]), bf(ffa_down[l]), vec(ln1_g[l]), vec(ln1_b[l]))
    ffb = (bf(ffb_gate[l]), bf(ffb_up[l]), bf(ffb_down[l]), vec(ln3_g[l]), vec(ln3_b[l]))

    def prompt_cols(x):
        return jnp.concatenate([x[..., :_C_Q], x[..., _C_KVS:_C_KVS + K_HALF],
                                x[..., _C_KVW:_C_KVW + K_HALF]], axis=-1)

    def prompt_rows(x):
        return jnp.pad(x[..., _C_Q:], [(0, 0)] * (x.ndim - 1) + [(0, GATE_ROWS - N_GATE)])

    w_rows, b_rows = bf(prompt_cols(w_in[l])), vec(prompt_cols(b_in[l]))
    w_tr, b_tr = bf(prompt_rows(w_in[l]).T), prompt_rows(b_in[l]).reshape(-1, 1)

    pps = t // PAGE_SIZE
    cmp_seqs = 4
    hp = _ffn_ln(x_prompt.reshape(rows_p, d), *ffa, tm=512)
    q, k_sel, k_win, kvc_t, kvs_t, kvw_t, gates_t, yconv, cstate = _proj_prompt(
        hp.reshape(n, t, d), w_rows, b_rows, w_tr, b_tr, w_conv[l], tm=512)
    kc = _compress(jnp.zeros((1,), jnp.int32), kvc_t, pe_t, w1_t, w2_t, n_steps=n // cmp_seqs,
                   n_pages=cmp_seqs * pps, pages_per_seq=pps, name="compress_prompt")
    kc = kc.reshape(n // cmp_seqs, 2, N_KV, cmp_seqs, pps, BPP, HEAD_DIM)
    kc = kc.transpose(0, 3, 2, 1, 4, 5, 6).reshape(n, N_KV, 2, pps * BPP, HEAD_DIM)
    zc = jnp.zeros_like(kc[:, 0, 0])
    kc_pad = bf(jnp.stack([jnp.concatenate([kc[:, 0, 0], zc], axis=-1),
                           jnp.concatenate([zc, kc[:, 1, 0]], axis=-1)], axis=1))
    vc_t = bf(jnp.swapaxes(kc[:, :, 1], -1, -2))
    o = _attn_prompt(q, k_sel, k_win, kvs_t, kvw_t, kc_pad, vc_t, gates_t)
    yp = _out_ln([o.reshape(rows_p, D_ATTN)], yconv.reshape(rows_p, D_CONV), hp, w_out_b,
                 vec(ln2_g[l]), vec(ln2_b[l]), tm=512)
    yp = _ffn_ln(yp, *ffb, tm=512).reshape(n, t, d)

    def from_feature_major(x_t):
        return jnp.moveaxis(x_t.reshape(x_t.shape[0], *kvshape, x_t.shape[-1]), -1, 1)[None]

    p_cmp = from_feature_major(kvc_t)
    p_sel = from_feature_major(kvs_t)
    p_win = from_feature_major(kvw_t[:, :, t - min(WINDOW, t):])
    p_conv = cstate[:, 8 - (CONV_W - 1):].reshape(1, n, CONV_W - 1, D_CONV)

    hs = _ffn_ln(x_sample.reshape(rows_s, d), *ffa, tm=rows_s)
    qs, kv3, gates_s, cb_s, u_s = _proj_sample(hs, w_main, b_main, w_gate, b_gate)
    kvc_s, kvs_s, kvw_s = kv3[:, :KV_W], kv3[:, KV_W:2 * KV_W], kv3[:, 2 * KV_W:]
    u3 = u_s.reshape(nseq, s_new, D_CONV)
    yconv_s = _conv_sample(cb_s.reshape(nseq, s_new, D_CONV), u3, state_conv[l], w_conv[l])
    s_conv = jnp.concatenate([state_conv[l], u3], axis=1)[:, -(CONV_W - 1):]

    def feature_major(x):
        return jnp.moveaxis(x, -4, -1).reshape(*x.shape[:-4], KV_W, x.shape[-4])

    def new_rows_t(x2d):
        xt = x2d.reshape(nseq, s_new, KV_W).transpose(0, 2, 1)
        return jnp.pad(xt, ((0, 0), (0, 0), (0, PAGE_SIZE - s_new)))

    kc_s = _compress(page_table.reshape(-1), feature_major(cache_cmp_kv[l]), pe_t, w1_t, w2_t,
                     n_steps=nseq, n_pages=n_pages, pages_per_seq=None, name="compress_decode")
    q_g = _to_group_layout(qs, nseq, s_new)
    zq = jnp.zeros_like(q_g)
    q2 = jnp.concatenate([jnp.concatenate([q_g, zq], axis=-1), jnp.concatenate([zq, q_g], axis=-1)], axis=-2)
    ocmp_c, idx_pad = _dec_cmp(q2, kc_s, past, s_new)
    idx = idx_pad[:, :N_SEL]
    idx2 = jnp.pad(bf(idx.reshape(nseq, N_KV, s_new * N_SEL)), ((0, 0), (0, s_new * GQA_R - N_KV), (0, 0)))
    win_t = feature_major(state_win_kv[l])
    kvw_new_t = new_rows_t(kvw_s)
    osel_c, owin_c = _dec_sel_win(
        idx.reshape(-1), page_table.reshape(-1), q_g, idx2, feature_major(cache_sel_kv[l]),
        new_rows_t(kvs_s), win_t, kvw_new_t, past, s_new, n_pages)
    branches = [_from_group_layout(x, nseq, s_new) for x in (ocmp_c, osel_c, owin_c)]
    ys = _out_ln(branches + [gates_s], yconv_s.reshape(rows_s, D_CONV), hs, w_out_b,
                 vec(ln2_g[l]), vec(ln2_b[l]), tm=rows_s)
    ys = _ffn_ln(ys, *ffb, tm=rows_s).reshape(nseq, s_new, d)
    n_keep = min(WINDOW, past + s_new)
    ctx_t = jnp.concatenate([win_t, kvw_new_t[:, :, :s_new]], axis=-1)[:, :, -n_keep:]
    s_cmp = kvc_s.reshape(1, nseq, s_new, *kvshape)
    s_sel = kvs_s.reshape(1, nseq, s_new, *kvshape)
    s_win = jnp.moveaxis(ctx_t.reshape(nseq, *kvshape, n_keep), -1, 1)[None]
    return (yp, ys, p_cmp, p_sel, p_win, p_conv, s_cmp, s_sel, s_win,
            s_conv.reshape(1, nseq, CONV_W - 1, D_CONV))
```

```python
import functools

import jax
import jax.numpy as jnp
from jax import lax
from jax.experimental import pallas as pl
from jax.experimental.pallas import tpu as pltpu

F32 = jnp.float32
BF16 = jnp.bfloat16

D_MODEL = 1024
D_CONV = 512
D_ATTN = 512
HEAD_DIM = 64
N_HEADS = 8
N_KV = 2
GQA_R = N_HEADS // N_KV
N_BR = 3
CONV_W = 3
BLK = 64
BLK_SHIFT = 6
N_SEL = 16
WINDOW = 512
PAGE_SIZE = 128
PAGE_SHIFT = 7
D_FF = 2816
KV_W = 2 * N_KV * HEAD_DIM
D_MAIN = 3 * D_CONV + D_ATTN + 3 * KV_W
N_GATE = N_BR * N_HEADS
ALPHA = 2.0 ** 0.25
SCALE = HEAD_DIM ** -0.5
LN_EPS = 1e-5
NEG = -1e30
FORCE = 1e4

LANES = 128
SUBLANES = 8
V7X_VMEM_BYTES = 64 * 1024 * 1024
VMEM_LIMIT_BYTES = V7X_VMEM_BYTES * 7 // 8

FFN_ROW_TILE = 1024
PROJ_ROW_TILE = 512
OUT_ROW_TILE = 512
CMP_SEQS_PER_STEP = 4


def _cparams(sem):
    return pltpu.CompilerParams(dimension_semantics=sem, vmem_limit_bytes=VMEM_LIMIT_BYTES)


def _layer_norm(y, g, b):
    mu = jnp.mean(y, axis=-1, keepdims=True)
    yc = y - mu
    var = jnp.mean(yc * yc, axis=-1, keepdims=True)
    return yc * lax.rsqrt(var + LN_EPS) * g + b


def _dot(a, b):
    return jnp.dot(a, b, preferred_element_type=F32)


def _dot_nt(a, b):
    return lax.dot_general(a, b, (((1,), (1,)), ((), ())), preferred_element_type=F32)


FFN_CHUNK = 256


def _ffn_ln_body(x_ref, wg_ref, wu_ref, wd_ref, g_ref, b_ref, o_ref):
    x = x_ref[...]
    xb = x.astype(BF16)
    acc = jnp.zeros(x.shape, F32)
    for c in range(D_FF // FFN_CHUNK):
        sl = slice(c * FFN_CHUNK, (c + 1) * FFN_CHUNK)
        gg = _dot(xb, wg_ref[:, sl])
        uu = _dot(xb, wu_ref[:, sl])
        hh = (gg * jax.nn.sigmoid(gg) * uu).astype(BF16)
        acc = acc + _dot(hh, wd_ref[sl, :])
    o_ref[...] = _layer_norm(ALPHA * x + acc, g_ref[...], b_ref[...])


def _ffn_ln(x2d, wg, wu, wd, g, b, tm):
    rows = x2d.shape[0]
    const = lambda i: (0, 0)
    return pl.pallas_call(
        _ffn_ln_body,
        grid=(rows // tm,),
        in_specs=[
            pl.BlockSpec((tm, D_MODEL), lambda i: (i, 0)),
            pl.BlockSpec((D_MODEL, D_FF), const, pipeline_mode=pl.Buffered(1)),
            pl.BlockSpec((D_MODEL, D_FF), const, pipeline_mode=pl.Buffered(1)),
            pl.BlockSpec((D_FF, D_MODEL), const, pipeline_mode=pl.Buffered(1)),
            pl.BlockSpec((1, D_MODEL), const),
            pl.BlockSpec((1, D_MODEL), const),
        ],
        out_specs=pl.BlockSpec((tm, D_MODEL), lambda i: (i, 0)),
        out_shape=jax.ShapeDtypeStruct((rows, D_MODEL), F32),
        compiler_params=_cparams(("arbitrary",)),
        name="ffn_ln",
    )(x2d, wg, wu, wd, g, b)


_C_CB, _C_CC, _C_CH, _C_Q = 0, D_CONV, 2 * D_CONV, 3 * D_CONV
_C_KVC = 3 * D_CONV + D_ATTN
_C_KVS = _C_KVC + KV_W
_C_KVW = _C_KVS + KV_W


def _proj_piece(hb, w_ref, b_ref, lo, width):
    return _dot(hb, w_ref[:, lo:lo + width]) + b_ref[:, lo:lo + width]


_P_K = 3 * D_CONV
K_HALF = N_KV * HEAD_DIM
P_ROW_COLS = _P_K + 2 * K_HALF
GATE_ROWS = 32
_T_KV = D_ATTN
_T_GATE = _T_KV + 3 * KV_W
P_T_ROWS = _T_GATE + GATE_ROWS


def _proj_prompt_body(h_ref, w_ref, b_ref, wt_ref, bt_ref, wc_ref,
                      qt_ref, ks_ref, kw_ref, kvct_ref, kvst_ref, kvwt_ref, gates_ref,
                      yconv_ref, cstate_ref, ubuf):
    i = pl.program_id(1)
    tm = h_ref.shape[0]
    hb = h_ref[...].astype(BF16)

    def piece_t(lo, height):
        return _dot_nt(wt_ref[lo:lo + height, :], hb) + bt_ref[lo:lo + height, :]

    qt_ref[...] = (piece_t(0, D_ATTN) * (SCALE * LOG2E)).astype(BF16)
    lane = lax.broadcasted_iota(jnp.int32, (1, KV_W), 1)
    blk = (i * tm + lax.broadcasted_iota(jnp.int32, (tm, 1), 0)) >> BLK_SHIFT
    aux0, aux1 = HEAD_DIM, 2 * HEAD_DIM
    one = (lane == aux0 + AUX_BLOCKS) | (lane == aux1 + AUX_BLOCKS)
    is_k = (lane < aux0) | (lane >= aux1 + HEAD_DIM)
    k4 = _proj_piece(hb, w_ref, b_ref, _P_K, 2 * K_HALF)
    for ref, lo, ones in ((ks_ref, 0, one | (lane == aux0 + blk) | (lane == aux1 + blk)),
                          (kw_ref, K_HALF, one)):
        k = k4[:, lo:lo + K_HALF]
        ref[...] = jnp.where(is_k, jnp.concatenate([k, k], axis=1),
                             jnp.where(ones, 1.0, 0.0)).astype(BF16)
    for k, ref in enumerate((kvct_ref, kvst_ref, kvwt_ref)):
        ref[...] = piece_t(_T_KV + k * KV_W, KV_W)
    gates_ref[...] = jax.nn.sigmoid(piece_t(_T_GATE, GATE_ROWS))
    cb = _proj_piece(hb, w_ref, b_ref, _C_CB, D_CONV)
    u = _proj_piece(hb, w_ref, b_ref, _C_CC, D_CONV) * _proj_piece(hb, w_ref, b_ref, _C_CH, D_CONV)

    head = SUBLANES

    @pl.when(i == 0)
    def _():
        ubuf[0:head, :] = jnp.zeros((head, D_CONV), F32)

    ubuf[head:head + tm, :] = u
    u1 = ubuf[head - 1:head - 1 + tm, :]
    u2 = ubuf[head - 2:head - 2 + tm, :]
    z = u2 * wc_ref[0:1, :] + u1 * wc_ref[1:2, :] + u * wc_ref[2:3, :]
    yconv_ref[...] = cb * z
    tail = u[tm - head:tm, :]
    ubuf[0:head, :] = tail
    cstate_ref[...] = tail


def _proj_prompt(h, w_rows, b_rows, w_t, b_t, w_conv, tm):
    n, t, _ = h.shape
    const = lambda a, i: (0, 0)
    row = lambda a, i: (a, i, 0)
    col = lambda a, i: (a, 0, i)

    def out(width, dtype=F32):
        return jax.ShapeDtypeStruct((n, t, width), dtype)

    def out_t(rows, dtype=F32):
        return jax.ShapeDtypeStruct((n, rows, t), dtype)

    return pl.pallas_call(
        _proj_prompt_body,
        grid=(n, t // tm),
        in_specs=[
            pl.BlockSpec((None, tm, D_MODEL), row),
            pl.BlockSpec((D_MODEL, P_ROW_COLS), const),
            pl.BlockSpec((1, P_ROW_COLS), const),
            pl.BlockSpec((P_T_ROWS, D_MODEL), const),
            pl.BlockSpec((P_T_ROWS, 1), const),
            pl.BlockSpec((CONV_W, D_CONV), const),
        ],
        out_specs=[
            pl.BlockSpec((None, D_ATTN, tm), col),
            pl.BlockSpec((None, tm, KV_W), row),
            pl.BlockSpec((None, tm, KV_W), row),
            pl.BlockSpec((None, KV_W, tm), col),
            pl.BlockSpec((None, KV_W, tm), col),
            pl.BlockSpec((None, KV_W, tm), col),
            pl.BlockSpec((None, GATE_ROWS, tm), col),
            pl.BlockSpec((None, tm, D_CONV), row),
            pl.BlockSpec((None, SUBLANES, D_CONV), lambda a, i: (a, 0, 0)),
        ],
        out_shape=[out_t(D_ATTN, BF16), out(KV_W, BF16), out(KV_W, BF16), out_t(KV_W), out_t(KV_W),
                   out_t(KV_W), out_t(GATE_ROWS), out(D_CONV),
                   jax.ShapeDtypeStruct((n, SUBLANES, D_CONV), F32)],
        scratch_shapes=[pltpu.VMEM((tm + SUBLANES, D_CONV), F32)],
        compiler_params=_cparams(("arbitrary", "arbitrary")),
        name="proj_prompt",
    )(h, w_rows, b_rows, w_t, b_t, w_conv)


def _proj_sample_body(h_ref, w_ref, b_ref, wgt_ref, bgt_ref, q_ref, kv_ref, gates_ref, cb_ref, u_ref):
    hb = h_ref[...].astype(BF16)
    q_ref[...] = (_proj_piece(hb, w_ref, b_ref, _C_Q, D_ATTN) * SCALE).astype(BF16)
    kv_ref[...] = _proj_piece(hb, w_ref, b_ref, _C_KVC, 3 * KV_W)
    gates_ref[...] = jax.nn.sigmoid(_dot(hb, wgt_ref[...]) + bgt_ref[...])
    cb_ref[...] = _proj_piece(hb, w_ref, b_ref, _C_CB, D_CONV)
    u_ref[...] = _proj_piece(hb, w_ref, b_ref, _C_CC, D_CONV) * _proj_piece(hb, w_ref, b_ref, _C_CH, D_CONV)


def _proj_sample(h2d, w_main, b_main, w_gate, b_gate):
    rows = h2d.shape[0]

    def out(width, dtype=F32):
        return jax.ShapeDtypeStruct((rows, width), dtype)

    return pl.pallas_call(
        _proj_sample_body,
        out_shape=[out(D_ATTN, BF16), out(3 * KV_W), out(LANES), out(D_CONV), out(D_CONV)],
        compiler_params=pltpu.CompilerParams(vmem_limit_bytes=VMEM_LIMIT_BYTES),
        name="proj_sample",
    )(h2d, w_main, b_main, w_gate, b_gate)


def _conv_sample_body(cb_ref, u_ref, prev_ref, wc_ref, y_ref):
    s = u_ref.shape[1]
    ext = [prev_ref[:, 0, :], prev_ref[:, 1, :]] + [u_ref[:, t, :] for t in range(s)]
    for t in range(s):
        z = ext[t] * wc_ref[0:1, :] + ext[t + 1] * wc_ref[1:2, :] + ext[t + 2] * wc_ref[2:3, :]
        y_ref[:, t, :] = cb_ref[:, t, :] * z


def _conv_sample(cb, u, prev, w_conv):
    return pl.pallas_call(
        _conv_sample_body,
        out_shape=jax.ShapeDtypeStruct(u.shape, F32),
        name="conv_sample",
    )(cb, u, prev, w_conv)


BPP = PAGE_SIZE // BLK
D_PAIRS = HEAD_DIM // 2


def _compress_body(pt_ref, src_ref, pe_ref, w1_ref, w2_ref, kc_ref, buf, sem, *, n_pages, pages_per_seq):
    n = pl.program_id(0)
    slot = n % 2

    def page_copy(step, j, sl):
        dst = buf.at[sl, :, j, :]
        if pages_per_seq is None:
            src = src_ref.at[pt_ref[step * n_pages + j]]
        else:
            seq = step * (n_pages // pages_per_seq) + j // pages_per_seq
            src = src_ref.at[seq, :, pl.ds((j % pages_per_seq) * PAGE_SIZE, PAGE_SIZE)]
        return pltpu.make_async_copy(src, dst, sem.at[sl])

    def start(step, sl):
        for j in range(n_pages):
            page_copy(step, j, sl).start()

    @pl.when(n == 0)
    def _():
        start(n, slot)

    @pl.when(n + 1 < pl.num_programs(0))
    def _():
        start(n + 1, 1 - slot)

    for j in range(n_pages):
        page_copy(n, j, slot).wait()

    def feature_rows(f):
        return buf[slot, f]

    for c in range(2):
        acc = None
        for dp in range(D_PAIRS):
            rows = []
            for g in range(N_KV):
                f0 = (c * N_KV + g) * HEAD_DIM + 2 * dp
                rows.append(jnp.concatenate(
                    [feature_rows(f0) + pe_ref[c, 2 * dp:2 * dp + 1, :],
                     feature_rows(f0 + 1) + pe_ref[c, 2 * dp + 1:2 * dp + 2, :]], axis=-1))
            x = jnp.concatenate(rows, axis=0).astype(BF16)
            d = _dot(x, w1_ref[c, dp])
            acc = d if acc is None else acc + d
        hid = (acc * jax.nn.sigmoid(acc)).astype(BF16)
        kc_ref[c] = _dot(hid, w2_ref[c])


def _compress(pt_flat, src, pe_t, w1_t, w2_t, n_steps, n_pages, pages_per_seq, name):
    const3 = lambda a, pt: (0, 0, 0)
    grid_spec = pltpu.PrefetchScalarGridSpec(
        num_scalar_prefetch=1,
        grid=(n_steps,),
        in_specs=[
            pl.BlockSpec(memory_space=pl.ANY),
            pl.BlockSpec((2, HEAD_DIM, PAGE_SIZE), const3),
            pl.BlockSpec((2, D_PAIRS, 2 * PAGE_SIZE, PAGE_SIZE), lambda a, pt: (0, 0, 0, 0)),
            pl.BlockSpec((2, PAGE_SIZE, PAGE_SIZE), const3),
        ],
        out_specs=pl.BlockSpec((None, 2, N_KV * n_pages, PAGE_SIZE), lambda a, pt: (a, 0, 0, 0)),
        scratch_shapes=[pltpu.VMEM((2, KV_W, n_pages, PAGE_SIZE), F32),
                        pltpu.SemaphoreType.DMA((2,))],
    )
    return pl.pallas_call(
        functools.partial(_compress_body, n_pages=n_pages, pages_per_seq=pages_per_seq),
        grid_spec=grid_spec,
        out_shape=jax.ShapeDtypeStruct((n_steps, 2, N_KV * n_pages, PAGE_SIZE), F32),
        compiler_params=_cparams(("arbitrary",)),
        name=name,
    )(pt_flat, src, pe_t, w1_t, w2_t)


ATT_TQ = 256
ATT_TK = ATT_TQ
WIN_TILES = WINDOW // ATT_TK + 1
AUX_BLOCKS = 32
LOG2E = 1.4426950408889634


V_ROWS = HEAD_DIM + 16


def _attn_update(s_t, v_t, state):
    m, acc = state
    m_new = jnp.maximum(m, jnp.max(s_t, axis=0, keepdims=True))
    e = jnp.exp2(s_t - m_new)
    acc = jnp.exp2(m - m_new) * acc + _dot(v_t, e.astype(BF16))
    return m_new, acc


def _attn_prompt_body(qt_ref, ks_ref, kw_ref, vs_ref, vw_ref, kc_ref, vct_ref, gates_ref, wb_ref, o_ref):
    tq, tk = ATT_TQ, ATT_TK
    qi = pl.program_id(1)
    nb = kc_ref.shape[1]
    t0 = qi * tq
    kd = qi
    tpos = t0 + lax.broadcasted_iota(jnp.int32, (1, tq), 1)
    jrow = lax.broadcasted_iota(jnp.int32, (nb, 1), 0)
    vis = ((jrow + 1) * BLK - 1) <= tpos
    cur = tpos >> BLK_SHIFT

    def heads_of(x):
        return jnp.concatenate([x] * GQA_R, axis=1)

    def group_queries(g, block_bias=None, switch=None):
        zeros = jnp.zeros((AUX_BLOCKS, tq), BF16)
        top = zeros if block_bias is None else block_bias
        low = zeros if switch is None else jnp.broadcast_to(switch, (AUX_BLOCKS, tq)).astype(BF16)
        aux = jnp.concatenate([top, low], axis=0)
        pieces = []
        for r in range(GQA_R):
            h = g * GQA_R + r
            qh = qt_ref[h * HEAD_DIM:(h + 1) * HEAD_DIM, :]
            pieces.append(jnp.concatenate([qh, aux] if g == 0 else [aux, qh], axis=0))
        return jnp.concatenate(pieces, axis=1)

    q4 = [group_queries(g) for g in range(N_KV)]
    vis4 = heads_of(vis)

    o_cmp, q4_sel = [], []
    for g in range(N_KV):
        s = jnp.where(vis4, _dot(kc_ref[g], q4[g]), NEG)
        m = jnp.max(s, axis=0, keepdims=True)
        e = jnp.where(vis4, jnp.exp2(s - m), 0.0)
        l = jnp.sum(e, axis=0, keepdims=True)
        p = e / jnp.where(l > 0.0, l, 1.0)
        imp = sum(p[:, r * tq:(r + 1) * tq] for r in range(GQA_R))
        o_cmp.append(_dot(vct_ref[g], p.astype(BF16)))

        score = jnp.where(jrow > cur, NEG, imp)
        score = jnp.where((jrow == 0) | (jrow == cur) | (jrow == cur - 1), FORCE, score)
        rank = jnp.zeros((nb, tq), F32)
        for j2 in range(nb):
            row = score[j2:j2 + 1, :]
            tie = jnp.where(jrow > j2, 1.0, 0.0)
            rank = rank + jnp.where(row > score, 1.0, jnp.where(row == score, tie, 0.0))
        q4_sel.append(group_queries(g, jnp.where(rank < N_SEL, 0.0, NEG).astype(BF16)))

    def key_tile(k_ref, kt, g):
        return k_ref[pl.ds(pl.multiple_of(kt * tk, tk), tk), g * LANES:(g + 1) * LANES]

    ones_rows = jnp.ones((V_ROWS - HEAD_DIM, tk), BF16)

    def val_tile(v_ref, kt, g):
        v = v_ref[g * HEAD_DIM:(g + 1) * HEAD_DIM, pl.ds(pl.multiple_of(kt * tk, tk), tk)]
        return jnp.concatenate([v.astype(BF16), ones_rows], axis=0)

    wide = GQA_R * tq
    init = (jnp.full((1, wide), NEG, F32), jnp.zeros((V_ROWS, wide), F32))

    def win_scores(back, g):
        kt = jnp.maximum(kd - back, 0)
        qw = q4[g] if back == 0 else group_queries(g, None, jnp.where(kd >= back, 0.0, NEG))
        s = _dot(key_tile(kw_ref, kt, g), qw)
        if back in (0, WIN_TILES - 1):
            s = s + heads_of(wb_ref[min(back, 1)])
        return s

    groups = range(N_KV)
    s_diag = [_dot(key_tile(ks_ref, kd, g), q4_sel[g]) + heads_of(wb_ref[0]) for g in groups]
    s_win = [[win_scores(back, g) for g in groups] for back in range(WIN_TILES)]
    sel = tuple(_attn_update(s_diag[g], val_tile(vs_ref, kd, g), init) for g in groups)
    win = [init] * N_KV
    for back in range(WIN_TILES):
        kt = jnp.maximum(kd - back, 0)
        win = [_attn_update(s_win[back][g], val_tile(vw_ref, kt, g), win[g]) for g in groups]

    def sel_tiles(kts, states):
        s = [[_dot(key_tile(ks_ref, kt, g), q4_sel[g]) for g in groups] for kt in kts]
        for i, kt in enumerate(kts):
            states = tuple(_attn_update(s[i][g], val_tile(vs_ref, kt, g), states[g]) for g in groups)
        return states

    sel = lax.cond((kd & 1) == 1, lambda st: sel_tiles((kd - 1,), st), lambda st: st, sel)
    sel = lax.fori_loop(0, kd >> 1, lambda j, st: sel_tiles((2 * j, 2 * j + 1), st), sel)

    gates = gates_ref[...]
    outs = []
    for h in range(N_HEADS):
        g, r = h // GQA_R, h % GQA_R
        cols = slice(r * tq, (r + 1) * tq)
        o_sel = sel[g][1][:HEAD_DIM, cols] * (1.0 / sel[g][1][HEAD_DIM:HEAD_DIM + 1, cols])
        o_win = win[g][1][:HEAD_DIM, cols] * (1.0 / win[g][1][HEAD_DIM:HEAD_DIM + 1, cols])
        outs.append(gates[h:h + 1, :] * o_cmp[g][:, cols] + gates[N_HEADS + h:N_HEADS + h + 1, :] * o_sel
                    + gates[2 * N_HEADS + h:2 * N_HEADS + h + 1, :] * o_win)
    o_ref[...] = jnp.concatenate(outs, axis=0).T


def _attn_prompt(q_t, k_sel, k_win, kvs_t, kvw_t, kc_pad, vc_t, gates_t):
    n, _, t = q_t.shape
    nb = kc_pad.shape[2]
    row = lambda a, i: (a, i, 0)
    seq = lambda a, i: (a, 0, 0)
    seq4 = lambda a, i: (a, 0, 0, 0)
    v_rows = lambda a, i: (a, 1, 0)

    def win_bias(back):
        dist = back * ATT_TK + jnp.arange(ATT_TQ)[None, :] - jnp.arange(ATT_TK)[:, None]
        return jnp.where((dist >= 0) & (dist < WINDOW), 0.0, NEG).astype(F32)

    wb = jnp.stack([win_bias(0), win_bias(WIN_TILES - 1)])
    return pl.pallas_call(
        _attn_prompt_body,
        grid=(n, t // ATT_TQ),
        in_specs=[
            pl.BlockSpec((None, D_ATTN, ATT_TQ), lambda a, i: (a, 0, i)),
            pl.BlockSpec((None, t, KV_W), seq),
            pl.BlockSpec((None, t, KV_W), seq),
            pl.BlockSpec((None, N_KV * HEAD_DIM, t), v_rows),
            pl.BlockSpec((None, N_KV * HEAD_DIM, t), v_rows),
            pl.BlockSpec((None, N_KV, nb, LANES), seq4),
            pl.BlockSpec((None, N_KV, HEAD_DIM, nb), seq4),
            pl.BlockSpec((None, GATE_ROWS, ATT_TQ), lambda a, i: (a, 0, i)),
            pl.BlockSpec((2, ATT_TK, ATT_TQ), lambda a, i: (0, 0, 0)),
        ],
        out_specs=pl.BlockSpec((None, ATT_TQ, D_ATTN), row),
        out_shape=jax.ShapeDtypeStruct((n, t, D_ATTN), F32),
        compiler_params=_cparams(("arbitrary", "arbitrary")),
        name="attn_prompt",
    )(q_t, k_sel, k_win, kvs_t, kvw_t, kc_pad, vc_t, gates_t, wb)


DEC_SEQ_PER_STEP = 8


def _dec_cmp_body(q2_ref, kc_ref, ocmp_ref, idx_ref, p_buf, imp_buf, *, past, s_new):
    assert BPP == 2
    ns = q2_ref.shape[0]
    qrows = s_new * GQA_R
    n_pages = kc_ref.shape[2] // N_KV
    nb = n_pages * BPP
    width = imp_buf.shape[1]
    imp_buf[...] = jnp.zeros(imp_buf.shape, F32)
    lane = lax.broadcasted_iota(jnp.int32, (1, width), 1)
    jc = jnp.where(lane < n_pages, BPP * lane, BPP * (lane - n_pages) + 1)
    jc = jnp.where(lane == nb, nb, jc)

    row2 = lax.broadcasted_iota(jnp.int32, (2 * qrows, 1), 0)
    half = jnp.where(row2 >= qrows, 1, 0)
    blk = BPP * lax.broadcasted_iota(jnp.int32, (1, n_pages), 1) + half
    tok = (row2 & (qrows - 1)) >> (GQA_R.bit_length() - 1)
    vis = ((blk + 1) * BLK - 1) <= (past + tok)

    def both(x):
        return jnp.concatenate([x, x], axis=0)

    def pages(a, c, g):
        return kc_ref[a, c, g * n_pages:(g + 1) * n_pages, :].astype(BF16)

    pairs = [(a, g) for a in range(ns) for g in range(N_KV)]
    scores = [_dot_nt(q2_ref[a, g], pages(a, 0, g)) for a, g in pairs]
    for i, (a, g) in enumerate(pairs):
        s = jnp.where(vis, scores[i], NEG)
        m = jnp.max(s, axis=-1, keepdims=True)
        e = jnp.where(vis, jnp.exp(s - both(jnp.maximum(m[:qrows], m[qrows:]))), 0.0)
        l = jnp.sum(e, axis=-1, keepdims=True)
        l = l[:qrows] + l[qrows:]
        p = e / both(jnp.where(l > 0.0, l, 1.0))
        o2 = _dot(p.astype(BF16), pages(a, 1, g))
        ocmp_ref[a, g] = o2[:qrows, :HEAD_DIM] + o2[qrows:, HEAD_DIM:]
        p_buf[i] = p
        imp = sum(p_buf[i, pl.ds(r, 2 * s_new, stride=GQA_R), :] for r in range(GQA_R))
        row0 = i * s_new
        imp_buf[row0:row0 + s_new, 0:n_pages] = imp[:s_new]
        imp_buf[row0:row0 + s_new, n_pages:nb] = imp[s_new:]

    rows = ns * N_KV * s_new
    imp = imp_buf[...]
    t_row = lax.broadcasted_iota(jnp.int32, (rows, 1), 0) & (s_new - 1)
    cur = (past + t_row) >> BLK_SHIFT
    score = jnp.where(jc > cur, NEG, imp)
    score = jnp.where((jc == 0) | (jc == cur) | (jc == cur - 1), FORCE, score)
    score = jnp.where(lane <= nb, score, -jnp.inf)
    jcf = jc.astype(F32)
    out_lane = lax.broadcasted_iota(jnp.int32, (1, LANES), 1)
    idx = jnp.zeros((rows, LANES), jnp.int32)
    for k in range(N_SEL):
        m = jnp.max(score, axis=-1, keepdims=True)
        first = jnp.min(jnp.where(score == m, jcf, 1e9), axis=-1, keepdims=True)
        idx = jnp.where(out_lane == k, first.astype(jnp.int32), idx)
        score = jnp.where(jcf == first, -jnp.inf, score)
    idx_ref[...] = idx


def _dec_cmp(q2, kc, past, s_new):
    nseq = q2.shape[0]
    qrows = s_new * GQA_R
    rows_kc = kc.shape[2]
    n_pages = rows_kc // N_KV
    nb = n_pages * BPP
    ns = DEC_SEQ_PER_STEP
    width = (nb + 1 + LANES - 1) // LANES * LANES
    seq4 = lambda i: (i, 0, 0, 0)
    return pl.pallas_call(
        functools.partial(_dec_cmp_body, past=past, s_new=s_new),
        grid=(nseq // ns,),
        in_specs=[
            pl.BlockSpec((ns, N_KV, 2 * qrows, 2 * HEAD_DIM), seq4),
            pl.BlockSpec((ns, 2, rows_kc, PAGE_SIZE), seq4),
        ],
        out_specs=[
            pl.BlockSpec((ns, N_KV, qrows, HEAD_DIM), seq4),
            pl.BlockSpec((ns * N_KV * s_new, LANES), lambda i: (i, 0)),
        ],
        out_shape=[jax.ShapeDtypeStruct((nseq, N_KV, qrows, HEAD_DIM), F32),
                   jax.ShapeDtypeStruct((nseq * N_KV * s_new, LANES), jnp.int32)],
        scratch_shapes=[pltpu.VMEM((ns * N_KV, 2 * qrows, n_pages), F32),
                        pltpu.VMEM((ns * N_KV * s_new, width), F32)],
        compiler_params=_cparams(("arbitrary",)),
        name="dec_cmp_select",
    )(q2, kc)


def _dec_sel_win_body(idx_ref, pt_ref, q_ref, idx2_ref, expand_ref, pool_ref, newblk_ref, win_ref,
                      kvw_new_ref, osel_ref, owin_ref, winout_ref, kvbuf, sem, *, past, s_new, n_pages):
    n_past_blk = past // BLK
    n = pl.program_id(0)
    nseq = pl.num_programs(0)
    slot = n % 2
    rows = s_new * GQA_R
    seg_lanes = N_SEL * PAGE_SIZE
    nkeys = s_new * seg_lanes

    def block_copy(seq, c, k, sl):
        g, t = c // s_new, c % s_new
        j = jnp.minimum(idx_ref[(seq * N_KV * s_new + c) * N_SEL + k], n_past_blk - 1)
        page = pt_ref[seq * n_pages + (j >> 1)]
        dst = kvbuf.at[sl, :, g, :, pl.ds((t * N_SEL + k) * PAGE_SIZE, PAGE_SIZE)]
        return pltpu.make_async_copy(pool_ref.at[page, :, g], dst, sem.at[sl])

    def start(seq, sl):
        for c in range(N_KV * s_new):
            for k in range(N_SEL):
                block_copy(seq, c, k, sl).start()

    @pl.when(n == 0)
    def _():
        start(n, slot)

    @pl.when(n + 1 < nseq)
    def _():
        start(n + 1, 1 - slot)

    for c in range(N_KV * s_new):
        for k in range(N_SEL):
            block_copy(n, c, k, slot).wait()

    assert win_ref.shape[1] == min(WINDOW, past + s_new)
    slid = jnp.concatenate([win_ref[...], kvw_new_ref[...]], axis=1)
    winout_ref[...] = slid[:, s_new:s_new + win_ref.shape[1]]

    t_row = lax.broadcasted_iota(jnp.int32, (rows, 1), 0) >> (GQA_R.bit_length() - 1)
    qpos = past + t_row
    lane = lax.broadcasted_iota(jnp.int32, (1, nkeys), 1)
    own = (lane >> (seg_lanes.bit_length() - 1)) == t_row
    row_in_page = lane & (PAGE_SIZE - 1)
    nlane = lax.broadcasted_iota(jnp.int32, (1, PAGE_SIZE), 1)
    nwin = win_ref.shape[1]
    wpos = past - nwin + lax.broadcasted_iota(jnp.int32, (1, nwin), 1)
    groups = range(N_KV)

    def krows(g):
        return pl.ds(g * HEAD_DIM, HEAD_DIM)

    def vrows(g):
        return pl.ds((N_KV + g) * HEAD_DIM, HEAD_DIM)

    q = [q_ref[g] for g in groups]
    jv2 = _dot(idx2_ref[...], expand_ref[...])
    s_sel = [_dot(q[g], kvbuf[slot, 0, g].astype(BF16)) for g in groups]
    s_new_sel = [_dot(q[g], newblk_ref[krows(g), :].astype(BF16)) for g in groups]
    s_win = [_dot(q[g], win_ref[krows(g), :].astype(BF16)) for g in groups]
    s_new_win = [_dot(q[g], kvw_new_ref[krows(g), :].astype(BF16)) for g in groups]

    def two_piece_softmax(sa, va, sb, vb):
        m = jnp.maximum(jnp.max(sa, axis=-1, keepdims=True), jnp.max(sb, axis=-1, keepdims=True))
        ea, eb = jnp.exp(sa - m), jnp.exp(sb - m)
        l = jnp.sum(ea, axis=-1, keepdims=True) + jnp.sum(eb, axis=-1, keepdims=True)
        return (_dot_nt(ea.astype(BF16), va) + _dot_nt(eb.astype(BF16), vb)) / l

    for g in groups:
        jv = jv2[g:g + 1, :].astype(jnp.int32)
        kpos = jv * BLK + (row_in_page & (BLK - 1))
        valid = (own & (jv < n_past_blk) & ((row_in_page >> BLK_SHIFT) == (jv & (BPP - 1)))
                 & (kpos <= qpos))
        has_new = jnp.max(jnp.where(own & (jv >= n_past_blk), 1.0, 0.0), axis=-1, keepdims=True)
        valid_n = (has_new > 0.5) & (nlane < BLK) & (n_past_blk * BLK + nlane <= qpos)
        osel_ref[g] = two_piece_softmax(
            jnp.where(valid, s_sel[g], NEG), kvbuf[slot, 1, g].astype(BF16),
            jnp.where(valid_n, s_new_sel[g], NEG), newblk_ref[vrows(g), :].astype(BF16))

        dw = qpos - wpos
        dn = qpos - (past + nlane)
        valid_w = (dw >= 0) & (dw < WINDOW) & (wpos >= 0)
        valid_n = (dn >= 0) & (dn < WINDOW) & (nlane < s_new)
        owin_ref[g] = two_piece_softmax(
            jnp.where(valid_w, s_win[g], NEG), win_ref[vrows(g), :].astype(BF16),
            jnp.where(valid_n, s_new_win[g], NEG), kvw_new_ref[vrows(g), :].astype(BF16))


def _dec_sel_win(idx_flat, pt_flat, q_g, idx2, pool_t, newblk_t, win_t, kvw_new_t, past, s_new, n_pages):
    nseq = q_g.shape[0]
    rows = s_new * GQA_R
    nwin = win_t.shape[2]
    nkeys = s_new * N_SEL * PAGE_SIZE
    n_phys = pool_t.shape[0]
    expand = (jnp.arange(nkeys)[None, :] // PAGE_SIZE == jnp.arange(s_new * N_SEL)[:, None]).astype(BF16)
    per_seq = lambda a, idx, pt: (a, 0, 0, 0)
    per_seq3 = lambda a, idx, pt: (a, 0, 0)
    grid_spec = pltpu.PrefetchScalarGridSpec(
        num_scalar_prefetch=2,
        grid=(nseq,),
        in_specs=[
            pl.BlockSpec((None, N_KV, rows, HEAD_DIM), per_seq),
            pl.BlockSpec((None, rows, s_new * N_SEL), per_seq3),
            pl.BlockSpec((s_new * N_SEL, nkeys), lambda a, idx, pt: (0, 0)),
            pl.BlockSpec(memory_space=pl.ANY),
            pl.BlockSpec((None, KV_W, PAGE_SIZE), per_seq3),
            pl.BlockSpec((None, KV_W, nwin), per_seq3),
            pl.BlockSpec((None, KV_W, PAGE_SIZE), per_seq3),
        ],
        out_specs=[
            pl.BlockSpec((None, N_KV, rows, HEAD_DIM), per_seq),
            pl.BlockSpec((None, N_KV, rows, HEAD_DIM), per_seq),
            pl.BlockSpec((None, KV_W, nwin), per_seq3),
        ],
        scratch_shapes=[pltpu.VMEM((2, 2, N_KV, HEAD_DIM, nkeys), F32),
                        pltpu.SemaphoreType.DMA((2,))],
    )
    shape = jax.ShapeDtypeStruct((nseq, N_KV, rows, HEAD_DIM), F32)
    return pl.pallas_call(
        functools.partial(_dec_sel_win_body, past=past, s_new=s_new, n_pages=n_pages),
        grid_spec=grid_spec,
        out_shape=[shape, shape, jax.ShapeDtypeStruct(win_t.shape, F32)],
        compiler_params=_cparams(("arbitrary",)),
        name="dec_sel_win",
    )(idx_flat, pt_flat, q_g, idx2, expand, pool_t.reshape(n_phys, 2, N_KV, HEAD_DIM, PAGE_SIZE),
      newblk_t, win_t, kvw_new_t)


def _out_ln_body(*refs, gated):
    if gated:
        ocmp_ref, osel_ref, owin_ref, gates_ref, yconv_ref, h_ref, wo_ref, g_ref, b_ref, y_ref = refs
        gates = gates_ref[...]
        pieces = []
        for h in range(N_HEADS):
            col = slice(h * HEAD_DIM, (h + 1) * HEAD_DIM)
            pieces.append(gates[:, h:h + 1] * ocmp_ref[:, col]
                          + gates[:, N_HEADS + h:N_HEADS + h + 1] * osel_ref[:, col]
                          + gates[:, 2 * N_HEADS + h:2 * N_HEADS + h + 1] * owin_ref[:, col])
        o = jnp.concatenate(pieces, axis=-1)
    else:
        o_ref, yconv_ref, h_ref, wo_ref, g_ref, b_ref, y_ref = refs
        o = o_ref[...]
    mixed = jnp.concatenate([o, yconv_ref[...]], axis=-1).astype(BF16)
    y_ref[...] = _layer_norm(ALPHA * h_ref[...] + _dot(mixed, wo_ref[...]), g_ref[...], b_ref[...])


def _out_ln(branches, yconv, h2d, w_out, g, b, tm):
    rows = h2d.shape[0]
    gated = len(branches) > 1
    row = lambda i: (i, 0)
    const = lambda i: (0, 0)
    widths = [D_ATTN, D_ATTN, D_ATTN, LANES] if gated else [D_ATTN]
    return pl.pallas_call(
        functools.partial(_out_ln_body, gated=gated),
        grid=(rows // tm,),
        in_specs=[pl.BlockSpec((tm, w), row) for w in widths] + [
            pl.BlockSpec((tm, D_CONV), row),
            pl.BlockSpec((tm, D_MODEL), row),
            pl.BlockSpec((D_MODEL, D_MODEL), const),
            pl.BlockSpec((1, D_MODEL), const),
            pl.BlockSpec((1, D_MODEL), const),
        ],
        out_specs=pl.BlockSpec((tm, D_MODEL), row),
        out_shape=jax.ShapeDtypeStruct((rows, D_MODEL), F32),
        compiler_params=_cparams(("arbitrary",)),
        name="out_ln",
    )(*branches, yconv, h2d, w_out, g, b)


def _block_diag_groups(w):
    z = jnp.zeros_like(w)
    return jnp.concatenate([jnp.concatenate([w, z], axis=-1),
                            jnp.concatenate([z, w], axis=-1)], axis=-2).astype(BF16)


def _to_group_layout(x, nseq, s_new):
    x = x.reshape(nseq, s_new, N_KV, GQA_R, HEAD_DIM).transpose(0, 2, 1, 3, 4)
    return x.reshape(nseq, N_KV, s_new * GQA_R, HEAD_DIM)


def _from_group_layout(x, nseq, s_new):
    x = x.reshape(nseq, N_KV, s_new, GQA_R, HEAD_DIM).transpose(0, 2, 1, 3, 4)
    return x.reshape(nseq * s_new, D_ATTN)


def kernel(x_prompt, x_sample, cache_cmp_kv, cache_sel_kv, state_win_kv, state_conv, page_table,
           ffa_gate, ffa_up, ffa_down, ln1_g, ln1_b, w_in, b_in, w_conv, cmp_w1, cmp_w2, cmp_pe,
           w_out, ln2_g, ln2_b, ffb_gate, ffb_up, ffb_down, ln3_g, ln3_b):
    depth = ffa_gate.shape[0]
    assert depth == 1
    l = 0
    n, t, d = x_prompt.shape
    nseq, s_new, _ = x_sample.shape
    n_pages = page_table.shape[1]
    past = n_pages * PAGE_SIZE
    rows_p, rows_s = n * t, nseq * s_new
    kvshape = (2, N_KV, HEAD_DIM)

    bf = lambda w: w.astype(BF16)
    vec = lambda v: v.reshape(1, -1)
    w_main = bf(w_in[l][:, :D_MAIN])
    b_main = vec(b_in[l][:D_MAIN])
    w_gate = bf(jnp.pad(w_in[l][:, D_MAIN:], ((0, 0), (0, LANES - N_GATE))))
    b_gate = vec(jnp.pad(b_in[l][D_MAIN:], (0, LANES - N_GATE)))
    pe_t = jnp.concatenate([cmp_pe[l].transpose(1, 2, 0)] * BPP, axis=-1)
    w1_t = _block_diag_groups(cmp_w1[l].transpose(0, 2, 1, 3)).reshape(
        2, D_PAIRS, 2 * PAGE_SIZE, PAGE_SIZE)
    w2_t = _block_diag_groups(cmp_w2[l])
    w_out_b = bf(w_out[l])
    ffa = (bf(ffa_gate[l]), bf(ffa_up[l]), bf(0.5 * ffa_down[l]), vec(ln1_g[l]), vec(ln1_b[l]))
    ffb = (bf(ffb_gate[l]), bf(ffb_up[l]), bf(0.5 * ffb_down[l]), vec(ln3_g[l]), vec(ln3_b[l]))

    def prompt_cols(x):
        return jnp.concatenate([x[..., :_C_Q], x[..., _C_KVS:_C_KVS + K_HALF],
                                x[..., _C_KVW:_C_KVW + K_HALF]], axis=-1)

    def prompt_rows(x):
        return jnp.pad(x[..., _C_Q:], [(0, 0)] * (x.ndim - 1) + [(0, GATE_ROWS - N_GATE)])

    w_rows, b_rows = bf(prompt_cols(w_in[l])), vec(prompt_cols(b_in[l]))
    w_tr, b_tr = bf(prompt_rows(w_in[l]).T), prompt_rows(b_in[l]).reshape(-1, 1)

    pps = t // PAGE_SIZE
    cmp_seqs = CMP_SEQS_PER_STEP
    hp = _ffn_ln(x_prompt.reshape(rows_p, d), *ffa, tm=FFN_ROW_TILE)
    q, k_sel, k_win, kvc_t, kvs_t, kvw_t, gates_t, yconv, cstate = _proj_prompt(
        hp.reshape(n, t, d), w_rows, b_rows, w_tr, b_tr, w_conv[l], tm=PROJ_ROW_TILE)
    kc = _compress(jnp.zeros((1,), jnp.int32), kvc_t, pe_t, w1_t, w2_t, n_steps=n // cmp_seqs,
                   n_pages=cmp_seqs * pps, pages_per_seq=pps, name="compress_prompt")
    kc = kc.reshape(n // cmp_seqs, 2, N_KV, cmp_seqs, pps, BPP, HEAD_DIM)
    kc = kc.transpose(0, 3, 2, 1, 4, 5, 6).reshape(n, N_KV, 2, pps * BPP, HEAD_DIM)
    zc = jnp.zeros_like(kc[:, 0, 0])
    kc_pad = bf(jnp.stack([jnp.concatenate([kc[:, 0, 0], zc], axis=-1),
                           jnp.concatenate([zc, kc[:, 1, 0]], axis=-1)], axis=1))
    vc_t = bf(jnp.swapaxes(kc[:, :, 1], -1, -2))
    o = _attn_prompt(q, k_sel, k_win, kvs_t, kvw_t, kc_pad, vc_t, gates_t)
    yp = _out_ln([o.reshape(rows_p, D_ATTN)], yconv.reshape(rows_p, D_CONV), hp, w_out_b,
                 vec(ln2_g[l]), vec(ln2_b[l]), tm=OUT_ROW_TILE)
    yp = _ffn_ln(yp, *ffb, tm=FFN_ROW_TILE).reshape(n, t, d)

    def from_feature_major(x_t):
        return jnp.moveaxis(x_t.reshape(x_t.shape[0], *kvshape, x_t.shape[-1]), -1, 1)[None]

    p_cmp = from_feature_major(kvc_t)
    p_sel = from_feature_major(kvs_t)
    p_win = from_feature_major(kvw_t[:, :, t - min(WINDOW, t):])
    p_conv = cstate[:, SUBLANES - (CONV_W - 1):].reshape(1, n, CONV_W - 1, D_CONV)

    hs = _ffn_ln(x_sample.reshape(rows_s, d), *ffa, tm=rows_s)
    qs, kv3, gates_s, cb_s, u_s = _proj_sample(hs, w_main, b_main, w_gate, b_gate)
    kvc_s, kvs_s, kvw_s = kv3[:, :KV_W], kv3[:, KV_W:2 * KV_W], kv3[:, 2 * KV_W:]
    u3 = u_s.reshape(nseq, s_new, D_CONV)
    yconv_s = _conv_sample(cb_s.reshape(nseq, s_new, D_CONV), u3, state_conv[l], w_conv[l])
    s_conv = jnp.concatenate([state_conv[l], u3], axis=1)[:, -(CONV_W - 1):]

    def feature_major(x):
        return jnp.moveaxis(x, -4, -1).reshape(*x.shape[:-4], KV_W, x.shape[-4])

    def new_rows_t(x2d):
        xt = x2d.reshape(nseq, s_new, KV_W).transpose(0, 2, 1)
        return jnp.pad(xt, ((0, 0), (0, 0), (0, PAGE_SIZE - s_new)))

    kc_s = _compress(page_table.reshape(-1), feature_major(cache_cmp_kv[l]), pe_t, w1_t, w2_t,
                     n_steps=nseq, n_pages=n_pages, pages_per_seq=None, name="compress_decode")
    q_g = _to_group_layout(qs, nseq, s_new)
    zq = jnp.zeros_like(q_g)
    q2 = jnp.concatenate([jnp.concatenate([q_g, zq], axis=-1), jnp.concatenate([zq, q_g], axis=-1)], axis=-2)
    ocmp_c, idx_pad = _dec_cmp(q2, kc_s, past, s_new)
    idx = idx_pad[:, :N_SEL]
    idx2 = jnp.pad(bf(idx.reshape(nseq, N_KV, s_new * N_SEL)), ((0, 0), (0, s_new * GQA_R - N_KV), (0, 0)))
    win_t = feature_major(state_win_kv[l])
    kvw_new_t = new_rows_t(kvw_s)
    osel_c, owin_c, win_next_t = _dec_sel_win(
        idx.reshape(-1), page_table.reshape(-1), q_g, idx2, feature_major(cache_sel_kv[l]),
        new_rows_t(kvs_s), win_t, kvw_new_t, past, s_new, n_pages)
    branches = [_from_group_layout(x, nseq, s_new) for x in (ocmp_c, osel_c, owin_c)]
    ys = _out_ln(branches + [gates_s], yconv_s.reshape(rows_s, D_CONV), hs, w_out_b,
                 vec(ln2_g[l]), vec(ln2_b[l]), tm=rows_s)
    ys = _ffn_ln(ys, *ffb, tm=rows_s).reshape(nseq, s_new, d)
    s_cmp = kvc_s.reshape(1, nseq, s_new, *kvshape)
    s_sel = kvs_s.reshape(1, nseq, s_new, *kvshape)
    s_win = from_feature_major(win_next_t)
    return (yp, ys, p_cmp, p_sel, p_win, p_conv, s_cmp, s_sel, s_win,
            s_conv.reshape(1, nseq, CONV_W - 1, D_CONV))
```

```python
import functools

import jax
import jax.numpy as jnp
from jax import lax
from jax.experimental import pallas as pl
from jax.experimental.pallas import tpu as pltpu

F32 = jnp.float32
BF16 = jnp.bfloat16

D_MODEL = 1024
D_CONV = 512
D_ATTN = 512
HEAD_DIM = 64
N_HEADS = 8
N_KV = 2
GQA_R = N_HEADS // N_KV
N_BR = 3
CONV_W = 3
BLK = 64
BLK_SHIFT = BLK.bit_length() - 1
N_SEL = 16
WINDOW = 512
PAGE_SIZE = 128
D_FF = 2816
KV_W = 2 * N_KV * HEAD_DIM
D_MAIN = 3 * D_CONV + D_ATTN + 3 * KV_W
N_GATE = N_BR * N_HEADS
ALPHA = 2.0 ** 0.25
SCALE = HEAD_DIM ** -0.5
LN_EPS = 1e-5
NEG = -1e30
FORCE = 1e4

LANES = 128
SUBLANES = 8
BF16_SUBLANES = 2 * SUBLANES
V7X_VMEM_BYTES = 64 * 1024 * 1024
VMEM_LIMIT_BYTES = V7X_VMEM_BYTES * 7 // 8

FFN_ROW_TILE = 1024
PROJ_ROW_TILE = 512
OUT_ROW_TILE = 512
CMP_SEQS_PER_STEP = 4


def _cparams(sem):
    return pltpu.CompilerParams(dimension_semantics=sem, vmem_limit_bytes=VMEM_LIMIT_BYTES)


def _layer_norm(y, g, b):
    mu = jnp.mean(y, axis=-1, keepdims=True)
    yc = y - mu
    var = jnp.mean(yc * yc, axis=-1, keepdims=True)
    return yc * lax.rsqrt(var + LN_EPS) * g + b


def _dot(a, b):
    return jnp.dot(a, b, preferred_element_type=F32)


def _dot_nt(a, b):
    return lax.dot_general(a, b, (((1,), (1,)), ((), ())), preferred_element_type=F32)


FFN_CHUNK = 256


def _ffn_ln_body(x_ref, wg_ref, wu_ref, wd_ref, g_ref, b_ref, o_ref):
    x = x_ref[...]
    xb = x.astype(BF16)
    acc = jnp.zeros(x.shape, F32)
    for c in range(D_FF // FFN_CHUNK):
        sl = slice(c * FFN_CHUNK, (c + 1) * FFN_CHUNK)
        gg = _dot(xb, wg_ref[:, sl])
        uu = _dot(xb, wu_ref[:, sl])
        hh = (gg * jax.nn.sigmoid(gg) * uu).astype(BF16)
        acc = acc + _dot(hh, wd_ref[sl, :])
    o_ref[...] = _layer_norm(ALPHA * x + acc, g_ref[...], b_ref[...])


def _ffn_ln(x2d, wg, wu, wd, g, b, tm):
    rows = x2d.shape[0]
    const = lambda i: (0, 0)
    return pl.pallas_call(
        _ffn_ln_body,
        grid=(rows // tm,),
        in_specs=[
            pl.BlockSpec((tm, D_MODEL), lambda i: (i, 0)),
            pl.BlockSpec((D_MODEL, D_FF), const, pipeline_mode=pl.Buffered(1)),
            pl.BlockSpec((D_MODEL, D_FF), const, pipeline_mode=pl.Buffered(1)),
            pl.BlockSpec((D_FF, D_MODEL), const, pipeline_mode=pl.Buffered(1)),
            pl.BlockSpec((1, D_MODEL), const),
            pl.BlockSpec((1, D_MODEL), const),
        ],
        out_specs=pl.BlockSpec((tm, D_MODEL), lambda i: (i, 0)),
        out_shape=jax.ShapeDtypeStruct((rows, D_MODEL), F32),
        compiler_params=_cparams(("arbitrary",)),
        name="ffn_ln",
    )(x2d, wg, wu, wd, g, b)


_C_CB, _C_CC, _C_CH, _C_Q = 0, D_CONV, 2 * D_CONV, 3 * D_CONV
_C_KVC = 3 * D_CONV + D_ATTN
_C_KVS = _C_KVC + KV_W
_C_KVW = _C_KVS + KV_W


def _proj_piece(hb, w_ref, b_ref, lo, width):
    return _dot(hb, w_ref[:, lo:lo + width]) + b_ref[:, lo:lo + width]


_P_K = 3 * D_CONV
K_HALF = N_KV * HEAD_DIM
P_ROW_COLS = _P_K + 2 * K_HALF
GATE_ROWS = 32
_T_KV = D_ATTN
_T_GATE = _T_KV + 3 * KV_W
P_T_ROWS = _T_GATE + GATE_ROWS


def _proj_prompt_body(h_ref, w_ref, b_ref, wt_ref, bt_ref, wc_ref,
                      qt_ref, ks_ref, kw_ref, kvct_ref, kvst_ref, kvwt_ref, gates_ref,
                      yconv_ref, cstate_ref, ubuf):
    i = pl.program_id(1)
    tm = h_ref.shape[0]
    hb = h_ref[...].astype(BF16)

    def piece_t(lo, height):
        return _dot_nt(wt_ref[lo:lo + height, :], hb) + bt_ref[lo:lo + height, :]

    qt_ref[...] = (piece_t(0, D_ATTN) * (SCALE * LOG2E)).astype(BF16)
    lane = lax.broadcasted_iota(jnp.int32, (1, KV_W), 1)
    blk = (i * tm + lax.broadcasted_iota(jnp.int32, (tm, 1), 0)) >> BLK_SHIFT
    aux0, aux1 = HEAD_DIM, 2 * HEAD_DIM
    one = (lane == aux0 + AUX_BLOCKS) | (lane == aux1 + AUX_BLOCKS)
    is_k = (lane < aux0) | (lane >= aux1 + HEAD_DIM)
    k4 = _proj_piece(hb, w_ref, b_ref, _P_K, 2 * K_HALF)
    for ref, lo, ones in ((ks_ref, 0, one | (lane == aux0 + blk) | (lane == aux1 + blk)),
                          (kw_ref, K_HALF, one)):
        k = k4[:, lo:lo + K_HALF]
        ref[...] = jnp.where(is_k, jnp.concatenate([k, k], axis=1),
                             jnp.where(ones, 1.0, 0.0)).astype(BF16)
    for k, ref in enumerate((kvct_ref, kvst_ref, kvwt_ref)):
        ref[...] = piece_t(_T_KV + k * KV_W, KV_W)
    gates_ref[...] = jax.nn.sigmoid(piece_t(_T_GATE, GATE_ROWS))
    cb = _proj_piece(hb, w_ref, b_ref, _C_CB, D_CONV)
    u = _proj_piece(hb, w_ref, b_ref, _C_CC, D_CONV) * _proj_piece(hb, w_ref, b_ref, _C_CH, D_CONV)

    head = SUBLANES

    @pl.when(i == 0)
    def _():
        ubuf[0:head, :] = jnp.zeros((head, D_CONV), F32)

    ubuf[head:head + tm, :] = u
    u1 = ubuf[head - 1:head - 1 + tm, :]
    u2 = ubuf[head - 2:head - 2 + tm, :]
    z = u2 * wc_ref[0:1, :] + u1 * wc_ref[1:2, :] + u * wc_ref[2:3, :]
    yconv_ref[...] = cb * z
    tail = u[tm - head:tm, :]
    ubuf[0:head, :] = tail
    cstate_ref[...] = tail


def _proj_prompt(h, w_rows, b_rows, w_t, b_t, w_conv, tm):
    n, t, _ = h.shape
    const = lambda a, i: (0, 0)
    row = lambda a, i: (a, i, 0)
    col = lambda a, i: (a, 0, i)

    def out(width, dtype=F32):
        return jax.ShapeDtypeStruct((n, t, width), dtype)

    def out_t(rows, dtype=F32):
        return jax.ShapeDtypeStruct((n, rows, t), dtype)

    return pl.pallas_call(
        _proj_prompt_body,
        grid=(n, t // tm),
        in_specs=[
            pl.BlockSpec((None, tm, D_MODEL), row),
            pl.BlockSpec((D_MODEL, P_ROW_COLS), const),
            pl.BlockSpec((1, P_ROW_COLS), const),
            pl.BlockSpec((P_T_ROWS, D_MODEL), const),
            pl.BlockSpec((P_T_ROWS, 1), const),
            pl.BlockSpec((CONV_W, D_CONV), const),
        ],
        out_specs=[
            pl.BlockSpec((None, D_ATTN, tm), col),
            pl.BlockSpec((None, tm, KV_W), row),
            pl.BlockSpec((None, tm, KV_W), row),
            pl.BlockSpec((None, KV_W, tm), col),
            pl.BlockSpec((None, KV_W, tm), col),
            pl.BlockSpec((None, KV_W, tm), col),
            pl.BlockSpec((None, GATE_ROWS, tm), col),
            pl.BlockSpec((None, tm, D_CONV), row),
            pl.BlockSpec((None, SUBLANES, D_CONV), lambda a, i: (a, 0, 0)),
        ],
        out_shape=[out_t(D_ATTN, BF16), out(KV_W, BF16), out(KV_W, BF16), out_t(KV_W), out_t(KV_W),
                   out_t(KV_W), out_t(GATE_ROWS), out(D_CONV),
                   jax.ShapeDtypeStruct((n, SUBLANES, D_CONV), F32)],
        scratch_shapes=[pltpu.VMEM((tm + SUBLANES, D_CONV), F32)],
        compiler_params=_cparams(("arbitrary", "arbitrary")),
        name="proj_prompt",
    )(h, w_rows, b_rows, w_t, b_t, w_conv)


def _proj_sample_body(h_ref, w_ref, b_ref, wgt_ref, bgt_ref, q_ref, kv_ref, gates_ref, cb_ref, u_ref):
    hb = h_ref[...].astype(BF16)
    q_ref[...] = (_proj_piece(hb, w_ref, b_ref, _C_Q, D_ATTN) * SCALE).astype(BF16)
    kv_ref[...] = _proj_piece(hb, w_ref, b_ref, _C_KVC, 3 * KV_W)
    gates_ref[...] = jax.nn.sigmoid(_dot(hb, wgt_ref[...]) + bgt_ref[...])
    cb_ref[...] = _proj_piece(hb, w_ref, b_ref, _C_CB, D_CONV)
    u_ref[...] = _proj_piece(hb, w_ref, b_ref, _C_CC, D_CONV) * _proj_piece(hb, w_ref, b_ref, _C_CH, D_CONV)


def _proj_sample(h2d, w_main, b_main, w_gate, b_gate):
    rows = h2d.shape[0]

    def out(width, dtype=F32):
        return jax.ShapeDtypeStruct((rows, width), dtype)

    return pl.pallas_call(
        _proj_sample_body,
        out_shape=[out(D_ATTN, BF16), out(3 * KV_W), out(LANES), out(D_CONV), out(D_CONV)],
        compiler_params=pltpu.CompilerParams(vmem_limit_bytes=VMEM_LIMIT_BYTES),
        name="proj_sample",
    )(h2d, w_main, b_main, w_gate, b_gate)


def _conv_sample_body(cb_ref, u_ref, prev_ref, wc_ref, y_ref):
    s = u_ref.shape[1]
    ext = [prev_ref[:, 0, :], prev_ref[:, 1, :]] + [u_ref[:, t, :] for t in range(s)]
    for t in range(s):
        z = ext[t] * wc_ref[0:1, :] + ext[t + 1] * wc_ref[1:2, :] + ext[t + 2] * wc_ref[2:3, :]
        y_ref[:, t, :] = cb_ref[:, t, :] * z


def _conv_sample(cb, u, prev, w_conv):
    return pl.pallas_call(
        _conv_sample_body,
        out_shape=jax.ShapeDtypeStruct(u.shape, F32),
        name="conv_sample",
    )(cb, u, prev, w_conv)


BPP = PAGE_SIZE // BLK
D_PAIRS = HEAD_DIM // 2


def _compress_body(pt_ref, src_ref, pe_ref, w1_ref, w2_ref, kc_ref, buf, sem, *, n_pages, pages_per_seq):
    n = pl.program_id(0)
    slot = n % 2

    def page_copy(step, j, sl):
        dst = buf.at[sl, :, j, :]
        if pages_per_seq is None:
            src = src_ref.at[pt_ref[step * n_pages + j]]
        else:
            seq = step * (n_pages // pages_per_seq) + j // pages_per_seq
            src = src_ref.at[seq, :, pl.ds((j % pages_per_seq) * PAGE_SIZE, PAGE_SIZE)]
        return pltpu.make_async_copy(src, dst, sem.at[sl])

    def start(step, sl):
        for j in range(n_pages):
            page_copy(step, j, sl).start()

    @pl.when(n == 0)
    def _():
        start(n, slot)

    @pl.when(n + 1 < pl.num_programs(0))
    def _():
        start(n + 1, 1 - slot)

    for j in range(n_pages):
        page_copy(n, j, slot).wait()

    def feature_rows(f):
        return buf[slot, f]

    for c in range(2):
        acc = None
        for dp in range(D_PAIRS):
            rows = []
            for g in range(N_KV):
                f0 = (c * N_KV + g) * HEAD_DIM + 2 * dp
                rows.append(jnp.concatenate(
                    [feature_rows(f0) + pe_ref[c, 2 * dp:2 * dp + 1, :],
                     feature_rows(f0 + 1) + pe_ref[c, 2 * dp + 1:2 * dp + 2, :]], axis=-1))
            x = jnp.concatenate(rows, axis=0).astype(BF16)
            d = _dot(x, w1_ref[c, dp])
            acc = d if acc is None else acc + d
        hid = (acc * jax.nn.sigmoid(acc)).astype(BF16)
        kc_ref[c] = _dot(hid, w2_ref[c])


def _compress(pt_flat, src, pe_t, w1_t, w2_t, n_steps, n_pages, pages_per_seq, name):
    const3 = lambda a, pt: (0, 0, 0)
    grid_spec = pltpu.PrefetchScalarGridSpec(
        num_scalar_prefetch=1,
        grid=(n_steps,),
        in_specs=[
            pl.BlockSpec(memory_space=pl.ANY),
            pl.BlockSpec((2, HEAD_DIM, PAGE_SIZE), const3),
            pl.BlockSpec((2, D_PAIRS, 2 * PAGE_SIZE, PAGE_SIZE), lambda a, pt: (0, 0, 0, 0)),
            pl.BlockSpec((2, PAGE_SIZE, PAGE_SIZE), const3),
        ],
        out_specs=pl.BlockSpec((None, 2, N_KV * n_pages, PAGE_SIZE), lambda a, pt: (a, 0, 0, 0)),
        scratch_shapes=[pltpu.VMEM((2, KV_W, n_pages, PAGE_SIZE), F32),
                        pltpu.SemaphoreType.DMA((2,))],
    )
    return pl.pallas_call(
        functools.partial(_compress_body, n_pages=n_pages, pages_per_seq=pages_per_seq),
        grid_spec=grid_spec,
        out_shape=jax.ShapeDtypeStruct((n_steps, 2, N_KV * n_pages, PAGE_SIZE), F32),
        compiler_params=_cparams(("arbitrary",)),
        name=name,
    )(pt_flat, src, pe_t, w1_t, w2_t)


ATT_TQ = 256
ATT_TK = ATT_TQ
WIN_TILES = WINDOW // ATT_TK + 1
AUX_BLOCKS = 32
LOG2E = 1.4426950408889634


V_ROWS = HEAD_DIM + BF16_SUBLANES


def _attn_update(s_t, v_t, state):
    m, acc = state
    m_new = jnp.maximum(m, jnp.max(s_t, axis=0, keepdims=True))
    e = jnp.exp2(s_t - m_new)
    acc = jnp.exp2(m - m_new) * acc + _dot(v_t, e.astype(BF16))
    return m_new, acc


def _attn_prompt_body(qt_ref, ks_ref, kw_ref, vs_ref, vw_ref, kc_ref, vct_ref, gates_ref, wb_ref, o_ref):
    tq, tk = ATT_TQ, ATT_TK
    qi = pl.program_id(1)
    nb = kc_ref.shape[1]
    t0 = qi * tq
    kd = qi
    tpos = t0 + lax.broadcasted_iota(jnp.int32, (1, tq), 1)
    jrow = lax.broadcasted_iota(jnp.int32, (nb, 1), 0)
    vis = ((jrow + 1) * BLK - 1) <= tpos
    cur = tpos >> BLK_SHIFT

    def heads_of(x):
        return jnp.concatenate([x] * GQA_R, axis=1)

    def group_queries(g, block_bias=None, switch=None):
        zeros = jnp.zeros((AUX_BLOCKS, tq), BF16)
        top = zeros if block_bias is None else block_bias
        low = zeros if switch is None else jnp.broadcast_to(switch, (AUX_BLOCKS, tq)).astype(BF16)
        aux = jnp.concatenate([top, low], axis=0)
        pieces = []
        for r in range(GQA_R):
            h = g * GQA_R + r
            qh = qt_ref[h * HEAD_DIM:(h + 1) * HEAD_DIM, :]
            pieces.append(jnp.concatenate([qh, aux] if g == 0 else [aux, qh], axis=0))
        return jnp.concatenate(pieces, axis=1)

    q4 = [group_queries(g) for g in range(N_KV)]
    vis4 = heads_of(vis)

    o_cmp, q4_sel = [], []
    for g in range(N_KV):
        s = jnp.where(vis4, _dot(kc_ref[g], q4[g]), NEG)
        m = jnp.max(s, axis=0, keepdims=True)
        e = jnp.where(vis4, jnp.exp2(s - m), 0.0)
        l = jnp.sum(e, axis=0, keepdims=True)
        p = e / jnp.where(l > 0.0, l, 1.0)
        imp = sum(p[:, r * tq:(r + 1) * tq] for r in range(GQA_R))
        o_cmp.append(_dot(vct_ref[g], p.astype(BF16)))

        score = jnp.where(jrow > cur, NEG, imp)
        score = jnp.where((jrow == 0) | (jrow == cur) | (jrow == cur - 1), FORCE, score)
        rank = jnp.zeros((nb, tq), F32)
        for j2 in range(nb):
            row = score[j2:j2 + 1, :]
            tie = jnp.where(jrow > j2, 1.0, 0.0)
            rank = rank + jnp.where(row > score, 1.0, jnp.where(row == score, tie, 0.0))
        q4_sel.append(group_queries(g, jnp.where(rank < N_SEL, 0.0, NEG).astype(BF16)))

    def key_tile(k_ref, kt, g):
        return k_ref[pl.ds(pl.multiple_of(kt * tk, tk), tk), g * LANES:(g + 1) * LANES]

    ones_rows = jnp.ones((V_ROWS - HEAD_DIM, tk), BF16)

    def val_tile(v_ref, kt, g):
        v = v_ref[g * HEAD_DIM:(g + 1) * HEAD_DIM, pl.ds(pl.multiple_of(kt * tk, tk), tk)]
        return jnp.concatenate([v.astype(BF16), ones_rows], axis=0)

    wide = GQA_R * tq
    init = (jnp.full((1, wide), NEG, F32), jnp.zeros((V_ROWS, wide), F32))

    def win_scores(back, g):
        kt = jnp.maximum(kd - back, 0)
        qw = q4[g] if back == 0 else group_queries(g, None, jnp.where(kd >= back, 0.0, NEG))
        s = _dot(key_tile(kw_ref, kt, g), qw)
        if back in (0, WIN_TILES - 1):
            s = s + heads_of(wb_ref[min(back, 1)])
        return s

    groups = range(N_KV)
    s_diag = [_dot(key_tile(ks_ref, kd, g), q4_sel[g]) + heads_of(wb_ref[0]) for g in groups]
    s_win = [[win_scores(back, g) for g in groups] for back in range(WIN_TILES)]
    sel = tuple(_attn_update(s_diag[g], val_tile(vs_ref, kd, g), init) for g in groups)
    win = [init] * N_KV
    for back in range(WIN_TILES):
        kt = jnp.maximum(kd - back, 0)
        win = [_attn_update(s_win[back][g], val_tile(vw_ref, kt, g), win[g]) for g in groups]

    def sel_tiles(kts, states):
        s = [[_dot(key_tile(ks_ref, kt, g), q4_sel[g]) for g in groups] for kt in kts]
        for i, kt in enumerate(kts):
            states = tuple(_attn_update(s[i][g], val_tile(vs_ref, kt, g), states[g]) for g in groups)
        return states

    sel = lax.cond((kd & 1) == 1, lambda st: sel_tiles((kd - 1,), st), lambda st: st, sel)
    sel = lax.fori_loop(0, kd >> 1, lambda j, st: sel_tiles((2 * j, 2 * j + 1), st), sel)

    gates = gates_ref[...]
    outs = []
    for h in range(N_HEADS):
        g, r = h // GQA_R, h % GQA_R
        cols = slice(r * tq, (r + 1) * tq)
        o_sel = sel[g][1][:HEAD_DIM, cols] * (1.0 / sel[g][1][HEAD_DIM:HEAD_DIM + 1, cols])
        o_win = win[g][1][:HEAD_DIM, cols] * (1.0 / win[g][1][HEAD_DIM:HEAD_DIM + 1, cols])
        outs.append(gates[h:h + 1, :] * o_cmp[g][:, cols] + gates[N_HEADS + h:N_HEADS + h + 1, :] * o_sel
                    + gates[2 * N_HEADS + h:2 * N_HEADS + h + 1, :] * o_win)
    o_ref[...] = jnp.concatenate(outs, axis=0).T


def _attn_prompt(q_t, k_sel, k_win, kvs_t, kvw_t, kc_pad, vc_t, gates_t):
    n, _, t = q_t.shape
    nb = kc_pad.shape[2]
    row = lambda a, i: (a, i, 0)
    seq = lambda a, i: (a, 0, 0)
    seq4 = lambda a, i: (a, 0, 0, 0)
    v_rows = lambda a, i: (a, 1, 0)

    def win_bias(back):
        dist = back * ATT_TK + jnp.arange(ATT_TQ)[None, :] - jnp.arange(ATT_TK)[:, None]
        return jnp.where((dist >= 0) & (dist < WINDOW), 0.0, NEG).astype(F32)

    wb = jnp.stack([win_bias(0), win_bias(WIN_TILES - 1)])
    return pl.pallas_call(
        _attn_prompt_body,
        grid=(n, t // ATT_TQ),
        in_specs=[
            pl.BlockSpec((None, D_ATTN, ATT_TQ), lambda a, i: (a, 0, i)),
            pl.BlockSpec((None, t, KV_W), seq),
            pl.BlockSpec((None, t, KV_W), seq),
            pl.BlockSpec((None, N_KV * HEAD_DIM, t), v_rows),
            pl.BlockSpec((None, N_KV * HEAD_DIM, t), v_rows),
            pl.BlockSpec((None, N_KV, nb, LANES), seq4),
            pl.BlockSpec((None, N_KV, HEAD_DIM, nb), seq4),
            pl.BlockSpec((None, GATE_ROWS, ATT_TQ), lambda a, i: (a, 0, i)),
            pl.BlockSpec((2, ATT_TK, ATT_TQ), lambda a, i: (0, 0, 0)),
        ],
        out_specs=pl.BlockSpec((None, ATT_TQ, D_ATTN), row),
        out_shape=jax.ShapeDtypeStruct((n, t, D_ATTN), F32),
        compiler_params=_cparams(("arbitrary", "arbitrary")),
        name="attn_prompt",
    )(q_t, k_sel, k_win, kvs_t, kvw_t, kc_pad, vc_t, gates_t, wb)


DEC_SEQ_PER_STEP = 8


def _dec_cmp_body(q2_ref, kc_ref, ocmp_ref, idx_ref, p_buf, imp_buf, *, past, s_new):
    assert BPP == 2
    ns = q2_ref.shape[0]
    qrows = s_new * GQA_R
    n_pages = kc_ref.shape[2] // N_KV
    nb = n_pages * BPP
    width = imp_buf.shape[1]
    imp_buf[...] = jnp.zeros(imp_buf.shape, F32)
    lane = lax.broadcasted_iota(jnp.int32, (1, width), 1)
    jc = jnp.where(lane < n_pages, BPP * lane, BPP * (lane - n_pages) + 1)
    jc = jnp.where(lane == nb, nb, jc)

    row2 = lax.broadcasted_iota(jnp.int32, (2 * qrows, 1), 0)
    half = jnp.where(row2 >= qrows, 1, 0)
    blk = BPP * lax.broadcasted_iota(jnp.int32, (1, n_pages), 1) + half
    tok = (row2 & (qrows - 1)) >> (GQA_R.bit_length() - 1)
    vis = ((blk + 1) * BLK - 1) <= (past + tok)

    def both(x):
        return jnp.concatenate([x, x], axis=0)

    def pages(a, c, g):
        return kc_ref[a, c, g * n_pages:(g + 1) * n_pages, :].astype(BF16)

    pairs = [(a, g) for a in range(ns) for g in range(N_KV)]
    scores = [_dot_nt(q2_ref[a, g], pages(a, 0, g)) for a, g in pairs]
    for i, (a, g) in enumerate(pairs):
        s = jnp.where(vis, scores[i], NEG)
        m = jnp.max(s, axis=-1, keepdims=True)
        e = jnp.where(vis, jnp.exp(s - both(jnp.maximum(m[:qrows], m[qrows:]))), 0.0)
        l = jnp.sum(e, axis=-1, keepdims=True)
        l = l[:qrows] + l[qrows:]
        p = e / both(jnp.where(l > 0.0, l, 1.0))
        o2 = _dot(p.astype(BF16), pages(a, 1, g))
        ocmp_ref[a, g] = o2[:qrows, :HEAD_DIM] + o2[qrows:, HEAD_DIM:]
        p_buf[i] = p
        imp = sum(p_buf[i, pl.ds(r, 2 * s_new, stride=GQA_R), :] for r in range(GQA_R))
        row0 = i * s_new
        imp_buf[row0:row0 + s_new, 0:n_pages] = imp[:s_new]
        imp_buf[row0:row0 + s_new, n_pages:nb] = imp[s_new:]

    rows = ns * N_KV * s_new
    imp = imp_buf[...]
    t_row = lax.broadcasted_iota(jnp.int32, (rows, 1), 0) & (s_new - 1)
    cur = (past + t_row) >> BLK_SHIFT
    score = jnp.where(jc > cur, NEG, imp)
    score = jnp.where((jc == 0) | (jc == cur) | (jc == cur - 1), FORCE, score)
    score = jnp.where(lane <= nb, score, -jnp.inf)
    jcf = jc.astype(F32)
    out_lane = lax.broadcasted_iota(jnp.int32, (1, LANES), 1)
    idx = jnp.zeros((rows, LANES), jnp.int32)
    for k in range(N_SEL):
        m = jnp.max(score, axis=-1, keepdims=True)
        first = jnp.min(jnp.where(score == m, jcf, 1e9), axis=-1, keepdims=True)
        idx = jnp.where(out_lane == k, first.astype(jnp.int32), idx)
        score = jnp.where(jcf == first, -jnp.inf, score)
    idx_ref[...] = idx


def _dec_cmp(q2, kc, past, s_new):
    nseq = q2.shape[0]
    qrows = s_new * GQA_R
    rows_kc = kc.shape[2]
    n_pages = rows_kc // N_KV
    nb = n_pages * BPP
    ns = DEC_SEQ_PER_STEP
    width = (nb + 1 + LANES - 1) // LANES * LANES
    seq4 = lambda i: (i, 0, 0, 0)
    return pl.pallas_call(
        functools.partial(_dec_cmp_body, past=past, s_new=s_new),
        grid=(nseq // ns,),
        in_specs=[
            pl.BlockSpec((ns, N_KV, 2 * qrows, 2 * HEAD_DIM), seq4),
            pl.BlockSpec((ns, 2, rows_kc, PAGE_SIZE), seq4),
        ],
        out_specs=[
            pl.BlockSpec((ns, N_KV, qrows, HEAD_DIM), seq4),
            pl.BlockSpec((ns * N_KV * s_new, LANES), lambda i: (i, 0)),
        ],
        out_shape=[jax.ShapeDtypeStruct((nseq, N_KV, qrows, HEAD_DIM), F32),
                   jax.ShapeDtypeStruct((nseq * N_KV * s_new, LANES), jnp.int32)],
        scratch_shapes=[pltpu.VMEM((ns * N_KV, 2 * qrows, n_pages), F32),
                        pltpu.VMEM((ns * N_KV * s_new, width), F32)],
        compiler_params=_cparams(("arbitrary",)),
        name="dec_cmp_select",
    )(q2, kc)


def _dec_sel_win_body(idx_ref, pt_ref, q_ref, idx2_ref, expand_ref, pool_ref, newblk_ref, win_ref,
                      kvw_new_ref, osel_ref, owin_ref, winout_ref, kvbuf, sem, *, past, s_new, n_pages):
    n_past_blk = past // BLK
    n = pl.program_id(0)
    nseq = pl.num_programs(0)
    slot = n % 2
    rows = s_new * GQA_R
    seg_lanes = N_SEL * PAGE_SIZE
    nkeys = s_new * seg_lanes

    def block_copy(seq, c, k, sl):
        g, t = c // s_new, c % s_new
        j = jnp.minimum(idx_ref[(seq * N_KV * s_new + c) * N_SEL + k], n_past_blk - 1)
        page = pt_ref[seq * n_pages + (j >> 1)]
        dst = kvbuf.at[sl, :, g, :, pl.ds((t * N_SEL + k) * PAGE_SIZE, PAGE_SIZE)]
        return pltpu.make_async_copy(pool_ref.at[page, :, g], dst, sem.at[sl])

    def start(seq, sl):
        for c in range(N_KV * s_new):
            for k in range(N_SEL):
                block_copy(seq, c, k, sl).start()

    @pl.when(n == 0)
    def _():
        start(n, slot)

    @pl.when(n + 1 < nseq)
    def _():
        start(n + 1, 1 - slot)

    for c in range(N_KV * s_new):
        for k in range(N_SEL):
            block_copy(n, c, k, slot).wait()

    assert win_ref.shape[1] == min(WINDOW, past + s_new)
    slid = jnp.concatenate([win_ref[...], kvw_new_ref[...]], axis=1)
    winout_ref[...] = slid[:, s_new:s_new + win_ref.shape[1]]

    t_row = lax.broadcasted_iota(jnp.int32, (rows, 1), 0) >> (GQA_R.bit_length() - 1)
    qpos = past + t_row
    lane = lax.broadcasted_iota(jnp.int32, (1, nkeys), 1)
    own = (lane >> (seg_lanes.bit_length() - 1)) == t_row
    row_in_page = lane & (PAGE_SIZE - 1)
    nlane = lax.broadcasted_iota(jnp.int32, (1, PAGE_SIZE), 1)
    nwin = win_ref.shape[1]
    wpos = past - nwin + lax.broadcasted_iota(jnp.int32, (1, nwin), 1)
    groups = range(N_KV)

    def krows(g):
        return pl.ds(g * HEAD_DIM, HEAD_DIM)

    def vrows(g):
        return pl.ds((N_KV + g) * HEAD_DIM, HEAD_DIM)

    q = [q_ref[g] for g in groups]
    jv2 = _dot(idx2_ref[...], expand_ref[...])
    s_sel = [_dot(q[g], kvbuf[slot, 0, g].astype(BF16)) for g in groups]
    s_new_sel = [_dot(q[g], newblk_ref[krows(g), :].astype(BF16)) for g in groups]
    s_win = [_dot(q[g], win_ref[krows(g), :].astype(BF16)) for g in groups]
    s_new_win = [_dot(q[g], kvw_new_ref[krows(g), :].astype(BF16)) for g in groups]

    def two_piece_softmax(sa, va, sb, vb):
        m = jnp.maximum(jnp.max(sa, axis=-1, keepdims=True), jnp.max(sb, axis=-1, keepdims=True))
        ea, eb = jnp.exp(sa - m), jnp.exp(sb - m)
        l = jnp.sum(ea, axis=-1, keepdims=True) + jnp.sum(eb, axis=-1, keepdims=True)
        return (_dot_nt(ea.astype(BF16), va) + _dot_nt(eb.astype(BF16), vb)) / l

    for g in groups:
        jv = jv2[g:g + 1, :].astype(jnp.int32)
        kpos = jv * BLK + (row_in_page & (BLK - 1))
        valid = (own & (jv < n_past_blk) & ((row_in_page >> BLK_SHIFT) == (jv & (BPP - 1)))
                 & (kpos <= qpos))
        has_new = jnp.max(jnp.where(own & (jv >= n_past_blk), 1.0, 0.0), axis=-1, keepdims=True)
        valid_n = (has_new > 0.5) & (nlane < BLK) & (n_past_blk * BLK + nlane <= qpos)
        osel_ref[g] = two_piece_softmax(
            jnp.where(valid, s_sel[g], NEG), kvbuf[slot, 1, g].astype(BF16),
            jnp.where(valid_n, s_new_sel[g], NEG), newblk_ref[vrows(g), :].astype(BF16))

        dw = qpos - wpos
        dn = qpos - (past + nlane)
        valid_w = (dw >= 0) & (dw < WINDOW) & (wpos >= 0)
        valid_n = (dn >= 0) & (dn < WINDOW) & (nlane < s_new)
        owin_ref[g] = two_piece_softmax(
            jnp.where(valid_w, s_win[g], NEG), win_ref[vrows(g), :].astype(BF16),
            jnp.where(valid_n, s_new_win[g], NEG), kvw_new_ref[vrows(g), :].astype(BF16))


def _dec_sel_win(idx_flat, pt_flat, q_g, idx2, pool_t, newblk_t, win_t, kvw_new_t, past, s_new, n_pages):
    nseq = q_g.shape[0]
    rows = s_new * GQA_R
    nwin = win_t.shape[2]
    nkeys = s_new * N_SEL * PAGE_SIZE
    n_phys = pool_t.shape[0]
    expand = (jnp.arange(nkeys)[None, :] // PAGE_SIZE == jnp.arange(s_new * N_SEL)[:, None]).astype(BF16)
    per_seq = lambda a, idx, pt: (a, 0, 0, 0)
    per_seq3 = lambda a, idx, pt: (a, 0, 0)
    grid_spec = pltpu.PrefetchScalarGridSpec(
        num_scalar_prefetch=2,
        grid=(nseq,),
        in_specs=[
            pl.BlockSpec((None, N_KV, rows, HEAD_DIM), per_seq),
            pl.BlockSpec((None, rows, s_new * N_SEL), per_seq3),
            pl.BlockSpec((s_new * N_SEL, nkeys), lambda a, idx, pt: (0, 0)),
            pl.BlockSpec(memory_space=pl.ANY),
            pl.BlockSpec((None, KV_W, PAGE_SIZE), per_seq3),
            pl.BlockSpec((None, KV_W, nwin), per_seq3),
            pl.BlockSpec((None, KV_W, PAGE_SIZE), per_seq3),
        ],
        out_specs=[
            pl.BlockSpec((None, N_KV, rows, HEAD_DIM), per_seq),
            pl.BlockSpec((None, N_KV, rows, HEAD_DIM), per_seq),
            pl.BlockSpec((None, KV_W, nwin), per_seq3),
        ],
        scratch_shapes=[pltpu.VMEM((2, 2, N_KV, HEAD_DIM, nkeys), F32),
                        pltpu.SemaphoreType.DMA((2,))],
    )
    shape = jax.ShapeDtypeStruct((nseq, N_KV, rows, HEAD_DIM), F32)
    return pl.pallas_call(
        functools.partial(_dec_sel_win_body, past=past, s_new=s_new, n_pages=n_pages),
        grid_spec=grid_spec,
        out_shape=[shape, shape, jax.ShapeDtypeStruct(win_t.shape, F32)],
        compiler_params=_cparams(("arbitrary",)),
        name="dec_sel_win",
    )(idx_flat, pt_flat, q_g, idx2, expand, pool_t.reshape(n_phys, 2, N_KV, HEAD_DIM, PAGE_SIZE),
      newblk_t, win_t, kvw_new_t)


def _out_ln_body(*refs, gated):
    if gated:
        ocmp_ref, osel_ref, owin_ref, gates_ref, yconv_ref, h_ref, wo_ref, g_ref, b_ref, y_ref = refs
        gates = gates_ref[...]
        pieces = []
        for h in range(N_HEADS):
            col = slice(h * HEAD_DIM, (h + 1) * HEAD_DIM)
            pieces.append(gates[:, h:h + 1] * ocmp_ref[:, col]
                          + gates[:, N_HEADS + h:N_HEADS + h + 1] * osel_ref[:, col]
                          + gates[:, 2 * N_HEADS + h:2 * N_HEADS + h + 1] * owin_ref[:, col])
        o = jnp.concatenate(pieces, axis=-1)
    else:
        o_ref, yconv_ref, h_ref, wo_ref, g_ref, b_ref, y_ref = refs
        o = o_ref[...]
    mixed = jnp.concatenate([o, yconv_ref[...]], axis=-1).astype(BF16)
    y_ref[...] = _layer_norm(ALPHA * h_ref[...] + _dot(mixed, wo_ref[...]), g_ref[...], b_ref[...])


def _out_ln(branches, yconv, h2d, w_out, g, b, tm):
    rows = h2d.shape[0]
    gated = len(branches) > 1
    row = lambda i: (i, 0)
    const = lambda i: (0, 0)
    widths = [D_ATTN, D_ATTN, D_ATTN, LANES] if gated else [D_ATTN]
    return pl.pallas_call(
        functools.partial(_out_ln_body, gated=gated),
        grid=(rows // tm,),
        in_specs=[pl.BlockSpec((tm, w), row) for w in widths] + [
            pl.BlockSpec((tm, D_CONV), row),
            pl.BlockSpec((tm, D_MODEL), row),
            pl.BlockSpec((D_MODEL, D_MODEL), const),
            pl.BlockSpec((1, D_MODEL), const),
            pl.BlockSpec((1, D_MODEL), const),
        ],
        out_specs=pl.BlockSpec((tm, D_MODEL), row),
        out_shape=jax.ShapeDtypeStruct((rows, D_MODEL), F32),
        compiler_params=_cparams(("arbitrary",)),
        name="out_ln",
    )(*branches, yconv, h2d, w_out, g, b)


def _block_diag_groups(w):
    z = jnp.zeros_like(w)
    return jnp.concatenate([jnp.concatenate([w, z], axis=-1),
                            jnp.concatenate([z, w], axis=-1)], axis=-2).astype(BF16)


def _to_group_layout(x, nseq, s_new):
    x = x.reshape(nseq, s_new, N_KV, GQA_R, HEAD_DIM).transpose(0, 2, 1, 3, 4)
    return x.reshape(nseq, N_KV, s_new * GQA_R, HEAD_DIM)


def _from_group_layout(x, nseq, s_new):
    x = x.reshape(nseq, N_KV, s_new, GQA_R, HEAD_DIM).transpose(0, 2, 1, 3, 4)
    return x.reshape(nseq * s_new, D_ATTN)


def kernel(x_prompt, x_sample, cache_cmp_kv, cache_sel_kv, state_win_kv, state_conv, page_table,
           ffa_gate, ffa_up, ffa_down, ln1_g, ln1_b, w_in, b_in, w_conv, cmp_w1, cmp_w2, cmp_pe,
           w_out, ln2_g, ln2_b, ffb_gate, ffb_up, ffb_down, ln3_g, ln3_b):
    depth = ffa_gate.shape[0]
    assert depth == 1
    l = 0
    n, t, d = x_prompt.shape
    nseq, s_new, _ = x_sample.shape
    n_pages = page_table.shape[1]
    past = n_pages * PAGE_SIZE
    rows_p, rows_s = n * t, nseq * s_new
    kvshape = (2, N_KV, HEAD_DIM)
    assert d == D_MODEL and rows_p % FFN_ROW_TILE == 0 and t % PROJ_ROW_TILE == 0 and t % ATT_TQ == 0
    assert t // BLK <= AUX_BLOCKS and n % CMP_SEQS_PER_STEP == 0 and nseq % DEC_SEQ_PER_STEP == 0
    assert s_new * GQA_R == BF16_SUBLANES and s_new <= BLK and past % BLK == 0
    assert state_win_kv.shape[2] == WINDOW

    bf = lambda w: w.astype(BF16)
    vec = lambda v: v.reshape(1, -1)
    w_main = bf(w_in[l][:, :D_MAIN])
    b_main = vec(b_in[l][:D_MAIN])
    w_gate = bf(jnp.pad(w_in[l][:, D_MAIN:], ((0, 0), (0, LANES - N_GATE))))
    b_gate = vec(jnp.pad(b_in[l][D_MAIN:], (0, LANES - N_GATE)))
    pe_t = jnp.concatenate([cmp_pe[l].transpose(1, 2, 0)] * BPP, axis=-1)
    w1_t = _block_diag_groups(cmp_w1[l].transpose(0, 2, 1, 3)).reshape(
        2, D_PAIRS, 2 * PAGE_SIZE, PAGE_SIZE)
    w2_t = _block_diag_groups(cmp_w2[l])
    w_out_b = bf(w_out[l])
    ffa = (bf(ffa_gate[l]), bf(ffa_up[l]), bf(0.5 * ffa_down[l]), vec(ln1_g[l]), vec(ln1_b[l]))
    ffb = (bf(ffb_gate[l]), bf(ffb_up[l]), bf(0.5 * ffb_down[l]), vec(ln3_g[l]), vec(ln3_b[l]))

    def prompt_cols(x):
        return jnp.concatenate([x[..., :_C_Q], x[..., _C_KVS:_C_KVS + K_HALF],
                                x[..., _C_KVW:_C_KVW + K_HALF]], axis=-1)

    def prompt_rows(x):
        return jnp.pad(x[..., _C_Q:], [(0, 0)] * (x.ndim - 1) + [(0, GATE_ROWS - N_GATE)])

    w_rows, b_rows = bf(prompt_cols(w_in[l])), vec(prompt_cols(b_in[l]))
    w_tr, b_tr = bf(prompt_rows(w_in[l]).T), prompt_rows(b_in[l]).reshape(-1, 1)

    pps = t // PAGE_SIZE
    cmp_seqs = CMP_SEQS_PER_STEP
    hp = _ffn_ln(x_prompt.reshape(rows_p, d), *ffa, tm=FFN_ROW_TILE)
    q, k_sel, k_win, kvc_t, kvs_t, kvw_t, gates_t, yconv, cstate = _proj_prompt(
        hp.reshape(n, t, d), w_rows, b_rows, w_tr, b_tr, w_conv[l], tm=PROJ_ROW_TILE)
    kc = _compress(jnp.zeros((1,), jnp.int32), kvc_t, pe_t, w1_t, w2_t, n_steps=n // cmp_seqs,
                   n_pages=cmp_seqs * pps, pages_per_seq=pps, name="compress_prompt")
    kc = kc.reshape(n // cmp_seqs, 2, N_KV, cmp_seqs, pps, BPP, HEAD_DIM)
    kc = kc.transpose(0, 3, 2, 1, 4, 5, 6).reshape(n, N_KV, 2, pps * BPP, HEAD_DIM)
    zc = jnp.zeros_like(kc[:, 0, 0])
    kc_pad = bf(jnp.stack([jnp.concatenate([kc[:, 0, 0], zc], axis=-1),
                           jnp.concatenate([zc, kc[:, 1, 0]], axis=-1)], axis=1))
    vc_t = bf(jnp.swapaxes(kc[:, :, 1], -1, -2))
    o = _attn_prompt(q, k_sel, k_win, kvs_t, kvw_t, kc_pad, vc_t, gates_t)
    yp = _out_ln([o.reshape(rows_p, D_ATTN)], yconv.reshape(rows_p, D_CONV), hp, w_out_b,
                 vec(ln2_g[l]), vec(ln2_b[l]), tm=OUT_ROW_TILE)
    yp = _ffn_ln(yp, *ffb, tm=FFN_ROW_TILE).reshape(n, t, d)

    def from_feature_major(x_t):
        return jnp.moveaxis(x_t.reshape(x_t.shape[0], *kvshape, x_t.shape[-1]), -1, 1)[None]

    p_cmp = from_feature_major(kvc_t)
    p_sel = from_feature_major(kvs_t)
    p_win = from_feature_major(kvw_t[:, :, t - min(WINDOW, t):])
    p_conv = cstate[:, SUBLANES - (CONV_W - 1):].reshape(1, n, CONV_W - 1, D_CONV)

    hs = _ffn_ln(x_sample.reshape(rows_s, d), *ffa, tm=rows_s)
    qs, kv3, gates_s, cb_s, u_s = _proj_sample(hs, w_main, b_main, w_gate, b_gate)
    kvc_s, kvs_s, kvw_s = kv3[:, :KV_W], kv3[:, KV_W:2 * KV_W], kv3[:, 2 * KV_W:]
    u3 = u_s.reshape(nseq, s_new, D_CONV)
    yconv_s = _conv_sample(cb_s.reshape(nseq, s_new, D_CONV), u3, state_conv[l], w_conv[l])
    s_conv = jnp.concatenate([state_conv[l], u3], axis=1)[:, -(CONV_W - 1):]

    def feature_major(x):
        return jnp.moveaxis(x, -4, -1).reshape(*x.shape[:-4], KV_W, x.shape[-4])

    def new_rows_t(x2d):
        xt = x2d.reshape(nseq, s_new, KV_W).transpose(0, 2, 1)
        return jnp.pad(xt, ((0, 0), (0, 0), (0, PAGE_SIZE - s_new)))

    kc_s = _compress(page_table.reshape(-1), feature_major(cache_cmp_kv[l]), pe_t, w1_t, w2_t,
                     n_steps=nseq, n_pages=n_pages, pages_per_seq=None, name="compress_decode")
    q_g = _to_group_layout(qs, nseq, s_new)
    zq = jnp.zeros_like(q_g)
    q2 = jnp.concatenate([jnp.concatenate([q_g, zq], axis=-1), jnp.concatenate([zq, q_g], axis=-1)], axis=-2)
    ocmp_c, idx_pad = _dec_cmp(q2, kc_s, past, s_new)
    idx = idx_pad[:, :N_SEL]
    idx2 = jnp.pad(bf(idx.reshape(nseq, N_KV, s_new * N_SEL)), ((0, 0), (0, s_new * GQA_R - N_KV), (0, 0)))
    win_t = feature_major(state_win_kv[l])
    kvw_new_t = new_rows_t(kvw_s)
    osel_c, owin_c, win_next_t = _dec_sel_win(
        idx.reshape(-1), page_table.reshape(-1), q_g, idx2, feature_major(cache_sel_kv[l]),
        new_rows_t(kvs_s), win_t, kvw_new_t, past, s_new, n_pages)
    branches = [_from_group_layout(x, nseq, s_new) for x in (ocmp_c, osel_c, owin_c)]
    ys = _out_ln(branches + [gates_s], yconv_s.reshape(rows_s, D_CONV), hs, w_out_b,
                 vec(ln2_g[l]), vec(ln2_b[l]), tm=rows_s)
    ys = _ffn_ln(ys, *ffb, tm=rows_s).reshape(nseq, s_new, d)
    s_cmp = kvc_s.reshape(1, nseq, s_new, *kvshape)
    s_sel = kvs_s.reshape(1, nseq, s_new, *kvshape)
    s_win = from_feature_major(win_next_t)
    return (yp, ys, p_cmp, p_sel, p_win, p_conv, s_cmp, s_sel, s_win,
            s_conv.reshape(1, nseq, CONV_W - 1, D_CONV))
```

```python
import functools

import jax
import jax.numpy as jnp
from jax import lax
from jax.experimental import pallas as pl
from jax.experimental.pallas import tpu as pltpu

F32 = jnp.float32
BF16 = jnp.bfloat16

D_MODEL = 1024
D_CONV = 512
D_ATTN = 512
HEAD_DIM = 64
N_HEADS = 8
N_KV = 2
GQA_R = N_HEADS // N_KV
N_BR = 3
CONV_W = 3
BLK = 64
BLK_SHIFT = BLK.bit_length() - 1
N_SEL = 16
WINDOW = 512
PAGE_SIZE = 128
D_FF = 2816
KV_W = 2 * N_KV * HEAD_DIM
D_MAIN = 3 * D_CONV + D_ATTN + 3 * KV_W
N_GATE = N_BR * N_HEADS
ALPHA = 2.0 ** 0.25
SCALE = HEAD_DIM ** -0.5
LN_EPS = 1e-5
NEG = -1e30
FORCE = 1e4

LANES = 128
SUBLANES = 8
BF16_SUBLANES = 2 * SUBLANES
V7X_VMEM_BYTES = 64 * 1024 * 1024
VMEM_LIMIT_BYTES = V7X_VMEM_BYTES * 7 // 8

FFN_ROW_TILE = 1024
PROJ_ROW_TILE = 512
OUT_ROW_TILE = 512
CMP_SEQS_PER_STEP = 4


def _cparams(sem):
    return pltpu.CompilerParams(dimension_semantics=sem, vmem_limit_bytes=VMEM_LIMIT_BYTES)


def _layer_norm(y, g, b):
    mu = jnp.mean(y, axis=-1, keepdims=True)
    yc = y - mu
    var = jnp.mean(yc * yc, axis=-1, keepdims=True)
    return yc * lax.rsqrt(var + LN_EPS) * g + b


def _dot(a, b):
    return jnp.dot(a, b, preferred_element_type=F32)


def _dot_nt(a, b):
    return lax.dot_general(a, b, (((1,), (1,)), ((), ())), preferred_element_type=F32)


FFN_CHUNK = 256


def _ffn_ln_body(x_ref, wg_ref, wu_ref, wd_ref, g_ref, b_ref, o_ref):
    x = x_ref[...]
    xb = x.astype(BF16)
    acc = jnp.zeros(x.shape, F32)
    for c in range(D_FF // FFN_CHUNK):
        sl = slice(c * FFN_CHUNK, (c + 1) * FFN_CHUNK)
        gg = _dot(xb, wg_ref[:, sl])
        uu = _dot(xb, wu_ref[:, sl])
        hh = (gg * jax.nn.sigmoid(gg) * uu).astype(BF16)
        acc = acc + _dot(hh, wd_ref[sl, :])
    o_ref[...] = _layer_norm(ALPHA * x + acc, g_ref[...], b_ref[...])


def _ffn_ln(x2d, wg, wu, wd, g, b, tm):
    rows = x2d.shape[0]
    const = lambda i: (0, 0)
    return pl.pallas_call(
        _ffn_ln_body,
        grid=(rows // tm,),
        in_specs=[
            pl.BlockSpec((tm, D_MODEL), lambda i: (i, 0)),
            pl.BlockSpec((D_MODEL, D_FF), const, pipeline_mode=pl.Buffered(1)),
            pl.BlockSpec((D_MODEL, D_FF), const, pipeline_mode=pl.Buffered(1)),
            pl.BlockSpec((D_FF, D_MODEL), const, pipeline_mode=pl.Buffered(1)),
            pl.BlockSpec((1, D_MODEL), const),
            pl.BlockSpec((1, D_MODEL), const),
        ],
        out_specs=pl.BlockSpec((tm, D_MODEL), lambda i: (i, 0)),
        out_shape=jax.ShapeDtypeStruct((rows, D_MODEL), F32),
        compiler_params=_cparams(("arbitrary",)),
        name="ffn_ln",
    )(x2d, wg, wu, wd, g, b)


_C_CB, _C_CC, _C_CH, _C_Q = 0, D_CONV, 2 * D_CONV, 3 * D_CONV
_C_KVC = 3 * D_CONV + D_ATTN
_C_KVS = _C_KVC + KV_W
_C_KVW = _C_KVS + KV_W


def _proj_piece(hb, w_ref, b_ref, lo, width):
    return _dot(hb, w_ref[:, lo:lo + width]) + b_ref[:, lo:lo + width]


_P_K = 3 * D_CONV
K_HALF = N_KV * HEAD_DIM
P_ROW_COLS = _P_K + 2 * K_HALF
GATE_ROWS = 32
_T_KV = D_ATTN
_T_GATE = _T_KV + 3 * KV_W
P_T_ROWS = _T_GATE + GATE_ROWS


def _proj_prompt_body(h_ref, w_ref, b_ref, wt_ref, bt_ref, wc_ref,
                      qt_ref, ks_ref, kw_ref, kvct_ref, kvst_ref, kvwt_ref, gates_ref,
                      yconv_ref, cstate_ref, ubuf):
    i = pl.program_id(1)
    tm = h_ref.shape[0]
    hb = h_ref[...].astype(BF16)

    def piece_t(lo, height):
        return _dot_nt(wt_ref[lo:lo + height, :], hb) + bt_ref[lo:lo + height, :]

    qt_ref[...] = (piece_t(0, D_ATTN) * (SCALE * LOG2E)).astype(BF16)
    lane = lax.broadcasted_iota(jnp.int32, (1, KV_W), 1)
    blk = (i * tm + lax.broadcasted_iota(jnp.int32, (tm, 1), 0)) >> BLK_SHIFT
    aux0, aux1 = HEAD_DIM, 2 * HEAD_DIM
    one = (lane == aux0 + AUX_BLOCKS) | (lane == aux1 + AUX_BLOCKS)
    is_k = (lane < aux0) | (lane >= aux1 + HEAD_DIM)
    k4 = _proj_piece(hb, w_ref, b_ref, _P_K, 2 * K_HALF)
    for ref, lo, ones in ((ks_ref, 0, one | (lane == aux0 + blk) | (lane == aux1 + blk)),
                          (kw_ref, K_HALF, one)):
        k = k4[:, lo:lo + K_HALF]
        ref[...] = jnp.where(is_k, jnp.concatenate([k, k], axis=1),
                             jnp.where(ones, 1.0, 0.0)).astype(BF16)
    for k, ref in enumerate((kvct_ref, kvst_ref, kvwt_ref)):
        ref[...] = piece_t(_T_KV + k * KV_W, KV_W)
    gates_ref[...] = jax.nn.sigmoid(piece_t(_T_GATE, GATE_ROWS))
    cb = _proj_piece(hb, w_ref, b_ref, _C_CB, D_CONV)
    u = _proj_piece(hb, w_ref, b_ref, _C_CC, D_CONV) * _proj_piece(hb, w_ref, b_ref, _C_CH, D_CONV)

    head = SUBLANES

    @pl.when(i == 0)
    def _():
        ubuf[0:head, :] = jnp.zeros((head, D_CONV), F32)

    ubuf[head:head + tm, :] = u
    u1 = ubuf[head - 1:head - 1 + tm, :]
    u2 = ubuf[head - 2:head - 2 + tm, :]
    z = u2 * wc_ref[0:1, :] + u1 * wc_ref[1:2, :] + u * wc_ref[2:3, :]
    yconv_ref[...] = cb * z
    tail = u[tm - head:tm, :]
    ubuf[0:head, :] = tail
    cstate_ref[...] = tail


def _proj_prompt(h, w_rows, b_rows, w_t, b_t, w_conv, tm):
    n, t, _ = h.shape
    const = lambda a, i: (0, 0)
    row = lambda a, i: (a, i, 0)
    col = lambda a, i: (a, 0, i)

    def out(width, dtype=F32):
        return jax.ShapeDtypeStruct((n, t, width), dtype)

    def out_t(rows, dtype=F32):
        return jax.ShapeDtypeStruct((n, rows, t), dtype)

    return pl.pallas_call(
        _proj_prompt_body,
        grid=(n, t // tm),
        in_specs=[
            pl.BlockSpec((None, tm, D_MODEL), row),
            pl.BlockSpec((D_MODEL, P_ROW_COLS), const),
            pl.BlockSpec((1, P_ROW_COLS), const),
            pl.BlockSpec((P_T_ROWS, D_MODEL), const),
            pl.BlockSpec((P_T_ROWS, 1), const),
            pl.BlockSpec((CONV_W, D_CONV), const),
        ],
        out_specs=[
            pl.BlockSpec((None, D_ATTN, tm), col),
            pl.BlockSpec((None, tm, KV_W), row),
            pl.BlockSpec((None, tm, KV_W), row),
            pl.BlockSpec((None, KV_W, tm), col),
            pl.BlockSpec((None, KV_W, tm), col),
            pl.BlockSpec((None, KV_W, tm), col),
            pl.BlockSpec((None, GATE_ROWS, tm), col),
            pl.BlockSpec((None, tm, D_CONV), row),
            pl.BlockSpec((None, SUBLANES, D_CONV), lambda a, i: (a, 0, 0)),
        ],
        out_shape=[out_t(D_ATTN, BF16), out(KV_W, BF16), out(KV_W, BF16), out_t(KV_W), out_t(KV_W),
                   out_t(KV_W), out_t(GATE_ROWS), out(D_CONV),
                   jax.ShapeDtypeStruct((n, SUBLANES, D_CONV), F32)],
        scratch_shapes=[pltpu.VMEM((tm + SUBLANES, D_CONV), F32)],
        compiler_params=_cparams(("arbitrary", "arbitrary")),
        name="proj_prompt",
    )(h, w_rows, b_rows, w_t, b_t, w_conv)


def _proj_sample_body(h_ref, w_ref, b_ref, wgt_ref, bgt_ref, q_ref, kv_ref, gates_ref, cb_ref, u_ref):
    hb = h_ref[...].astype(BF16)
    q_ref[...] = (_proj_piece(hb, w_ref, b_ref, _C_Q, D_ATTN) * SCALE).astype(BF16)
    kv_ref[...] = _proj_piece(hb, w_ref, b_ref, _C_KVC, 3 * KV_W)
    gates_ref[...] = jax.nn.sigmoid(_dot(hb, wgt_ref[...]) + bgt_ref[...])
    cb_ref[...] = _proj_piece(hb, w_ref, b_ref, _C_CB, D_CONV)
    u_ref[...] = _proj_piece(hb, w_ref, b_ref, _C_CC, D_CONV) * _proj_piece(hb, w_ref, b_ref, _C_CH, D_CONV)


def _proj_sample(h2d, w_main, b_main, w_gate, b_gate):
    rows = h2d.shape[0]

    def out(width, dtype=F32):
        return jax.ShapeDtypeStruct((rows, width), dtype)

    return pl.pallas_call(
        _proj_sample_body,
        out_shape=[out(D_ATTN, BF16), out(3 * KV_W), out(LANES), out(D_CONV), out(D_CONV)],
        compiler_params=pltpu.CompilerParams(vmem_limit_bytes=VMEM_LIMIT_BYTES),
        name="proj_sample",
    )(h2d, w_main, b_main, w_gate, b_gate)


def _conv_sample_body(cb_ref, u_ref, prev_ref, wc_ref, y_ref):
    s = u_ref.shape[1]
    ext = [prev_ref[:, 0, :], prev_ref[:, 1, :]] + [u_ref[:, t, :] for t in range(s)]
    for t in range(s):
        z = ext[t] * wc_ref[0:1, :] + ext[t + 1] * wc_ref[1:2, :] + ext[t + 2] * wc_ref[2:3, :]
        y_ref[:, t, :] = cb_ref[:, t, :] * z


def _conv_sample(cb, u, prev, w_conv):
    return pl.pallas_call(
        _conv_sample_body,
        out_shape=jax.ShapeDtypeStruct(u.shape, F32),
        name="conv_sample",
    )(cb, u, prev, w_conv)


BPP = PAGE_SIZE // BLK
D_PAIRS = HEAD_DIM // 2


def _compress_body(pt_ref, src_ref, pe_ref, w1_ref, w2_ref, kc_ref, buf, sem, *, n_pages, pages_per_seq):
    n = pl.program_id(0)
    slot = n % 2

    def page_copy(step, j, sl):
        dst = buf.at[sl, :, j, :]
        if pages_per_seq is None:
            src = src_ref.at[pt_ref[step * n_pages + j]]
        else:
            seq = step * (n_pages // pages_per_seq) + j // pages_per_seq
            src = src_ref.at[seq, :, pl.ds((j % pages_per_seq) * PAGE_SIZE, PAGE_SIZE)]
        return pltpu.make_async_copy(src, dst, sem.at[sl])

    def start(step, sl):
        for j in range(n_pages):
            page_copy(step, j, sl).start()

    @pl.when(n == 0)
    def _():
        start(n, slot)

    @pl.when(n + 1 < pl.num_programs(0))
    def _():
        start(n + 1, 1 - slot)

    for j in range(n_pages):
        page_copy(n, j, slot).wait()

    def feature_rows(f):
        return buf[slot, f]

    for c in range(2):
        acc = None
        for dp in range(D_PAIRS):
            rows = []
            for g in range(N_KV):
                f0 = (c * N_KV + g) * HEAD_DIM + 2 * dp
                rows.append(jnp.concatenate(
                    [feature_rows(f0) + pe_ref[c, 2 * dp:2 * dp + 1, :],
                     feature_rows(f0 + 1) + pe_ref[c, 2 * dp + 1:2 * dp + 2, :]], axis=-1))
            x = jnp.concatenate(rows, axis=0).astype(BF16)
            d = _dot(x, w1_ref[c, dp])
            acc = d if acc is None else acc + d
        hid = (acc * jax.nn.sigmoid(acc)).astype(BF16)
        kc_ref[c] = _dot(hid, w2_ref[c])


def _compress(pt_flat, src, pe_t, w1_t, w2_t, n_steps, n_pages, pages_per_seq, name):
    const3 = lambda a, pt: (0, 0, 0)
    grid_spec = pltpu.PrefetchScalarGridSpec(
        num_scalar_prefetch=1,
        grid=(n_steps,),
        in_specs=[
            pl.BlockSpec(memory_space=pl.ANY),
            pl.BlockSpec((2, HEAD_DIM, PAGE_SIZE), const3),
            pl.BlockSpec((2, D_PAIRS, 2 * PAGE_SIZE, PAGE_SIZE), lambda a, pt: (0, 0, 0, 0)),
            pl.BlockSpec((2, PAGE_SIZE, PAGE_SIZE), const3),
        ],
        out_specs=pl.BlockSpec((None, 2, N_KV * n_pages, PAGE_SIZE), lambda a, pt: (a, 0, 0, 0)),
        scratch_shapes=[pltpu.VMEM((2, KV_W, n_pages, PAGE_SIZE), F32),
                        pltpu.SemaphoreType.DMA((2,))],
    )
    return pl.pallas_call(
        functools.partial(_compress_body, n_pages=n_pages, pages_per_seq=pages_per_seq),
        grid_spec=grid_spec,
        out_shape=jax.ShapeDtypeStruct((n_steps, 2, N_KV * n_pages, PAGE_SIZE), F32),
        compiler_params=_cparams(("arbitrary",)),
        name=name,
    )(pt_flat, src, pe_t, w1_t, w2_t)


ATT_TQ = 256
ATT_TK = ATT_TQ
WIN_TILES = WINDOW // ATT_TK + 1
AUX_BLOCKS = 32
LOG2E = 1.4426950408889634


V_ROWS = HEAD_DIM + BF16_SUBLANES


def _attn_update(s_t, v_t, state):
    m, acc = state
    m_new = jnp.maximum(m, jnp.max(s_t, axis=0, keepdims=True))
    e = jnp.exp2(s_t - m_new)
    acc = jnp.exp2(m - m_new) * acc + _dot(v_t, e.astype(BF16))
    return m_new, acc


def _attn_prompt_body(qt_ref, ks_ref, kw_ref, vs_ref, vw_ref, kc_ref, vct_ref, gates_ref, wb_ref, o_ref):
    tq, tk = ATT_TQ, ATT_TK
    qi = pl.program_id(1)
    nb = kc_ref.shape[1]
    t0 = qi * tq
    kd = qi
    tpos = t0 + lax.broadcasted_iota(jnp.int32, (1, tq), 1)
    jrow = lax.broadcasted_iota(jnp.int32, (nb, 1), 0)
    vis = ((jrow + 1) * BLK - 1) <= tpos
    cur = tpos >> BLK_SHIFT

    def heads_of(x):
        return jnp.concatenate([x] * GQA_R, axis=1)

    def group_queries(g, block_bias=None, switch=None):
        zeros = jnp.zeros((AUX_BLOCKS, tq), BF16)
        top = zeros if block_bias is None else block_bias
        low = zeros if switch is None else jnp.broadcast_to(switch, (AUX_BLOCKS, tq)).astype(BF16)
        aux = jnp.concatenate([top, low], axis=0)
        pieces = []
        for r in range(GQA_R):
            h = g * GQA_R + r
            qh = qt_ref[h * HEAD_DIM:(h + 1) * HEAD_DIM, :]
            pieces.append(jnp.concatenate([qh, aux] if g == 0 else [aux, qh], axis=0))
        return jnp.concatenate(pieces, axis=1)

    q4 = [group_queries(g) for g in range(N_KV)]
    vis4 = heads_of(vis)

    o_cmp, q4_sel = [], []
    for g in range(N_KV):
        s = jnp.where(vis4, _dot(kc_ref[g], q4[g]), NEG)
        m = jnp.max(s, axis=0, keepdims=True)
        e = jnp.where(vis4, jnp.exp2(s - m), 0.0)
        l = jnp.sum(e, axis=0, keepdims=True)
        p = e / jnp.where(l > 0.0, l, 1.0)
        imp = sum(p[:, r * tq:(r + 1) * tq] for r in range(GQA_R))
        o_cmp.append(_dot(vct_ref[g], p.astype(BF16)))

        score = jnp.where(jrow > cur, NEG, imp)
        score = jnp.where((jrow == 0) | (jrow == cur) | (jrow == cur - 1), FORCE, score)
        rank = jnp.zeros((nb, tq), F32)
        for j2 in range(nb):
            row = score[j2:j2 + 1, :]
            tie = jnp.where(jrow > j2, 1.0, 0.0)
            rank = rank + jnp.where(row > score, 1.0, jnp.where(row == score, tie, 0.0))
        q4_sel.append(group_queries(g, jnp.where(rank < N_SEL, 0.0, NEG).astype(BF16)))

    def key_tile(k_ref, kt, g):
        return k_ref[pl.ds(pl.multiple_of(kt * tk, tk), tk), g * LANES:(g + 1) * LANES]

    ones_rows = jnp.ones((V_ROWS - HEAD_DIM, tk), BF16)

    def val_tile(v_ref, kt, g):
        v = v_ref[g * HEAD_DIM:(g + 1) * HEAD_DIM, pl.ds(pl.multiple_of(kt * tk, tk), tk)]
        return jnp.concatenate([v.astype(BF16), ones_rows], axis=0)

    wide = GQA_R * tq
    init = (jnp.full((1, wide), NEG, F32), jnp.zeros((V_ROWS, wide), F32))

    def win_scores(back, g):
        kt = jnp.maximum(kd - back, 0)
        qw = q4[g] if back == 0 else group_queries(g, None, jnp.where(kd >= back, 0.0, NEG))
        s = _dot(key_tile(kw_ref, kt, g), qw)
        if back in (0, WIN_TILES - 1):
            s = s + heads_of(wb_ref[min(back, 1)])
        return s

    groups = range(N_KV)
    s_diag = [_dot(key_tile(ks_ref, kd, g), q4_sel[g]) + heads_of(wb_ref[0]) for g in groups]
    s_win = [[win_scores(back, g) for g in groups] for back in range(WIN_TILES)]
    sel = tuple(_attn_update(s_diag[g], val_tile(vs_ref, kd, g), init) for g in groups)
    win = [init] * N_KV
    for back in range(WIN_TILES):
        kt = jnp.maximum(kd - back, 0)
        win = [_attn_update(s_win[back][g], val_tile(vw_ref, kt, g), win[g]) for g in groups]

    def sel_tiles(kts, states):
        s = [[_dot(key_tile(ks_ref, kt, g), q4_sel[g]) for g in groups] for kt in kts]
        for i, kt in enumerate(kts):
            states = tuple(_attn_update(s[i][g], val_tile(vs_ref, kt, g), states[g]) for g in groups)
        return states

    sel = lax.cond((kd & 1) == 1, lambda st: sel_tiles((kd - 1,), st), lambda st: st, sel)
    sel = lax.fori_loop(0, kd >> 1, lambda j, st: sel_tiles((2 * j, 2 * j + 1), st), sel)

    gates = gates_ref[...]
    outs = []
    for h in range(N_HEADS):
        g, r = h // GQA_R, h % GQA_R
        cols = slice(r * tq, (r + 1) * tq)
        o_sel = sel[g][1][:HEAD_DIM, cols] * (1.0 / sel[g][1][HEAD_DIM:HEAD_DIM + 1, cols])
        o_win = win[g][1][:HEAD_DIM, cols] * (1.0 / win[g][1][HEAD_DIM:HEAD_DIM + 1, cols])
        outs.append(gates[h:h + 1, :] * o_cmp[g][:, cols] + gates[N_HEADS + h:N_HEADS + h + 1, :] * o_sel
                    + gates[2 * N_HEADS + h:2 * N_HEADS + h + 1, :] * o_win)
    o_ref[...] = jnp.concatenate(outs, axis=0).T


def _attn_prompt(q_t, k_sel, k_win, kvs_t, kvw_t, kc_pad, vc_t, gates_t):
    n, _, t = q_t.shape
    nb = kc_pad.shape[2]
    row = lambda a, i: (a, i, 0)
    seq = lambda a, i: (a, 0, 0)
    seq4 = lambda a, i: (a, 0, 0, 0)
    v_rows = lambda a, i: (a, 1, 0)

    def win_bias(back):
        dist = back * ATT_TK + jnp.arange(ATT_TQ)[None, :] - jnp.arange(ATT_TK)[:, None]
        return jnp.where((dist >= 0) & (dist < WINDOW), 0.0, NEG).astype(F32)

    wb = jnp.stack([win_bias(0), win_bias(WIN_TILES - 1)])
    return pl.pallas_call(
        _attn_prompt_body,
        grid=(n, t // ATT_TQ),
        in_specs=[
            pl.BlockSpec((None, D_ATTN, ATT_TQ), lambda a, i: (a, 0, i)),
            pl.BlockSpec((None, t, KV_W), seq),
            pl.BlockSpec((None, t, KV_W), seq),
            pl.BlockSpec((None, N_KV * HEAD_DIM, t), v_rows),
            pl.BlockSpec((None, N_KV * HEAD_DIM, t), v_rows),
            pl.BlockSpec((None, N_KV, nb, LANES), seq4),
            pl.BlockSpec((None, N_KV, HEAD_DIM, nb), seq4),
            pl.BlockSpec((None, GATE_ROWS, ATT_TQ), lambda a, i: (a, 0, i)),
            pl.BlockSpec((2, ATT_TK, ATT_TQ), lambda a, i: (0, 0, 0)),
        ],
        out_specs=pl.BlockSpec((None, ATT_TQ, D_ATTN), row),
        out_shape=jax.ShapeDtypeStruct((n, t, D_ATTN), F32),
        compiler_params=_cparams(("arbitrary", "arbitrary")),
        name="attn_prompt",
    )(q_t, k_sel, k_win, kvs_t, kvw_t, kc_pad, vc_t, gates_t, wb)


DEC_SEQ_PER_STEP = 8


def _dec_cmp_body(q2_ref, kc_ref, pt_ref, ocmp_ref, idx_ref, page_ref, p_buf, imp_buf, *, past, s_new):
    assert BPP == 2
    ns = q2_ref.shape[0]
    qrows = s_new * GQA_R
    n_pages = kc_ref.shape[2] // N_KV
    nb = n_pages * BPP
    width = imp_buf.shape[1]
    imp_buf[...] = jnp.zeros(imp_buf.shape, F32)
    lane = lax.broadcasted_iota(jnp.int32, (1, width), 1)
    jc = jnp.where(lane < n_pages, BPP * lane, BPP * (lane - n_pages) + 1)
    jc = jnp.where(lane == nb, nb, jc)

    row2 = lax.broadcasted_iota(jnp.int32, (2 * qrows, 1), 0)
    half = jnp.where(row2 >= qrows, 1, 0)
    blk = BPP * lax.broadcasted_iota(jnp.int32, (1, n_pages), 1) + half
    tok = (row2 & (qrows - 1)) >> (GQA_R.bit_length() - 1)
    vis = ((blk + 1) * BLK - 1) <= (past + tok)

    def both(x):
        return jnp.concatenate([x, x], axis=0)

    def pages(a, c, g):
        return kc_ref[a, c, g * n_pages:(g + 1) * n_pages, :].astype(BF16)

    pairs = [(a, g) for a in range(ns) for g in range(N_KV)]
    scores = [_dot_nt(q2_ref[a, g], pages(a, 0, g)) for a, g in pairs]
    for i, (a, g) in enumerate(pairs):
        s = jnp.where(vis, scores[i], NEG)
        m = jnp.max(s, axis=-1, keepdims=True)
        e = jnp.where(vis, jnp.exp(s - both(jnp.maximum(m[:qrows], m[qrows:]))), 0.0)
        l = jnp.sum(e, axis=-1, keepdims=True)
        l = l[:qrows] + l[qrows:]
        p = e / both(jnp.where(l > 0.0, l, 1.0))
        o2 = _dot(p.astype(BF16), pages(a, 1, g))
        ocmp_ref[a, g] = o2[:qrows, :HEAD_DIM] + o2[qrows:, HEAD_DIM:]
        p_buf[i] = p
        imp = sum(p_buf[i, pl.ds(r, 2 * s_new, stride=GQA_R), :] for r in range(GQA_R))
        row0 = i * s_new
        imp_buf[row0:row0 + s_new, 0:n_pages] = imp[:s_new]
        imp_buf[row0:row0 + s_new, n_pages:nb] = imp[s_new:]

    rows = ns * N_KV * s_new
    imp = imp_buf[...]
    t_row = lax.broadcasted_iota(jnp.int32, (rows, 1), 0) & (s_new - 1)
    cur = (past + t_row) >> BLK_SHIFT
    score = jnp.where(jc > cur, NEG, imp)
    score = jnp.where((jc == 0) | (jc == cur) | (jc == cur - 1), FORCE, score)
    score = jnp.where(lane <= nb, score, -jnp.inf)
    jcf = jc.astype(F32)
    out_lane = lax.broadcasted_iota(jnp.int32, (1, LANES), 1)
    idx = jnp.zeros((rows, LANES), jnp.int32)
    for k in range(N_SEL):
        m = jnp.max(score, axis=-1, keepdims=True)
        first = jnp.min(jnp.where(score == m, jcf, 1e9), axis=-1, keepdims=True)
        idx = jnp.where(out_lane == k, first.astype(jnp.int32), idx)
        score = jnp.where(jcf == first, -jnp.inf, score)
    idx_ref[...] = idx
    per_seq = N_KV * s_new
    pt_rows = jnp.concatenate([jnp.broadcast_to(pt_ref[a:a + 1, :], (per_seq, n_pages)) for a in range(ns)],
                              axis=0)
    page_ref[...] = jnp.take_along_axis(pt_rows, jnp.minimum(idx, nb - 1) >> (BPP.bit_length() - 1), axis=1)


def _dec_cmp(q2, kc, page_table, past, s_new):
    nseq = q2.shape[0]
    qrows = s_new * GQA_R
    rows_kc = kc.shape[2]
    n_pages = rows_kc // N_KV
    assert n_pages == LANES
    nb = n_pages * BPP
    ns = DEC_SEQ_PER_STEP
    width = (nb + 1 + LANES - 1) // LANES * LANES
    seq4 = lambda i: (i, 0, 0, 0)
    rows2 = lambda i: (i, 0)
    idx_shape = jax.ShapeDtypeStruct((nseq * N_KV * s_new, LANES), jnp.int32)
    return pl.pallas_call(
        functools.partial(_dec_cmp_body, past=past, s_new=s_new),
        grid=(nseq // ns,),
        in_specs=[
            pl.BlockSpec((ns, N_KV, 2 * qrows, 2 * HEAD_DIM), seq4),
            pl.BlockSpec((ns, 2, rows_kc, PAGE_SIZE), seq4),
            pl.BlockSpec((ns, n_pages), rows2),
        ],
        out_specs=[
            pl.BlockSpec((ns, N_KV, qrows, HEAD_DIM), seq4),
            pl.BlockSpec((ns * N_KV * s_new, LANES), rows2),
            pl.BlockSpec((ns * N_KV * s_new, LANES), rows2),
        ],
        out_shape=[jax.ShapeDtypeStruct((nseq, N_KV, qrows, HEAD_DIM), F32), idx_shape, idx_shape],
        scratch_shapes=[pltpu.VMEM((ns * N_KV, 2 * qrows, n_pages), F32),
                        pltpu.VMEM((ns * N_KV * s_new, width), F32)],
        compiler_params=_cparams(("arbitrary",)),
        name="dec_cmp_select",
    )(q2, kc, page_table)


def _dec_sel_win_body(page_ref, q_ref, idx2_ref, expand_ref, pool_ref, newblk_ref, win_ref,
                      kvw_new_ref, osel_ref, owin_ref, winout_ref, kvbuf, sem, *, past, s_new):
    n_past_blk = past // BLK
    n = pl.program_id(0)
    nseq = pl.num_programs(0)
    slot = n % 2
    rows = s_new * GQA_R
    seg_lanes = N_SEL * PAGE_SIZE
    nkeys = s_new * seg_lanes

    def block_copy(seq, c, k, sl):
        g, t = c // s_new, c % s_new
        page = page_ref[(seq * N_KV * s_new + c) * N_SEL + k]
        dst = kvbuf.at[sl, :, g, :, pl.ds((t * N_SEL + k) * PAGE_SIZE, PAGE_SIZE)]
        return pltpu.make_async_copy(pool_ref.at[page, :, g], dst, sem.at[sl])

    def start(seq, sl):
        for c in range(N_KV * s_new):
            for k in range(N_SEL):
                block_copy(seq, c, k, sl).start()

    @pl.when(n == 0)
    def _():
        start(n, slot)

    @pl.when(n + 1 < nseq)
    def _():
        start(n + 1, 1 - slot)

    for c in range(N_KV * s_new):
        for k in range(N_SEL):
            block_copy(n, c, k, slot).wait()

    assert win_ref.shape[1] == min(WINDOW, past + s_new)
    slid = jnp.concatenate([win_ref[...], kvw_new_ref[...]], axis=1)
    winout_ref[...] = slid[:, s_new:s_new + win_ref.shape[1]]

    t_row = lax.broadcasted_iota(jnp.int32, (rows, 1), 0) >> (GQA_R.bit_length() - 1)
    qpos = past + t_row
    lane = lax.broadcasted_iota(jnp.int32, (1, nkeys), 1)
    own = (lane >> (seg_lanes.bit_length() - 1)) == t_row
    row_in_page = lane & (PAGE_SIZE - 1)
    nlane = lax.broadcasted_iota(jnp.int32, (1, PAGE_SIZE), 1)
    nwin = win_ref.shape[1]
    wpos = past - nwin + lax.broadcasted_iota(jnp.int32, (1, nwin), 1)
    groups = range(N_KV)

    def krows(g):
        return pl.ds(g * HEAD_DIM, HEAD_DIM)

    def vrows(g):
        return pl.ds((N_KV + g) * HEAD_DIM, HEAD_DIM)

    q = [q_ref[g] for g in groups]
    jv2 = _dot(idx2_ref[...], expand_ref[...])
    s_sel = [_dot(q[g], kvbuf[slot, 0, g].astype(BF16)) for g in groups]
    s_new_sel = [_dot(q[g], newblk_ref[krows(g), :].astype(BF16)) for g in groups]
    s_win = [_dot(q[g], win_ref[krows(g), :].astype(BF16)) for g in groups]
    s_new_win = [_dot(q[g], kvw_new_ref[krows(g), :].astype(BF16)) for g in groups]

    def two_piece_softmax(sa, va, sb, vb):
        m = jnp.maximum(jnp.max(sa, axis=-1, keepdims=True), jnp.max(sb, axis=-1, keepdims=True))
        ea, eb = jnp.exp(sa - m), jnp.exp(sb - m)
        l = jnp.sum(ea, axis=-1, keepdims=True) + jnp.sum(eb, axis=-1, keepdims=True)
        return (_dot_nt(ea.astype(BF16), va) + _dot_nt(eb.astype(BF16), vb)) / l

    for g in groups:
        jv = jv2[g:g + 1, :].astype(jnp.int32)
        kpos = jv * BLK + (row_in_page & (BLK - 1))
        valid = (own & (jv < n_past_blk) & ((row_in_page >> BLK_SHIFT) == (jv & (BPP - 1)))
                 & (kpos <= qpos))
        has_new = jnp.max(jnp.where(own & (jv >= n_past_blk), 1.0, 0.0), axis=-1, keepdims=True)
        valid_n = (has_new > 0.5) & (nlane < BLK) & (n_past_blk * BLK + nlane <= qpos)
        osel_ref[g] = two_piece_softmax(
            jnp.where(valid, s_sel[g], NEG), kvbuf[slot, 1, g].astype(BF16),
            jnp.where(valid_n, s_new_sel[g], NEG), newblk_ref[vrows(g), :].astype(BF16))

        dw = qpos - wpos
        dn = qpos - (past + nlane)
        valid_w = (dw >= 0) & (dw < WINDOW) & (wpos >= 0)
        valid_n = (dn >= 0) & (dn < WINDOW) & (nlane < s_new)
        owin_ref[g] = two_piece_softmax(
            jnp.where(valid_w, s_win[g], NEG), win_ref[vrows(g), :].astype(BF16),
            jnp.where(valid_n, s_new_win[g], NEG), kvw_new_ref[vrows(g), :].astype(BF16))


def _dec_sel_win(page_flat, q_g, idx2, pool_t, newblk_t, win_t, kvw_new_t, past, s_new):
    nseq = q_g.shape[0]
    rows = s_new * GQA_R
    nwin = win_t.shape[2]
    nkeys = s_new * N_SEL * PAGE_SIZE
    n_phys = pool_t.shape[0]
    expand = (jnp.arange(nkeys)[None, :] // PAGE_SIZE == jnp.arange(s_new * N_SEL)[:, None]).astype(BF16)
    per_seq = lambda a, pg: (a, 0, 0, 0)
    per_seq3 = lambda a, pg: (a, 0, 0)
    grid_spec = pltpu.PrefetchScalarGridSpec(
        num_scalar_prefetch=1,
        grid=(nseq,),
        in_specs=[
            pl.BlockSpec((None, N_KV, rows, HEAD_DIM), per_seq),
            pl.BlockSpec((None, rows, s_new * N_SEL), per_seq3),
            pl.BlockSpec((s_new * N_SEL, nkeys), lambda a, pg: (0, 0)),
            pl.BlockSpec(memory_space=pl.ANY),
            pl.BlockSpec((None, KV_W, PAGE_SIZE), per_seq3),
            pl.BlockSpec((None, KV_W, nwin), per_seq3),
            pl.BlockSpec((None, KV_W, PAGE_SIZE), per_seq3),
        ],
        out_specs=[
            pl.BlockSpec((None, N_KV, rows, HEAD_DIM), per_seq),
            pl.BlockSpec((None, N_KV, rows, HEAD_DIM), per_seq),
            pl.BlockSpec((None, KV_W, nwin), per_seq3),
        ],
        scratch_shapes=[pltpu.VMEM((2, 2, N_KV, HEAD_DIM, nkeys), F32),
                        pltpu.SemaphoreType.DMA((2,))],
    )
    shape = jax.ShapeDtypeStruct((nseq, N_KV, rows, HEAD_DIM), F32)
    return pl.pallas_call(
        functools.partial(_dec_sel_win_body, past=past, s_new=s_new),
        grid_spec=grid_spec,
        out_shape=[shape, shape, jax.ShapeDtypeStruct(win_t.shape, F32)],
        compiler_params=_cparams(("arbitrary",)),
        name="dec_sel_win",
    )(page_flat, q_g, idx2, expand, pool_t.reshape(n_phys, 2, N_KV, HEAD_DIM, PAGE_SIZE),
      newblk_t, win_t, kvw_new_t)


def _out_ln_body(*refs, gated):
    if gated:
        ocmp_ref, osel_ref, owin_ref, gates_ref, yconv_ref, h_ref, wo_ref, g_ref, b_ref, y_ref = refs
        gates = gates_ref[...]
        pieces = []
        for h in range(N_HEADS):
            col = slice(h * HEAD_DIM, (h + 1) * HEAD_DIM)
            pieces.append(gates[:, h:h + 1] * ocmp_ref[:, col]
                          + gates[:, N_HEADS + h:N_HEADS + h + 1] * osel_ref[:, col]
                          + gates[:, 2 * N_HEADS + h:2 * N_HEADS + h + 1] * owin_ref[:, col])
        o = jnp.concatenate(pieces, axis=-1)
    else:
        o_ref, yconv_ref, h_ref, wo_ref, g_ref, b_ref, y_ref = refs
        o = o_ref[...]
    mixed = jnp.concatenate([o, yconv_ref[...]], axis=-1).astype(BF16)
    y_ref[...] = _layer_norm(ALPHA * h_ref[...] + _dot(mixed, wo_ref[...]), g_ref[...], b_ref[...])


def _out_ln(branches, yconv, h2d, w_out, g, b, tm):
    rows = h2d.shape[0]
    gated = len(branches) > 1
    row = lambda i: (i, 0)
    const = lambda i: (0, 0)
    widths = [D_ATTN, D_ATTN, D_ATTN, LANES] if gated else [D_ATTN]
    return pl.pallas_call(
        functools.partial(_out_ln_body, gated=gated),
        grid=(rows // tm,),
        in_specs=[pl.BlockSpec((tm, w), row) for w in widths] + [
            pl.BlockSpec((tm, D_CONV), row),
            pl.BlockSpec((tm, D_MODEL), row),
            pl.BlockSpec((D_MODEL, D_MODEL), const),
            pl.BlockSpec((1, D_MODEL), const),
            pl.BlockSpec((1, D_MODEL), const),
        ],
        out_specs=pl.BlockSpec((tm, D_MODEL), row),
        out_shape=jax.ShapeDtypeStruct((rows, D_MODEL), F32),
        compiler_params=_cparams(("arbitrary",)),
        name="out_ln",
    )(*branches, yconv, h2d, w_out, g, b)


def _block_diag_groups(w):
    z = jnp.zeros_like(w)
    return jnp.concatenate([jnp.concatenate([w, z], axis=-1),
                            jnp.concatenate([z, w], axis=-1)], axis=-2).astype(BF16)


def _to_group_layout(x, nseq, s_new):
    x = x.reshape(nseq, s_new, N_KV, GQA_R, HEAD_DIM).transpose(0, 2, 1, 3, 4)
    return x.reshape(nseq, N_KV, s_new * GQA_R, HEAD_DIM)


def _from_group_layout(x, nseq, s_new):
    x = x.reshape(nseq, N_KV, s_new, GQA_R, HEAD_DIM).transpose(0, 2, 1, 3, 4)
    return x.reshape(nseq * s_new, D_ATTN)


def kernel(x_prompt, x_sample, cache_cmp_kv, cache_sel_kv, state_win_kv, state_conv, page_table,
           ffa_gate, ffa_up, ffa_down, ln1_g, ln1_b, w_in, b_in, w_conv, cmp_w1, cmp_w2, cmp_pe,
           w_out, ln2_g, ln2_b, ffb_gate, ffb_up, ffb_down, ln3_g, ln3_b):
    depth = ffa_gate.shape[0]
    assert depth == 1
    l = 0
    n, t, d = x_prompt.shape
    nseq, s_new, _ = x_sample.shape
    n_pages = page_table.shape[1]
    past = n_pages * PAGE_SIZE
    rows_p, rows_s = n * t, nseq * s_new
    kvshape = (2, N_KV, HEAD_DIM)
    assert d == D_MODEL and rows_p % FFN_ROW_TILE == 0 and t % PROJ_ROW_TILE == 0 and t % ATT_TQ == 0
    assert t // BLK <= AUX_BLOCKS and n % CMP_SEQS_PER_STEP == 0 and nseq % DEC_SEQ_PER_STEP == 0
    assert s_new * GQA_R == BF16_SUBLANES and s_new <= BLK and past % BLK == 0
    assert state_win_kv.shape[2] == WINDOW

    bf = lambda w: w.astype(BF16)
    vec = lambda v: v.reshape(1, -1)
    w_main = bf(w_in[l][:, :D_MAIN])
    b_main = vec(b_in[l][:D_MAIN])
    w_gate = bf(jnp.pad(w_in[l][:, D_MAIN:], ((0, 0), (0, LANES - N_GATE))))
    b_gate = vec(jnp.pad(b_in[l][D_MAIN:], (0, LANES - N_GATE)))
    pe_t = jnp.concatenate([cmp_pe[l].transpose(1, 2, 0)] * BPP, axis=-1)
    w1_t = _block_diag_groups(cmp_w1[l].transpose(0, 2, 1, 3)).reshape(
        2, D_PAIRS, 2 * PAGE_SIZE, PAGE_SIZE)
    w2_t = _block_diag_groups(cmp_w2[l])
    w_out_b = bf(w_out[l])
    ffa = (bf(ffa_gate[l]), bf(ffa_up[l]), bf(0.5 * ffa_down[l]), vec(ln1_g[l]), vec(ln1_b[l]))
    ffb = (bf(ffb_gate[l]), bf(ffb_up[l]), bf(0.5 * ffb_down[l]), vec(ln3_g[l]), vec(ln3_b[l]))

    def prompt_cols(x):
        return jnp.concatenate([x[..., :_C_Q], x[..., _C_KVS:_C_KVS + K_HALF],
                                x[..., _C_KVW:_C_KVW + K_HALF]], axis=-1)

    def prompt_rows(x):
        return jnp.pad(x[..., _C_Q:], [(0, 0)] * (x.ndim - 1) + [(0, GATE_ROWS - N_GATE)])

    w_rows, b_rows = bf(prompt_cols(w_in[l])), vec(prompt_cols(b_in[l]))
    w_tr, b_tr = bf(prompt_rows(w_in[l]).T), prompt_rows(b_in[l]).reshape(-1, 1)

    pps = t // PAGE_SIZE
    cmp_seqs = CMP_SEQS_PER_STEP
    hp = _ffn_ln(x_prompt.reshape(rows_p, d), *ffa, tm=FFN_ROW_TILE)
    q, k_sel, k_win, kvc_t, kvs_t, kvw_t, gates_t, yconv, cstate = _proj_prompt(
        hp.reshape(n, t, d), w_rows, b_rows, w_tr, b_tr, w_conv[l], tm=PROJ_ROW_TILE)
    kc = _compress(jnp.zeros((1,), jnp.int32), kvc_t, pe_t, w1_t, w2_t, n_steps=n // cmp_seqs,
                   n_pages=cmp_seqs * pps, pages_per_seq=pps, name="compress_prompt")
    kc = kc.reshape(n // cmp_seqs, 2, N_KV, cmp_seqs, pps, BPP, HEAD_DIM)
    kc = kc.transpose(0, 3, 2, 1, 4, 5, 6).reshape(n, N_KV, 2, pps * BPP, HEAD_DIM)
    zc = jnp.zeros_like(kc[:, 0, 0])
    kc_pad = bf(jnp.stack([jnp.concatenate([kc[:, 0, 0], zc], axis=-1),
                           jnp.concatenate([zc, kc[:, 1, 0]], axis=-1)], axis=1))
    vc_t = bf(jnp.swapaxes(kc[:, :, 1], -1, -2))
    o = _attn_prompt(q, k_sel, k_win, kvs_t, kvw_t, kc_pad, vc_t, gates_t)
    yp = _out_ln([o.reshape(rows_p, D_ATTN)], yconv.reshape(rows_p, D_CONV), hp, w_out_b,
                 vec(ln2_g[l]), vec(ln2_b[l]), tm=OUT_ROW_TILE)
    yp = _ffn_ln(yp, *ffb, tm=FFN_ROW_TILE).reshape(n, t, d)

    def from_feature_major(x_t):
        return jnp.moveaxis(x_t.reshape(x_t.shape[0], *kvshape, x_t.shape[-1]), -1, 1)[None]

    p_cmp = from_feature_major(kvc_t)
    p_sel = from_feature_major(kvs_t)
    p_win = from_feature_major(kvw_t[:, :, t - min(WINDOW, t):])
    p_conv = cstate[:, SUBLANES - (CONV_W - 1):].reshape(1, n, CONV_W - 1, D_CONV)

    hs = _ffn_ln(x_sample.reshape(rows_s, d), *ffa, tm=rows_s)
    qs, kv3, gates_s, cb_s, u_s = _proj_sample(hs, w_main, b_main, w_gate, b_gate)
    kvc_s, kvs_s, kvw_s = kv3[:, :KV_W], kv3[:, KV_W:2 * KV_W], kv3[:, 2 * KV_W:]
    u3 = u_s.reshape(nseq, s_new, D_CONV)
    yconv_s = _conv_sample(cb_s.reshape(nseq, s_new, D_CONV), u3, state_conv[l], w_conv[l])
    s_conv = jnp.concatenate([state_conv[l], u3], axis=1)[:, -(CONV_W - 1):]

    def feature_major(x):
        return jnp.moveaxis(x, -4, -1).reshape(*x.shape[:-4], KV_W, x.shape[-4])

    def new_rows_t(x2d):
        xt = x2d.reshape(nseq, s_new, KV_W).transpose(0, 2, 1)
        return jnp.pad(xt, ((0, 0), (0, 0), (0, PAGE_SIZE - s_new)))

    kc_s = _compress(page_table.reshape(-1), feature_major(cache_cmp_kv[l]), pe_t, w1_t, w2_t,
                     n_steps=nseq, n_pages=n_pages, pages_per_seq=None, name="compress_decode")
    q_g = _to_group_layout(qs, nseq, s_new)
    zq = jnp.zeros_like(q_g)
    q2 = jnp.concatenate([jnp.concatenate([q_g, zq], axis=-1), jnp.concatenate([zq, q_g], axis=-1)], axis=-2)
    ocmp_c, idx_pad, page_pad = _dec_cmp(q2, kc_s, page_table, past, s_new)
    idx = idx_pad[:, :N_SEL]
    idx2 = jnp.pad(bf(idx.reshape(nseq, N_KV, s_new * N_SEL)), ((0, 0), (0, s_new * GQA_R - N_KV), (0, 0)))
    win_t = feature_major(state_win_kv[l])
    kvw_new_t = new_rows_t(kvw_s)
    osel_c, owin_c, win_next_t = _dec_sel_win(
        page_pad[:, :N_SEL].reshape(-1), q_g, idx2, feature_major(cache_sel_kv[l]),
        new_rows_t(kvs_s), win_t, kvw_new_t, past, s_new)
    branches = [_from_group_layout(x, nseq, s_new) for x in (ocmp_c, osel_c, owin_c)]
    ys = _out_ln(branches + [gates_s], yconv_s.reshape(rows_s, D_CONV), hs, w_out_b,
                 vec(ln2_g[l]), vec(ln2_b[l]), tm=rows_s)
    ys = _ffn_ln(ys, *ffb, tm=rows_s).reshape(nseq, s_new, d)
    s_cmp = kvc_s.reshape(1, nseq, s_new, *kvshape)
    s_sel = kvs_s.reshape(1, nseq, s_new, *kvshape)
    s_win = from_feature_major(win_next_t)
    return (yp, ys, p_cmp, p_sel, p_win, p_conv, s_cmp, s_sel, s_win,
            s_conv.reshape(1, nseq, CONV_W - 1, D_CONV))
```

```python
import functools

import jax
import jax.numpy as jnp
from jax import lax
from jax.experimental import pallas as pl
from jax.experimental.pallas import tpu as pltpu

F32 = jnp.float32
BF16 = jnp.bfloat16

D_MODEL = 1024
D_CONV = 512
D_ATTN = 512
HEAD_DIM = 64
N_HEADS = 8
N_KV = 2
GQA_R = N_HEADS // N_KV
N_BR = 3
CONV_W = 3
BLK = 64
BLK_SHIFT = BLK.bit_length() - 1
N_SEL = 16
WINDOW = 512
PAGE_SIZE = 128
D_FF = 2816
KV_W = 2 * N_KV * HEAD_DIM
D_MAIN = 3 * D_CONV + D_ATTN + 3 * KV_W
N_GATE = N_BR * N_HEADS
ALPHA = 2.0 ** 0.25
SCALE = HEAD_DIM ** -0.5
LN_EPS = 1e-5
NEG = -1e30
FORCE = 1e4

LANES = 128
SUBLANES = 8
BF16_SUBLANES = 2 * SUBLANES
V7X_VMEM_BYTES = 64 * 1024 * 1024
VMEM_LIMIT_BYTES = V7X_VMEM_BYTES * 7 // 8

FFN_ROW_TILE = 1024
PROJ_ROW_TILE = 1024
OUT_ROW_TILE = 1024
CMP_SEQS_PER_STEP = 4


def _cparams(sem):
    return pltpu.CompilerParams(dimension_semantics=sem, vmem_limit_bytes=VMEM_LIMIT_BYTES)


def _layer_norm(y, g, b):
    mu = jnp.mean(y, axis=-1, keepdims=True)
    yc = y - mu
    var = jnp.mean(yc * yc, axis=-1, keepdims=True)
    return yc * lax.rsqrt(var + LN_EPS) * g + b


def _dot(a, b):
    return jnp.dot(a, b, preferred_element_type=F32)


def _dot_nt(a, b):
    return lax.dot_general(a, b, (((1,), (1,)), ((), ())), preferred_element_type=F32)


FFN_CHUNK = 256


def _ffn_ln_body(x_ref, wg_ref, wu_ref, wd_ref, g_ref, b_ref, o_ref):
    x = x_ref[...]
    xb = x.astype(BF16)
    acc = jnp.zeros(x.shape, F32)
    for c in range(D_FF // FFN_CHUNK):
        sl = slice(c * FFN_CHUNK, (c + 1) * FFN_CHUNK)
        gg = _dot(xb, wg_ref[:, sl])
        uu = _dot(xb, wu_ref[:, sl])
        hh = (gg * jax.nn.sigmoid(gg) * uu).astype(BF16)
        acc = acc + _dot(hh, wd_ref[sl, :])
    o_ref[...] = _layer_norm(ALPHA * x + acc, g_ref[...], b_ref[...])


def _ffn_ln(x2d, wg, wu, wd, g, b, tm):
    rows = x2d.shape[0]
    const = lambda i: (0, 0)
    return pl.pallas_call(
        _ffn_ln_body,
        grid=(rows // tm,),
        in_specs=[
            pl.BlockSpec((tm, D_MODEL), lambda i: (i, 0)),
            pl.BlockSpec((D_MODEL, D_FF), const, pipeline_mode=pl.Buffered(1)),
            pl.BlockSpec((D_MODEL, D_FF), const, pipeline_mode=pl.Buffered(1)),
            pl.BlockSpec((D_FF, D_MODEL), const, pipeline_mode=pl.Buffered(1)),
            pl.BlockSpec((1, D_MODEL), const),
            pl.BlockSpec((1, D_MODEL), const),
        ],
        out_specs=pl.BlockSpec((tm, D_MODEL), lambda i: (i, 0)),
        out_shape=jax.ShapeDtypeStruct((rows, D_MODEL), F32),
        compiler_params=_cparams(("arbitrary",)),
        name="ffn_ln",
    )(x2d, wg, wu, wd, g, b)


_C_CB, _C_CC, _C_CH, _C_Q = 0, D_CONV, 2 * D_CONV, 3 * D_CONV
_C_KVC = 3 * D_CONV + D_ATTN
_C_KVS = _C_KVC + KV_W
_C_KVW = _C_KVS + KV_W


def _proj_piece(hb, w_ref, b_ref, lo, width):
    return _dot(hb, w_ref[:, lo:lo + width]) + b_ref[:, lo:lo + width]


_P_K = 3 * D_CONV
K_HALF = N_KV * HEAD_DIM
P_ROW_COLS = _P_K + 2 * K_HALF
GATE_ROWS = 32
_T_KV = D_ATTN
_T_GATE = _T_KV + 3 * KV_W
P_T_ROWS = _T_GATE + GATE_ROWS


def _proj_prompt_body(h_ref, w_ref, b_ref, wt_ref, bt_ref, wc_ref,
                      qt_ref, ks_ref, kw_ref, kvct_ref, kvst_ref, kvwt_ref, gates_ref,
                      yconv_ref, cstate_ref, ubuf):
    i = pl.program_id(1)
    tm = h_ref.shape[0]
    hb = h_ref[...].astype(BF16)

    def piece_t(lo, height):
        return _dot_nt(wt_ref[lo:lo + height, :], hb) + bt_ref[lo:lo + height, :]

    qt_ref[...] = (piece_t(0, D_ATTN) * (SCALE * LOG2E)).astype(BF16)
    lane = lax.broadcasted_iota(jnp.int32, (1, KV_W), 1)
    blk = (i * tm + lax.broadcasted_iota(jnp.int32, (tm, 1), 0)) >> BLK_SHIFT
    aux0, aux1 = HEAD_DIM, 2 * HEAD_DIM
    one = (lane == aux0 + AUX_BLOCKS) | (lane == aux1 + AUX_BLOCKS)
    is_k = (lane < aux0) | (lane >= aux1 + HEAD_DIM)
    k4 = _proj_piece(hb, w_ref, b_ref, _P_K, 2 * K_HALF)
    for ref, lo, ones in ((ks_ref, 0, one | (lane == aux0 + blk) | (lane == aux1 + blk)),
                          (kw_ref, K_HALF, one)):
        k = k4[:, lo:lo + K_HALF]
        ref[...] = jnp.where(is_k, jnp.concatenate([k, k], axis=1),
                             jnp.where(ones, 1.0, 0.0)).astype(BF16)
    for k, ref in enumerate((kvct_ref, kvst_ref, kvwt_ref)):
        ref[...] = piece_t(_T_KV + k * KV_W, KV_W)
    gates_ref[...] = jax.nn.sigmoid(piece_t(_T_GATE, GATE_ROWS))
    cb = _proj_piece(hb, w_ref, b_ref, _C_CB, D_CONV)
    u = _proj_piece(hb, w_ref, b_ref, _C_CC, D_CONV) * _proj_piece(hb, w_ref, b_ref, _C_CH, D_CONV)

    head = SUBLANES

    @pl.when(i == 0)
    def _():
        ubuf[0:head, :] = jnp.zeros((head, D_CONV), F32)

    ubuf[head:head + tm, :] = u
    u1 = ubuf[head - 1:head - 1 + tm, :]
    u2 = ubuf[head - 2:head - 2 + tm, :]
    z = u2 * wc_ref[0:1, :] + u1 * wc_ref[1:2, :] + u * wc_ref[2:3, :]
    yconv_ref[...] = cb * z
    tail = u[tm - head:tm, :]
    ubuf[0:head, :] = tail
    cstate_ref[...] = tail


def _proj_prompt(h, w_rows, b_rows, w_t, b_t, w_conv, tm):
    n, t, _ = h.shape
    const = lambda a, i: (0, 0)
    row = lambda a, i: (a, i, 0)
    col = lambda a, i: (a, 0, i)

    def out(width, dtype=F32):
        return jax.ShapeDtypeStruct((n, t, width), dtype)

    def out_t(rows, dtype=F32):
        return jax.ShapeDtypeStruct((n, rows, t), dtype)

    return pl.pallas_call(
        _proj_prompt_body,
        grid=(n, t // tm),
        in_specs=[
            pl.BlockSpec((None, tm, D_MODEL), row),
            pl.BlockSpec((D_MODEL, P_ROW_COLS), const),
            pl.BlockSpec((1, P_ROW_COLS), const),
            pl.BlockSpec((P_T_ROWS, D_MODEL), const),
            pl.BlockSpec((P_T_ROWS, 1), const),
            pl.BlockSpec((CONV_W, D_CONV), const),
        ],
        out_specs=[
            pl.BlockSpec((None, D_ATTN, tm), col),
            pl.BlockSpec((None, tm, KV_W), row),
            pl.BlockSpec((None, tm, KV_W), row),
            pl.BlockSpec((None, KV_W, tm), col),
            pl.BlockSpec((None, KV_W, tm), col),
            pl.BlockSpec((None, KV_W, tm), col),
            pl.BlockSpec((None, GATE_ROWS, tm), col),
            pl.BlockSpec((None, tm, D_CONV), row),
            pl.BlockSpec((None, SUBLANES, D_CONV), lambda a, i: (a, 0, 0)),
        ],
        out_shape=[out_t(D_ATTN, BF16), out(KV_W, BF16), out(KV_W, BF16), out_t(KV_W), out_t(KV_W),
                   out_t(KV_W), out_t(GATE_ROWS), out(D_CONV),
                   jax.ShapeDtypeStruct((n, SUBLANES, D_CONV), F32)],
        scratch_shapes=[pltpu.VMEM((tm + SUBLANES, D_CONV), F32)],
        compiler_params=_cparams(("arbitrary", "arbitrary")),
        name="proj_prompt",
    )(h, w_rows, b_rows, w_t, b_t, w_conv)


def _proj_sample_body(h_ref, w_ref, b_ref, wgt_ref, bgt_ref, q_ref, kv_ref, gates_ref, cb_ref, u_ref):
    hb = h_ref[...].astype(BF16)
    q_ref[...] = (_proj_piece(hb, w_ref, b_ref, _C_Q, D_ATTN) * SCALE).astype(BF16)
    kv_ref[...] = _proj_piece(hb, w_ref, b_ref, _C_KVC, 3 * KV_W)
    gates_ref[...] = jax.nn.sigmoid(_dot(hb, wgt_ref[...]) + bgt_ref[...])
    cb_ref[...] = _proj_piece(hb, w_ref, b_ref, _C_CB, D_CONV)
    u_ref[...] = _proj_piece(hb, w_ref, b_ref, _C_CC, D_CONV) * _proj_piece(hb, w_ref, b_ref, _C_CH, D_CONV)


def _proj_sample(h2d, w_main, b_main, w_gate, b_gate):
    rows = h2d.shape[0]

    def out(width, dtype=F32):
        return jax.ShapeDtypeStruct((rows, width), dtype)

    return pl.pallas_call(
        _proj_sample_body,
        out_shape=[out(D_ATTN, BF16), out(3 * KV_W), out(LANES), out(D_CONV), out(D_CONV)],
        compiler_params=pltpu.CompilerParams(vmem_limit_bytes=VMEM_LIMIT_BYTES),
        name="proj_sample",
    )(h2d, w_main, b_main, w_gate, b_gate)


def _conv_sample_body(cb_ref, u_ref, prev_ref, wc_ref, y_ref):
    s = u_ref.shape[1]
    ext = [prev_ref[:, 0, :], prev_ref[:, 1, :]] + [u_ref[:, t, :] for t in range(s)]
    for t in range(s):
        z = ext[t] * wc_ref[0:1, :] + ext[t + 1] * wc_ref[1:2, :] + ext[t + 2] * wc_ref[2:3, :]
        y_ref[:, t, :] = cb_ref[:, t, :] * z


def _conv_sample(cb, u, prev, w_conv):
    return pl.pallas_call(
        _conv_sample_body,
        out_shape=jax.ShapeDtypeStruct(u.shape, F32),
        name="conv_sample",
    )(cb, u, prev, w_conv)


BPP = PAGE_SIZE // BLK
D_PAIRS = HEAD_DIM // 2


def _compress_body(pt_ref, src_ref, pe_ref, w1_ref, w2_ref, kc_ref, buf, sem, *, n_pages, pages_per_seq):
    n = pl.program_id(0)
    slot = n % 2

    def page_copy(step, j, sl):
        dst = buf.at[sl, :, j, :]
        if pages_per_seq is None:
            src = src_ref.at[pt_ref[step * n_pages + j]]
        else:
            seq = step * (n_pages // pages_per_seq) + j // pages_per_seq
            src = src_ref.at[seq, :, pl.ds((j % pages_per_seq) * PAGE_SIZE, PAGE_SIZE)]
        return pltpu.make_async_copy(src, dst, sem.at[sl])

    def start(step, sl):
        for j in range(n_pages):
            page_copy(step, j, sl).start()

    @pl.when(n == 0)
    def _():
        start(n, slot)

    @pl.when(n + 1 < pl.num_programs(0))
    def _():
        start(n + 1, 1 - slot)

    for j in range(n_pages):
        page_copy(n, j, slot).wait()

    def feature_rows(f):
        return buf[slot, f]

    for c in range(2):
        acc = None
        for dp in range(D_PAIRS):
            rows = []
            for g in range(N_KV):
                f0 = (c * N_KV + g) * HEAD_DIM + 2 * dp
                rows.append(jnp.concatenate(
                    [feature_rows(f0) + pe_ref[c, 2 * dp:2 * dp + 1, :],
                     feature_rows(f0 + 1) + pe_ref[c, 2 * dp + 1:2 * dp + 2, :]], axis=-1))
            x = jnp.concatenate(rows, axis=0).astype(BF16)
            d = _dot(x, w1_ref[c, dp])
            acc = d if acc is None else acc + d
        hid = (acc * jax.nn.sigmoid(acc)).astype(BF16)
        kc_ref[c] = _dot(hid, w2_ref[c])


def _compress(pt_flat, src, pe_t, w1_t, w2_t, n_steps, n_pages, pages_per_seq, name):
    const3 = lambda a, pt: (0, 0, 0)
    grid_spec = pltpu.PrefetchScalarGridSpec(
        num_scalar_prefetch=1,
        grid=(n_steps,),
        in_specs=[
            pl.BlockSpec(memory_space=pl.ANY),
            pl.BlockSpec((2, HEAD_DIM, PAGE_SIZE), const3),
            pl.BlockSpec((2, D_PAIRS, 2 * PAGE_SIZE, PAGE_SIZE), lambda a, pt: (0, 0, 0, 0)),
            pl.BlockSpec((2, PAGE_SIZE, PAGE_SIZE), const3),
        ],
        out_specs=pl.BlockSpec((None, 2, N_KV * n_pages, PAGE_SIZE), lambda a, pt: (a, 0, 0, 0)),
        scratch_shapes=[pltpu.VMEM((2, KV_W, n_pages, PAGE_SIZE), F32),
                        pltpu.SemaphoreType.DMA((2,))],
    )
    return pl.pallas_call(
        functools.partial(_compress_body, n_pages=n_pages, pages_per_seq=pages_per_seq),
        grid_spec=grid_spec,
        out_shape=jax.ShapeDtypeStruct((n_steps, 2, N_KV * n_pages, PAGE_SIZE), F32),
        compiler_params=_cparams(("arbitrary",)),
        name=name,
    )(pt_flat, src, pe_t, w1_t, w2_t)


ATT_TQ = 256
ATT_TK = ATT_TQ
WIN_TILES = WINDOW // ATT_TK + 1
AUX_BLOCKS = 32
LOG2E = 1.4426950408889634


V_ROWS = HEAD_DIM + BF16_SUBLANES


def _attn_update(s_t, v_t, state):
    m, acc = state
    m_new = jnp.maximum(m, jnp.max(s_t, axis=0, keepdims=True))
    e = jnp.exp2(s_t - m_new)
    acc = jnp.exp2(m - m_new) * acc + _dot(v_t, e.astype(BF16))
    return m_new, acc


def _attn_prompt_body(qt_ref, ks_ref, kw_ref, vs_ref, vw_ref, kc_ref, vct_ref, gates_ref, wb_ref, o_ref):
    tq, tk = ATT_TQ, ATT_TK
    qi = pl.program_id(1)
    nb = kc_ref.shape[1]
    t0 = qi * tq
    kd = qi
    tpos = t0 + lax.broadcasted_iota(jnp.int32, (1, tq), 1)
    jrow = lax.broadcasted_iota(jnp.int32, (nb, 1), 0)
    vis = ((jrow + 1) * BLK - 1) <= tpos
    cur = tpos >> BLK_SHIFT

    def heads_of(x):
        return jnp.concatenate([x] * GQA_R, axis=1)

    def group_queries(g, block_bias=None, switch=None):
        zeros = jnp.zeros((AUX_BLOCKS, tq), BF16)
        top = zeros if block_bias is None else block_bias
        low = zeros if switch is None else jnp.broadcast_to(switch, (AUX_BLOCKS, tq)).astype(BF16)
        aux = jnp.concatenate([top, low], axis=0)
        pieces = []
        for r in range(GQA_R):
            h = g * GQA_R + r
            qh = qt_ref[h * HEAD_DIM:(h + 1) * HEAD_DIM, :]
            pieces.append(jnp.concatenate([qh, aux] if g == 0 else [aux, qh], axis=0))
        return jnp.concatenate(pieces, axis=1)

    q4 = [group_queries(g) for g in range(N_KV)]
    vis4 = heads_of(vis)

    o_cmp, q4_sel = [], []
    for g in range(N_KV):
        s = jnp.where(vis4, _dot(kc_ref[g], q4[g]), NEG)
        m = jnp.max(s, axis=0, keepdims=True)
        e = jnp.where(vis4, jnp.exp2(s - m), 0.0)
        l = jnp.sum(e, axis=0, keepdims=True)
        p = e / jnp.where(l > 0.0, l, 1.0)
        imp = sum(p[:, r * tq:(r + 1) * tq] for r in range(GQA_R))
        o_cmp.append(_dot(vct_ref[g], p.astype(BF16)))

        score = jnp.where(jrow > cur, NEG, imp)
        score = jnp.where((jrow == 0) | (jrow == cur) | (jrow == cur - 1), FORCE, score)
        rank = jnp.zeros((nb, tq), F32)
        for j2 in range(nb):
            row = score[j2:j2 + 1, :]
            tie = jnp.where(jrow > j2, 1.0, 0.0)
            rank = rank + jnp.where(row > score, 1.0, jnp.where(row == score, tie, 0.0))
        q4_sel.append(group_queries(g, jnp.where(rank < N_SEL, 0.0, NEG).astype(BF16)))

    def key_tile(k_ref, kt, g):
        return k_ref[pl.ds(pl.multiple_of(kt * tk, tk), tk), g * LANES:(g + 1) * LANES]

    ones_rows = jnp.ones((V_ROWS - HEAD_DIM, tk), BF16)

    def val_tile(v_ref, kt, g):
        v = v_ref[g * HEAD_DIM:(g + 1) * HEAD_DIM, pl.ds(pl.multiple_of(kt * tk, tk), tk)]
        return jnp.concatenate([v.astype(BF16), ones_rows], axis=0)

    wide = GQA_R * tq
    init = (jnp.full((1, wide), NEG, F32), jnp.zeros((V_ROWS, wide), F32))

    def win_scores(back, g):
        kt = jnp.maximum(kd - back, 0)
        qw = q4[g] if back == 0 else group_queries(g, None, jnp.where(kd >= back, 0.0, NEG))
        s = _dot(key_tile(kw_ref, kt, g), qw)
        if back in (0, WIN_TILES - 1):
            s = s + heads_of(wb_ref[min(back, 1)])
        return s

    groups = range(N_KV)
    s_diag = [_dot(key_tile(ks_ref, kd, g), q4_sel[g]) + heads_of(wb_ref[0]) for g in groups]
    s_win = [[win_scores(back, g) for g in groups] for back in range(WIN_TILES)]
    sel = tuple(_attn_update(s_diag[g], val_tile(vs_ref, kd, g), init) for g in groups)
    win = [init] * N_KV
    for back in range(WIN_TILES):
        kt = jnp.maximum(kd - back, 0)
        win = [_attn_update(s_win[back][g], val_tile(vw_ref, kt, g), win[g]) for g in groups]

    def sel_tiles(kts, states):
        s = [[_dot(key_tile(ks_ref, kt, g), q4_sel[g]) for g in groups] for kt in kts]
        for i, kt in enumerate(kts):
            states = tuple(_attn_update(s[i][g], val_tile(vs_ref, kt, g), states[g]) for g in groups)
        return states

    sel = lax.cond((kd & 1) == 1, lambda st: sel_tiles((kd - 1,), st), lambda st: st, sel)
    sel = lax.fori_loop(0, kd >> 1, lambda j, st: sel_tiles((2 * j, 2 * j + 1), st), sel)

    gates = gates_ref[...]
    outs = []
    for h in range(N_HEADS):
        g, r = h // GQA_R, h % GQA_R
        cols = slice(r * tq, (r + 1) * tq)
        o_sel = sel[g][1][:HEAD_DIM, cols] * (1.0 / sel[g][1][HEAD_DIM:HEAD_DIM + 1, cols])
        o_win = win[g][1][:HEAD_DIM, cols] * (1.0 / win[g][1][HEAD_DIM:HEAD_DIM + 1, cols])
        outs.append(gates[h:h + 1, :] * o_cmp[g][:, cols] + gates[N_HEADS + h:N_HEADS + h + 1, :] * o_sel
                    + gates[2 * N_HEADS + h:2 * N_HEADS + h + 1, :] * o_win)
    o_ref[...] = jnp.concatenate(outs, axis=0).T


def _attn_prompt(q_t, k_sel, k_win, kvs_t, kvw_t, kc_pad, vc_t, gates_t):
    n, _, t = q_t.shape
    nb = kc_pad.shape[2]
    row = lambda a, i: (a, i, 0)
    seq = lambda a, i: (a, 0, 0)
    seq4 = lambda a, i: (a, 0, 0, 0)
    v_rows = lambda a, i: (a, 1, 0)

    def win_bias(back):
        dist = back * ATT_TK + jnp.arange(ATT_TQ)[None, :] - jnp.arange(ATT_TK)[:, None]
        return jnp.where((dist >= 0) & (dist < WINDOW), 0.0, NEG).astype(F32)

    wb = jnp.stack([win_bias(0), win_bias(WIN_TILES - 1)])
    return pl.pallas_call(
        _attn_prompt_body,
        grid=(n, t // ATT_TQ),
        in_specs=[
            pl.BlockSpec((None, D_ATTN, ATT_TQ), lambda a, i: (a, 0, i)),
            pl.BlockSpec((None, t, KV_W), seq),
            pl.BlockSpec((None, t, KV_W), seq),
            pl.BlockSpec((None, N_KV * HEAD_DIM, t), v_rows),
            pl.BlockSpec((None, N_KV * HEAD_DIM, t), v_rows),
            pl.BlockSpec((None, N_KV, nb, LANES), seq4),
            pl.BlockSpec((None, N_KV, HEAD_DIM, nb), seq4),
            pl.BlockSpec((None, GATE_ROWS, ATT_TQ), lambda a, i: (a, 0, i)),
            pl.BlockSpec((2, ATT_TK, ATT_TQ), lambda a, i: (0, 0, 0)),
        ],
        out_specs=pl.BlockSpec((None, ATT_TQ, D_ATTN), row),
        out_shape=jax.ShapeDtypeStruct((n, t, D_ATTN), F32),
        compiler_params=_cparams(("arbitrary", "arbitrary")),
        name="attn_prompt",
    )(q_t, k_sel, k_win, kvs_t, kvw_t, kc_pad, vc_t, gates_t, wb)


DEC_SEQ_PER_STEP = 8


def _dec_cmp_body(q2_ref, kc_ref, pt_ref, ocmp_ref, idx_ref, page_ref, p_buf, imp_buf, *, past, s_new):
    assert BPP == 2
    ns = q2_ref.shape[0]
    qrows = s_new * GQA_R
    n_pages = kc_ref.shape[2] // N_KV
    nb = n_pages * BPP
    width = imp_buf.shape[1]
    imp_buf[...] = jnp.zeros(imp_buf.shape, F32)
    lane = lax.broadcasted_iota(jnp.int32, (1, width), 1)
    jc = jnp.where(lane < n_pages, BPP * lane, BPP * (lane - n_pages) + 1)
    jc = jnp.where(lane == nb, nb, jc)

    row2 = lax.broadcasted_iota(jnp.int32, (2 * qrows, 1), 0)
    half = jnp.where(row2 >= qrows, 1, 0)
    blk = BPP * lax.broadcasted_iota(jnp.int32, (1, n_pages), 1) + half
    tok = (row2 & (qrows - 1)) >> (GQA_R.bit_length() - 1)
    vis = ((blk + 1) * BLK - 1) <= (past + tok)

    def both(x):
        return jnp.concatenate([x, x], axis=0)

    def pages(a, c, g):
        return kc_ref[a, c, g * n_pages:(g + 1) * n_pages, :].astype(BF16)

    pairs = [(a, g) for a in range(ns) for g in range(N_KV)]
    scores = [_dot_nt(q2_ref[a, g], pages(a, 0, g)) for a, g in pairs]
    for i, (a, g) in enumerate(pairs):
        s = jnp.where(vis, scores[i], NEG)
        m = jnp.max(s, axis=-1, keepdims=True)
        e = jnp.where(vis, jnp.exp(s - both(jnp.maximum(m[:qrows], m[qrows:]))), 0.0)
        l = jnp.sum(e, axis=-1, keepdims=True)
        l = l[:qrows] + l[qrows:]
        p = e / both(jnp.where(l > 0.0, l, 1.0))
        o2 = _dot(p.astype(BF16), pages(a, 1, g))
        ocmp_ref[a, g] = o2[:qrows, :HEAD_DIM] + o2[qrows:, HEAD_DIM:]
        p_buf[i] = p
        imp = sum(p_buf[i, pl.ds(r, 2 * s_new, stride=GQA_R), :] for r in range(GQA_R))
        row0 = i * s_new
        imp_buf[row0:row0 + s_new, 0:n_pages] = imp[:s_new]
        imp_buf[row0:row0 + s_new, n_pages:nb] = imp[s_new:]

    rows = ns * N_KV * s_new
    imp = imp_buf[...]
    t_row = lax.broadcasted_iota(jnp.int32, (rows, 1), 0) & (s_new - 1)
    cur = (past + t_row) >> BLK_SHIFT
    score = jnp.where(jc > cur, NEG, imp)
    score = jnp.where((jc == 0) | (jc == cur) | (jc == cur - 1), FORCE, score)
    score = jnp.where(lane <= nb, score, -jnp.inf)
    jcf = jc.astype(F32)
    out_lane = lax.broadcasted_iota(jnp.int32, (1, LANES), 1)
    idx = jnp.zeros((rows, LANES), jnp.int32)
    for k in range(N_SEL):
        m = jnp.max(score, axis=-1, keepdims=True)
        first = jnp.min(jnp.where(score == m, jcf, 1e9), axis=-1, keepdims=True)
        idx = jnp.where(out_lane == k, first.astype(jnp.int32), idx)
        score = jnp.where(jcf == first, -jnp.inf, score)
    idx_ref[...] = idx
    per_seq = N_KV * s_new
    pt_rows = jnp.concatenate([jnp.broadcast_to(pt_ref[a:a + 1, :], (per_seq, n_pages)) for a in range(ns)],
                              axis=0)
    page_ref[...] = jnp.take_along_axis(pt_rows, jnp.minimum(idx, nb - 1) >> (BPP.bit_length() - 1), axis=1)


def _dec_cmp(q2, kc, page_table, past, s_new):
    nseq = q2.shape[0]
    qrows = s_new * GQA_R
    rows_kc = kc.shape[2]
    n_pages = rows_kc // N_KV
    assert n_pages == LANES
    nb = n_pages * BPP
    ns = DEC_SEQ_PER_STEP
    width = (nb + 1 + LANES - 1) // LANES * LANES
    seq4 = lambda i: (i, 0, 0, 0)
    rows2 = lambda i: (i, 0)
    idx_shape = jax.ShapeDtypeStruct((nseq * N_KV * s_new, LANES), jnp.int32)
    return pl.pallas_call(
        functools.partial(_dec_cmp_body, past=past, s_new=s_new),
        grid=(nseq // ns,),
        in_specs=[
            pl.BlockSpec((ns, N_KV, 2 * qrows, 2 * HEAD_DIM), seq4),
            pl.BlockSpec((ns, 2, rows_kc, PAGE_SIZE), seq4),
            pl.BlockSpec((ns, n_pages), rows2),
        ],
        out_specs=[
            pl.BlockSpec((ns, N_KV, qrows, HEAD_DIM), seq4),
            pl.BlockSpec((ns * N_KV * s_new, LANES), rows2),
            pl.BlockSpec((ns * N_KV * s_new, LANES), rows2),
        ],
        out_shape=[jax.ShapeDtypeStruct((nseq, N_KV, qrows, HEAD_DIM), F32), idx_shape, idx_shape],
        scratch_shapes=[pltpu.VMEM((ns * N_KV, 2 * qrows, n_pages), F32),
                        pltpu.VMEM((ns * N_KV * s_new, width), F32)],
        compiler_params=_cparams(("arbitrary",)),
        name="dec_cmp_select",
    )(q2, kc, page_table)


def _dec_sel_win_body(page_ref, q_ref, idx2_ref, expand_ref, pool_ref, newblk_ref, win_ref,
                      kvw_new_ref, osel_ref, owin_ref, winout_ref, kvbuf, sem, *, past, s_new):
    n_past_blk = past // BLK
    n = pl.program_id(0)
    nseq = pl.num_programs(0)
    slot = n % 2
    rows = s_new * GQA_R
    seg_lanes = N_SEL * PAGE_SIZE
    nkeys = s_new * seg_lanes

    def block_copy(seq, c, k, sl):
        g, t = c // s_new, c % s_new
        page = page_ref[(seq * N_KV * s_new + c) * N_SEL + k]
        dst = kvbuf.at[sl, :, g, :, pl.ds((t * N_SEL + k) * PAGE_SIZE, PAGE_SIZE)]
        return pltpu.make_async_copy(pool_ref.at[page, :, g], dst, sem.at[sl])

    def start(seq, sl):
        for c in range(N_KV * s_new):
            for k in range(N_SEL):
                block_copy(seq, c, k, sl).start()

    @pl.when(n == 0)
    def _():
        start(n, slot)

    @pl.when(n + 1 < nseq)
    def _():
        start(n + 1, 1 - slot)

    for c in range(N_KV * s_new):
        for k in range(N_SEL):
            block_copy(n, c, k, slot).wait()

    assert win_ref.shape[1] == min(WINDOW, past + s_new)
    slid = jnp.concatenate([win_ref[...], kvw_new_ref[...]], axis=1)
    winout_ref[...] = slid[:, s_new:s_new + win_ref.shape[1]]

    t_row = lax.broadcasted_iota(jnp.int32, (rows, 1), 0) >> (GQA_R.bit_length() - 1)
    qpos = past + t_row
    lane = lax.broadcasted_iota(jnp.int32, (1, nkeys), 1)
    own = (lane >> (seg_lanes.bit_length() - 1)) == t_row
    row_in_page = lane & (PAGE_SIZE - 1)
    nlane = lax.broadcasted_iota(jnp.int32, (1, PAGE_SIZE), 1)
    nwin = win_ref.shape[1]
    wpos = past - nwin + lax.broadcasted_iota(jnp.int32, (1, nwin), 1)
    groups = range(N_KV)

    def krows(g):
        return pl.ds(g * HEAD_DIM, HEAD_DIM)

    def vrows(g):
        return pl.ds((N_KV + g) * HEAD_DIM, HEAD_DIM)

    q = [q_ref[g] for g in groups]
    jv2 = _dot(idx2_ref[...], expand_ref[...])
    s_sel = [_dot(q[g], kvbuf[slot, 0, g].astype(BF16)) for g in groups]
    s_new_sel = [_dot(q[g], newblk_ref[krows(g), :].astype(BF16)) for g in groups]
    s_win = [_dot(q[g], win_ref[krows(g), :].astype(BF16)) for g in groups]
    s_new_win = [_dot(q[g], kvw_new_ref[krows(g), :].astype(BF16)) for g in groups]

    def two_piece_softmax(sa, va, sb, vb):
        m = jnp.maximum(jnp.max(sa, axis=-1, keepdims=True), jnp.max(sb, axis=-1, keepdims=True))
        ea, eb = jnp.exp(sa - m), jnp.exp(sb - m)
        l = jnp.sum(ea, axis=-1, keepdims=True) + jnp.sum(eb, axis=-1, keepdims=True)
        return (_dot_nt(ea.astype(BF16), va) + _dot_nt(eb.astype(BF16), vb)) / l

    for g in groups:
        jv = jv2[g:g + 1, :].astype(jnp.int32)
        kpos = jv * BLK + (row_in_page & (BLK - 1))
        valid = (own & (jv < n_past_blk) & ((row_in_page >> BLK_SHIFT) == (jv & (BPP - 1)))
                 & (kpos <= qpos))
        has_new = jnp.max(jnp.where(own & (jv >= n_past_blk), 1.0, 0.0), axis=-1, keepdims=True)
        valid_n = (has_new > 0.5) & (nlane < BLK) & (n_past_blk * BLK + nlane <= qpos)
        osel_ref[g] = two_piece_softmax(
            jnp.where(valid, s_sel[g], NEG), kvbuf[slot, 1, g].astype(BF16),
            jnp.where(valid_n, s_new_sel[g], NEG), newblk_ref[vrows(g), :].astype(BF16))

        dw = qpos - wpos
        dn = qpos - (past + nlane)
        valid_w = (dw >= 0) & (dw < WINDOW) & (wpos >= 0)
        valid_n = (dn >= 0) & (dn < WINDOW) & (nlane < s_new)
        owin_ref[g] = two_piece_softmax(
            jnp.where(valid_w, s_win[g], NEG), win_ref[vrows(g), :].astype(BF16),
            jnp.where(valid_n, s_new_win[g], NEG), kvw_new_ref[vrows(g), :].astype(BF16))


def _dec_sel_win(page_flat, q_g, idx2, pool_t, newblk_t, win_t, kvw_new_t, past, s_new):
    nseq = q_g.shape[0]
    rows = s_new * GQA_R
    nwin = win_t.shape[2]
    nkeys = s_new * N_SEL * PAGE_SIZE
    n_phys = pool_t.shape[0]
    expand = (jnp.arange(nkeys)[None, :] // PAGE_SIZE == jnp.arange(s_new * N_SEL)[:, None]).astype(BF16)
    per_seq = lambda a, pg: (a, 0, 0, 0)
    per_seq3 = lambda a, pg: (a, 0, 0)
    grid_spec = pltpu.PrefetchScalarGridSpec(
        num_scalar_prefetch=1,
        grid=(nseq,),
        in_specs=[
            pl.BlockSpec((None, N_KV, rows, HEAD_DIM), per_seq),
            pl.BlockSpec((None, rows, s_new * N_SEL), per_seq3),
            pl.BlockSpec((s_new * N_SEL, nkeys), lambda a, pg: (0, 0)),
            pl.BlockSpec(memory_space=pl.ANY),
            pl.BlockSpec((None, KV_W, PAGE_SIZE), per_seq3),
            pl.BlockSpec((None, KV_W, nwin), per_seq3),
            pl.BlockSpec((None, KV_W, PAGE_SIZE), per_seq3),
        ],
        out_specs=[
            pl.BlockSpec((None, N_KV, rows, HEAD_DIM), per_seq),
            pl.BlockSpec((None, N_KV, rows, HEAD_DIM), per_seq),
            pl.BlockSpec((None, KV_W, nwin), per_seq3),
        ],
        scratch_shapes=[pltpu.VMEM((2, 2, N_KV, HEAD_DIM, nkeys), F32),
                        pltpu.SemaphoreType.DMA((2,))],
    )
    shape = jax.ShapeDtypeStruct((nseq, N_KV, rows, HEAD_DIM), F32)
    return pl.pallas_call(
        functools.partial(_dec_sel_win_body, past=past, s_new=s_new),
        grid_spec=grid_spec,
        out_shape=[shape, shape, jax.ShapeDtypeStruct(win_t.shape, F32)],
        compiler_params=_cparams(("arbitrary",)),
        name="dec_sel_win",
    )(page_flat, q_g, idx2, expand, pool_t.reshape(n_phys, 2, N_KV, HEAD_DIM, PAGE_SIZE),
      newblk_t, win_t, kvw_new_t)


def _out_ln_body(*refs, gated):
    if gated:
        ocmp_ref, osel_ref, owin_ref, gates_ref, yconv_ref, h_ref, wo_ref, g_ref, b_ref, y_ref = refs
        gates = gates_ref[...]
        pieces = []
        for h in range(N_HEADS):
            col = slice(h * HEAD_DIM, (h + 1) * HEAD_DIM)
            pieces.append(gates[:, h:h + 1] * ocmp_ref[:, col]
                          + gates[:, N_HEADS + h:N_HEADS + h + 1] * osel_ref[:, col]
                          + gates[:, 2 * N_HEADS + h:2 * N_HEADS + h + 1] * owin_ref[:, col])
        o = jnp.concatenate(pieces, axis=-1)
    else:
        o_ref, yconv_ref, h_ref, wo_ref, g_ref, b_ref, y_ref = refs
        o = o_ref[...]
    mixed = jnp.concatenate([o, yconv_ref[...]], axis=-1).astype(BF16)
    y_ref[...] = _layer_norm(ALPHA * h_ref[...] + _dot(mixed, wo_ref[...]), g_ref[...], b_ref[...])


def _out_ln(branches, yconv, h2d, w_out, g, b, tm):
    rows = h2d.shape[0]
    gated = len(branches) > 1
    row = lambda i: (i, 0)
    const = lambda i: (0, 0)
    widths = [D_ATTN, D_ATTN, D_ATTN, LANES] if gated else [D_ATTN]
    return pl.pallas_call(
        functools.partial(_out_ln_body, gated=gated),
        grid=(rows // tm,),
        in_specs=[pl.BlockSpec((tm, w), row) for w in widths] + [
            pl.BlockSpec((tm, D_CONV), row),
            pl.BlockSpec((tm, D_MODEL), row),
            pl.BlockSpec((D_MODEL, D_MODEL), const),
            pl.BlockSpec((1, D_MODEL), const),
            pl.BlockSpec((1, D_MODEL), const),
        ],
        out_specs=pl.BlockSpec((tm, D_MODEL), row),
        out_shape=jax.ShapeDtypeStruct((rows, D_MODEL), F32),
        compiler_params=_cparams(("arbitrary",)),
        name="out_ln",
    )(*branches, yconv, h2d, w_out, g, b)


def _block_diag_groups(w):
    z = jnp.zeros_like(w)
    return jnp.concatenate([jnp.concatenate([w, z], axis=-1),
                            jnp.concatenate([z, w], axis=-1)], axis=-2).astype(BF16)


def _to_group_layout(x, nseq, s_new):
    x = x.reshape(nseq, s_new, N_KV, GQA_R, HEAD_DIM).transpose(0, 2, 1, 3, 4)
    return x.reshape(nseq, N_KV, s_new * GQA_R, HEAD_DIM)


def _from_group_layout(x, nseq, s_new):
    x = x.reshape(nseq, N_KV, s_new, GQA_R, HEAD_DIM).transpose(0, 2, 1, 3, 4)
    return x.reshape(nseq * s_new, D_ATTN)


def kernel(x_prompt, x_sample, cache_cmp_kv, cache_sel_kv, state_win_kv, state_conv, page_table,
           ffa_gate, ffa_up, ffa_down, ln1_g, ln1_b, w_in, b_in, w_conv, cmp_w1, cmp_w2, cmp_pe,
           w_out, ln2_g, ln2_b, ffb_gate, ffb_up, ffb_down, ln3_g, ln3_b):
    depth = ffa_gate.shape[0]
    assert depth == 1
    l = 0
    n, t, d = x_prompt.shape
    nseq, s_new, _ = x_sample.shape
    n_pages = page_table.shape[1]
    past = n_pages * PAGE_SIZE
    rows_p, rows_s = n * t, nseq * s_new
    kvshape = (2, N_KV, HEAD_DIM)
    assert d == D_MODEL and rows_p % FFN_ROW_TILE == 0 and t % PROJ_ROW_TILE == 0 and t % ATT_TQ == 0
    assert t // BLK <= AUX_BLOCKS and n % CMP_SEQS_PER_STEP == 0 and nseq % DEC_SEQ_PER_STEP == 0
    assert s_new * GQA_R == BF16_SUBLANES and s_new <= BLK and past % BLK == 0
    assert state_win_kv.shape[2] == WINDOW

    bf = lambda w: w.astype(BF16)
    vec = lambda v: v.reshape(1, -1)
    w_main = bf(w_in[l][:, :D_MAIN])
    b_main = vec(b_in[l][:D_MAIN])
    w_gate = bf(jnp.pad(w_in[l][:, D_MAIN:], ((0, 0), (0, LANES - N_GATE))))
    b_gate = vec(jnp.pad(b_in[l][D_MAIN:], (0, LANES - N_GATE)))
    pe_t = jnp.concatenate([cmp_pe[l].transpose(1, 2, 0)] * BPP, axis=-1)
    w1_t = _block_diag_groups(cmp_w1[l].transpose(0, 2, 1, 3)).reshape(
        2, D_PAIRS, 2 * PAGE_SIZE, PAGE_SIZE)
    w2_t = _block_diag_groups(cmp_w2[l])
    w_out_b = bf(w_out[l])
    ffa = (bf(ffa_gate[l]), bf(ffa_up[l]), bf(0.5 * ffa_down[l]), vec(ln1_g[l]), vec(ln1_b[l]))
    ffb = (bf(ffb_gate[l]), bf(ffb_up[l]), bf(0.5 * ffb_down[l]), vec(ln3_g[l]), vec(ln3_b[l]))

    def prompt_cols(x):
        return jnp.concatenate([x[..., :_C_Q], x[..., _C_KVS:_C_KVS + K_HALF],
                                x[..., _C_KVW:_C_KVW + K_HALF]], axis=-1)

    def prompt_rows(x):
        return jnp.pad(x[..., _C_Q:], [(0, 0)] * (x.ndim - 1) + [(0, GATE_ROWS - N_GATE)])

    w_rows, b_rows = bf(prompt_cols(w_in[l])), vec(prompt_cols(b_in[l]))
    w_tr, b_tr = bf(prompt_rows(w_in[l]).T), prompt_rows(b_in[l]).reshape(-1, 1)

    pps = t // PAGE_SIZE
    cmp_seqs = CMP_SEQS_PER_STEP
    hp = _ffn_ln(x_prompt.reshape(rows_p, d), *ffa, tm=FFN_ROW_TILE)
    q, k_sel, k_win, kvc_t, kvs_t, kvw_t, gates_t, yconv, cstate = _proj_prompt(
        hp.reshape(n, t, d), w_rows, b_rows, w_tr, b_tr, w_conv[l], tm=PROJ_ROW_TILE)
    kc = _compress(jnp.zeros((1,), jnp.int32), kvc_t, pe_t, w1_t, w2_t, n_steps=n // cmp_seqs,
                   n_pages=cmp_seqs * pps, pages_per_seq=pps, name="compress_prompt")
    kc = kc.reshape(n // cmp_seqs, 2, N_KV, cmp_seqs, pps, BPP, HEAD_DIM)
    kc = kc.transpose(0, 3, 2, 1, 4, 5, 6).reshape(n, N_KV, 2, pps * BPP, HEAD_DIM)
    zc = jnp.zeros_like(kc[:, 0, 0])
    kc_pad = bf(jnp.stack([jnp.concatenate([kc[:, 0, 0], zc], axis=-1),
                           jnp.concatenate([zc, kc[:, 1, 0]], axis=-1)], axis=1))
    vc_t = bf(jnp.swapaxes(kc[:, :, 1], -1, -2))
    o = _attn_prompt(q, k_sel, k_win, kvs_t, kvw_t, kc_pad, vc_t, gates_t)
    yp = _out_ln([o.reshape(rows_p, D_ATTN)], yconv.reshape(rows_p, D_CONV), hp, w_out_b,
                 vec(ln2_g[l]), vec(ln2_b[l]), tm=OUT_ROW_TILE)
    yp = _ffn_ln(yp, *ffb, tm=FFN_ROW_TILE).reshape(n, t, d)

    def from_feature_major(x_t):
        return jnp.moveaxis(x_t.reshape(x_t.shape[0], *kvshape, x_t.shape[-1]), -1, 1)[None]

    p_cmp = from_feature_major(kvc_t)
    p_sel = from_feature_major(kvs_t)
    p_win = from_feature_major(kvw_t[:, :, t - min(WINDOW, t):])
    p_conv = cstate[:, SUBLANES - (CONV_W - 1):].reshape(1, n, CONV_W - 1, D_CONV)

    hs = _ffn_ln(x_sample.reshape(rows_s, d), *ffa, tm=rows_s)
    qs, kv3, gates_s, cb_s, u_s = _proj_sample(hs, w_main, b_main, w_gate, b_gate)
    kvc_s, kvs_s, kvw_s = kv3[:, :KV_W], kv3[:, KV_W:2 * KV_W], kv3[:, 2 * KV_W:]
    u3 = u_s.reshape(nseq, s_new, D_CONV)
    yconv_s = _conv_sample(cb_s.reshape(nseq, s_new, D_CONV), u3, state_conv[l], w_conv[l])
    s_conv = jnp.concatenate([state_conv[l], u3], axis=1)[:, -(CONV_W - 1):]

    def feature_major(x):
        return jnp.moveaxis(x, -4, -1).reshape(*x.shape[:-4], KV_W, x.shape[-4])

    def new_rows_t(x2d):
        xt = x2d.reshape(nseq, s_new, KV_W).transpose(0, 2, 1)
        return jnp.pad(xt, ((0, 0), (0, 0), (0, PAGE_SIZE - s_new)))

    kc_s = _compress(page_table.reshape(-1), feature_major(cache_cmp_kv[l]), pe_t, w1_t, w2_t,
                     n_steps=nseq, n_pages=n_pages, pages_per_seq=None, name="compress_decode")
    q_g = _to_group_layout(qs, nseq, s_new)
    zq = jnp.zeros_like(q_g)
    q2 = jnp.concatenate([jnp.concatenate([q_g, zq], axis=-1), jnp.concatenate([zq, q_g], axis=-1)], axis=-2)
    ocmp_c, idx_pad, page_pad = _dec_cmp(q2, kc_s, page_table, past, s_new)
    idx = idx_pad[:, :N_SEL]
    idx2 = jnp.pad(bf(idx.reshape(nseq, N_KV, s_new * N_SEL)), ((0, 0), (0, s_new * GQA_R - N_KV), (0, 0)))
    win_t = feature_major(state_win_kv[l])
    kvw_new_t = new_rows_t(kvw_s)
    osel_c, owin_c, win_next_t = _dec_sel_win(
        page_pad[:, :N_SEL].reshape(-1), q_g, idx2, feature_major(cache_sel_kv[l]),
        new_rows_t(kvs_s), win_t, kvw_new_t, past, s_new)
    branches = [_from_group_layout(x, nseq, s_new) for x in (ocmp_c, osel_c, owin_c)]
    ys = _out_ln(branches + [gates_s], yconv_s.reshape(rows_s, D_CONV), hs, w_out_b,
                 vec(ln2_g[l]), vec(ln2_b[l]), tm=rows_s)
    ys = _ffn_ln(ys, *ffb, tm=rows_s).reshape(nseq, s_new, d)
    s_cmp = kvc_s.reshape(1, nseq, s_new, *kvshape)
    s_sel = kvs_s.reshape(1, nseq, s_new, *kvshape)
    s_win = from_feature_major(win_next_t)
    return (yp, ys, p_cmp, p_sel, p_win, p_conv, s_cmp, s_sel, s_win,
            s_conv.reshape(1, nseq, CONV_W - 1, D_CONV))
```

```python
import functools

import jax
import jax.numpy as jnp
from jax import lax
from jax.experimental import pallas as pl
from jax.experimental.pallas import tpu as pltpu

F32 = jnp.float32
BF16 = jnp.bfloat16

D_MODEL = 1024
D_CONV = 512
D_ATTN = 512
HEAD_DIM = 64
N_HEADS = 8
N_KV = 2
GQA_R = N_HEADS // N_KV
N_BR = 3
CONV_W = 3
BLK = 64
BLK_SHIFT = BLK.bit_length() - 1
N_SEL = 16
WINDOW = 512
PAGE_SIZE = 128
D_FF = 2816
KV_W = 2 * N_KV * HEAD_DIM
D_MAIN = 3 * D_CONV + D_ATTN + 3 * KV_W
N_GATE = N_BR * N_HEADS
ALPHA = 2.0 ** 0.25
SCALE = HEAD_DIM ** -0.5
LN_EPS = 1e-5
NEG = -1e30
FORCE = 1e4

LANES = 128
SUBLANES = 8
BF16_SUBLANES = 2 * SUBLANES
V7X_VMEM_BYTES = 64 * 1024 * 1024
VMEM_LIMIT_BYTES = V7X_VMEM_BYTES * 7 // 8

FFN_ROW_TILE = 1024
PROJ_ROW_TILE = 1024
OUT_ROW_TILE = 1024
CMP_SEQS_PER_STEP = 4


def _cparams(sem):
    return pltpu.CompilerParams(dimension_semantics=sem, vmem_limit_bytes=VMEM_LIMIT_BYTES)


def _layer_norm(y, g, b):
    mu = jnp.mean(y, axis=-1, keepdims=True)
    yc = y - mu
    var = jnp.mean(yc * yc, axis=-1, keepdims=True)
    return yc * lax.rsqrt(var + LN_EPS) * g + b


def _dot(a, b):
    return jnp.dot(a, b, preferred_element_type=F32)


def _dot_nt(a, b):
    return lax.dot_general(a, b, (((1,), (1,)), ((), ())), preferred_element_type=F32)


FFN_CHUNK = 256


def _ffn_ln_body(x_ref, wg_ref, wu_ref, wd_ref, g_ref, b_ref, o_ref):
    x = x_ref[...]
    xb = x.astype(BF16)
    acc = jnp.zeros(x.shape, F32)
    for c in range(D_FF // FFN_CHUNK):
        sl = slice(c * FFN_CHUNK, (c + 1) * FFN_CHUNK)
        gg = _dot(xb, wg_ref[:, sl])
        uu = _dot(xb, wu_ref[:, sl])
        hh = (gg * jax.nn.sigmoid(gg) * uu).astype(BF16)
        acc = acc + _dot(hh, wd_ref[sl, :])
    o_ref[...] = _layer_norm(ALPHA * x + acc, g_ref[...], b_ref[...])


def _ffn_ln(x2d, wg, wu, wd, g, b, tm):
    rows = x2d.shape[0]
    const = lambda i: (0, 0)
    return pl.pallas_call(
        _ffn_ln_body,
        grid=(rows // tm,),
        in_specs=[
            pl.BlockSpec((tm, D_MODEL), lambda i: (i, 0)),
            pl.BlockSpec((D_MODEL, D_FF), const, pipeline_mode=pl.Buffered(1)),
            pl.BlockSpec((D_MODEL, D_FF), const, pipeline_mode=pl.Buffered(1)),
            pl.BlockSpec((D_FF, D_MODEL), const, pipeline_mode=pl.Buffered(1)),
            pl.BlockSpec((1, D_MODEL), const),
            pl.BlockSpec((1, D_MODEL), const),
        ],
        out_specs=pl.BlockSpec((tm, D_MODEL), lambda i: (i, 0)),
        out_shape=jax.ShapeDtypeStruct((rows, D_MODEL), F32),
        compiler_params=_cparams(("arbitrary",)),
        name="ffn_ln",
    )(x2d, wg, wu, wd, g, b)


_C_CB, _C_CC, _C_CH, _C_Q = 0, D_CONV, 2 * D_CONV, 3 * D_CONV
_C_KVC = 3 * D_CONV + D_ATTN
_C_KVS = _C_KVC + KV_W
_C_KVW = _C_KVS + KV_W


def _proj_piece(hb, w_ref, b_ref, lo, width):
    return _dot(hb, w_ref[:, lo:lo + width]) + b_ref[:, lo:lo + width]


_P_K = 3 * D_CONV
K_HALF = N_KV * HEAD_DIM
P_ROW_COLS = _P_K + 2 * K_HALF
GATE_ROWS = 32
_T_KV = D_ATTN
_T_GATE = _T_KV + 3 * KV_W
P_T_ROWS = _T_GATE + GATE_ROWS


def _proj_prompt_body(h_ref, w_ref, b_ref, wt_ref, bt_ref, wc_ref,
                      qt_ref, ks_ref, kw_ref, kvct_ref, kvst_ref, kvwt_ref, gates_ref,
                      yconv_ref, cstate_ref, ubuf):
    i = pl.program_id(1)
    tm = h_ref.shape[0]
    hb = h_ref[...].astype(BF16)

    def piece_t(lo, height):
        return _dot_nt(wt_ref[lo:lo + height, :], hb) + bt_ref[lo:lo + height, :]

    qt_ref[...] = (piece_t(0, D_ATTN) * (SCALE * LOG2E)).astype(BF16)
    lane = lax.broadcasted_iota(jnp.int32, (1, KV_W), 1)
    blk = (i * tm + lax.broadcasted_iota(jnp.int32, (tm, 1), 0)) >> BLK_SHIFT
    aux0, aux1 = HEAD_DIM, 2 * HEAD_DIM
    one = (lane == aux0 + AUX_BLOCKS) | (lane == aux1 + AUX_BLOCKS)
    is_k = (lane < aux0) | (lane >= aux1 + HEAD_DIM)
    k4 = _proj_piece(hb, w_ref, b_ref, _P_K, 2 * K_HALF)
    for ref, lo, ones in ((ks_ref, 0, one | (lane == aux0 + blk) | (lane == aux1 + blk)),
                          (kw_ref, K_HALF, one)):
        k = k4[:, lo:lo + K_HALF]
        ref[...] = jnp.where(is_k, jnp.concatenate([k, k], axis=1),
                             jnp.where(ones, 1.0, 0.0)).astype(BF16)
    for k, ref in enumerate((kvct_ref, kvst_ref, kvwt_ref)):
        ref[...] = piece_t(_T_KV + k * KV_W, KV_W)
    gates_ref[...] = jax.nn.sigmoid(piece_t(_T_GATE, GATE_ROWS))
    cb = _proj_piece(hb, w_ref, b_ref, _C_CB, D_CONV)
    u = _proj_piece(hb, w_ref, b_ref, _C_CC, D_CONV) * _proj_piece(hb, w_ref, b_ref, _C_CH, D_CONV)

    head = SUBLANES

    @pl.when(i == 0)
    def _():
        ubuf[0:head, :] = jnp.zeros((head, D_CONV), F32)

    ubuf[head:head + tm, :] = u
    u1 = ubuf[head - 1:head - 1 + tm, :]
    u2 = ubuf[head - 2:head - 2 + tm, :]
    z = u2 * wc_ref[0:1, :] + u1 * wc_ref[1:2, :] + u * wc_ref[2:3, :]
    yconv_ref[...] = (cb * z).astype(BF16)
    tail = u[tm - head:tm, :]
    ubuf[0:head, :] = tail
    cstate_ref[...] = tail


def _proj_prompt(h, w_rows, b_rows, w_t, b_t, w_conv, tm):
    n, t, _ = h.shape
    const = lambda a, i: (0, 0)
    row = lambda a, i: (a, i, 0)
    col = lambda a, i: (a, 0, i)

    def out(width, dtype=F32):
        return jax.ShapeDtypeStruct((n, t, width), dtype)

    def out_t(rows, dtype=F32):
        return jax.ShapeDtypeStruct((n, rows, t), dtype)

    return pl.pallas_call(
        _proj_prompt_body,
        grid=(n, t // tm),
        in_specs=[
            pl.BlockSpec((None, tm, D_MODEL), row),
            pl.BlockSpec((D_MODEL, P_ROW_COLS), const),
            pl.BlockSpec((1, P_ROW_COLS), const),
            pl.BlockSpec((P_T_ROWS, D_MODEL), const),
            pl.BlockSpec((P_T_ROWS, 1), const),
            pl.BlockSpec((CONV_W, D_CONV), const),
        ],
        out_specs=[
            pl.BlockSpec((None, D_ATTN, tm), col),
            pl.BlockSpec((None, tm, KV_W), row),
            pl.BlockSpec((None, tm, KV_W), row),
            pl.BlockSpec((None, KV_W, tm), col),
            pl.BlockSpec((None, KV_W, tm), col),
            pl.BlockSpec((None, KV_W, tm), col),
            pl.BlockSpec((None, GATE_ROWS, tm), col),
            pl.BlockSpec((None, tm, D_CONV), row),
            pl.BlockSpec((None, SUBLANES, D_CONV), lambda a, i: (a, 0, 0)),
        ],
        out_shape=[out_t(D_ATTN, BF16), out(KV_W, BF16), out(KV_W, BF16), out_t(KV_W), out_t(KV_W),
                   out_t(KV_W), out_t(GATE_ROWS), out(D_CONV, BF16),
                   jax.ShapeDtypeStruct((n, SUBLANES, D_CONV), F32)],
        scratch_shapes=[pltpu.VMEM((tm + SUBLANES, D_CONV), F32)],
        compiler_params=_cparams(("arbitrary", "arbitrary")),
        name="proj_prompt",
    )(h, w_rows, b_rows, w_t, b_t, w_conv)


def _proj_sample_body(h_ref, w_ref, b_ref, wgt_ref, bgt_ref, q_ref, kv_ref, gates_ref, cb_ref, u_ref):
    hb = h_ref[...].astype(BF16)
    q_ref[...] = (_proj_piece(hb, w_ref, b_ref, _C_Q, D_ATTN) * SCALE).astype(BF16)
    kv_ref[...] = _proj_piece(hb, w_ref, b_ref, _C_KVC, 3 * KV_W)
    gates_ref[...] = jax.nn.sigmoid(_dot(hb, wgt_ref[...]) + bgt_ref[...])
    cb_ref[...] = _proj_piece(hb, w_ref, b_ref, _C_CB, D_CONV)
    u_ref[...] = _proj_piece(hb, w_ref, b_ref, _C_CC, D_CONV) * _proj_piece(hb, w_ref, b_ref, _C_CH, D_CONV)


def _proj_sample(h2d, w_main, b_main, w_gate, b_gate):
    rows = h2d.shape[0]

    def out(width, dtype=F32):
        return jax.ShapeDtypeStruct((rows, width), dtype)

    return pl.pallas_call(
        _proj_sample_body,
        out_shape=[out(D_ATTN, BF16), out(3 * KV_W), out(LANES), out(D_CONV), out(D_CONV)],
        compiler_params=pltpu.CompilerParams(vmem_limit_bytes=VMEM_LIMIT_BYTES),
        name="proj_sample",
    )(h2d, w_main, b_main, w_gate, b_gate)


def _conv_sample_body(cb_ref, u_ref, prev_ref, wc_ref, y_ref):
    s = u_ref.shape[1]
    ext = [prev_ref[:, 0, :], prev_ref[:, 1, :]] + [u_ref[:, t, :] for t in range(s)]
    for t in range(s):
        z = ext[t] * wc_ref[0:1, :] + ext[t + 1] * wc_ref[1:2, :] + ext[t + 2] * wc_ref[2:3, :]
        y_ref[:, t, :] = cb_ref[:, t, :] * z


def _conv_sample(cb, u, prev, w_conv):
    return pl.pallas_call(
        _conv_sample_body,
        out_shape=jax.ShapeDtypeStruct(u.shape, F32),
        name="conv_sample",
    )(cb, u, prev, w_conv)


BPP = PAGE_SIZE // BLK
D_PAIRS = HEAD_DIM // 2


def _compress_body(pt_ref, src_ref, pe_ref, w1_ref, w2_ref, kc_ref, buf, sem, *, n_pages, pages_per_seq):
    n = pl.program_id(0)
    slot = n % 2

    def page_copy(step, j, sl):
        dst = buf.at[sl, :, j, :]
        if pages_per_seq is None:
            src = src_ref.at[pt_ref[step * n_pages + j]]
        else:
            seq = step * (n_pages // pages_per_seq) + j // pages_per_seq
            src = src_ref.at[seq, :, pl.ds((j % pages_per_seq) * PAGE_SIZE, PAGE_SIZE)]
        return pltpu.make_async_copy(src, dst, sem.at[sl])

    def start(step, sl):
        for j in range(n_pages):
            page_copy(step, j, sl).start()

    @pl.when(n == 0)
    def _():
        start(n, slot)

    @pl.when(n + 1 < pl.num_programs(0))
    def _():
        start(n + 1, 1 - slot)

    for j in range(n_pages):
        page_copy(n, j, slot).wait()

    def feature_rows(f):
        return buf[slot, f]

    for c in range(2):
        acc = None
        for dp in range(D_PAIRS):
            rows = []
            for g in range(N_KV):
                f0 = (c * N_KV + g) * HEAD_DIM + 2 * dp
                rows.append(jnp.concatenate(
                    [feature_rows(f0) + pe_ref[c, 2 * dp:2 * dp + 1, :],
                     feature_rows(f0 + 1) + pe_ref[c, 2 * dp + 1:2 * dp + 2, :]], axis=-1))
            x = jnp.concatenate(rows, axis=0).astype(BF16)
            d = _dot(x, w1_ref[c, dp])
            acc = d if acc is None else acc + d
        hid = (acc * jax.nn.sigmoid(acc)).astype(BF16)
        kc_ref[c] = _dot(hid, w2_ref[c])


def _compress(pt_flat, src, pe_t, w1_t, w2_t, n_steps, n_pages, pages_per_seq, name):
    const3 = lambda a, pt: (0, 0, 0)
    grid_spec = pltpu.PrefetchScalarGridSpec(
        num_scalar_prefetch=1,
        grid=(n_steps,),
        in_specs=[
            pl.BlockSpec(memory_space=pl.ANY),
            pl.BlockSpec((2, HEAD_DIM, PAGE_SIZE), const3),
            pl.BlockSpec((2, D_PAIRS, 2 * PAGE_SIZE, PAGE_SIZE), lambda a, pt: (0, 0, 0, 0)),
            pl.BlockSpec((2, PAGE_SIZE, PAGE_SIZE), const3),
        ],
        out_specs=pl.BlockSpec((None, 2, N_KV * n_pages, PAGE_SIZE), lambda a, pt: (a, 0, 0, 0)),
        scratch_shapes=[pltpu.VMEM((2, KV_W, n_pages, PAGE_SIZE), F32),
                        pltpu.SemaphoreType.DMA((2,))],
    )
    return pl.pallas_call(
        functools.partial(_compress_body, n_pages=n_pages, pages_per_seq=pages_per_seq),
        grid_spec=grid_spec,
        out_shape=jax.ShapeDtypeStruct((n_steps, 2, N_KV * n_pages, PAGE_SIZE), F32),
        compiler_params=_cparams(("arbitrary",)),
        name=name,
    )(pt_flat, src, pe_t, w1_t, w2_t)


ATT_TQ = 256
ATT_TK = ATT_TQ
WIN_TILES = WINDOW // ATT_TK + 1
AUX_BLOCKS = 32
LOG2E = 1.4426950408889634


V_ROWS = HEAD_DIM + BF16_SUBLANES


def _attn_update(s_t, v_t, state):
    m, acc = state
    m_new = jnp.maximum(m, jnp.max(s_t, axis=0, keepdims=True))
    e = jnp.exp2(s_t - m_new)
    acc = jnp.exp2(m - m_new) * acc + _dot(v_t, e.astype(BF16))
    return m_new, acc


def _attn_prompt_body(qt_ref, ks_ref, kw_ref, vs_ref, vw_ref, kc_ref, vct_ref, gates_ref, wb_ref, o_ref):
    tq, tk = ATT_TQ, ATT_TK
    qi = pl.program_id(1)
    nb = kc_ref.shape[1]
    t0 = qi * tq
    kd = qi
    tpos = t0 + lax.broadcasted_iota(jnp.int32, (1, tq), 1)
    jrow = lax.broadcasted_iota(jnp.int32, (nb, 1), 0)
    vis = ((jrow + 1) * BLK - 1) <= tpos
    cur = tpos >> BLK_SHIFT

    def heads_of(x):
        return jnp.concatenate([x] * GQA_R, axis=1)

    def group_queries(g, block_bias=None, switch=None):
        zeros = jnp.zeros((AUX_BLOCKS, tq), BF16)
        top = zeros if block_bias is None else block_bias
        low = zeros if switch is None else jnp.broadcast_to(switch, (AUX_BLOCKS, tq)).astype(BF16)
        aux = jnp.concatenate([top, low], axis=0)
        pieces = []
        for r in range(GQA_R):
            h = g * GQA_R + r
            qh = qt_ref[h * HEAD_DIM:(h + 1) * HEAD_DIM, :]
            pieces.append(jnp.concatenate([qh, aux] if g == 0 else [aux, qh], axis=0))
        return jnp.concatenate(pieces, axis=1)

    q4 = [group_queries(g) for g in range(N_KV)]
    vis4 = heads_of(vis)

    o_cmp, q4_sel = [], []
    for g in range(N_KV):
        s = jnp.where(vis4, _dot(kc_ref[g], q4[g]), NEG)
        m = jnp.max(s, axis=0, keepdims=True)
        e = jnp.where(vis4, jnp.exp2(s - m), 0.0)
        l = jnp.sum(e, axis=0, keepdims=True)
        p = e / jnp.where(l > 0.0, l, 1.0)
        imp = sum(p[:, r * tq:(r + 1) * tq] for r in range(GQA_R))
        o_cmp.append(_dot(vct_ref[g], p.astype(BF16)))

        score = jnp.where(jrow > cur, NEG, imp)
        score = jnp.where((jrow == 0) | (jrow == cur) | (jrow == cur - 1), FORCE, score)
        rank = jnp.zeros((nb, tq), F32)
        for j2 in range(nb):
            row = score[j2:j2 + 1, :]
            tie = jnp.where(jrow > j2, 1.0, 0.0)
            rank = rank + jnp.where(row > score, 1.0, jnp.where(row == score, tie, 0.0))
        q4_sel.append(group_queries(g, jnp.where(rank < N_SEL, 0.0, NEG).astype(BF16)))

    def key_tile(k_ref, kt, g):
        return k_ref[pl.ds(pl.multiple_of(kt * tk, tk), tk), g * LANES:(g + 1) * LANES]

    ones_rows = jnp.ones((V_ROWS - HEAD_DIM, tk), BF16)

    def val_tile(v_ref, kt, g):
        v = v_ref[g * HEAD_DIM:(g + 1) * HEAD_DIM, pl.ds(pl.multiple_of(kt * tk, tk), tk)]
        return jnp.concatenate([v.astype(BF16), ones_rows], axis=0)

    wide = GQA_R * tq
    init = (jnp.full((1, wide), NEG, F32), jnp.zeros((V_ROWS, wide), F32))

    def win_scores(back, g):
        kt = jnp.maximum(kd - back, 0)
        qw = q4[g] if back == 0 else group_queries(g, None, jnp.where(kd >= back, 0.0, NEG))
        s = _dot(key_tile(kw_ref, kt, g), qw)
        if back in (0, WIN_TILES - 1):
            s = s + heads_of(wb_ref[min(back, 1)])
        return s

    groups = range(N_KV)
    s_diag = [_dot(key_tile(ks_ref, kd, g), q4_sel[g]) + heads_of(wb_ref[0]) for g in groups]
    s_win = [[win_scores(back, g) for g in groups] for back in range(WIN_TILES)]
    sel = tuple(_attn_update(s_diag[g], val_tile(vs_ref, kd, g), init) for g in groups)
    win = [init] * N_KV
    for back in range(WIN_TILES):
        kt = jnp.maximum(kd - back, 0)
        win = [_attn_update(s_win[back][g], val_tile(vw_ref, kt, g), win[g]) for g in groups]

    def sel_tiles(kts, states):
        s = [[_dot(key_tile(ks_ref, kt, g), q4_sel[g]) for g in groups] for kt in kts]
        for i, kt in enumerate(kts):
            states = tuple(_attn_update(s[i][g], val_tile(vs_ref, kt, g), states[g]) for g in groups)
        return states

    sel = lax.cond((kd & 1) == 1, lambda st: sel_tiles((kd - 1,), st), lambda st: st, sel)
    sel = lax.fori_loop(0, kd >> 1, lambda j, st: sel_tiles((2 * j, 2 * j + 1), st), sel)

    gates = gates_ref[...]
    outs = []
    for h in range(N_HEADS):
        g, r = h // GQA_R, h % GQA_R
        cols = slice(r * tq, (r + 1) * tq)
        o_sel = sel[g][1][:HEAD_DIM, cols] * (1.0 / sel[g][1][HEAD_DIM:HEAD_DIM + 1, cols])
        o_win = win[g][1][:HEAD_DIM, cols] * (1.0 / win[g][1][HEAD_DIM:HEAD_DIM + 1, cols])
        outs.append(gates[h:h + 1, :] * o_cmp[g][:, cols] + gates[N_HEADS + h:N_HEADS + h + 1, :] * o_sel
                    + gates[2 * N_HEADS + h:2 * N_HEADS + h + 1, :] * o_win)
    o_ref[...] = jnp.concatenate(outs, axis=0).T.astype(BF16)


def _attn_prompt(q_t, k_sel, k_win, kvs_t, kvw_t, kc_pad, vc_t, gates_t):
    n, _, t = q_t.shape
    nb = kc_pad.shape[2]
    row = lambda a, i: (a, i, 0)
    seq = lambda a, i: (a, 0, 0)
    seq4 = lambda a, i: (a, 0, 0, 0)
    v_rows = lambda a, i: (a, 1, 0)

    def win_bias(back):
        dist = back * ATT_TK + jnp.arange(ATT_TQ)[None, :] - jnp.arange(ATT_TK)[:, None]
        return jnp.where((dist >= 0) & (dist < WINDOW), 0.0, NEG).astype(F32)

    wb = jnp.stack([win_bias(0), win_bias(WIN_TILES - 1)])
    return pl.pallas_call(
        _attn_prompt_body,
        grid=(n, t // ATT_TQ),
        in_specs=[
            pl.BlockSpec((None, D_ATTN, ATT_TQ), lambda a, i: (a, 0, i)),
            pl.BlockSpec((None, t, KV_W), seq),
            pl.BlockSpec((None, t, KV_W), seq),
            pl.BlockSpec((None, N_KV * HEAD_DIM, t), v_rows),
            pl.BlockSpec((None, N_KV * HEAD_DIM, t), v_rows),
            pl.BlockSpec((None, N_KV, nb, LANES), seq4),
            pl.BlockSpec((None, N_KV, HEAD_DIM, nb), seq4),
            pl.BlockSpec((None, GATE_ROWS, ATT_TQ), lambda a, i: (a, 0, i)),
            pl.BlockSpec((2, ATT_TK, ATT_TQ), lambda a, i: (0, 0, 0)),
        ],
        out_specs=pl.BlockSpec((None, ATT_TQ, D_ATTN), row),
        out_shape=jax.ShapeDtypeStruct((n, t, D_ATTN), BF16),
        compiler_params=_cparams(("arbitrary", "arbitrary")),
        name="attn_prompt",
    )(q_t, k_sel, k_win, kvs_t, kvw_t, kc_pad, vc_t, gates_t, wb)


DEC_SEQ_PER_STEP = 8


def _dec_cmp_body(q2_ref, kc_ref, pt_ref, ocmp_ref, idx_ref, page_ref, p_buf, imp_buf, *, past, s_new):
    assert BPP == 2
    ns = q2_ref.shape[0]
    qrows = s_new * GQA_R
    n_pages = kc_ref.shape[2] // N_KV
    nb = n_pages * BPP
    width = imp_buf.shape[1]
    imp_buf[...] = jnp.zeros(imp_buf.shape, F32)
    lane = lax.broadcasted_iota(jnp.int32, (1, width), 1)
    jc = jnp.where(lane < n_pages, BPP * lane, BPP * (lane - n_pages) + 1)
    jc = jnp.where(lane == nb, nb, jc)

    row2 = lax.broadcasted_iota(jnp.int32, (2 * qrows, 1), 0)
    half = jnp.where(row2 >= qrows, 1, 0)
    blk = BPP * lax.broadcasted_iota(jnp.int32, (1, n_pages), 1) + half
    tok = (row2 & (qrows - 1)) >> (GQA_R.bit_length() - 1)
    vis = ((blk + 1) * BLK - 1) <= (past + tok)

    def both(x):
        return jnp.concatenate([x, x], axis=0)

    def pages(a, c, g):
        return kc_ref[a, c, g * n_pages:(g + 1) * n_pages, :].astype(BF16)

    pairs = [(a, g) for a in range(ns) for g in range(N_KV)]
    scores = [_dot_nt(q2_ref[a, g], pages(a, 0, g)) for a, g in pairs]
    for i, (a, g) in enumerate(pairs):
        s = jnp.where(vis, scores[i], NEG)
        m = jnp.max(s, axis=-1, keepdims=True)
        e = jnp.where(vis, jnp.exp(s - both(jnp.maximum(m[:qrows], m[qrows:]))), 0.0)
        l = jnp.sum(e, axis=-1, keepdims=True)
        l = l[:qrows] + l[qrows:]
        p = e / both(jnp.where(l > 0.0, l, 1.0))
        o2 = _dot(p.astype(BF16), pages(a, 1, g))
        ocmp_ref[a, g] = o2[:qrows, :HEAD_DIM] + o2[qrows:, HEAD_DIM:]
        p_buf[i] = p
        imp = sum(p_buf[i, pl.ds(r, 2 * s_new, stride=GQA_R), :] for r in range(GQA_R))
        row0 = i * s_new
        imp_buf[row0:row0 + s_new, 0:n_pages] = imp[:s_new]
        imp_buf[row0:row0 + s_new, n_pages:nb] = imp[s_new:]

    rows = ns * N_KV * s_new
    imp = imp_buf[...]
    t_row = lax.broadcasted_iota(jnp.int32, (rows, 1), 0) & (s_new - 1)
    cur = (past + t_row) >> BLK_SHIFT
    score = jnp.where(jc > cur, NEG, imp)
    score = jnp.where((jc == 0) | (jc == cur) | (jc == cur - 1), FORCE, score)
    score = jnp.where(lane <= nb, score, -jnp.inf)
    jcf = jc.astype(F32)
    out_lane = lax.broadcasted_iota(jnp.int32, (1, LANES), 1)
    idx = jnp.zeros((rows, LANES), jnp.int32)
    for k in range(N_SEL):
        m = jnp.max(score, axis=-1, keepdims=True)
        first = jnp.min(jnp.where(score == m, jcf, 1e9), axis=-1, keepdims=True)
        idx = jnp.where(out_lane == k, first.astype(jnp.int32), idx)
        score = jnp.where(jcf == first, -jnp.inf, score)
    idx_ref[...] = idx
    per_seq = N_KV * s_new
    pt_rows = jnp.concatenate([jnp.broadcast_to(pt_ref[a:a + 1, :], (per_seq, n_pages)) for a in range(ns)],
                              axis=0)
    page_ref[...] = jnp.take_along_axis(pt_rows, jnp.minimum(idx, nb - 1) >> (BPP.bit_length() - 1), axis=1)


def _dec_cmp(q2, kc, page_table, past, s_new):
    nseq = q2.shape[0]
    qrows = s_new * GQA_R
    rows_kc = kc.shape[2]
    n_pages = rows_kc // N_KV
    assert n_pages == LANES
    nb = n_pages * BPP
    ns = DEC_SEQ_PER_STEP
    width = (nb + 1 + LANES - 1) // LANES * LANES
    seq4 = lambda i: (i, 0, 0, 0)
    rows2 = lambda i: (i, 0)
    idx_shape = jax.ShapeDtypeStruct((nseq * N_KV * s_new, LANES), jnp.int32)
    return pl.pallas_call(
        functools.partial(_dec_cmp_body, past=past, s_new=s_new),
        grid=(nseq // ns,),
        in_specs=[
            pl.BlockSpec((ns, N_KV, 2 * qrows, 2 * HEAD_DIM), seq4),
            pl.BlockSpec((ns, 2, rows_kc, PAGE_SIZE), seq4),
            pl.BlockSpec((ns, n_pages), rows2),
        ],
        out_specs=[
            pl.BlockSpec((ns, N_KV, qrows, HEAD_DIM), seq4),
            pl.BlockSpec((ns * N_KV * s_new, LANES), rows2),
            pl.BlockSpec((ns * N_KV * s_new, LANES), rows2),
        ],
        out_shape=[jax.ShapeDtypeStruct((nseq, N_KV, qrows, HEAD_DIM), F32), idx_shape, idx_shape],
        scratch_shapes=[pltpu.VMEM((ns * N_KV, 2 * qrows, n_pages), F32),
                        pltpu.VMEM((ns * N_KV * s_new, width), F32)],
        compiler_params=_cparams(("arbitrary",)),
        name="dec_cmp_select",
    )(q2, kc, page_table)


def _dec_sel_win_body(page_ref, q_ref, idx2_ref, expand_ref, pool_ref, newblk_ref, win_ref,
                      kvw_new_ref, osel_ref, owin_ref, winout_ref, kvbuf, sem, *, past, s_new):
    n_past_blk = past // BLK
    n = pl.program_id(0)
    nseq = pl.num_programs(0)
    slot = n % 2
    rows = s_new * GQA_R
    seg_lanes = N_SEL * PAGE_SIZE
    nkeys = s_new * seg_lanes

    def block_copy(seq, c, k, sl):
        g, t = c // s_new, c % s_new
        page = page_ref[(seq * N_KV * s_new + c) * N_SEL + k]
        dst = kvbuf.at[sl, :, g, :, pl.ds((t * N_SEL + k) * PAGE_SIZE, PAGE_SIZE)]
        return pltpu.make_async_copy(pool_ref.at[page, :, g], dst, sem.at[sl])

    def start(seq, sl):
        for c in range(N_KV * s_new):
            for k in range(N_SEL):
                block_copy(seq, c, k, sl).start()

    @pl.when(n == 0)
    def _():
        start(n, slot)

    @pl.when(n + 1 < nseq)
    def _():
        start(n + 1, 1 - slot)

    for c in range(N_KV * s_new):
        for k in range(N_SEL):
            block_copy(n, c, k, slot).wait()

    assert win_ref.shape[1] == min(WINDOW, past + s_new)
    slid = jnp.concatenate([win_ref[...], kvw_new_ref[...]], axis=1)
    winout_ref[...] = slid[:, s_new:s_new + win_ref.shape[1]]

    t_row = lax.broadcasted_iota(jnp.int32, (rows, 1), 0) >> (GQA_R.bit_length() - 1)
    qpos = past + t_row
    lane = lax.broadcasted_iota(jnp.int32, (1, nkeys), 1)
    own = (lane >> (seg_lanes.bit_length() - 1)) == t_row
    row_in_page = lane & (PAGE_SIZE - 1)
    nlane = lax.broadcasted_iota(jnp.int32, (1, PAGE_SIZE), 1)
    nwin = win_ref.shape[1]
    wpos = past - nwin + lax.broadcasted_iota(jnp.int32, (1, nwin), 1)
    groups = range(N_KV)

    def krows(g):
        return pl.ds(g * HEAD_DIM, HEAD_DIM)

    def vrows(g):
        return pl.ds((N_KV + g) * HEAD_DIM, HEAD_DIM)

    q = [q_ref[g] for g in groups]
    jv2 = _dot(idx2_ref[...], expand_ref[...])
    s_sel = [_dot(q[g], kvbuf[slot, 0, g].astype(BF16)) for g in groups]
    s_new_sel = [_dot(q[g], newblk_ref[krows(g), :].astype(BF16)) for g in groups]
    s_win = [_dot(q[g], win_ref[krows(g), :].astype(BF16)) for g in groups]
    s_new_win = [_dot(q[g], kvw_new_ref[krows(g), :].astype(BF16)) for g in groups]

    def two_piece_softmax(sa, va, sb, vb):
        m = jnp.maximum(jnp.max(sa, axis=-1, keepdims=True), jnp.max(sb, axis=-1, keepdims=True))
        ea, eb = jnp.exp(sa - m), jnp.exp(sb - m)
        l = jnp.sum(ea, axis=-1, keepdims=True) + jnp.sum(eb, axis=-1, keepdims=True)
        return (_dot_nt(ea.astype(BF16), va) + _dot_nt(eb.astype(BF16), vb)) / l

    for g in groups:
        jv = jv2[g:g + 1, :].astype(jnp.int32)
        kpos = jv * BLK + (row_in_page & (BLK - 1))
        valid = (own & (jv < n_past_blk) & ((row_in_page >> BLK_SHIFT) == (jv & (BPP - 1)))
                 & (kpos <= qpos))
        has_new = jnp.max(jnp.where(own & (jv >= n_past_blk), 1.0, 0.0), axis=-1, keepdims=True)
        valid_n = (has_new > 0.5) & (nlane < BLK) & (n_past_blk * BLK + nlane <= qpos)
        osel_ref[g] = two_piece_softmax(
            jnp.where(valid, s_sel[g], NEG), kvbuf[slot, 1, g].astype(BF16),
            jnp.where(valid_n, s_new_sel[g], NEG), newblk_ref[vrows(g), :].astype(BF16))

        dw = qpos - wpos
        dn = qpos - (past + nlane)
        valid_w = (dw >= 0) & (dw < WINDOW) & (wpos >= 0)
        valid_n = (dn >= 0) & (dn < WINDOW) & (nlane < s_new)
        owin_ref[g] = two_piece_softmax(
            jnp.where(valid_w, s_win[g], NEG), win_ref[vrows(g), :].astype(BF16),
            jnp.where(valid_n, s_new_win[g], NEG), kvw_new_ref[vrows(g), :].astype(BF16))


def _dec_sel_win(page_flat, q_g, idx2, pool_t, newblk_t, win_t, kvw_new_t, past, s_new):
    nseq = q_g.shape[0]
    rows = s_new * GQA_R
    nwin = win_t.shape[2]
    nkeys = s_new * N_SEL * PAGE_SIZE
    n_phys = pool_t.shape[0]
    expand = (jnp.arange(nkeys)[None, :] // PAGE_SIZE == jnp.arange(s_new * N_SEL)[:, None]).astype(BF16)
    per_seq = lambda a, pg: (a, 0, 0, 0)
    per_seq3 = lambda a, pg: (a, 0, 0)
    grid_spec = pltpu.PrefetchScalarGridSpec(
        num_scalar_prefetch=1,
        grid=(nseq,),
        in_specs=[
            pl.BlockSpec((None, N_KV, rows, HEAD_DIM), per_seq),
            pl.BlockSpec((None, rows, s_new * N_SEL), per_seq3),
            pl.BlockSpec((s_new * N_SEL, nkeys), lambda a, pg: (0, 0)),
            pl.BlockSpec(memory_space=pl.ANY),
            pl.BlockSpec((None, KV_W, PAGE_SIZE), per_seq3),
            pl.BlockSpec((None, KV_W, nwin), per_seq3),
            pl.BlockSpec((None, KV_W, PAGE_SIZE), per_seq3),
        ],
        out_specs=[
            pl.BlockSpec((None, N_KV, rows, HEAD_DIM), per_seq),
            pl.BlockSpec((None, N_KV, rows, HEAD_DIM), per_seq),
            pl.BlockSpec((None, KV_W, nwin), per_seq3),
        ],
        scratch_shapes=[pltpu.VMEM((2, 2, N_KV, HEAD_DIM, nkeys), F32),
                        pltpu.SemaphoreType.DMA((2,))],
    )
    shape = jax.ShapeDtypeStruct((nseq, N_KV, rows, HEAD_DIM), F32)
    return pl.pallas_call(
        functools.partial(_dec_sel_win_body, past=past, s_new=s_new),
        grid_spec=grid_spec,
        out_shape=[shape, shape, jax.ShapeDtypeStruct(win_t.shape, F32)],
        compiler_params=_cparams(("arbitrary",)),
        name="dec_sel_win",
    )(page_flat, q_g, idx2, expand, pool_t.reshape(n_phys, 2, N_KV, HEAD_DIM, PAGE_SIZE),
      newblk_t, win_t, kvw_new_t)


def _out_ln_body(*refs, gated):
    if gated:
        ocmp_ref, osel_ref, owin_ref, gates_ref, yconv_ref, h_ref, wo_ref, g_ref, b_ref, y_ref = refs
        gates = gates_ref[...]
        pieces = []
        for h in range(N_HEADS):
            col = slice(h * HEAD_DIM, (h + 1) * HEAD_DIM)
            pieces.append(gates[:, h:h + 1] * ocmp_ref[:, col]
                          + gates[:, N_HEADS + h:N_HEADS + h + 1] * osel_ref[:, col]
                          + gates[:, 2 * N_HEADS + h:2 * N_HEADS + h + 1] * owin_ref[:, col])
        o = jnp.concatenate(pieces, axis=-1)
    else:
        o_ref, yconv_ref, h_ref, wo_ref, g_ref, b_ref, y_ref = refs
        o = o_ref[...]
    mixed = jnp.concatenate([o.astype(BF16), yconv_ref[...].astype(BF16)], axis=-1)
    y_ref[...] = _layer_norm(ALPHA * h_ref[...] + _dot(mixed, wo_ref[...]), g_ref[...], b_ref[...])


def _out_ln(branches, yconv, h2d, w_out, g, b, tm):
    rows = h2d.shape[0]
    gated = len(branches) > 1
    row = lambda i: (i, 0)
    const = lambda i: (0, 0)
    widths = [D_ATTN, D_ATTN, D_ATTN, LANES] if gated else [D_ATTN]
    return pl.pallas_call(
        functools.partial(_out_ln_body, gated=gated),
        grid=(rows // tm,),
        in_specs=[pl.BlockSpec((tm, w), row) for w in widths] + [
            pl.BlockSpec((tm, D_CONV), row),
            pl.BlockSpec((tm, D_MODEL), row),
            pl.BlockSpec((D_MODEL, D_MODEL), const),
            pl.BlockSpec((1, D_MODEL), const),
            pl.BlockSpec((1, D_MODEL), const),
        ],
        out_specs=pl.BlockSpec((tm, D_MODEL), row),
        out_shape=jax.ShapeDtypeStruct((rows, D_MODEL), F32),
        compiler_params=_cparams(("arbitrary",)),
        name="out_ln",
    )(*branches, yconv, h2d, w_out, g, b)


def _block_diag_groups(w):
    z = jnp.zeros_like(w)
    return jnp.concatenate([jnp.concatenate([w, z], axis=-1),
                            jnp.concatenate([z, w], axis=-1)], axis=-2).astype(BF16)


def _to_group_layout(x, nseq, s_new):
    x = x.reshape(nseq, s_new, N_KV, GQA_R, HEAD_DIM).transpose(0, 2, 1, 3, 4)
    return x.reshape(nseq, N_KV, s_new * GQA_R, HEAD_DIM)


def _from_group_layout(x, nseq, s_new):
    x = x.reshape(nseq, N_KV, s_new, GQA_R, HEAD_DIM).transpose(0, 2, 1, 3, 4)
    return x.reshape(nseq * s_new, D_ATTN)


def kernel(x_prompt, x_sample, cache_cmp_kv, cache_sel_kv, state_win_kv, state_conv, page_table,
           ffa_gate, ffa_up, ffa_down, ln1_g, ln1_b, w_in, b_in, w_conv, cmp_w1, cmp_w2, cmp_pe,
           w_out, ln2_g, ln2_b, ffb_gate, ffb_up, ffb_down, ln3_g, ln3_b):
    depth = ffa_gate.shape[0]
    assert depth == 1
    l = 0
    n, t, d = x_prompt.shape
    nseq, s_new, _ = x_sample.shape
    n_pages = page_table.shape[1]
    past = n_pages * PAGE_SIZE
    rows_p, rows_s = n * t, nseq * s_new
    kvshape = (2, N_KV, HEAD_DIM)
    assert d == D_MODEL and rows_p % FFN_ROW_TILE == 0 and t % PROJ_ROW_TILE == 0 and t % ATT_TQ == 0
    assert t // BLK <= AUX_BLOCKS and n % CMP_SEQS_PER_STEP == 0 and nseq % DEC_SEQ_PER_STEP == 0
    assert s_new * GQA_R == BF16_SUBLANES and s_new <= BLK and past % BLK == 0
    assert state_win_kv.shape[2] == WINDOW

    bf = lambda w: w.astype(BF16)
    vec = lambda v: v.reshape(1, -1)
    w_main = bf(w_in[l][:, :D_MAIN])
    b_main = vec(b_in[l][:D_MAIN])
    w_gate = bf(jnp.pad(w_in[l][:, D_MAIN:], ((0, 0), (0, LANES - N_GATE))))
    b_gate = vec(jnp.pad(b_in[l][D_MAIN:], (0, LANES - N_GATE)))
    pe_t = jnp.concatenate([cmp_pe[l].transpose(1, 2, 0)] * BPP, axis=-1)
    w1_t = _block_diag_groups(cmp_w1[l].transpose(0, 2, 1, 3)).reshape(
        2, D_PAIRS, 2 * PAGE_SIZE, PAGE_SIZE)
    w2_t = _block_diag_groups(cmp_w2[l])
    w_out_b = bf(w_out[l])
    ffa = (bf(ffa_gate[l]), bf(ffa_up[l]), bf(0.5 * ffa_down[l]), vec(ln1_g[l]), vec(ln1_b[l]))
    ffb = (bf(ffb_gate[l]), bf(ffb_up[l]), bf(0.5 * ffb_down[l]), vec(ln3_g[l]), vec(ln3_b[l]))

    def prompt_cols(x):
        return jnp.concatenate([x[..., :_C_Q], x[..., _C_KVS:_C_KVS + K_HALF],
                                x[..., _C_KVW:_C_KVW + K_HALF]], axis=-1)

    def prompt_rows(x):
        return jnp.pad(x[..., _C_Q:], [(0, 0)] * (x.ndim - 1) + [(0, GATE_ROWS - N_GATE)])

    w_rows, b_rows = bf(prompt_cols(w_in[l])), vec(prompt_cols(b_in[l]))
    w_tr, b_tr = bf(prompt_rows(w_in[l]).T), prompt_rows(b_in[l]).reshape(-1, 1)

    pps = t // PAGE_SIZE
    cmp_seqs = CMP_SEQS_PER_STEP
    hp = _ffn_ln(x_prompt.reshape(rows_p, d), *ffa, tm=FFN_ROW_TILE)
    q, k_sel, k_win, kvc_t, kvs_t, kvw_t, gates_t, yconv, cstate = _proj_prompt(
        hp.reshape(n, t, d), w_rows, b_rows, w_tr, b_tr, w_conv[l], tm=PROJ_ROW_TILE)
    kc = _compress(jnp.zeros((1,), jnp.int32), kvc_t, pe_t, w1_t, w2_t, n_steps=n // cmp_seqs,
                   n_pages=cmp_seqs * pps, pages_per_seq=pps, name="compress_prompt")
    kc = kc.reshape(n // cmp_seqs, 2, N_KV, cmp_seqs, pps, BPP, HEAD_DIM)
    kc = kc.transpose(0, 3, 2, 1, 4, 5, 6).reshape(n, N_KV, 2, pps * BPP, HEAD_DIM)
    zc = jnp.zeros_like(kc[:, 0, 0])
    kc_pad = bf(jnp.stack([jnp.concatenate([kc[:, 0, 0], zc], axis=-1),
                           jnp.concatenate([zc, kc[:, 1, 0]], axis=-1)], axis=1))
    vc_t = bf(jnp.swapaxes(kc[:, :, 1], -1, -2))
    o = _attn_prompt(q, k_sel, k_win, kvs_t, kvw_t, kc_pad, vc_t, gates_t)
    yp = _out_ln([o.reshape(rows_p, D_ATTN)], yconv.reshape(rows_p, D_CONV), hp, w_out_b,
                 vec(ln2_g[l]), vec(ln2_b[l]), tm=OUT_ROW_TILE)
    yp = _ffn_ln(yp, *ffb, tm=FFN_ROW_TILE).reshape(n, t, d)

    def from_feature_major(x_t):
        return jnp.moveaxis(x_t.reshape(x_t.shape[0], *kvshape, x_t.shape[-1]), -1, 1)[None]

    p_cmp = from_feature_major(kvc_t)
    p_sel = from_feature_major(kvs_t)
    p_win = from_feature_major(kvw_t[:, :, t - min(WINDOW, t):])
    p_conv = cstate[:, SUBLANES - (CONV_W - 1):].reshape(1, n, CONV_W - 1, D_CONV)

    hs = _ffn_ln(x_sample.reshape(rows_s, d), *ffa, tm=rows_s)
    qs, kv3, gates_s, cb_s, u_s = _proj_sample(hs, w_main, b_main, w_gate, b_gate)
    kvc_s, kvs_s, kvw_s = kv3[:, :KV_W], kv3[:, KV_W:2 * KV_W], kv3[:, 2 * KV_W:]
    u3 = u_s.reshape(nseq, s_new, D_CONV)
    yconv_s = _conv_sample(cb_s.reshape(nseq, s_new, D_CONV), u3, state_conv[l], w_conv[l])
    s_conv = jnp.concatenate([state_conv[l], u3], axis=1)[:, -(CONV_W - 1):]

    def feature_major(x):
        return jnp.moveaxis(x, -4, -1).reshape(*x.shape[:-4], KV_W, x.shape[-4])

    def new_rows_t(x2d):
        xt = x2d.reshape(nseq, s_new, KV_W).transpose(0, 2, 1)
        return jnp.pad(xt, ((0, 0), (0, 0), (0, PAGE_SIZE - s_new)))

    kc_s = _compress(page_table.reshape(-1), feature_major(cache_cmp_kv[l]), pe_t, w1_t, w2_t,
                     n_steps=nseq, n_pages=n_pages, pages_per_seq=None, name="compress_decode")
    q_g = _to_group_layout(qs, nseq, s_new)
    zq = jnp.zeros_like(q_g)
    q2 = jnp.concatenate([jnp.concatenate([q_g, zq], axis=-1), jnp.concatenate([zq, q_g], axis=-1)], axis=-2)
    ocmp_c, idx_pad, page_pad = _dec_cmp(q2, kc_s, page_table, past, s_new)
    idx = idx_pad[:, :N_SEL]
    idx2 = jnp.pad(bf(idx.reshape(nseq, N_KV, s_new * N_SEL)), ((0, 0), (0, s_new * GQA_R - N_KV), (0, 0)))
    win_t = feature_major(state_win_kv[l])
    kvw_new_t = new_rows_t(kvw_s)
    osel_c, owin_c, win_next_t = _dec_sel_win(
        page_pad[:, :N_SEL].reshape(-1), q_g, idx2, feature_major(cache_sel_kv[l]),
        new_rows_t(kvs_s), win_t, kvw_new_t, past, s_new)
    branches = [_from_group_layout(x, nseq, s_new) for x in (ocmp_c, osel_c, owin_c)]
    ys = _out_ln(branches + [gates_s], yconv_s.reshape(rows_s, D_CONV), hs, w_out_b,
                 vec(ln2_g[l]), vec(ln2_b[l]), tm=rows_s)
    ys = _ffn_ln(ys, *ffb, tm=rows_s).reshape(nseq, s_new, d)
    s_cmp = kvc_s.reshape(1, nseq, s_new, *kvshape)
    s_sel = kvs_s.reshape(1, nseq, s_new, *kvshape)
    s_win = from_feature_major(win_next_t)
    return (yp, ys, p_cmp, p_sel, p_win, p_conv, s_cmp, s_sel, s_win,
            s_conv.reshape(1, nseq, CONV_W - 1, D_CONV))
```

```python
import functools

import jax
import jax.numpy as jnp
from jax import lax
from jax.experimental import pallas as pl
from jax.experimental.pallas import tpu as pltpu

F32 = jnp.float32
BF16 = jnp.bfloat16

D_MODEL = 1024
D_CONV = 512
D_ATTN = 512
HEAD_DIM = 64
N_HEADS = 8
N_KV = 2
GQA_R = N_HEADS // N_KV
N_BR = 3
CONV_W = 3
BLK = 64
BLK_SHIFT = BLK.bit_length() - 1
N_SEL = 16
WINDOW = 512
PAGE_SIZE = 128
D_FF = 2816
KV_W = 2 * N_KV * HEAD_DIM
D_MAIN = 3 * D_CONV + D_ATTN + 3 * KV_W
N_GATE = N_BR * N_HEADS
ALPHA = 2.0 ** 0.25
SCALE = HEAD_DIM ** -0.5
LN_EPS = 1e-5
NEG = -1e30
FORCE = 1e4

LANES = 128
SUBLANES = 8
BF16_SUBLANES = 2 * SUBLANES
V7X_VMEM_BYTES = 64 * 1024 * 1024
VMEM_LIMIT_BYTES = V7X_VMEM_BYTES * 7 // 8

FFN_ROW_TILE = 512
PROJ_ROW_TILE = 1024
OUT_ROW_TILE = 1024
CMP_SEQS_PER_STEP = 4


def _cparams(sem):
    return pltpu.CompilerParams(dimension_semantics=sem, vmem_limit_bytes=VMEM_LIMIT_BYTES)


def _layer_norm(y, g, b):
    mu = jnp.mean(y, axis=-1, keepdims=True)
    yc = y - mu
    var = jnp.mean(yc * yc, axis=-1, keepdims=True)
    return yc * lax.rsqrt(var + LN_EPS) * g + b


def _dot(a, b):
    return jnp.dot(a, b, preferred_element_type=F32)


def _dot_nt(a, b):
    return lax.dot_general(a, b, (((1,), (1,)), ((), ())), preferred_element_type=F32)


FFN_CHUNK = 256


def _ffn_ln_body(x_ref, wg_ref, wu_ref, wd_ref, g_ref, b_ref, o_ref):
    x = x_ref[...]
    xb = x.astype(BF16)
    acc = jnp.zeros(x.shape, F32)
    for c in range(D_FF // FFN_CHUNK):
        sl = slice(c * FFN_CHUNK, (c + 1) * FFN_CHUNK)
        gg = _dot(xb, wg_ref[:, sl].astype(BF16))
        uu = _dot(xb, wu_ref[:, sl].astype(BF16))
        hh = (gg * jax.nn.sigmoid(gg) * uu).astype(BF16)
        acc = acc + _dot(hh, wd_ref[sl, :].astype(BF16))
    o_ref[...] = _layer_norm(ALPHA * x + 0.5 * acc, g_ref[...], b_ref[...])


def _ffn_ln(x2d, wg, wu, wd, g, b, tm):
    rows = x2d.shape[0]
    const = lambda i: (0, 0)
    return pl.pallas_call(
        _ffn_ln_body,
        grid=(rows // tm,),
        in_specs=[
            pl.BlockSpec((tm, D_MODEL), lambda i: (i, 0)),
            pl.BlockSpec((D_MODEL, D_FF), const, pipeline_mode=pl.Buffered(1)),
            pl.BlockSpec((D_MODEL, D_FF), const, pipeline_mode=pl.Buffered(1)),
            pl.BlockSpec((D_FF, D_MODEL), const, pipeline_mode=pl.Buffered(1)),
            pl.BlockSpec((1, D_MODEL), const),
            pl.BlockSpec((1, D_MODEL), const),
        ],
        out_specs=pl.BlockSpec((tm, D_MODEL), lambda i: (i, 0)),
        out_shape=jax.ShapeDtypeStruct((rows, D_MODEL), F32),
        compiler_params=_cparams(("arbitrary",)),
        name="ffn_ln",
    )(x2d, wg, wu, wd, g, b)


_C_CB, _C_CC, _C_CH, _C_Q = 0, D_CONV, 2 * D_CONV, 3 * D_CONV
_C_KVC = 3 * D_CONV + D_ATTN
_C_KVS = _C_KVC + KV_W
_C_KVW = _C_KVS + KV_W


def _proj_piece(hb, w_ref, b_ref, lo, width):
    return _dot(hb, w_ref[:, lo:lo + width]) + b_ref[:, lo:lo + width]


_P_K = 3 * D_CONV
K_HALF = N_KV * HEAD_DIM
P_ROW_COLS = _P_K + 2 * K_HALF
GATE_ROWS = 32
_T_KV = D_ATTN
_T_GATE = _T_KV + 3 * KV_W
P_T_ROWS = _T_GATE + GATE_ROWS


def _proj_prompt_body(h_ref, w_ref, b_ref, wt_ref, bt_ref, wc_ref,
                      qt_ref, ks_ref, kw_ref, kvct_ref, kvst_ref, kvwt_ref, gates_ref,
                      yconv_ref, cstate_ref, ubuf):
    i = pl.program_id(1)
    tm = h_ref.shape[0]
    hb = h_ref[...].astype(BF16)

    def piece_t(lo, height):
        return _dot_nt(wt_ref[lo:lo + height, :], hb) + bt_ref[lo:lo + height, :]

    qt_ref[...] = (piece_t(0, D_ATTN) * (SCALE * LOG2E)).astype(BF16)
    lane = lax.broadcasted_iota(jnp.int32, (1, KV_W), 1)
    blk = (i * tm + lax.broadcasted_iota(jnp.int32, (tm, 1), 0)) >> BLK_SHIFT
    aux0, aux1 = HEAD_DIM, 2 * HEAD_DIM
    one = (lane == aux0 + AUX_BLOCKS) | (lane == aux1 + AUX_BLOCKS)
    is_k = (lane < aux0) | (lane >= aux1 + HEAD_DIM)
    k4 = _proj_piece(hb, w_ref, b_ref, _P_K, 2 * K_HALF)
    for ref, lo, ones in ((ks_ref, 0, one | (lane == aux0 + blk) | (lane == aux1 + blk)),
                          (kw_ref, K_HALF, one)):
        k = k4[:, lo:lo + K_HALF]
        ref[...] = jnp.where(is_k, jnp.concatenate([k, k], axis=1),
                             jnp.where(ones, 1.0, 0.0)).astype(BF16)
    for k, ref in enumerate((kvct_ref, kvst_ref, kvwt_ref)):
        ref[...] = piece_t(_T_KV + k * KV_W, KV_W)
    gates_ref[...] = jax.nn.sigmoid(piece_t(_T_GATE, GATE_ROWS))
    cb = _proj_piece(hb, w_ref, b_ref, _C_CB, D_CONV)
    u = _proj_piece(hb, w_ref, b_ref, _C_CC, D_CONV) * _proj_piece(hb, w_ref, b_ref, _C_CH, D_CONV)

    head = SUBLANES

    @pl.when(i == 0)
    def _():
        ubuf[0:head, :] = jnp.zeros((head, D_CONV), F32)

    ubuf[head:head + tm, :] = u
    u1 = ubuf[head - 1:head - 1 + tm, :]
    u2 = ubuf[head - 2:head - 2 + tm, :]
    z = u2 * wc_ref[0:1, :] + u1 * wc_ref[1:2, :] + u * wc_ref[2:3, :]
    yconv_ref[...] = cb * z
    tail = u[tm - head:tm, :]
    ubuf[0:head, :] = tail
    cstate_ref[...] = tail


def _proj_prompt(h, w_rows, b_rows, w_t, b_t, w_conv, tm):
    n, t, _ = h.shape
    const = lambda a, i: (0, 0)
    row = lambda a, i: (a, i, 0)
    col = lambda a, i: (a, 0, i)

    def out(width, dtype=F32):
        return jax.ShapeDtypeStruct((n, t, width), dtype)

    def out_t(rows, dtype=F32):
        return jax.ShapeDtypeStruct((n, rows, t), dtype)

    return pl.pallas_call(
        _proj_prompt_body,
        grid=(n, t // tm),
        in_specs=[
            pl.BlockSpec((None, tm, D_MODEL), row),
            pl.BlockSpec((D_MODEL, P_ROW_COLS), const),
            pl.BlockSpec((1, P_ROW_COLS), const),
            pl.BlockSpec((P_T_ROWS, D_MODEL), const),
            pl.BlockSpec((P_T_ROWS, 1), const),
            pl.BlockSpec((CONV_W, D_CONV), const),
        ],
        out_specs=[
            pl.BlockSpec((None, D_ATTN, tm), col),
            pl.BlockSpec((None, tm, KV_W), row),
            pl.BlockSpec((None, tm, KV_W), row),
            pl.BlockSpec((None, KV_W, tm), col),
            pl.BlockSpec((None, KV_W, tm), col),
            pl.BlockSpec((None, KV_W, tm), col),
            pl.BlockSpec((None, GATE_ROWS, tm), col),
            pl.BlockSpec((None, tm, D_CONV), row),
            pl.BlockSpec((None, SUBLANES, D_CONV), lambda a, i: (a, 0, 0)),
        ],
        out_shape=[out_t(D_ATTN, BF16), out(KV_W, BF16), out(KV_W, BF16), out_t(KV_W), out_t(KV_W),
                   out_t(KV_W), out_t(GATE_ROWS), out(D_CONV),
                   jax.ShapeDtypeStruct((n, SUBLANES, D_CONV), F32)],
        scratch_shapes=[pltpu.VMEM((tm + SUBLANES, D_CONV), F32)],
        compiler_params=_cparams(("arbitrary", "arbitrary")),
        name="proj_prompt",
    )(h, w_rows, b_rows, w_t, b_t, w_conv)


def _proj_sample_body(h_ref, w_ref, b_ref, wgt_ref, bgt_ref, q_ref, kv_ref, gates_ref, cb_ref, u_ref):
    hb = h_ref[...].astype(BF16)
    q_ref[...] = (_proj_piece(hb, w_ref, b_ref, _C_Q, D_ATTN) * SCALE).astype(BF16)
    kv_ref[...] = _proj_piece(hb, w_ref, b_ref, _C_KVC, 3 * KV_W)
    gates_ref[...] = jax.nn.sigmoid(_dot(hb, wgt_ref[...]) + bgt_ref[...])
    cb_ref[...] = _proj_piece(hb, w_ref, b_ref, _C_CB, D_CONV)
    u_ref[...] = _proj_piece(hb, w_ref, b_ref, _C_CC, D_CONV) * _proj_piece(hb, w_ref, b_ref, _C_CH, D_CONV)


def _proj_sample(h2d, w_main, b_main, w_gate, b_gate):
    rows = h2d.shape[0]

    def out(width, dtype=F32):
        return jax.ShapeDtypeStruct((rows, width), dtype)

    return pl.pallas_call(
        _proj_sample_body,
        out_shape=[out(D_ATTN, BF16), out(3 * KV_W), out(LANES), out(D_CONV), out(D_CONV)],
        compiler_params=pltpu.CompilerParams(vmem_limit_bytes=VMEM_LIMIT_BYTES),
        name="proj_sample",
    )(h2d, w_main, b_main, w_gate, b_gate)


def _conv_sample_body(cb_ref, u_ref, prev_ref, wc_ref, y_ref):
    s = u_ref.shape[1]
    ext = [prev_ref[:, 0, :], prev_ref[:, 1, :]] + [u_ref[:, t, :] for t in range(s)]
    for t in range(s):
        z = ext[t] * wc_ref[0:1, :] + ext[t + 1] * wc_ref[1:2, :] + ext[t + 2] * wc_ref[2:3, :]
        y_ref[:, t, :] = cb_ref[:, t, :] * z


def _conv_sample(cb, u, prev, w_conv):
    return pl.pallas_call(
        _conv_sample_body,
        out_shape=jax.ShapeDtypeStruct(u.shape, F32),
        name="conv_sample",
    )(cb, u, prev, w_conv)


BPP = PAGE_SIZE // BLK
D_PAIRS = HEAD_DIM // 2


def _compress_body(pt_ref, src_ref, pe_ref, w1_ref, w2_ref, kc_ref, buf, sem, *, n_pages, pages_per_seq):
    n = pl.program_id(0)
    slot = n % 2

    def page_copy(step, j, sl):
        dst = buf.at[sl, :, j, :]
        if pages_per_seq is None:
            src = src_ref.at[pt_ref[step * n_pages + j]]
        else:
            seq = step * (n_pages // pages_per_seq) + j // pages_per_seq
            src = src_ref.at[seq, :, pl.ds((j % pages_per_seq) * PAGE_SIZE, PAGE_SIZE)]
        return pltpu.make_async_copy(src, dst, sem.at[sl])

    def start(step, sl):
        for j in range(n_pages):
            page_copy(step, j, sl).start()

    @pl.when(n == 0)
    def _():
        start(n, slot)

    @pl.when(n + 1 < pl.num_programs(0))
    def _():
        start(n + 1, 1 - slot)

    for j in range(n_pages):
        page_copy(n, j, slot).wait()

    def feature_rows(f):
        return buf[slot, f]

    for c in range(2):
        acc = None
        for dp in range(D_PAIRS):
            rows = []
            for g in range(N_KV):
                f0 = (c * N_KV + g) * HEAD_DIM + 2 * dp
                rows.append(jnp.concatenate(
                    [feature_rows(f0) + pe_ref[c, 2 * dp:2 * dp + 1, :],
                     feature_rows(f0 + 1) + pe_ref[c, 2 * dp + 1:2 * dp + 2, :]], axis=-1))
            x = jnp.concatenate(rows, axis=0).astype(BF16)
            d = _dot(x, w1_ref[c, dp])
            acc = d if acc is None else acc + d
        hid = (acc * jax.nn.sigmoid(acc)).astype(BF16)
        kc_ref[c] = _dot(hid, w2_ref[c])


def _compress(pt_flat, src, pe_t, w1_t, w2_t, n_steps, n_pages, pages_per_seq, name):
    const3 = lambda a, pt: (0, 0, 0)
    grid_spec = pltpu.PrefetchScalarGridSpec(
        num_scalar_prefetch=1,
        grid=(n_steps,),
        in_specs=[
            pl.BlockSpec(memory_space=pl.ANY),
            pl.BlockSpec((2, HEAD_DIM, PAGE_SIZE), const3),
            pl.BlockSpec((2, D_PAIRS, 2 * PAGE_SIZE, PAGE_SIZE), lambda a, pt: (0, 0, 0, 0)),
            pl.BlockSpec((2, PAGE_SIZE, PAGE_SIZE), const3),
        ],
        out_specs=pl.BlockSpec((None, 2, N_KV * n_pages, PAGE_SIZE), lambda a, pt: (a, 0, 0, 0)),
        scratch_shapes=[pltpu.VMEM((2, KV_W, n_pages, PAGE_SIZE), F32),
                        pltpu.SemaphoreType.DMA((2,))],
    )
    return pl.pallas_call(
        functools.partial(_compress_body, n_pages=n_pages, pages_per_seq=pages_per_seq),
        grid_spec=grid_spec,
        out_shape=jax.ShapeDtypeStruct((n_steps, 2, N_KV * n_pages, PAGE_SIZE), F32),
        compiler_params=_cparams(("arbitrary",)),
        name=name,
    )(pt_flat, src, pe_t, w1_t, w2_t)


ATT_TQ = 256
ATT_TK = ATT_TQ
WIN_TILES = WINDOW // ATT_TK + 1
AUX_BLOCKS = 32
LOG2E = 1.4426950408889634


V_ROWS = HEAD_DIM + BF16_SUBLANES


def _attn_update(s_t, v_t, state):
    m, acc = state
    m_new = jnp.maximum(m, jnp.max(s_t, axis=0, keepdims=True))
    e = jnp.exp2(s_t - m_new)
    acc = jnp.exp2(m - m_new) * acc + _dot(v_t, e.astype(BF16))
    return m_new, acc


def _attn_prompt_body(qt_ref, ks_ref, kw_ref, vs_ref, vw_ref, kc_ref, vct_ref, gates_ref, wb_ref, o_ref):
    tq, tk = ATT_TQ, ATT_TK
    qi = pl.program_id(1)
    nb = kc_ref.shape[1]
    t0 = qi * tq
    kd = qi
    tpos = t0 + lax.broadcasted_iota(jnp.int32, (1, tq), 1)
    jrow = lax.broadcasted_iota(jnp.int32, (nb, 1), 0)
    vis = ((jrow + 1) * BLK - 1) <= tpos
    cur = tpos >> BLK_SHIFT

    def heads_of(x):
        return jnp.concatenate([x] * GQA_R, axis=1)

    def group_queries(g, block_bias=None, switch=None):
        zeros = jnp.zeros((AUX_BLOCKS, tq), BF16)
        top = zeros if block_bias is None else block_bias
        low = zeros if switch is None else jnp.broadcast_to(switch, (AUX_BLOCKS, tq)).astype(BF16)
        aux = jnp.concatenate([top, low], axis=0)
        pieces = []
        for r in range(GQA_R):
            h = g * GQA_R + r
            qh = qt_ref[h * HEAD_DIM:(h + 1) * HEAD_DIM, :]
            pieces.append(jnp.concatenate([qh, aux] if g == 0 else [aux, qh], axis=0))
        return jnp.concatenate(pieces, axis=1)

    q4 = [group_queries(g) for g in range(N_KV)]
    vis4 = heads_of(vis)

    o_cmp, q4_sel = [], []
    for g in range(N_KV):
        s = jnp.where(vis4, _dot(kc_ref[g], q4[g]), NEG)
        m = jnp.max(s, axis=0, keepdims=True)
        e = jnp.where(vis4, jnp.exp2(s - m), 0.0)
        l = jnp.sum(e, axis=0, keepdims=True)
        p = e / jnp.where(l > 0.0, l, 1.0)
        imp = sum(p[:, r * tq:(r + 1) * tq] for r in range(GQA_R))
        o_cmp.append(_dot(vct_ref[g], p.astype(BF16)))

        score = jnp.where(jrow > cur, NEG, imp)
        score = jnp.where((jrow == 0) | (jrow == cur) | (jrow == cur - 1), FORCE, score)
        rank = jnp.zeros((nb, tq), F32)
        for j2 in range(nb):
            row = score[j2:j2 + 1, :]
            tie = jnp.where(jrow > j2, 1.0, 0.0)
            rank = rank + jnp.where(row > score, 1.0, jnp.where(row == score, tie, 0.0))
        q4_sel.append(group_queries(g, jnp.where(rank < N_SEL, 0.0, NEG).astype(BF16)))

    def key_tile(k_ref, kt, g):
        return k_ref[pl.ds(pl.multiple_of(kt * tk, tk), tk), g * LANES:(g + 1) * LANES]

    ones_rows = jnp.ones((V_ROWS - HEAD_DIM, tk), BF16)

    def val_tile(v_ref, kt, g):
        v = v_ref[g * HEAD_DIM:(g + 1) * HEAD_DIM, pl.ds(pl.multiple_of(kt * tk, tk), tk)]
        return jnp.concatenate([v.astype(BF16), ones_rows], axis=0)

    wide = GQA_R * tq
    init = (jnp.full((1, wide), NEG, F32), jnp.zeros((V_ROWS, wide), F32))

    def win_scores(back, g):
        kt = jnp.maximum(kd - back, 0)
        qw = q4[g] if back == 0 else group_queries(g, None, jnp.where(kd >= back, 0.0, NEG))
        s = _dot(key_tile(kw_ref, kt, g), qw)
        if back in (0, WIN_TILES - 1):
            s = s + heads_of(wb_ref[min(back, 1)])
        return s

    groups = range(N_KV)
    s_diag = [_dot(key_tile(ks_ref, kd, g), q4_sel[g]) + heads_of(wb_ref[0]) for g in groups]
    s_win = [[win_scores(back, g) for g in groups] for back in range(WIN_TILES)]
    sel = tuple(_attn_update(s_diag[g], val_tile(vs_ref, kd, g), init) for g in groups)
    win = [init] * N_KV
    for back in range(WIN_TILES):
        kt = jnp.maximum(kd - back, 0)
        win = [_attn_update(s_win[back][g], val_tile(vw_ref, kt, g), win[g]) for g in groups]

    def sel_tiles(kts, states):
        s = [[_dot(key_tile(ks_ref, kt, g), q4_sel[g]) for g in groups] for kt in kts]
        for i, kt in enumerate(kts):
            states = tuple(_attn_update(s[i][g], val_tile(vs_ref, kt, g), states[g]) for g in groups)
        return states

    sel = lax.cond((kd & 1) == 1, lambda st: sel_tiles((kd - 1,), st), lambda st: st, sel)
    sel = lax.fori_loop(0, kd >> 1, lambda j, st: sel_tiles((2 * j, 2 * j + 1), st), sel)

    gates = gates_ref[...]
    outs = []
    for h in range(N_HEADS):
        g, r = h // GQA_R, h % GQA_R
        cols = slice(r * tq, (r + 1) * tq)
        o_sel = sel[g][1][:HEAD_DIM, cols] * (1.0 / sel[g][1][HEAD_DIM:HEAD_DIM + 1, cols])
        o_win = win[g][1][:HEAD_DIM, cols] * (1.0 / win[g][1][HEAD_DIM:HEAD_DIM + 1, cols])
        outs.append(gates[h:h + 1, :] * o_cmp[g][:, cols] + gates[N_HEADS + h:N_HEADS + h + 1, :] * o_sel
                    + gates[2 * N_HEADS + h:2 * N_HEADS + h + 1, :] * o_win)
    o_ref[...] = jnp.concatenate(outs, axis=0).T


def _attn_prompt(q_t, k_sel, k_win, kvs_t, kvw_t, kc_pad, vc_t, gates_t):
    n, _, t = q_t.shape
    nb = kc_pad.shape[2]
    row = lambda a, i: (a, i, 0)
    seq = lambda a, i: (a, 0, 0)
    seq4 = lambda a, i: (a, 0, 0, 0)
    v_rows = lambda a, i: (a, 1, 0)

    def win_bias(back):
        dist = back * ATT_TK + jnp.arange(ATT_TQ)[None, :] - jnp.arange(ATT_TK)[:, None]
        return jnp.where((dist >= 0) & (dist < WINDOW), 0.0, NEG).astype(F32)

    wb = jnp.stack([win_bias(0), win_bias(WIN_TILES - 1)])
    return pl.pallas_call(
        _attn_prompt_body,
        grid=(n, t // ATT_TQ),
        in_specs=[
            pl.BlockSpec((None, D_ATTN, ATT_TQ), lambda a, i: (a, 0, i)),
            pl.BlockSpec((None, t, KV_W), seq),
            pl.BlockSpec((None, t, KV_W), seq),
            pl.BlockSpec((None, N_KV * HEAD_DIM, t), v_rows),
            pl.BlockSpec((None, N_KV * HEAD_DIM, t), v_rows),
            pl.BlockSpec((None, N_KV, nb, LANES), seq4),
            pl.BlockSpec((None, N_KV, HEAD_DIM, nb), seq4),
            pl.BlockSpec((None, GATE_ROWS, ATT_TQ), lambda a, i: (a, 0, i)),
            pl.BlockSpec((2, ATT_TK, ATT_TQ), lambda a, i: (0, 0, 0)),
        ],
        out_specs=pl.BlockSpec((None, ATT_TQ, D_ATTN), row),
        out_shape=jax.ShapeDtypeStruct((n, t, D_ATTN), F32),
        compiler_params=_cparams(("arbitrary", "arbitrary")),
        name="attn_prompt",
    )(q_t, k_sel, k_win, kvs_t, kvw_t, kc_pad, vc_t, gates_t, wb)


DEC_SEQ_PER_STEP = 8


def _dec_cmp_body(q2_ref, kc_ref, pt_ref, ocmp_ref, idx_ref, page_ref, p_buf, imp_buf, *, past, s_new):
    assert BPP == 2
    ns = q2_ref.shape[0]
    qrows = s_new * GQA_R
    n_pages = kc_ref.shape[2] // N_KV
    nb = n_pages * BPP
    width = imp_buf.shape[1]
    imp_buf[...] = jnp.zeros(imp_buf.shape, F32)
    lane = lax.broadcasted_iota(jnp.int32, (1, width), 1)
    jc = jnp.where(lane < n_pages, BPP * lane, BPP * (lane - n_pages) + 1)
    jc = jnp.where(lane == nb, nb, jc)

    row2 = lax.broadcasted_iota(jnp.int32, (2 * qrows, 1), 0)
    half = jnp.where(row2 >= qrows, 1, 0)
    blk = BPP * lax.broadcasted_iota(jnp.int32, (1, n_pages), 1) + half
    tok = (row2 & (qrows - 1)) >> (GQA_R.bit_length() - 1)
    vis = ((blk + 1) * BLK - 1) <= (past + tok)

    def both(x):
        return jnp.concatenate([x, x], axis=0)

    def pages(a, c, g):
        return kc_ref[a, c, g * n_pages:(g + 1) * n_pages, :].astype(BF16)

    pairs = [(a, g) for a in range(ns) for g in range(N_KV)]
    scores = [_dot_nt(q2_ref[a, g], pages(a, 0, g)) for a, g in pairs]
    for i, (a, g) in enumerate(pairs):
        s = jnp.where(vis, scores[i], NEG)
        m = jnp.max(s, axis=-1, keepdims=True)
        e = jnp.where(vis, jnp.exp(s - both(jnp.maximum(m[:qrows], m[qrows:]))), 0.0)
        l = jnp.sum(e, axis=-1, keepdims=True)
        l = l[:qrows] + l[qrows:]
        p = e / both(jnp.where(l > 0.0, l, 1.0))
        o2 = _dot(p.astype(BF16), pages(a, 1, g))
        ocmp_ref[a, g] = o2[:qrows, :HEAD_DIM] + o2[qrows:, HEAD_DIM:]
        p_buf[i] = p
        imp = sum(p_buf[i, pl.ds(r, 2 * s_new, stride=GQA_R), :] for r in range(GQA_R))
        row0 = i * s_new
        imp_buf[row0:row0 + s_new, 0:n_pages] = imp[:s_new]
        imp_buf[row0:row0 + s_new, n_pages:nb] = imp[s_new:]

    rows = ns * N_KV * s_new
    imp = imp_buf[...]
    t_row = lax.broadcasted_iota(jnp.int32, (rows, 1), 0) & (s_new - 1)
    cur = (past + t_row) >> BLK_SHIFT
    score = jnp.where(jc > cur, NEG, imp)
    score = jnp.where((jc == 0) | (jc == cur) | (jc == cur - 1), FORCE, score)
    score = jnp.where(lane <= nb, score, -jnp.inf)
    jcf = jc.astype(F32)
    out_lane = lax.broadcasted_iota(jnp.int32, (1, LANES), 1)
    idx = jnp.zeros((rows, LANES), jnp.int32)
    for k in range(N_SEL):
        m = jnp.max(score, axis=-1, keepdims=True)
        first = jnp.min(jnp.where(score == m, jcf, 1e9), axis=-1, keepdims=True)
        idx = jnp.where(out_lane == k, first.astype(jnp.int32), idx)
        score = jnp.where(jcf == first, -jnp.inf, score)
    idx_ref[...] = idx
    per_seq = N_KV * s_new
    pt_rows = jnp.concatenate([jnp.broadcast_to(pt_ref[a:a + 1, :], (per_seq, n_pages)) for a in range(ns)],
                              axis=0)
    page_ref[...] = jnp.take_along_axis(pt_rows, jnp.minimum(idx, nb - 1) >> (BPP.bit_length() - 1), axis=1)


def _dec_cmp(q2, kc, page_table, past, s_new):
    nseq = q2.shape[0]
    qrows = s_new * GQA_R
    rows_kc = kc.shape[2]
    n_pages = rows_kc // N_KV
    assert n_pages == LANES
    nb = n_pages * BPP
    ns = DEC_SEQ_PER_STEP
    width = (nb + 1 + LANES - 1) // LANES * LANES
    seq4 = lambda i: (i, 0, 0, 0)
    rows2 = lambda i: (i, 0)
    idx_shape = jax.ShapeDtypeStruct((nseq * N_KV * s_new, LANES), jnp.int32)
    return pl.pallas_call(
        functools.partial(_dec_cmp_body, past=past, s_new=s_new),
        grid=(nseq // ns,),
        in_specs=[
            pl.BlockSpec((ns, N_KV, 2 * qrows, 2 * HEAD_DIM), seq4),
            pl.BlockSpec((ns, 2, rows_kc, PAGE_SIZE), seq4),
            pl.BlockSpec((ns, n_pages), rows2),
        ],
        out_specs=[
            pl.BlockSpec((ns, N_KV, qrows, HEAD_DIM), seq4),
            pl.BlockSpec((ns * N_KV * s_new, LANES), rows2),
            pl.BlockSpec((ns * N_KV * s_new, LANES), rows2),
        ],
        out_shape=[jax.ShapeDtypeStruct((nseq, N_KV, qrows, HEAD_DIM), F32), idx_shape, idx_shape],
        scratch_shapes=[pltpu.VMEM((ns * N_KV, 2 * qrows, n_pages), F32),
                        pltpu.VMEM((ns * N_KV * s_new, width), F32)],
        compiler_params=_cparams(("arbitrary",)),
        name="dec_cmp_select",
    )(q2, kc, page_table)


def _dec_sel_win_body(page_ref, q_ref, idx2_ref, expand_ref, pool_ref, newblk_ref, win_ref,
                      kvw_new_ref, osel_ref, owin_ref, winout_ref, kvbuf, sem, *, past, s_new):
    n_past_blk = past // BLK
    n = pl.program_id(0)
    nseq = pl.num_programs(0)
    slot = n % 2
    rows = s_new * GQA_R
    seg_lanes = N_SEL * PAGE_SIZE
    nkeys = s_new * seg_lanes

    def block_copy(seq, c, k, sl):
        g, t = c // s_new, c % s_new
        page = page_ref[(seq * N_KV * s_new + c) * N_SEL + k]
        dst = kvbuf.at[sl, :, g, :, pl.ds((t * N_SEL + k) * PAGE_SIZE, PAGE_SIZE)]
        return pltpu.make_async_copy(pool_ref.at[page, :, g], dst, sem.at[sl])

    def start(seq, sl):
        for c in range(N_KV * s_new):
            for k in range(N_SEL):
                block_copy(seq, c, k, sl).start()

    @pl.when(n == 0)
    def _():
        start(n, slot)

    @pl.when(n + 1 < nseq)
    def _():
        start(n + 1, 1 - slot)

    for c in range(N_KV * s_new):
        for k in range(N_SEL):
            block_copy(n, c, k, slot).wait()

    assert win_ref.shape[1] == min(WINDOW, past + s_new)
    slid = jnp.concatenate([win_ref[...], kvw_new_ref[...]], axis=1)
    winout_ref[...] = slid[:, s_new:s_new + win_ref.shape[1]]

    t_row = lax.broadcasted_iota(jnp.int32, (rows, 1), 0) >> (GQA_R.bit_length() - 1)
    qpos = past + t_row
    lane = lax.broadcasted_iota(jnp.int32, (1, nkeys), 1)
    own = (lane >> (seg_lanes.bit_length() - 1)) == t_row
    row_in_page = lane & (PAGE_SIZE - 1)
    nlane = lax.broadcasted_iota(jnp.int32, (1, PAGE_SIZE), 1)
    nwin = win_ref.shape[1]
    wpos = past - nwin + lax.broadcasted_iota(jnp.int32, (1, nwin), 1)
    groups = range(N_KV)

    def krows(g):
        return pl.ds(g * HEAD_DIM, HEAD_DIM)

    def vrows(g):
        return pl.ds((N_KV + g) * HEAD_DIM, HEAD_DIM)

    q = [q_ref[g] for g in groups]
    jv2 = _dot(idx2_ref[...], expand_ref[...])
    s_sel = [_dot(q[g], kvbuf[slot, 0, g].astype(BF16)) for g in groups]
    s_new_sel = [_dot(q[g], newblk_ref[krows(g), :].astype(BF16)) for g in groups]
    s_win = [_dot(q[g], win_ref[krows(g), :].astype(BF16)) for g in groups]
    s_new_win = [_dot(q[g], kvw_new_ref[krows(g), :].astype(BF16)) for g in groups]

    def two_piece_softmax(sa, va, sb, vb):
        m = jnp.maximum(jnp.max(sa, axis=-1, keepdims=True), jnp.max(sb, axis=-1, keepdims=True))
        ea, eb = jnp.exp(sa - m), jnp.exp(sb - m)
        l = jnp.sum(ea, axis=-1, keepdims=True) + jnp.sum(eb, axis=-1, keepdims=True)
        return (_dot_nt(ea.astype(BF16), va) + _dot_nt(eb.astype(BF16), vb)) / l

    for g in groups:
        jv = jv2[g:g + 1, :].astype(jnp.int32)
        kpos = jv * BLK + (row_in_page & (BLK - 1))
        valid = (own & (jv < n_past_blk) & ((row_in_page >> BLK_SHIFT) == (jv & (BPP - 1)))
                 & (kpos <= qpos))
        has_new = jnp.max(jnp.where(own & (jv >= n_past_blk), 1.0, 0.0), axis=-1, keepdims=True)
        valid_n = (has_new > 0.5) & (nlane < BLK) & (n_past_blk * BLK + nlane <= qpos)
        osel_ref[g] = two_piece_softmax(
            jnp.where(valid, s_sel[g], NEG), kvbuf[slot, 1, g].astype(BF16),
            jnp.where(valid_n, s_new_sel[g], NEG), newblk_ref[vrows(g), :].astype(BF16))

        dw = qpos - wpos
        dn = qpos - (past + nlane)
        valid_w = (dw >= 0) & (dw < WINDOW) & (wpos >= 0)
        valid_n = (dn >= 0) & (dn < WINDOW) & (nlane < s_new)
        owin_ref[g] = two_piece_softmax(
            jnp.where(valid_w, s_win[g], NEG), win_ref[vrows(g), :].astype(BF16),
            jnp.where(valid_n, s_new_win[g], NEG), kvw_new_ref[vrows(g), :].astype(BF16))


def _dec_sel_win(page_flat, q_g, idx2, pool_t, newblk_t, win_t, kvw_new_t, past, s_new):
    nseq = q_g.shape[0]
    rows = s_new * GQA_R
    nwin = win_t.shape[2]
    nkeys = s_new * N_SEL * PAGE_SIZE
    n_phys = pool_t.shape[0]
    expand = (jnp.arange(nkeys)[None, :] // PAGE_SIZE == jnp.arange(s_new * N_SEL)[:, None]).astype(BF16)
    per_seq = lambda a, pg: (a, 0, 0, 0)
    per_seq3 = lambda a, pg: (a, 0, 0)
    grid_spec = pltpu.PrefetchScalarGridSpec(
        num_scalar_prefetch=1,
        grid=(nseq,),
        in_specs=[
            pl.BlockSpec((None, N_KV, rows, HEAD_DIM), per_seq),
            pl.BlockSpec((None, rows, s_new * N_SEL), per_seq3),
            pl.BlockSpec((s_new * N_SEL, nkeys), lambda a, pg: (0, 0)),
            pl.BlockSpec(memory_space=pl.ANY),
            pl.BlockSpec((None, KV_W, PAGE_SIZE), per_seq3),
            pl.BlockSpec((None, KV_W, nwin), per_seq3),
            pl.BlockSpec((None, KV_W, PAGE_SIZE), per_seq3),
        ],
        out_specs=[
            pl.BlockSpec((None, N_KV, rows, HEAD_DIM), per_seq),
            pl.BlockSpec((None, N_KV, rows, HEAD_DIM), per_seq),
            pl.BlockSpec((None, KV_W, nwin), per_seq3),
        ],
        scratch_shapes=[pltpu.VMEM((2, 2, N_KV, HEAD_DIM, nkeys), F32),
                        pltpu.SemaphoreType.DMA((2,))],
    )
    shape = jax.ShapeDtypeStruct((nseq, N_KV, rows, HEAD_DIM), F32)
    return pl.pallas_call(
        functools.partial(_dec_sel_win_body, past=past, s_new=s_new),
        grid_spec=grid_spec,
        out_shape=[shape, shape, jax.ShapeDtypeStruct(win_t.shape, F32)],
        compiler_params=_cparams(("arbitrary",)),
        name="dec_sel_win",
    )(page_flat, q_g, idx2, expand, pool_t.reshape(n_phys, 2, N_KV, HEAD_DIM, PAGE_SIZE),
      newblk_t, win_t, kvw_new_t)


def _out_ln_body(*refs, gated):
    if gated:
        ocmp_ref, osel_ref, owin_ref, gates_ref, yconv_ref, h_ref, wo_ref, g_ref, b_ref, y_ref = refs
        gates = gates_ref[...]
        pieces = []
        for h in range(N_HEADS):
            col = slice(h * HEAD_DIM, (h + 1) * HEAD_DIM)
            pieces.append(gates[:, h:h + 1] * ocmp_ref[:, col]
                          + gates[:, N_HEADS + h:N_HEADS + h + 1] * osel_ref[:, col]
                          + gates[:, 2 * N_HEADS + h:2 * N_HEADS + h + 1] * owin_ref[:, col])
        o = jnp.concatenate(pieces, axis=-1)
    else:
        o_ref, yconv_ref, h_ref, wo_ref, g_ref, b_ref, y_ref = refs
        o = o_ref[...]
    mixed = jnp.concatenate([o, yconv_ref[...]], axis=-1).astype(BF16)
    y_ref[...] = _layer_norm(ALPHA * h_ref[...] + _dot(mixed, wo_ref[...]), g_ref[...], b_ref[...])


def _out_ln(branches, yconv, h2d, w_out, g, b, tm):
    rows = h2d.shape[0]
    gated = len(branches) > 1
    row = lambda i: (i, 0)
    const = lambda i: (0, 0)
    widths = [D_ATTN, D_ATTN, D_ATTN, LANES] if gated else [D_ATTN]
    return pl.pallas_call(
        functools.partial(_out_ln_body, gated=gated),
        grid=(rows // tm,),
        in_specs=[pl.BlockSpec((tm, w), row) for w in widths] + [
            pl.BlockSpec((tm, D_CONV), row),
            pl.BlockSpec((tm, D_MODEL), row),
            pl.BlockSpec((D_MODEL, D_MODEL), const),
            pl.BlockSpec((1, D_MODEL), const),
            pl.BlockSpec((1, D_MODEL), const),
        ],
        out_specs=pl.BlockSpec((tm, D_MODEL), row),
        out_shape=jax.ShapeDtypeStruct((rows, D_MODEL), F32),
        compiler_params=_cparams(("arbitrary",)),
        name="out_ln",
    )(*branches, yconv, h2d, w_out, g, b)


def _block_diag_groups(w):
    z = jnp.zeros_like(w)
    return jnp.concatenate([jnp.concatenate([w, z], axis=-1),
                            jnp.concatenate([z, w], axis=-1)], axis=-2).astype(BF16)


def _to_group_layout(x, nseq, s_new):
    x = x.reshape(nseq, s_new, N_KV, GQA_R, HEAD_DIM).transpose(0, 2, 1, 3, 4)
    return x.reshape(nseq, N_KV, s_new * GQA_R, HEAD_DIM)


def _from_group_layout(x, nseq, s_new):
    x = x.reshape(nseq, N_KV, s_new, GQA_R, HEAD_DIM).transpose(0, 2, 1, 3, 4)
    return x.reshape(nseq * s_new, D_ATTN)


def kernel(x_prompt, x_sample, cache_cmp_kv, cache_sel_kv, state_win_kv, state_conv, page_table,
           ffa_gate, ffa_up, ffa_down, ln1_g, ln1_b, w_in, b_in, w_conv, cmp_w1, cmp_w2, cmp_pe,
           w_out, ln2_g, ln2_b, ffb_gate, ffb_up, ffb_down, ln3_g, ln3_b):
    depth = ffa_gate.shape[0]
    assert depth == 1
    l = 0
    n, t, d = x_prompt.shape
    nseq, s_new, _ = x_sample.shape
    n_pages = page_table.shape[1]
    past = n_pages * PAGE_SIZE
    rows_p, rows_s = n * t, nseq * s_new
    kvshape = (2, N_KV, HEAD_DIM)
    assert d == D_MODEL and rows_p % FFN_ROW_TILE == 0 and t % PROJ_ROW_TILE == 0 and t % ATT_TQ == 0
    assert t // BLK <= AUX_BLOCKS and n % CMP_SEQS_PER_STEP == 0 and nseq % DEC_SEQ_PER_STEP == 0
    assert s_new * GQA_R == BF16_SUBLANES and s_new <= BLK and past % BLK == 0
    assert state_win_kv.shape[2] == WINDOW

    bf = lambda w: w.astype(BF16)
    vec = lambda v: v.reshape(1, -1)
    w_main = bf(w_in[l][:, :D_MAIN])
    b_main = vec(b_in[l][:D_MAIN])
    w_gate = bf(jnp.pad(w_in[l][:, D_MAIN:], ((0, 0), (0, LANES - N_GATE))))
    b_gate = vec(jnp.pad(b_in[l][D_MAIN:], (0, LANES - N_GATE)))
    pe_t = jnp.concatenate([cmp_pe[l].transpose(1, 2, 0)] * BPP, axis=-1)
    w1_t = _block_diag_groups(cmp_w1[l].transpose(0, 2, 1, 3)).reshape(
        2, D_PAIRS, 2 * PAGE_SIZE, PAGE_SIZE)
    w2_t = _block_diag_groups(cmp_w2[l])
    w_out_b = bf(w_out[l])
    ffa = (ffa_gate[l], ffa_up[l], ffa_down[l], vec(ln1_g[l]), vec(ln1_b[l]))
    ffb = (ffb_gate[l], ffb_up[l], ffb_down[l], vec(ln3_g[l]), vec(ln3_b[l]))

    def prompt_cols(x):
        return jnp.concatenate([x[..., :_C_Q], x[..., _C_KVS:_C_KVS + K_HALF],
                                x[..., _C_KVW:_C_KVW + K_HALF]], axis=-1)

    def prompt_rows(x):
        return jnp.pad(x[..., _C_Q:], [(0, 0)] * (x.ndim - 1) + [(0, GATE_ROWS - N_GATE)])

    w_rows, b_rows = bf(prompt_cols(w_in[l])), vec(prompt_cols(b_in[l]))
    w_tr, b_tr = bf(prompt_rows(w_in[l]).T), prompt_rows(b_in[l]).reshape(-1, 1)

    pps = t // PAGE_SIZE
    cmp_seqs = CMP_SEQS_PER_STEP
    hp = _ffn_ln(x_prompt.reshape(rows_p, d), *ffa, tm=FFN_ROW_TILE)
    q, k_sel, k_win, kvc_t, kvs_t, kvw_t, gates_t, yconv, cstate = _proj_prompt(
        hp.reshape(n, t, d), w_rows, b_rows, w_tr, b_tr, w_conv[l], tm=PROJ_ROW_TILE)
    kc = _compress(jnp.zeros((1,), jnp.int32), kvc_t, pe_t, w1_t, w2_t, n_steps=n // cmp_seqs,
                   n_pages=cmp_seqs * pps, pages_per_seq=pps, name="compress_prompt")
    kc = kc.reshape(n // cmp_seqs, 2, N_KV, cmp_seqs, pps, BPP, HEAD_DIM)
    kc = kc.transpose(0, 3, 2, 1, 4, 5, 6).reshape(n, N_KV, 2, pps * BPP, HEAD_DIM)
    zc = jnp.zeros_like(kc[:, 0, 0])
    kc_pad = bf(jnp.stack([jnp.concatenate([kc[:, 0, 0], zc], axis=-1),
                           jnp.concatenate([zc, kc[:, 1, 0]], axis=-1)], axis=1))
    vc_t = bf(jnp.swapaxes(kc[:, :, 1], -1, -2))
    o = _attn_prompt(q, k_sel, k_win, kvs_t, kvw_t, kc_pad, vc_t, gates_t)
    yp = _out_ln([o.reshape(rows_p, D_ATTN)], yconv.reshape(rows_p, D_CONV), hp, w_out_b,
                 vec(ln2_g[l]), vec(ln2_b[l]), tm=OUT_ROW_TILE)
    yp = _ffn_ln(yp, *ffb, tm=FFN_ROW_TILE).reshape(n, t, d)

    def from_feature_major(x_t):
        return jnp.moveaxis(x_t.reshape(x_t.shape[0], *kvshape, x_t.shape[-1]), -1, 1)[None]

    p_cmp = from_feature_major(kvc_t)
    p_sel = from_feature_major(kvs_t)
    p_win = from_feature_major(kvw_t[:, :, t - min(WINDOW, t):])
    p_conv = cstate[:, SUBLANES - (CONV_W - 1):].reshape(1, n, CONV_W - 1, D_CONV)

    hs = _ffn_ln(x_sample.reshape(rows_s, d), *ffa, tm=rows_s)
    qs, kv3, gates_s, cb_s, u_s = _proj_sample(hs, w_main, b_main, w_gate, b_gate)
    kvc_s, kvs_s, kvw_s = kv3[:, :KV_W], kv3[:, KV_W:2 * KV_W], kv3[:, 2 * KV_W:]
    u3 = u_s.reshape(nseq, s_new, D_CONV)
    yconv_s = _conv_sample(cb_s.reshape(nseq, s_new, D_CONV), u3, state_conv[l], w_conv[l])
    s_conv = jnp.concatenate([state_conv[l], u3], axis=1)[:, -(CONV_W - 1):]

    def feature_major(x):
        return jnp.moveaxis(x, -4, -1).reshape(*x.shape[:-4], KV_W, x.shape[-4])

    def new_rows_t(x2d):
        xt = x2d.reshape(nseq, s_new, KV_W).transpose(0, 2, 1)
        return jnp.pad(xt, ((0, 0), (0, 0), (0, PAGE_SIZE - s_new)))

    kc_s = _compress(page_table.reshape(-1), feature_major(cache_cmp_kv[l]), pe_t, w1_t, w2_t,
                     n_steps=nseq, n_pages=n_pages, pages_per_seq=None, name="compress_decode")
    q_g = _to_group_layout(qs, nseq, s_new)
    zq = jnp.zeros_like(q_g)
    q2 = jnp.concatenate([jnp.concatenate([q_g, zq], axis=-1), jnp.concatenate([zq, q_g], axis=-1)], axis=-2)
    ocmp_c, idx_pad, page_pad = _dec_cmp(q2, kc_s, page_table, past, s_new)
    idx = idx_pad[:, :N_SEL]
    idx2 = jnp.pad(bf(idx.reshape(nseq, N_KV, s_new * N_SEL)), ((0, 0), (0, s_new * GQA_R - N_KV), (0, 0)))
    win_t = feature_major(state_win_kv[l])
    kvw_new_t = new_rows_t(kvw_s)
    osel_c, owin_c, win_next_t = _dec_sel_win(
        page_pad[:, :N_SEL].reshape(-1), q_g, idx2, feature_major(cache_sel_kv[l]),
        new_rows_t(kvs_s), win_t, kvw_new_t, past, s_new)
    branches = [_from_group_layout(x, nseq, s_new) for x in (ocmp_c, osel_c, owin_c)]
    ys = _out_ln(branches + [gates_s], yconv_s.reshape(rows_s, D_CONV), hs, w_out_b,
                 vec(ln2_g[l]), vec(ln2_b[l]), tm=rows_s)
    ys = _ffn_ln(ys, *ffb, tm=rows_s).reshape(nseq, s_new, d)
    s_cmp = kvc_s.reshape(1, nseq, s_new, *kvshape)
    s_sel = kvs_s.reshape(1, nseq, s_new, *kvshape)
    s_win = from_feature_major(win_next_t)
    return (yp, ys, p_cmp, p_sel, p_win, p_conv, s_cmp, s_sel, s_win,
            s_conv.reshape(1, nseq, CONV_W - 1, D_CONV))
```

```python
import functools

import jax
import jax.numpy as jnp
from jax import lax
from jax.experimental import pallas as pl
from jax.experimental.pallas import tpu as pltpu

F32 = jnp.float32
BF16 = jnp.bfloat16

D_MODEL = 1024
D_CONV = 512
D_ATTN = 512
HEAD_DIM = 64
N_HEADS = 8
N_KV = 2
GQA_R = N_HEADS // N_KV
N_BR = 3
CONV_W = 3
BLK = 64
BLK_SHIFT = BLK.bit_length() - 1
N_SEL = 16
WINDOW = 512
PAGE_SIZE = 128
D_FF = 2816
KV_W = 2 * N_KV * HEAD_DIM
D_MAIN = 3 * D_CONV + D_ATTN + 3 * KV_W
N_GATE = N_BR * N_HEADS
ALPHA = 2.0 ** 0.25
SCALE = HEAD_DIM ** -0.5
LN_EPS = 1e-5
NEG = -1e30
FORCE = 1e4

LANES = 128
SUBLANES = 8
BF16_SUBLANES = 2 * SUBLANES
V7X_VMEM_BYTES = 64 * 1024 * 1024
VMEM_LIMIT_BYTES = V7X_VMEM_BYTES * 7 // 8

FFN_ROW_TILE = 512
PROJ_ROW_TILE = 1024
OUT_ROW_TILE = 1024
CMP_SEQS_PER_STEP = 4


def _cparams(sem):
    return pltpu.CompilerParams(dimension_semantics=sem, vmem_limit_bytes=VMEM_LIMIT_BYTES)


def _layer_norm(y, g, b):
    mu = jnp.mean(y, axis=-1, keepdims=True)
    yc = y - mu
    var = jnp.mean(yc * yc, axis=-1, keepdims=True)
    return yc * lax.rsqrt(var + LN_EPS) * g + b


def _dot(a, b):
    return jnp.dot(a, b, preferred_element_type=F32)


def _dot_nt(a, b):
    return lax.dot_general(a, b, (((1,), (1,)), ((), ())), preferred_element_type=F32)


FFN_CHUNK = 256


def _ffn_ln_body(x_ref, xs_ref, wg_ref, wu_ref, wd_ref, g_ref, b_ref, o_ref, os_ref):
    def ffn_ln(x):
        xb = x.astype(BF16)
        acc = jnp.zeros(x.shape, F32)
        for c in range(D_FF // FFN_CHUNK):
            sl = slice(c * FFN_CHUNK, (c + 1) * FFN_CHUNK)
            gg = _dot(xb, wg_ref[:, sl].astype(BF16))
            uu = _dot(xb, wu_ref[:, sl].astype(BF16))
            hh = (gg * jax.nn.sigmoid(gg) * uu).astype(BF16)
            acc = acc + _dot(hh, wd_ref[sl, :].astype(BF16))
        return _layer_norm(ALPHA * x + 0.5 * acc, g_ref[...], b_ref[...])

    o_ref[...] = ffn_ln(x_ref[...])

    @pl.when(pl.program_id(0) == 0)
    def _():
        os_ref[...] = ffn_ln(xs_ref[...])


def _ffn_ln(x2d, xs2d, wg, wu, wd, g, b, tm):
    rows, rows_s = x2d.shape[0], xs2d.shape[0]
    const = lambda i: (0, 0)
    return pl.pallas_call(
        _ffn_ln_body,
        grid=(rows // tm,),
        in_specs=[
            pl.BlockSpec((tm, D_MODEL), lambda i: (i, 0)),
            pl.BlockSpec((rows_s, D_MODEL), const),
            pl.BlockSpec((D_MODEL, D_FF), const, pipeline_mode=pl.Buffered(1)),
            pl.BlockSpec((D_MODEL, D_FF), const, pipeline_mode=pl.Buffered(1)),
            pl.BlockSpec((D_FF, D_MODEL), const, pipeline_mode=pl.Buffered(1)),
            pl.BlockSpec((1, D_MODEL), const),
            pl.BlockSpec((1, D_MODEL), const),
        ],
        out_specs=[pl.BlockSpec((tm, D_MODEL), lambda i: (i, 0)),
                   pl.BlockSpec((rows_s, D_MODEL), const)],
        out_shape=[jax.ShapeDtypeStruct((rows, D_MODEL), F32),
                   jax.ShapeDtypeStruct((rows_s, D_MODEL), F32)],
        compiler_params=_cparams(("arbitrary",)),
        name="ffn_ln",
    )(x2d, xs2d, wg, wu, wd, g, b)


_C_CB, _C_CC, _C_CH, _C_Q = 0, D_CONV, 2 * D_CONV, 3 * D_CONV
_C_KVC = 3 * D_CONV + D_ATTN
_C_KVS = _C_KVC + KV_W
_C_KVW = _C_KVS + KV_W


def _proj_piece(hb, w_ref, b_ref, lo, width):
    return _dot(hb, w_ref[:, lo:lo + width]) + b_ref[:, lo:lo + width]


_P_K = 3 * D_CONV
K_HALF = N_KV * HEAD_DIM
P_ROW_COLS = _P_K + 2 * K_HALF
GATE_ROWS = 32
_T_KV = D_ATTN
_T_GATE = _T_KV + 3 * KV_W
P_T_ROWS = _T_GATE + GATE_ROWS


def _proj_prompt_body(h_ref, w_ref, b_ref, wt_ref, bt_ref, wc_ref,
                      qt_ref, ks_ref, kw_ref, kvct_ref, kvst_ref, kvwt_ref, gates_ref,
                      yconv_ref, cstate_ref, ubuf):
    i = pl.program_id(1)
    tm = h_ref.shape[0]
    hb = h_ref[...].astype(BF16)

    def piece_t(lo, height):
        return _dot_nt(wt_ref[lo:lo + height, :], hb) + bt_ref[lo:lo + height, :]

    qt_ref[...] = (piece_t(0, D_ATTN) * (SCALE * LOG2E)).astype(BF16)
    lane = lax.broadcasted_iota(jnp.int32, (1, KV_W), 1)
    blk = (i * tm + lax.broadcasted_iota(jnp.int32, (tm, 1), 0)) >> BLK_SHIFT
    aux0, aux1 = HEAD_DIM, 2 * HEAD_DIM
    one = (lane == aux0 + AUX_BLOCKS) | (lane == aux1 + AUX_BLOCKS)
    is_k = (lane < aux0) | (lane >= aux1 + HEAD_DIM)
    k4 = _proj_piece(hb, w_ref, b_ref, _P_K, 2 * K_HALF)
    for ref, lo, ones in ((ks_ref, 0, one | (lane == aux0 + blk) | (lane == aux1 + blk)),
                          (kw_ref, K_HALF, one)):
        k = k4[:, lo:lo + K_HALF]
        ref[...] = jnp.where(is_k, jnp.concatenate([k, k], axis=1),
                             jnp.where(ones, 1.0, 0.0)).astype(BF16)
    for k, ref in enumerate((kvct_ref, kvst_ref, kvwt_ref)):
        ref[...] = piece_t(_T_KV + k * KV_W, KV_W)
    gates_ref[...] = jax.nn.sigmoid(piece_t(_T_GATE, GATE_ROWS))
    cb = _proj_piece(hb, w_ref, b_ref, _C_CB, D_CONV)
    u = _proj_piece(hb, w_ref, b_ref, _C_CC, D_CONV) * _proj_piece(hb, w_ref, b_ref, _C_CH, D_CONV)

    head = SUBLANES

    @pl.when(i == 0)
    def _():
        ubuf[0:head, :] = jnp.zeros((head, D_CONV), F32)

    ubuf[head:head + tm, :] = u
    u1 = ubuf[head - 1:head - 1 + tm, :]
    u2 = ubuf[head - 2:head - 2 + tm, :]
    z = u2 * wc_ref[0:1, :] + u1 * wc_ref[1:2, :] + u * wc_ref[2:3, :]
    yconv_ref[...] = cb * z
    tail = u[tm - head:tm, :]
    ubuf[0:head, :] = tail
    cstate_ref[...] = tail


def _proj_prompt(h, w_rows, b_rows, w_t, b_t, w_conv, tm):
    n, t, _ = h.shape
    const = lambda a, i: (0, 0)
    row = lambda a, i: (a, i, 0)
    col = lambda a, i: (a, 0, i)

    def out(width, dtype=F32):
        return jax.ShapeDtypeStruct((n, t, width), dtype)

    def out_t(rows, dtype=F32):
        return jax.ShapeDtypeStruct((n, rows, t), dtype)

    return pl.pallas_call(
        _proj_prompt_body,
        grid=(n, t // tm),
        in_specs=[
            pl.BlockSpec((None, tm, D_MODEL), row),
            pl.BlockSpec((D_MODEL, P_ROW_COLS), const),
            pl.BlockSpec((1, P_ROW_COLS), const),
            pl.BlockSpec((P_T_ROWS, D_MODEL), const),
            pl.BlockSpec((P_T_ROWS, 1), const),
            pl.BlockSpec((CONV_W, D_CONV), const),
        ],
        out_specs=[
            pl.BlockSpec((None, D_ATTN, tm), col),
            pl.BlockSpec((None, tm, KV_W), row),
            pl.BlockSpec((None, tm, KV_W), row),
            pl.BlockSpec((None, KV_W, tm), col),
            pl.BlockSpec((None, KV_W, tm), col),
            pl.BlockSpec((None, KV_W, tm), col),
            pl.BlockSpec((None, GATE_ROWS, tm), col),
            pl.BlockSpec((None, tm, D_CONV), row),
            pl.BlockSpec((None, SUBLANES, D_CONV), lambda a, i: (a, 0, 0)),
        ],
        out_shape=[out_t(D_ATTN, BF16), out(KV_W, BF16), out(KV_W, BF16), out_t(KV_W), out_t(KV_W),
                   out_t(KV_W), out_t(GATE_ROWS), out(D_CONV),
                   jax.ShapeDtypeStruct((n, SUBLANES, D_CONV), F32)],
        scratch_shapes=[pltpu.VMEM((tm + SUBLANES, D_CONV), F32)],
        compiler_params=_cparams(("arbitrary", "arbitrary")),
        name="proj_prompt",
    )(h, w_rows, b_rows, w_t, b_t, w_conv)


def _proj_sample_body(h_ref, w_ref, b_ref, wgt_ref, bgt_ref, q_ref, kv_ref, gates_ref, cb_ref, u_ref):
    hb = h_ref[...].astype(BF16)
    q_ref[...] = (_proj_piece(hb, w_ref, b_ref, _C_Q, D_ATTN) * SCALE).astype(BF16)
    kv_ref[...] = _proj_piece(hb, w_ref, b_ref, _C_KVC, 3 * KV_W)
    gates_ref[...] = jax.nn.sigmoid(_dot(hb, wgt_ref[...]) + bgt_ref[...])
    cb_ref[...] = _proj_piece(hb, w_ref, b_ref, _C_CB, D_CONV)
    u_ref[...] = _proj_piece(hb, w_ref, b_ref, _C_CC, D_CONV) * _proj_piece(hb, w_ref, b_ref, _C_CH, D_CONV)


def _proj_sample(h2d, w_main, b_main, w_gate, b_gate):
    rows = h2d.shape[0]

    def out(width, dtype=F32):
        return jax.ShapeDtypeStruct((rows, width), dtype)

    return pl.pallas_call(
        _proj_sample_body,
        out_shape=[out(D_ATTN, BF16), out(3 * KV_W), out(LANES), out(D_CONV), out(D_CONV)],
        compiler_params=pltpu.CompilerParams(vmem_limit_bytes=VMEM_LIMIT_BYTES),
        name="proj_sample",
    )(h2d, w_main, b_main, w_gate, b_gate)


def _conv_sample_body(cb_ref, u_ref, prev_ref, wc_ref, y_ref):
    s = u_ref.shape[1]
    ext = [prev_ref[:, 0, :], prev_ref[:, 1, :]] + [u_ref[:, t, :] for t in range(s)]
    for t in range(s):
        z = ext[t] * wc_ref[0:1, :] + ext[t + 1] * wc_ref[1:2, :] + ext[t + 2] * wc_ref[2:3, :]
        y_ref[:, t, :] = cb_ref[:, t, :] * z


def _conv_sample(cb, u, prev, w_conv):
    return pl.pallas_call(
        _conv_sample_body,
        out_shape=jax.ShapeDtypeStruct(u.shape, F32),
        name="conv_sample",
    )(cb, u, prev, w_conv)


BPP = PAGE_SIZE // BLK
D_PAIRS = HEAD_DIM // 2


def _compress_body(pt_ref, src_ref, pe_ref, w1_ref, w2_ref, kc_ref, buf, sem, *, n_pages, pages_per_seq):
    n = pl.program_id(0)
    slot = n % 2

    def page_copy(step, j, sl):
        dst = buf.at[sl, :, j, :]
        if pages_per_seq is None:
            src = src_ref.at[pt_ref[step * n_pages + j]]
        else:
            seq = step * (n_pages // pages_per_seq) + j // pages_per_seq
            src = src_ref.at[seq, :, pl.ds((j % pages_per_seq) * PAGE_SIZE, PAGE_SIZE)]
        return pltpu.make_async_copy(src, dst, sem.at[sl])

    def start(step, sl):
        for j in range(n_pages):
            page_copy(step, j, sl).start()

    @pl.when(n == 0)
    def _():
        start(n, slot)

    @pl.when(n + 1 < pl.num_programs(0))
    def _():
        start(n + 1, 1 - slot)

    for j in range(n_pages):
        page_copy(n, j, slot).wait()

    def feature_rows(f):
        return buf[slot, f]

    for c in range(2):
        acc = None
        for dp in range(D_PAIRS):
            rows = []
            for g in range(N_KV):
                f0 = (c * N_KV + g) * HEAD_DIM + 2 * dp
                rows.append(jnp.concatenate(
                    [feature_rows(f0) + pe_ref[c, 2 * dp:2 * dp + 1, :],
                     feature_rows(f0 + 1) + pe_ref[c, 2 * dp + 1:2 * dp + 2, :]], axis=-1))
            x = jnp.concatenate(rows, axis=0).astype(BF16)
            d = _dot(x, w1_ref[c, dp])
            acc = d if acc is None else acc + d
        hid = (acc * jax.nn.sigmoid(acc)).astype(BF16)
        kc_ref[c] = _dot(hid, w2_ref[c])


def _compress(pt_flat, src, pe_t, w1_t, w2_t, n_steps, n_pages, pages_per_seq, name):
    const3 = lambda a, pt: (0, 0, 0)
    grid_spec = pltpu.PrefetchScalarGridSpec(
        num_scalar_prefetch=1,
        grid=(n_steps,),
        in_specs=[
            pl.BlockSpec(memory_space=pl.ANY),
            pl.BlockSpec((2, HEAD_DIM, PAGE_SIZE), const3),
            pl.BlockSpec((2, D_PAIRS, 2 * PAGE_SIZE, PAGE_SIZE), lambda a, pt: (0, 0, 0, 0)),
            pl.BlockSpec((2, PAGE_SIZE, PAGE_SIZE), const3),
        ],
        out_specs=pl.BlockSpec((None, 2, N_KV * n_pages, PAGE_SIZE), lambda a, pt: (a, 0, 0, 0)),
        scratch_shapes=[pltpu.VMEM((2, KV_W, n_pages, PAGE_SIZE), F32),
                        pltpu.SemaphoreType.DMA((2,))],
    )
    return pl.pallas_call(
        functools.partial(_compress_body, n_pages=n_pages, pages_per_seq=pages_per_seq),
        grid_spec=grid_spec,
        out_shape=jax.ShapeDtypeStruct((n_steps, 2, N_KV * n_pages, PAGE_SIZE), F32),
        compiler_params=_cparams(("arbitrary",)),
        name=name,
    )(pt_flat, src, pe_t, w1_t, w2_t)


ATT_TQ = 256
ATT_TK = ATT_TQ
WIN_TILES = WINDOW // ATT_TK + 1
AUX_BLOCKS = 32
LOG2E = 1.4426950408889634


V_ROWS = HEAD_DIM + BF16_SUBLANES


def _attn_update(s_t, v_t, state):
    m, acc = state
    m_new = jnp.maximum(m, jnp.max(s_t, axis=0, keepdims=True))
    e = jnp.exp2(s_t - m_new)
    acc = jnp.exp2(m - m_new) * acc + _dot(v_t, e.astype(BF16))
    return m_new, acc


def _attn_prompt_body(qt_ref, ks_ref, kw_ref, vs_ref, vw_ref, kc_ref, vct_ref, gates_ref, wb_ref, o_ref):
    tq, tk = ATT_TQ, ATT_TK
    qi = pl.program_id(1)
    nb = kc_ref.shape[1]
    t0 = qi * tq
    kd = qi
    tpos = t0 + lax.broadcasted_iota(jnp.int32, (1, tq), 1)
    jrow = lax.broadcasted_iota(jnp.int32, (nb, 1), 0)
    vis = ((jrow + 1) * BLK - 1) <= tpos
    cur = tpos >> BLK_SHIFT

    def heads_of(x):
        return jnp.concatenate([x] * GQA_R, axis=1)

    def group_queries(g, block_bias=None, switch=None):
        zeros = jnp.zeros((AUX_BLOCKS, tq), BF16)
        top = zeros if block_bias is None else block_bias
        low = zeros if switch is None else jnp.broadcast_to(switch, (AUX_BLOCKS, tq)).astype(BF16)
        aux = jnp.concatenate([top, low], axis=0)
        pieces = []
        for r in range(GQA_R):
            h = g * GQA_R + r
            qh = qt_ref[h * HEAD_DIM:(h + 1) * HEAD_DIM, :]
            pieces.append(jnp.concatenate([qh, aux] if g == 0 else [aux, qh], axis=0))
        return jnp.concatenate(pieces, axis=1)

    q4 = [group_queries(g) for g in range(N_KV)]
    vis4 = heads_of(vis)

    o_cmp, q4_sel = [], []
    for g in range(N_KV):
        s = jnp.where(vis4, _dot(kc_ref[g], q4[g]), NEG)
        m = jnp.max(s, axis=0, keepdims=True)
        e = jnp.where(vis4, jnp.exp2(s - m), 0.0)
        l = jnp.sum(e, axis=0, keepdims=True)
        p = e / jnp.where(l > 0.0, l, 1.0)
        imp = sum(p[:, r * tq:(r + 1) * tq] for r in range(GQA_R))
        o_cmp.append(_dot(vct_ref[g], p.astype(BF16)))

        score = jnp.where(jrow > cur, NEG, imp)
        score = jnp.where((jrow == 0) | (jrow == cur) | (jrow == cur - 1), FORCE, score)
        rank = jnp.zeros((nb, tq), F32)
        for j2 in range(nb):
            row = score[j2:j2 + 1, :]
            tie = jnp.where(jrow > j2, 1.0, 0.0)
            rank = rank + jnp.where(row > score, 1.0, jnp.where(row == score, tie, 0.0))
        q4_sel.append(group_queries(g, jnp.where(rank < N_SEL, 0.0, NEG).astype(BF16)))

    def key_tile(k_ref, kt, g):
        return k_ref[pl.ds(pl.multiple_of(kt * tk, tk), tk), g * LANES:(g + 1) * LANES]

    ones_rows = jnp.ones((V_ROWS - HEAD_DIM, tk), BF16)

    def val_tile(v_ref, kt, g):
        v = v_ref[g * HEAD_DIM:(g + 1) * HEAD_DIM, pl.ds(pl.multiple_of(kt * tk, tk), tk)]
        return jnp.concatenate([v.astype(BF16), ones_rows], axis=0)

    wide = GQA_R * tq
    init = (jnp.full((1, wide), NEG, F32), jnp.zeros((V_ROWS, wide), F32))

    def win_scores(back, g):
        kt = jnp.maximum(kd - back, 0)
        qw = q4[g] if back == 0 else group_queries(g, None, jnp.where(kd >= back, 0.0, NEG))
        s = _dot(key_tile(kw_ref, kt, g), qw)
        if back in (0, WIN_TILES - 1):
            s = s + heads_of(wb_ref[min(back, 1)])
        return s

    groups = range(N_KV)
    s_diag = [_dot(key_tile(ks_ref, kd, g), q4_sel[g]) + heads_of(wb_ref[0]) for g in groups]
    s_win = [[win_scores(back, g) for g in groups] for back in range(WIN_TILES)]
    sel = tuple(_attn_update(s_diag[g], val_tile(vs_ref, kd, g), init) for g in groups)
    win = [init] * N_KV
    for back in range(WIN_TILES):
        kt = jnp.maximum(kd - back, 0)
        win = [_attn_update(s_win[back][g], val_tile(vw_ref, kt, g), win[g]) for g in groups]

    def sel_tiles(kts, states):
        s = [[_dot(key_tile(ks_ref, kt, g), q4_sel[g]) for g in groups] for kt in kts]
        for i, kt in enumerate(kts):
            states = tuple(_attn_update(s[i][g], val_tile(vs_ref, kt, g), states[g]) for g in groups)
        return states

    sel = lax.cond((kd & 1) == 1, lambda st: sel_tiles((kd - 1,), st), lambda st: st, sel)
    sel = lax.fori_loop(0, kd >> 1, lambda j, st: sel_tiles((2 * j, 2 * j + 1), st), sel)

    gates = gates_ref[...]
    outs = []
    for h in range(N_HEADS):
        g, r = h // GQA_R, h % GQA_R
        cols = slice(r * tq, (r + 1) * tq)
        o_sel = sel[g][1][:HEAD_DIM, cols] * (1.0 / sel[g][1][HEAD_DIM:HEAD_DIM + 1, cols])
        o_win = win[g][1][:HEAD_DIM, cols] * (1.0 / win[g][1][HEAD_DIM:HEAD_DIM + 1, cols])
        outs.append(gates[h:h + 1, :] * o_cmp[g][:, cols] + gates[N_HEADS + h:N_HEADS + h + 1, :] * o_sel
                    + gates[2 * N_HEADS + h:2 * N_HEADS + h + 1, :] * o_win)
    o_ref[...] = jnp.concatenate(outs, axis=0).T


def _attn_prompt(q_t, k_sel, k_win, kvs_t, kvw_t, kc_pad, vc_t, gates_t):
    n, _, t = q_t.shape
    nb = kc_pad.shape[2]
    row = lambda a, i: (a, i, 0)
    seq = lambda a, i: (a, 0, 0)
    seq4 = lambda a, i: (a, 0, 0, 0)
    v_rows = lambda a, i: (a, 1, 0)

    def win_bias(back):
        dist = back * ATT_TK + jnp.arange(ATT_TQ)[None, :] - jnp.arange(ATT_TK)[:, None]
        return jnp.where((dist >= 0) & (dist < WINDOW), 0.0, NEG).astype(F32)

    wb = jnp.stack([win_bias(0), win_bias(WIN_TILES - 1)])
    return pl.pallas_call(
        _attn_prompt_body,
        grid=(n, t // ATT_TQ),
        in_specs=[
            pl.BlockSpec((None, D_ATTN, ATT_TQ), lambda a, i: (a, 0, i)),
            pl.BlockSpec((None, t, KV_W), seq),
            pl.BlockSpec((None, t, KV_W), seq),
            pl.BlockSpec((None, N_KV * HEAD_DIM, t), v_rows),
            pl.BlockSpec((None, N_KV * HEAD_DIM, t), v_rows),
            pl.BlockSpec((None, N_KV, nb, LANES), seq4),
            pl.BlockSpec((None, N_KV, HEAD_DIM, nb), seq4),
            pl.BlockSpec((None, GATE_ROWS, ATT_TQ), lambda a, i: (a, 0, i)),
            pl.BlockSpec((2, ATT_TK, ATT_TQ), lambda a, i: (0, 0, 0)),
        ],
        out_specs=pl.BlockSpec((None, ATT_TQ, D_ATTN), row),
        out_shape=jax.ShapeDtypeStruct((n, t, D_ATTN), F32),
        compiler_params=_cparams(("arbitrary", "arbitrary")),
        name="attn_prompt",
    )(q_t, k_sel, k_win, kvs_t, kvw_t, kc_pad, vc_t, gates_t, wb)


DEC_SEQ_PER_STEP = 8


def _dec_cmp_body(q2_ref, kc_ref, pt_ref, ocmp_ref, idx_ref, page_ref, p_buf, imp_buf, *, past, s_new):
    assert BPP == 2
    ns = q2_ref.shape[0]
    qrows = s_new * GQA_R
    n_pages = kc_ref.shape[2] // N_KV
    nb = n_pages * BPP
    width = imp_buf.shape[1]
    imp_buf[...] = jnp.zeros(imp_buf.shape, F32)
    lane = lax.broadcasted_iota(jnp.int32, (1, width), 1)
    jc = jnp.where(lane < n_pages, BPP * lane, BPP * (lane - n_pages) + 1)
    jc = jnp.where(lane == nb, nb, jc)

    row2 = lax.broadcasted_iota(jnp.int32, (2 * qrows, 1), 0)
    half = jnp.where(row2 >= qrows, 1, 0)
    blk = BPP * lax.broadcasted_iota(jnp.int32, (1, n_pages), 1) + half
    tok = (row2 & (qrows - 1)) >> (GQA_R.bit_length() - 1)
    vis = ((blk + 1) * BLK - 1) <= (past + tok)

    def both(x):
        return jnp.concatenate([x, x], axis=0)

    def pages(a, c, g):
        return kc_ref[a, c, g * n_pages:(g + 1) * n_pages, :].astype(BF16)

    pairs = [(a, g) for a in range(ns) for g in range(N_KV)]
    scores = [_dot_nt(q2_ref[a, g], pages(a, 0, g)) for a, g in pairs]
    for i, (a, g) in enumerate(pairs):
        s = jnp.where(vis, scores[i], NEG)
        m = jnp.max(s, axis=-1, keepdims=True)
        e = jnp.where(vis, jnp.exp(s - both(jnp.maximum(m[:qrows], m[qrows:]))), 0.0)
        l = jnp.sum(e, axis=-1, keepdims=True)
        l = l[:qrows] + l[qrows:]
        p = e / both(jnp.where(l > 0.0, l, 1.0))
        o2 = _dot(p.astype(BF16), pages(a, 1, g))
        ocmp_ref[a, g] = o2[:qrows, :HEAD_DIM] + o2[qrows:, HEAD_DIM:]
        p_buf[i] = p
        imp = sum(p_buf[i, pl.ds(r, 2 * s_new, stride=GQA_R), :] for r in range(GQA_R))
        row0 = i * s_new
        imp_buf[row0:row0 + s_new, 0:n_pages] = imp[:s_new]
        imp_buf[row0:row0 + s_new, n_pages:nb] = imp[s_new:]

    rows = ns * N_KV * s_new
    imp = imp_buf[...]
    t_row = lax.broadcasted_iota(jnp.int32, (rows, 1), 0) & (s_new - 1)
    cur = (past + t_row) >> BLK_SHIFT
    score = jnp.where(jc > cur, NEG, imp)
    score = jnp.where((jc == 0) | (jc == cur) | (jc == cur - 1), FORCE, score)
    score = jnp.where(lane <= nb, score, -jnp.inf)
    jcf = jc.astype(F32)
    out_lane = lax.broadcasted_iota(jnp.int32, (1, LANES), 1)
    idx = jnp.zeros((rows, LANES), jnp.int32)
    for k in range(N_SEL):
        m = jnp.max(score, axis=-1, keepdims=True)
        first = jnp.min(jnp.where(score == m, jcf, 1e9), axis=-1, keepdims=True)
        idx = jnp.where(out_lane == k, first.astype(jnp.int32), idx)
        score = jnp.where(jcf == first, -jnp.inf, score)
    idx_ref[...] = idx
    per_seq = N_KV * s_new
    pt_rows = jnp.concatenate([jnp.broadcast_to(pt_ref[a:a + 1, :], (per_seq, n_pages)) for a in range(ns)],
                              axis=0)
    page_ref[...] = jnp.take_along_axis(pt_rows, jnp.minimum(idx, nb - 1) >> (BPP.bit_length() - 1), axis=1)


def _dec_cmp(q2, kc, page_table, past, s_new):
    nseq = q2.shape[0]
    qrows = s_new * GQA_R
    rows_kc = kc.shape[2]
    n_pages = rows_kc // N_KV
    assert n_pages == LANES
    nb = n_pages * BPP
    ns = DEC_SEQ_PER_STEP
    width = (nb + 1 + LANES - 1) // LANES * LANES
    seq4 = lambda i: (i, 0, 0, 0)
    rows2 = lambda i: (i, 0)
    idx_shape = jax.ShapeDtypeStruct((nseq * N_KV * s_new, LANES), jnp.int32)
    return pl.pallas_call(
        functools.partial(_dec_cmp_body, past=past, s_new=s_new),
        grid=(nseq // ns,),
        in_specs=[
            pl.BlockSpec((ns, N_KV, 2 * qrows, 2 * HEAD_DIM), seq4),
            pl.BlockSpec((ns, 2, rows_kc, PAGE_SIZE), seq4),
            pl.BlockSpec((ns, n_pages), rows2),
        ],
        out_specs=[
            pl.BlockSpec((ns, N_KV, qrows, HEAD_DIM), seq4),
            pl.BlockSpec((ns * N_KV * s_new, LANES), rows2),
            pl.BlockSpec((ns * N_KV * s_new, LANES), rows2),
        ],
        out_shape=[jax.ShapeDtypeStruct((nseq, N_KV, qrows, HEAD_DIM), F32), idx_shape, idx_shape],
        scratch_shapes=[pltpu.VMEM((ns * N_KV, 2 * qrows, n_pages), F32),
                        pltpu.VMEM((ns * N_KV * s_new, width), F32)],
        compiler_params=_cparams(("arbitrary",)),
        name="dec_cmp_select",
    )(q2, kc, page_table)


def _dec_sel_win_body(page_ref, q_ref, idx2_ref, expand_ref, pool_ref, newblk_ref, win_ref,
                      kvw_new_ref, osel_ref, owin_ref, winout_ref, kvbuf, sem, *, past, s_new):
    n_past_blk = past // BLK
    n = pl.program_id(0)
    nseq = pl.num_programs(0)
    slot = n % 2
    rows = s_new * GQA_R
    seg_lanes = N_SEL * PAGE_SIZE
    nkeys = s_new * seg_lanes

    def block_copy(seq, c, k, sl):
        g, t = c // s_new, c % s_new
        page = page_ref[(seq * N_KV * s_new + c) * N_SEL + k]
        dst = kvbuf.at[sl, :, g, :, pl.ds((t * N_SEL + k) * PAGE_SIZE, PAGE_SIZE)]
        return pltpu.make_async_copy(pool_ref.at[page, :, g], dst, sem.at[sl])

    def start(seq, sl):
        for c in range(N_KV * s_new):
            for k in range(N_SEL):
                block_copy(seq, c, k, sl).start()

    @pl.when(n == 0)
    def _():
        start(n, slot)

    @pl.when(n + 1 < nseq)
    def _():
        start(n + 1, 1 - slot)

    for c in range(N_KV * s_new):
        for k in range(N_SEL):
            block_copy(n, c, k, slot).wait()

    assert win_ref.shape[1] == min(WINDOW, past + s_new)
    slid = jnp.concatenate([win_ref[...], kvw_new_ref[...]], axis=1)
    winout_ref[...] = slid[:, s_new:s_new + win_ref.shape[1]]

    t_row = lax.broadcasted_iota(jnp.int32, (rows, 1), 0) >> (GQA_R.bit_length() - 1)
    qpos = past + t_row
    lane = lax.broadcasted_iota(jnp.int32, (1, nkeys), 1)
    own = (lane >> (seg_lanes.bit_length() - 1)) == t_row
    row_in_page = lane & (PAGE_SIZE - 1)
    nlane = lax.broadcasted_iota(jnp.int32, (1, PAGE_SIZE), 1)
    nwin = win_ref.shape[1]
    wpos = past - nwin + lax.broadcasted_iota(jnp.int32, (1, nwin), 1)
    groups = range(N_KV)

    def krows(g):
        return pl.ds(g * HEAD_DIM, HEAD_DIM)

    def vrows(g):
        return pl.ds((N_KV + g) * HEAD_DIM, HEAD_DIM)

    q = [q_ref[g] for g in groups]
    jv2 = _dot(idx2_ref[...], expand_ref[...])
    s_sel = [_dot(q[g], kvbuf[slot, 0, g].astype(BF16)) for g in groups]
    s_new_sel = [_dot(q[g], newblk_ref[krows(g), :].astype(BF16)) for g in groups]
    s_win = [_dot(q[g], win_ref[krows(g), :].astype(BF16)) for g in groups]
    s_new_win = [_dot(q[g], kvw_new_ref[krows(g), :].astype(BF16)) for g in groups]

    def two_piece_softmax(sa, va, sb, vb):
        m = jnp.maximum(jnp.max(sa, axis=-1, keepdims=True), jnp.max(sb, axis=-1, keepdims=True))
        ea, eb = jnp.exp(sa - m), jnp.exp(sb - m)
        l = jnp.sum(ea, axis=-1, keepdims=True) + jnp.sum(eb, axis=-1, keepdims=True)
        return (_dot_nt(ea.astype(BF16), va) + _dot_nt(eb.astype(BF16), vb)) / l

    for g in groups:
        jv = jv2[g:g + 1, :].astype(jnp.int32)
        kpos = jv * BLK + (row_in_page & (BLK - 1))
        valid = (own & (jv < n_past_blk) & ((row_in_page >> BLK_SHIFT) == (jv & (BPP - 1)))
                 & (kpos <= qpos))
        has_new = jnp.max(jnp.where(own & (jv >= n_past_blk), 1.0, 0.0), axis=-1, keepdims=True)
        valid_n = (has_new > 0.5) & (nlane < BLK) & (n_past_blk * BLK + nlane <= qpos)
        osel_ref[g] = two_piece_softmax(
            jnp.where(valid, s_sel[g], NEG), kvbuf[slot, 1, g].astype(BF16),
            jnp.where(valid_n, s_new_sel[g], NEG), newblk_ref[vrows(g), :].astype(BF16))

        dw = qpos - wpos
        dn = qpos - (past + nlane)
        valid_w = (dw >= 0) & (dw < WINDOW) & (wpos >= 0)
        valid_n = (dn >= 0) & (dn < WINDOW) & (nlane < s_new)
        owin_ref[g] = two_piece_softmax(
            jnp.where(valid_w, s_win[g], NEG), win_ref[vrows(g), :].astype(BF16),
            jnp.where(valid_n, s_new_win[g], NEG), kvw_new_ref[vrows(g), :].astype(BF16))


def _dec_sel_win(page_flat, q_g, idx2, pool_t, newblk_t, win_t, kvw_new_t, past, s_new):
    nseq = q_g.shape[0]
    rows = s_new * GQA_R
    nwin = win_t.shape[2]
    nkeys = s_new * N_SEL * PAGE_SIZE
    n_phys = pool_t.shape[0]
    expand = (jnp.arange(nkeys)[None, :] // PAGE_SIZE == jnp.arange(s_new * N_SEL)[:, None]).astype(BF16)
    per_seq = lambda a, pg: (a, 0, 0, 0)
    per_seq3 = lambda a, pg: (a, 0, 0)
    grid_spec = pltpu.PrefetchScalarGridSpec(
        num_scalar_prefetch=1,
        grid=(nseq,),
        in_specs=[
            pl.BlockSpec((None, N_KV, rows, HEAD_DIM), per_seq),
            pl.BlockSpec((None, rows, s_new * N_SEL), per_seq3),
            pl.BlockSpec((s_new * N_SEL, nkeys), lambda a, pg: (0, 0)),
            pl.BlockSpec(memory_space=pl.ANY),
            pl.BlockSpec((None, KV_W, PAGE_SIZE), per_seq3),
            pl.BlockSpec((None, KV_W, nwin), per_seq3),
            pl.BlockSpec((None, KV_W, PAGE_SIZE), per_seq3),
        ],
        out_specs=[
            pl.BlockSpec((None, N_KV, rows, HEAD_DIM), per_seq),
            pl.BlockSpec((None, N_KV, rows, HEAD_DIM), per_seq),
            pl.BlockSpec((None, KV_W, nwin), per_seq3),
        ],
        scratch_shapes=[pltpu.VMEM((2, 2, N_KV, HEAD_DIM, nkeys), F32),
                        pltpu.SemaphoreType.DMA((2,))],
    )
    shape = jax.ShapeDtypeStruct((nseq, N_KV, rows, HEAD_DIM), F32)
    return pl.pallas_call(
        functools.partial(_dec_sel_win_body, past=past, s_new=s_new),
        grid_spec=grid_spec,
        out_shape=[shape, shape, jax.ShapeDtypeStruct(win_t.shape, F32)],
        compiler_params=_cparams(("arbitrary",)),
        name="dec_sel_win",
    )(page_flat, q_g, idx2, expand, pool_t.reshape(n_phys, 2, N_KV, HEAD_DIM, PAGE_SIZE),
      newblk_t, win_t, kvw_new_t)


def _out_ln_body(*refs, gated):
    if gated:
        ocmp_ref, osel_ref, owin_ref, gates_ref, yconv_ref, h_ref, wo_ref, g_ref, b_ref, y_ref = refs
        gates = gates_ref[...]
        pieces = []
        for h in range(N_HEADS):
            col = slice(h * HEAD_DIM, (h + 1) * HEAD_DIM)
            pieces.append(gates[:, h:h + 1] * ocmp_ref[:, col]
                          + gates[:, N_HEADS + h:N_HEADS + h + 1] * osel_ref[:, col]
                          + gates[:, 2 * N_HEADS + h:2 * N_HEADS + h + 1] * owin_ref[:, col])
        o = jnp.concatenate(pieces, axis=-1)
    else:
        o_ref, yconv_ref, h_ref, wo_ref, g_ref, b_ref, y_ref = refs
        o = o_ref[...]
    mixed = jnp.concatenate([o, yconv_ref[...]], axis=-1).astype(BF16)
    y_ref[...] = _layer_norm(ALPHA * h_ref[...] + _dot(mixed, wo_ref[...]), g_ref[...], b_ref[...])


def _out_ln(branches, yconv, h2d, w_out, g, b, tm):
    rows = h2d.shape[0]
    gated = len(branches) > 1
    row = lambda i: (i, 0)
    const = lambda i: (0, 0)
    widths = [D_ATTN, D_ATTN, D_ATTN, LANES] if gated else [D_ATTN]
    return pl.pallas_call(
        functools.partial(_out_ln_body, gated=gated),
        grid=(rows // tm,),
        in_specs=[pl.BlockSpec((tm, w), row) for w in widths] + [
            pl.BlockSpec((tm, D_CONV), row),
            pl.BlockSpec((tm, D_MODEL), row),
            pl.BlockSpec((D_MODEL, D_MODEL), const),
            pl.BlockSpec((1, D_MODEL), const),
            pl.BlockSpec((1, D_MODEL), const),
        ],
        out_specs=pl.BlockSpec((tm, D_MODEL), row),
        out_shape=jax.ShapeDtypeStruct((rows, D_MODEL), F32),
        compiler_params=_cparams(("arbitrary",)),
        name="out_ln",
    )(*branches, yconv, h2d, w_out, g, b)


def _block_diag_groups(w):
    z = jnp.zeros_like(w)
    return jnp.concatenate([jnp.concatenate([w, z], axis=-1),
                            jnp.concatenate([z, w], axis=-1)], axis=-2).astype(BF16)


def _to_group_layout(x, nseq, s_new):
    x = x.reshape(nseq, s_new, N_KV, GQA_R, HEAD_DIM).transpose(0, 2, 1, 3, 4)
    return x.reshape(nseq, N_KV, s_new * GQA_R, HEAD_DIM)


def _from_group_layout(x, nseq, s_new):
    x = x.reshape(nseq, N_KV, s_new, GQA_R, HEAD_DIM).transpose(0, 2, 1, 3, 4)
    return x.reshape(nseq * s_new, D_ATTN)


def kernel(x_prompt, x_sample, cache_cmp_kv, cache_sel_kv, state_win_kv, state_conv, page_table,
           ffa_gate, ffa_up, ffa_down, ln1_g, ln1_b, w_in, b_in, w_conv, cmp_w1, cmp_w2, cmp_pe,
           w_out, ln2_g, ln2_b, ffb_gate, ffb_up, ffb_down, ln3_g, ln3_b):
    depth = ffa_gate.shape[0]
    assert depth == 1
    l = 0
    n, t, d = x_prompt.shape
    nseq, s_new, _ = x_sample.shape
    n_pages = page_table.shape[1]
    past = n_pages * PAGE_SIZE
    rows_p, rows_s = n * t, nseq * s_new
    kvshape = (2, N_KV, HEAD_DIM)
    assert d == D_MODEL and rows_p % FFN_ROW_TILE == 0 and t % PROJ_ROW_TILE == 0 and t % ATT_TQ == 0
    assert t // BLK <= AUX_BLOCKS and n % CMP_SEQS_PER_STEP == 0 and nseq % DEC_SEQ_PER_STEP == 0
    assert s_new * GQA_R == BF16_SUBLANES and s_new <= BLK and past % BLK == 0
    assert state_win_kv.shape[2] == WINDOW

    bf = lambda w: w.astype(BF16)
    vec = lambda v: v.reshape(1, -1)
    w_main = bf(w_in[l][:, :D_MAIN])
    b_main = vec(b_in[l][:D_MAIN])
    w_gate = bf(jnp.pad(w_in[l][:, D_MAIN:], ((0, 0), (0, LANES - N_GATE))))
    b_gate = vec(jnp.pad(b_in[l][D_MAIN:], (0, LANES - N_GATE)))
    pe_t = jnp.concatenate([cmp_pe[l].transpose(1, 2, 0)] * BPP, axis=-1)
    w1_t = _block_diag_groups(cmp_w1[l].transpose(0, 2, 1, 3)).reshape(
        2, D_PAIRS, 2 * PAGE_SIZE, PAGE_SIZE)
    w2_t = _block_diag_groups(cmp_w2[l])
    w_out_b = bf(w_out[l])
    ffa = (ffa_gate[l], ffa_up[l], ffa_down[l], vec(ln1_g[l]), vec(ln1_b[l]))
    ffb = (ffb_gate[l], ffb_up[l], ffb_down[l], vec(ln3_g[l]), vec(ln3_b[l]))

    def prompt_cols(x):
        return jnp.concatenate([x[..., :_C_Q], x[..., _C_KVS:_C_KVS + K_HALF],
                                x[..., _C_KVW:_C_KVW + K_HALF]], axis=-1)

    def prompt_rows(x):
        return jnp.pad(x[..., _C_Q:], [(0, 0)] * (x.ndim - 1) + [(0, GATE_ROWS - N_GATE)])

    w_rows, b_rows = bf(prompt_cols(w_in[l])), vec(prompt_cols(b_in[l]))
    w_tr, b_tr = bf(prompt_rows(w_in[l]).T), prompt_rows(b_in[l]).reshape(-1, 1)

    pps = t // PAGE_SIZE
    cmp_seqs = CMP_SEQS_PER_STEP
    hp, hs = _ffn_ln(x_prompt.reshape(rows_p, d), x_sample.reshape(rows_s, d), *ffa, tm=FFN_ROW_TILE)
    q, k_sel, k_win, kvc_t, kvs_t, kvw_t, gates_t, yconv, cstate = _proj_prompt(
        hp.reshape(n, t, d), w_rows, b_rows, w_tr, b_tr, w_conv[l], tm=PROJ_ROW_TILE)
    kc = _compress(jnp.zeros((1,), jnp.int32), kvc_t, pe_t, w1_t, w2_t, n_steps=n // cmp_seqs,
                   n_pages=cmp_seqs * pps, pages_per_seq=pps, name="compress_prompt")
    kc = kc.reshape(n // cmp_seqs, 2, N_KV, cmp_seqs, pps, BPP, HEAD_DIM)
    kc = kc.transpose(0, 3, 2, 1, 4, 5, 6).reshape(n, N_KV, 2, pps * BPP, HEAD_DIM)
    zc = jnp.zeros_like(kc[:, 0, 0])
    kc_pad = bf(jnp.stack([jnp.concatenate([kc[:, 0, 0], zc], axis=-1),
                           jnp.concatenate([zc, kc[:, 1, 0]], axis=-1)], axis=1))
    vc_t = bf(jnp.swapaxes(kc[:, :, 1], -1, -2))
    o = _attn_prompt(q, k_sel, k_win, kvs_t, kvw_t, kc_pad, vc_t, gates_t)
    yp = _out_ln([o.reshape(rows_p, D_ATTN)], yconv.reshape(rows_p, D_CONV), hp, w_out_b,
                 vec(ln2_g[l]), vec(ln2_b[l]), tm=OUT_ROW_TILE)

    def from_feature_major(x_t):
        return jnp.moveaxis(x_t.reshape(x_t.shape[0], *kvshape, x_t.shape[-1]), -1, 1)[None]

    p_cmp = from_feature_major(kvc_t)
    p_sel = from_feature_major(kvs_t)
    p_win = from_feature_major(kvw_t[:, :, t - min(WINDOW, t):])
    p_conv = cstate[:, SUBLANES - (CONV_W - 1):].reshape(1, n, CONV_W - 1, D_CONV)

    qs, kv3, gates_s, cb_s, u_s = _proj_sample(hs, w_main, b_main, w_gate, b_gate)
    kvc_s, kvs_s, kvw_s = kv3[:, :KV_W], kv3[:, KV_W:2 * KV_W], kv3[:, 2 * KV_W:]
    u3 = u_s.reshape(nseq, s_new, D_CONV)
    yconv_s = _conv_sample(cb_s.reshape(nseq, s_new, D_CONV), u3, state_conv[l], w_conv[l])
    s_conv = jnp.concatenate([state_conv[l], u3], axis=1)[:, -(CONV_W - 1):]

    def feature_major(x):
        return jnp.moveaxis(x, -4, -1).reshape(*x.shape[:-4], KV_W, x.shape[-4])

    def new_rows_t(x2d):
        xt = x2d.reshape(nseq, s_new, KV_W).transpose(0, 2, 1)
        return jnp.pad(xt, ((0, 0), (0, 0), (0, PAGE_SIZE - s_new)))

    kc_s = _compress(page_table.reshape(-1), feature_major(cache_cmp_kv[l]), pe_t, w1_t, w2_t,
                     n_steps=nseq, n_pages=n_pages, pages_per_seq=None, name="compress_decode")
    q_g = _to_group_layout(qs, nseq, s_new)
    zq = jnp.zeros_like(q_g)
    q2 = jnp.concatenate([jnp.concatenate([q_g, zq], axis=-1), jnp.concatenate([zq, q_g], axis=-1)], axis=-2)
    ocmp_c, idx_pad, page_pad = _dec_cmp(q2, kc_s, page_table, past, s_new)
    idx = idx_pad[:, :N_SEL]
    idx2 = jnp.pad(bf(idx.reshape(nseq, N_KV, s_new * N_SEL)), ((0, 0), (0, s_new * GQA_R - N_KV), (0, 0)))
    win_t = feature_major(state_win_kv[l])
    kvw_new_t = new_rows_t(kvw_s)
    osel_c, owin_c, win_next_t = _dec_sel_win(
        page_pad[:, :N_SEL].reshape(-1), q_g, idx2, feature_major(cache_sel_kv[l]),
        new_rows_t(kvs_s), win_t, kvw_new_t, past, s_new)
    branches = [_from_group_layout(x, nseq, s_new) for x in (ocmp_c, osel_c, owin_c)]
    ys = _out_ln(branches + [gates_s], yconv_s.reshape(rows_s, D_CONV), hs, w_out_b,
                 vec(ln2_g[l]), vec(ln2_b[l]), tm=rows_s)
    yp, ys = _ffn_ln(yp, ys, *ffb, tm=FFN_ROW_TILE)
    yp, ys = yp.reshape(n, t, d), ys.reshape(nseq, s_new, d)
    s_cmp = kvc_s.reshape(1, nseq, s_new, *kvshape)
    s_sel = kvs_s.reshape(1, nseq, s_new, *kvshape)
    s_win = from_feature_major(win_next_t)
    return (yp, ys, p_cmp, p_sel, p_win, p_conv, s_cmp, s_sel, s_win,
            s_conv.reshape(1, nseq, CONV_W - 1, D_CONV))
```

```python
import functools

import jax
import jax.numpy as jnp
from jax import lax
from jax.experimental import pallas as pl
from jax.experimental.pallas import tpu as pltpu

F32 = jnp.float32
BF16 = jnp.bfloat16

D_MODEL = 1024
D_CONV = 512
D_ATTN = 512
HEAD_DIM = 64
N_HEADS = 8
N_KV = 2
GQA_R = N_HEADS // N_KV
N_BR = 3
CONV_W = 3
BLK = 64
BLK_SHIFT = BLK.bit_length() - 1
N_SEL = 16
WINDOW = 512
PAGE_SIZE = 128
D_FF = 2816
KV_W = 2 * N_KV * HEAD_DIM
D_MAIN = 3 * D_CONV + D_ATTN + 3 * KV_W
N_GATE = N_BR * N_HEADS
ALPHA = 2.0 ** 0.25
SCALE = HEAD_DIM ** -0.5
LN_EPS = 1e-5
NEG = -1e30
FORCE = 1e4

LANES = 128
SUBLANES = 8
BF16_SUBLANES = 2 * SUBLANES
V7X_VMEM_BYTES = 64 * 1024 * 1024
VMEM_LIMIT_BYTES = V7X_VMEM_BYTES * 7 // 8

FFN_ROW_TILE = 512
PROJ_ROW_TILE = 1024
OUT_ROW_TILE = 1024
CMP_SEQS_PER_STEP = 4


def _cparams(sem):
    return pltpu.CompilerParams(dimension_semantics=sem, vmem_limit_bytes=VMEM_LIMIT_BYTES)


def _layer_norm(y, g, b):
    mu = jnp.mean(y, axis=-1, keepdims=True)
    yc = y - mu
    var = jnp.mean(yc * yc, axis=-1, keepdims=True)
    return yc * lax.rsqrt(var + LN_EPS) * g + b


def _dot(a, b):
    return jnp.dot(a, b, preferred_element_type=F32)


def _dot_nt(a, b):
    return lax.dot_general(a, b, (((1,), (1,)), ((), ())), preferred_element_type=F32)


FFN_CHUNK = 256


def _ffn_ln_body(x_ref, xs_ref, wg_ref, wu_ref, wd_ref, g_ref, b_ref, o_ref, os_ref):
    def ffn_ln(x):
        xb = x.astype(BF16)
        acc = jnp.zeros(x.shape, F32)
        for c in range(D_FF // FFN_CHUNK):
            sl = slice(c * FFN_CHUNK, (c + 1) * FFN_CHUNK)
            gg = _dot(xb, wg_ref[:, sl].astype(BF16))
            uu = _dot(xb, wu_ref[:, sl].astype(BF16))
            hh = (gg * jax.nn.sigmoid(gg) * uu).astype(BF16)
            acc = acc + _dot(hh, wd_ref[sl, :].astype(BF16))
        return _layer_norm(ALPHA * x + 0.5 * acc, g_ref[...], b_ref[...])

    o_ref[...] = ffn_ln(x_ref[...])

    @pl.when(pl.program_id(0) == 0)
    def _():
        os_ref[...] = ffn_ln(xs_ref[...])


def _ffn_ln(x2d, xs2d, wg, wu, wd, g, b, tm):
    rows, rows_s = x2d.shape[0], xs2d.shape[0]
    const = lambda i: (0, 0)
    return pl.pallas_call(
        _ffn_ln_body,
        grid=(rows // tm,),
        in_specs=[
            pl.BlockSpec((tm, D_MODEL), lambda i: (i, 0)),
            pl.BlockSpec((rows_s, D_MODEL), const),
            pl.BlockSpec((D_MODEL, D_FF), const, pipeline_mode=pl.Buffered(1)),
            pl.BlockSpec((D_MODEL, D_FF), const, pipeline_mode=pl.Buffered(1)),
            pl.BlockSpec((D_FF, D_MODEL), const, pipeline_mode=pl.Buffered(1)),
            pl.BlockSpec((1, D_MODEL), const),
            pl.BlockSpec((1, D_MODEL), const),
        ],
        out_specs=[pl.BlockSpec((tm, D_MODEL), lambda i: (i, 0)),
                   pl.BlockSpec((rows_s, D_MODEL), const)],
        out_shape=[jax.ShapeDtypeStruct((rows, D_MODEL), F32),
                   jax.ShapeDtypeStruct((rows_s, D_MODEL), F32)],
        compiler_params=_cparams(("arbitrary",)),
        name="ffn_ln",
    )(x2d, xs2d, wg, wu, wd, g, b)


_C_CB, _C_CC, _C_CH, _C_Q = 0, D_CONV, 2 * D_CONV, 3 * D_CONV
_C_KVC = 3 * D_CONV + D_ATTN
_C_KVS = _C_KVC + KV_W
_C_KVW = _C_KVS + KV_W


def _proj_piece(hb, w_ref, b_ref, lo, width):
    return _dot(hb, w_ref[:, lo:lo + width]) + b_ref[:, lo:lo + width]


_P_K = 3 * D_CONV
K_HALF = N_KV * HEAD_DIM
P_ROW_COLS = _P_K + 2 * K_HALF
GATE_ROWS = 32
_T_KV = D_ATTN
_T_GATE = _T_KV + 3 * KV_W
P_T_ROWS = _T_GATE + GATE_ROWS


def _proj_prompt_body(h_ref, w_ref, b_ref, wt_ref, bt_ref, wc_ref,
                      qt_ref, ks_ref, kw_ref, kvct_ref, kvst_ref, kvwt_ref, gates_ref,
                      yconv_ref, cstate_ref, ubuf):
    i = pl.program_id(1)
    tm = h_ref.shape[0]
    hb = h_ref[...].astype(BF16)

    def piece_t(lo, height):
        return _dot_nt(wt_ref[lo:lo + height, :], hb) + bt_ref[lo:lo + height, :]

    qt_ref[...] = (piece_t(0, D_ATTN) * (SCALE * LOG2E)).astype(BF16)
    lane = lax.broadcasted_iota(jnp.int32, (1, KV_W), 1)
    blk = (i * tm + lax.broadcasted_iota(jnp.int32, (tm, 1), 0)) >> BLK_SHIFT
    aux0, aux1 = HEAD_DIM, 2 * HEAD_DIM
    one = (lane == aux0 + AUX_BLOCKS) | (lane == aux1 + AUX_BLOCKS)
    is_k = (lane < aux0) | (lane >= aux1 + HEAD_DIM)
    k4 = _proj_piece(hb, w_ref, b_ref, _P_K, 2 * K_HALF)
    for ref, lo, ones in ((ks_ref, 0, one | (lane == aux0 + blk) | (lane == aux1 + blk)),
                          (kw_ref, K_HALF, one)):
        k = k4[:, lo:lo + K_HALF]
        ref[...] = jnp.where(is_k, jnp.concatenate([k, k], axis=1),
                             jnp.where(ones, 1.0, 0.0)).astype(BF16)
    for k, ref in enumerate((kvct_ref, kvst_ref, kvwt_ref)):
        ref[...] = piece_t(_T_KV + k * KV_W, KV_W)
    gates_ref[...] = jax.nn.sigmoid(piece_t(_T_GATE, GATE_ROWS))
    cb = _proj_piece(hb, w_ref, b_ref, _C_CB, D_CONV)
    u = _proj_piece(hb, w_ref, b_ref, _C_CC, D_CONV) * _proj_piece(hb, w_ref, b_ref, _C_CH, D_CONV)

    head = SUBLANES

    @pl.when(i == 0)
    def _():
        ubuf[0:head, :] = jnp.zeros((head, D_CONV), F32)

    ubuf[head:head + tm, :] = u
    u1 = ubuf[head - 1:head - 1 + tm, :]
    u2 = ubuf[head - 2:head - 2 + tm, :]
    z = u2 * wc_ref[0:1, :] + u1 * wc_ref[1:2, :] + u * wc_ref[2:3, :]
    yconv_ref[...] = cb * z
    tail = u[tm - head:tm, :]
    ubuf[0:head, :] = tail
    cstate_ref[...] = tail


def _proj_prompt(h, w_rows, b_rows, w_t, b_t, w_conv, tm):
    n, t, _ = h.shape
    const = lambda a, i: (0, 0)
    row = lambda a, i: (a, i, 0)
    col = lambda a, i: (a, 0, i)

    def out(width, dtype=F32):
        return jax.ShapeDtypeStruct((n, t, width), dtype)

    def out_t(rows, dtype=F32):
        return jax.ShapeDtypeStruct((n, rows, t), dtype)

    return pl.pallas_call(
        _proj_prompt_body,
        grid=(n, t // tm),
        in_specs=[
            pl.BlockSpec((None, tm, D_MODEL), row),
            pl.BlockSpec((D_MODEL, P_ROW_COLS), const),
            pl.BlockSpec((1, P_ROW_COLS), const),
            pl.BlockSpec((P_T_ROWS, D_MODEL), const),
            pl.BlockSpec((P_T_ROWS, 1), const),
            pl.BlockSpec((CONV_W, D_CONV), const),
        ],
        out_specs=[
            pl.BlockSpec((None, D_ATTN, tm), col),
            pl.BlockSpec((None, tm, KV_W), row),
            pl.BlockSpec((None, tm, KV_W), row),
            pl.BlockSpec((None, KV_W, tm), col),
            pl.BlockSpec((None, KV_W, tm), col),
            pl.BlockSpec((None, KV_W, tm), col),
            pl.BlockSpec((None, GATE_ROWS, tm), col),
            pl.BlockSpec((None, tm, D_CONV), row),
            pl.BlockSpec((None, SUBLANES, D_CONV), lambda a, i: (a, 0, 0)),
        ],
        out_shape=[out_t(D_ATTN, BF16), out(KV_W, BF16), out(KV_W, BF16), out_t(KV_W), out_t(KV_W),
                   out_t(KV_W), out_t(GATE_ROWS), out(D_CONV),
                   jax.ShapeDtypeStruct((n, SUBLANES, D_CONV), F32)],
        scratch_shapes=[pltpu.VMEM((tm + SUBLANES, D_CONV), F32)],
        compiler_params=_cparams(("arbitrary", "arbitrary")),
        name="proj_prompt",
    )(h, w_rows, b_rows, w_t, b_t, w_conv)


def _proj_sample_body(h_ref, w_ref, b_ref, wgt_ref, bgt_ref, q_ref, kv_ref, gates_ref, cb_ref, u_ref):
    hb = h_ref[...].astype(BF16)
    q_ref[...] = (_proj_piece(hb, w_ref, b_ref, _C_Q, D_ATTN) * SCALE).astype(BF16)
    kv_ref[...] = _proj_piece(hb, w_ref, b_ref, _C_KVC, 3 * KV_W)
    gates_ref[...] = jax.nn.sigmoid(_dot(hb, wgt_ref[...]) + bgt_ref[...])
    cb_ref[...] = _proj_piece(hb, w_ref, b_ref, _C_CB, D_CONV)
    u_ref[...] = _proj_piece(hb, w_ref, b_ref, _C_CC, D_CONV) * _proj_piece(hb, w_ref, b_ref, _C_CH, D_CONV)


def _proj_sample(h2d, w_main, b_main, w_gate, b_gate):
    rows = h2d.shape[0]

    def out(width, dtype=F32):
        return jax.ShapeDtypeStruct((rows, width), dtype)

    return pl.pallas_call(
        _proj_sample_body,
        out_shape=[out(D_ATTN, BF16), out(3 * KV_W), out(LANES), out(D_CONV), out(D_CONV)],
        compiler_params=pltpu.CompilerParams(vmem_limit_bytes=VMEM_LIMIT_BYTES),
        name="proj_sample",
    )(h2d, w_main, b_main, w_gate, b_gate)


def _conv_sample_body(cb_ref, u_ref, prev_ref, wc_ref, y_ref):
    s = u_ref.shape[1]
    ext = [prev_ref[:, 0, :], prev_ref[:, 1, :]] + [u_ref[:, t, :] for t in range(s)]
    for t in range(s):
        z = ext[t] * wc_ref[0:1, :] + ext[t + 1] * wc_ref[1:2, :] + ext[t + 2] * wc_ref[2:3, :]
        y_ref[:, t, :] = cb_ref[:, t, :] * z


def _conv_sample(cb, u, prev, w_conv):
    return pl.pallas_call(
        _conv_sample_body,
        out_shape=jax.ShapeDtypeStruct(u.shape, F32),
        name="conv_sample",
    )(cb, u, prev, w_conv)


BPP = PAGE_SIZE // BLK
D_PAIRS = HEAD_DIM // 2


def _compress_body(pt_ref, src_ref, pe_ref, w1_ref, w2_ref, kc_ref, buf, sem, *, n_pages, pages_per_seq):
    n = pl.program_id(0)
    slot = n % 2

    def page_copy(step, j, sl):
        dst = buf.at[sl, :, j, :]
        if pages_per_seq is None:
            src = src_ref.at[pt_ref[step * n_pages + j]]
        else:
            seq = step * (n_pages // pages_per_seq) + j // pages_per_seq
            src = src_ref.at[seq, :, pl.ds((j % pages_per_seq) * PAGE_SIZE, PAGE_SIZE)]
        return pltpu.make_async_copy(src, dst, sem.at[sl])

    def start(step, sl):
        for j in range(n_pages):
            page_copy(step, j, sl).start(priority=j % 2)

    @pl.when(n == 0)
    def _():
        start(n, slot)

    @pl.when(n + 1 < pl.num_programs(0))
    def _():
        start(n + 1, 1 - slot)

    for j in range(n_pages):
        page_copy(n, j, slot).wait()

    def feature_rows(f):
        return buf[slot, f]

    for c in range(2):
        acc = None
        for dp in range(D_PAIRS):
            rows = []
            for g in range(N_KV):
                f0 = (c * N_KV + g) * HEAD_DIM + 2 * dp
                rows.append(jnp.concatenate(
                    [feature_rows(f0) + pe_ref[c, 2 * dp:2 * dp + 1, :],
                     feature_rows(f0 + 1) + pe_ref[c, 2 * dp + 1:2 * dp + 2, :]], axis=-1))
            x = jnp.concatenate(rows, axis=0).astype(BF16)
            d = _dot(x, w1_ref[c, dp])
            acc = d if acc is None else acc + d
        hid = (acc * jax.nn.sigmoid(acc)).astype(BF16)
        kc_ref[c] = _dot(hid, w2_ref[c])


def _compress(pt_flat, src, pe_t, w1_t, w2_t, n_steps, n_pages, pages_per_seq, name):
    const3 = lambda a, pt: (0, 0, 0)
    grid_spec = pltpu.PrefetchScalarGridSpec(
        num_scalar_prefetch=1,
        grid=(n_steps,),
        in_specs=[
            pl.BlockSpec(memory_space=pl.ANY),
            pl.BlockSpec((2, HEAD_DIM, PAGE_SIZE), const3),
            pl.BlockSpec((2, D_PAIRS, 2 * PAGE_SIZE, PAGE_SIZE), lambda a, pt: (0, 0, 0, 0)),
            pl.BlockSpec((2, PAGE_SIZE, PAGE_SIZE), const3),
        ],
        out_specs=pl.BlockSpec((None, 2, N_KV * n_pages, PAGE_SIZE), lambda a, pt: (a, 0, 0, 0)),
        scratch_shapes=[pltpu.VMEM((2, KV_W, n_pages, PAGE_SIZE), F32),
                        pltpu.SemaphoreType.DMA((2,))],
    )
    return pl.pallas_call(
        functools.partial(_compress_body, n_pages=n_pages, pages_per_seq=pages_per_seq),
        grid_spec=grid_spec,
        out_shape=jax.ShapeDtypeStruct((n_steps, 2, N_KV * n_pages, PAGE_SIZE), F32),
        compiler_params=_cparams(("arbitrary",)),
        name=name,
    )(pt_flat, src, pe_t, w1_t, w2_t)


ATT_TQ = 256
ATT_TK = ATT_TQ
WIN_TILES = WINDOW // ATT_TK + 1
AUX_BLOCKS = 32
LOG2E = 1.4426950408889634


V_ROWS = HEAD_DIM + BF16_SUBLANES


def _attn_update(s_t, v_t, state):
    m, acc = state
    m_new = jnp.maximum(m, jnp.max(s_t, axis=0, keepdims=True))
    e = jnp.exp2(s_t - m_new)
    acc = jnp.exp2(m - m_new) * acc + _dot(v_t, e.astype(BF16))
    return m_new, acc


def _attn_prompt_body(qt_ref, ks_ref, kw_ref, vs_ref, vw_ref, kc_ref, vct_ref, gates_ref, wb_ref, o_ref):
    tq, tk = ATT_TQ, ATT_TK
    qi = pl.program_id(1)
    nb = kc_ref.shape[1]
    t0 = qi * tq
    kd = qi
    tpos = t0 + lax.broadcasted_iota(jnp.int32, (1, tq), 1)
    jrow = lax.broadcasted_iota(jnp.int32, (nb, 1), 0)
    vis = ((jrow + 1) * BLK - 1) <= tpos
    cur = tpos >> BLK_SHIFT

    def heads_of(x):
        return jnp.concatenate([x] * GQA_R, axis=1)

    def group_queries(g, block_bias=None, switch=None):
        zeros = jnp.zeros((AUX_BLOCKS, tq), BF16)
        top = zeros if block_bias is None else block_bias
        low = zeros if switch is None else jnp.broadcast_to(switch, (AUX_BLOCKS, tq)).astype(BF16)
        aux = jnp.concatenate([top, low], axis=0)
        pieces = []
        for r in range(GQA_R):
            h = g * GQA_R + r
            qh = qt_ref[h * HEAD_DIM:(h + 1) * HEAD_DIM, :]
            pieces.append(jnp.concatenate([qh, aux] if g == 0 else [aux, qh], axis=0))
        return jnp.concatenate(pieces, axis=1)

    q4 = [group_queries(g) for g in range(N_KV)]
    vis4 = heads_of(vis)

    o_cmp, q4_sel = [], []
    for g in range(N_KV):
        s = jnp.where(vis4, _dot(kc_ref[g], q4[g]), NEG)
        m = jnp.max(s, axis=0, keepdims=True)
        e = jnp.where(vis4, jnp.exp2(s - m), 0.0)
        l = jnp.sum(e, axis=0, keepdims=True)
        p = e / jnp.where(l > 0.0, l, 1.0)
        imp = sum(p[:, r * tq:(r + 1) * tq] for r in range(GQA_R))
        o_cmp.append(_dot(vct_ref[g], p.astype(BF16)))

        score = jnp.where(jrow > cur, NEG, imp)
        score = jnp.where((jrow == 0) | (jrow == cur) | (jrow == cur - 1), FORCE, score)
        rank = jnp.zeros((nb, tq), F32)
        for j2 in range(nb):
            row = score[j2:j2 + 1, :]
            tie = jnp.where(jrow > j2, 1.0, 0.0)
            rank = rank + jnp.where(row > score, 1.0, jnp.where(row == score, tie, 0.0))
        q4_sel.append(group_queries(g, jnp.where(rank < N_SEL, 0.0, NEG).astype(BF16)))

    def key_tile(k_ref, kt, g):
        return k_ref[pl.ds(pl.multiple_of(kt * tk, tk), tk), g * LANES:(g + 1) * LANES]

    ones_rows = jnp.ones((V_ROWS - HEAD_DIM, tk), BF16)

    def val_tile(v_ref, kt, g):
        v = v_ref[g * HEAD_DIM:(g + 1) * HEAD_DIM, pl.ds(pl.multiple_of(kt * tk, tk), tk)]
        return jnp.concatenate([v.astype(BF16), ones_rows], axis=0)

    wide = GQA_R * tq
    init = (jnp.full((1, wide), NEG, F32), jnp.zeros((V_ROWS, wide), F32))

    def win_scores(back, g):
        kt = jnp.maximum(kd - back, 0)
        qw = q4[g] if back == 0 else group_queries(g, None, jnp.where(kd >= back, 0.0, NEG))
        s = _dot(key_tile(kw_ref, kt, g), qw)
        if back in (0, WIN_TILES - 1):
            s = s + heads_of(wb_ref[min(back, 1)])
        return s

    groups = range(N_KV)
    s_diag = [_dot(key_tile(ks_ref, kd, g), q4_sel[g]) + heads_of(wb_ref[0]) for g in groups]
    s_win = [[win_scores(back, g) for g in groups] for back in range(WIN_TILES)]
    sel = tuple(_attn_update(s_diag[g], val_tile(vs_ref, kd, g), init) for g in groups)
    win = [init] * N_KV
    for back in range(WIN_TILES):
        kt = jnp.maximum(kd - back, 0)
        win = [_attn_update(s_win[back][g], val_tile(vw_ref, kt, g), win[g]) for g in groups]

    def sel_tiles(kts, states):
        s = [[_dot(key_tile(ks_ref, kt, g), q4_sel[g]) for g in groups] for kt in kts]
        for i, kt in enumerate(kts):
            states = tuple(_attn_update(s[i][g], val_tile(vs_ref, kt, g), states[g]) for g in groups)
        return states

    sel = lax.cond((kd & 1) == 1, lambda st: sel_tiles((kd - 1,), st), lambda st: st, sel)
    sel = lax.fori_loop(0, kd >> 1, lambda j, st: sel_tiles((2 * j, 2 * j + 1), st), sel)

    gates = gates_ref[...]
    outs = []
    for h in range(N_HEADS):
        g, r = h // GQA_R, h % GQA_R
        cols = slice(r * tq, (r + 1) * tq)
        o_sel = sel[g][1][:HEAD_DIM, cols] * (1.0 / sel[g][1][HEAD_DIM:HEAD_DIM + 1, cols])
        o_win = win[g][1][:HEAD_DIM, cols] * (1.0 / win[g][1][HEAD_DIM:HEAD_DIM + 1, cols])
        outs.append(gates[h:h + 1, :] * o_cmp[g][:, cols] + gates[N_HEADS + h:N_HEADS + h + 1, :] * o_sel
                    + gates[2 * N_HEADS + h:2 * N_HEADS + h + 1, :] * o_win)
    o_ref[...] = jnp.concatenate(outs, axis=0).T


def _attn_prompt(q_t, k_sel, k_win, kvs_t, kvw_t, kc_pad, vc_t, gates_t):
    n, _, t = q_t.shape
    nb = kc_pad.shape[2]
    row = lambda a, i: (a, i, 0)
    seq = lambda a, i: (a, 0, 0)
    seq4 = lambda a, i: (a, 0, 0, 0)
    v_rows = lambda a, i: (a, 1, 0)

    def win_bias(back):
        dist = back * ATT_TK + jnp.arange(ATT_TQ)[None, :] - jnp.arange(ATT_TK)[:, None]
        return jnp.where((dist >= 0) & (dist < WINDOW), 0.0, NEG).astype(F32)

    wb = jnp.stack([win_bias(0), win_bias(WIN_TILES - 1)])
    return pl.pallas_call(
        _attn_prompt_body,
        grid=(n, t // ATT_TQ),
        in_specs=[
            pl.BlockSpec((None, D_ATTN, ATT_TQ), lambda a, i: (a, 0, i)),
            pl.BlockSpec((None, t, KV_W), seq),
            pl.BlockSpec((None, t, KV_W), seq),
            pl.BlockSpec((None, N_KV * HEAD_DIM, t), v_rows),
            pl.BlockSpec((None, N_KV * HEAD_DIM, t), v_rows),
            pl.BlockSpec((None, N_KV, nb, LANES), seq4),
            pl.BlockSpec((None, N_KV, HEAD_DIM, nb), seq4),
            pl.BlockSpec((None, GATE_ROWS, ATT_TQ), lambda a, i: (a, 0, i)),
            pl.BlockSpec((2, ATT_TK, ATT_TQ), lambda a, i: (0, 0, 0)),
        ],
        out_specs=pl.BlockSpec((None, ATT_TQ, D_ATTN), row),
        out_shape=jax.ShapeDtypeStruct((n, t, D_ATTN), F32),
        compiler_params=_cparams(("arbitrary", "arbitrary")),
        name="attn_prompt",
    )(q_t, k_sel, k_win, kvs_t, kvw_t, kc_pad, vc_t, gates_t, wb)


DEC_SEQ_PER_STEP = 8


def _dec_cmp_body(q2_ref, kc_ref, pt_ref, ocmp_ref, idx_ref, page_ref, p_buf, imp_buf, *, past, s_new):
    assert BPP == 2
    ns = q2_ref.shape[0]
    qrows = s_new * GQA_R
    n_pages = kc_ref.shape[2] // N_KV
    nb = n_pages * BPP
    width = imp_buf.shape[1]
    imp_buf[...] = jnp.zeros(imp_buf.shape, F32)
    lane = lax.broadcasted_iota(jnp.int32, (1, width), 1)
    jc = jnp.where(lane < n_pages, BPP * lane, BPP * (lane - n_pages) + 1)
    jc = jnp.where(lane == nb, nb, jc)

    row2 = lax.broadcasted_iota(jnp.int32, (2 * qrows, 1), 0)
    half = jnp.where(row2 >= qrows, 1, 0)
    blk = BPP * lax.broadcasted_iota(jnp.int32, (1, n_pages), 1) + half
    tok = (row2 & (qrows - 1)) >> (GQA_R.bit_length() - 1)
    vis = ((blk + 1) * BLK - 1) <= (past + tok)

    def both(x):
        return jnp.concatenate([x, x], axis=0)

    def pages(a, c, g):
        return kc_ref[a, c, g * n_pages:(g + 1) * n_pages, :].astype(BF16)

    pairs = [(a, g) for a in range(ns) for g in range(N_KV)]
    scores = [_dot_nt(q2_ref[a, g], pages(a, 0, g)) for a, g in pairs]
    for i, (a, g) in enumerate(pairs):
        s = jnp.where(vis, scores[i], NEG)
        m = jnp.max(s, axis=-1, keepdims=True)
        e = jnp.where(vis, jnp.exp(s - both(jnp.maximum(m[:qrows], m[qrows:]))), 0.0)
        l = jnp.sum(e, axis=-1, keepdims=True)
        l = l[:qrows] + l[qrows:]
        p = e / both(jnp.where(l > 0.0, l, 1.0))
        o2 = _dot(p.astype(BF16), pages(a, 1, g))
        ocmp_ref[a, g] = o2[:qrows, :HEAD_DIM] + o2[qrows:, HEAD_DIM:]
        p_buf[i] = p
        imp = sum(p_buf[i, pl.ds(r, 2 * s_new, stride=GQA_R), :] for r in range(GQA_R))
        row0 = i * s_new
        imp_buf[row0:row0 + s_new, 0:n_pages] = imp[:s_new]
        imp_buf[row0:row0 + s_new, n_pages:nb] = imp[s_new:]

    rows = ns * N_KV * s_new
    imp = imp_buf[...]
    t_row = lax.broadcasted_iota(jnp.int32, (rows, 1), 0) & (s_new - 1)
    cur = (past + t_row) >> BLK_SHIFT
    score = jnp.where(jc > cur, NEG, imp)
    score = jnp.where((jc == 0) | (jc == cur) | (jc == cur - 1), FORCE, score)
    score = jnp.where(lane <= nb, score, -jnp.inf)
    jcf = jc.astype(F32)
    out_lane = lax.broadcasted_iota(jnp.int32, (1, LANES), 1)
    idx = jnp.zeros((rows, LANES), jnp.int32)
    for k in range(N_SEL):
        m = jnp.max(score, axis=-1, keepdims=True)
        first = jnp.min(jnp.where(score == m, jcf, 1e9), axis=-1, keepdims=True)
        idx = jnp.where(out_lane == k, first.astype(jnp.int32), idx)
        score = jnp.where(jcf == first, -jnp.inf, score)
    idx_ref[...] = idx
    per_seq = N_KV * s_new
    pt_rows = jnp.concatenate([jnp.broadcast_to(pt_ref[a:a + 1, :], (per_seq, n_pages)) for a in range(ns)],
                              axis=0)
    page_ref[...] = jnp.take_along_axis(pt_rows, jnp.minimum(idx, nb - 1) >> (BPP.bit_length() - 1), axis=1)


def _dec_cmp(q2, kc, page_table, past, s_new):
    nseq = q2.shape[0]
    qrows = s_new * GQA_R
    rows_kc = kc.shape[2]
    n_pages = rows_kc // N_KV
    assert n_pages == LANES
    nb = n_pages * BPP
    ns = DEC_SEQ_PER_STEP
    width = (nb + 1 + LANES - 1) // LANES * LANES
    seq4 = lambda i: (i, 0, 0, 0)
    rows2 = lambda i: (i, 0)
    idx_shape = jax.ShapeDtypeStruct((nseq * N_KV * s_new, LANES), jnp.int32)
    return pl.pallas_call(
        functools.partial(_dec_cmp_body, past=past, s_new=s_new),
        grid=(nseq // ns,),
        in_specs=[
            pl.BlockSpec((ns, N_KV, 2 * qrows, 2 * HEAD_DIM), seq4),
            pl.BlockSpec((ns, 2, rows_kc, PAGE_SIZE), seq4),
            pl.BlockSpec((ns, n_pages), rows2),
        ],
        out_specs=[
            pl.BlockSpec((ns, N_KV, qrows, HEAD_DIM), seq4),
            pl.BlockSpec((ns * N_KV * s_new, LANES), rows2),
            pl.BlockSpec((ns * N_KV * s_new, LANES), rows2),
        ],
        out_shape=[jax.ShapeDtypeStruct((nseq, N_KV, qrows, HEAD_DIM), F32), idx_shape, idx_shape],
        scratch_shapes=[pltpu.VMEM((ns * N_KV, 2 * qrows, n_pages), F32),
                        pltpu.VMEM((ns * N_KV * s_new, width), F32)],
        compiler_params=_cparams(("arbitrary",)),
        name="dec_cmp_select",
    )(q2, kc, page_table)


def _dec_sel_win_body(page_ref, q_ref, idx2_ref, expand_ref, pool_ref, newblk_ref, win_ref,
                      kvw_new_ref, osel_ref, owin_ref, winout_ref, kvbuf, sem, *, past, s_new):
    n_past_blk = past // BLK
    n = pl.program_id(0)
    nseq = pl.num_programs(0)
    slot = n % 2
    rows = s_new * GQA_R
    seg_lanes = N_SEL * PAGE_SIZE
    nkeys = s_new * seg_lanes

    def block_copy(seq, c, k, sl):
        g, t = c // s_new, c % s_new
        page = page_ref[(seq * N_KV * s_new + c) * N_SEL + k]
        dst = kvbuf.at[sl, :, g, :, pl.ds((t * N_SEL + k) * PAGE_SIZE, PAGE_SIZE)]
        return pltpu.make_async_copy(pool_ref.at[page, :, g], dst, sem.at[sl])

    def start(seq, sl):
        for c in range(N_KV * s_new):
            for k in range(N_SEL):
                block_copy(seq, c, k, sl).start(priority=k % 2)

    @pl.when(n == 0)
    def _():
        start(n, slot)

    @pl.when(n + 1 < nseq)
    def _():
        start(n + 1, 1 - slot)

    for c in range(N_KV * s_new):
        for k in range(N_SEL):
            block_copy(n, c, k, slot).wait()

    assert win_ref.shape[1] == min(WINDOW, past + s_new)
    slid = jnp.concatenate([win_ref[...], kvw_new_ref[...]], axis=1)
    winout_ref[...] = slid[:, s_new:s_new + win_ref.shape[1]]

    t_row = lax.broadcasted_iota(jnp.int32, (rows, 1), 0) >> (GQA_R.bit_length() - 1)
    qpos = past + t_row
    lane = lax.broadcasted_iota(jnp.int32, (1, nkeys), 1)
    own = (lane >> (seg_lanes.bit_length() - 1)) == t_row
    row_in_page = lane & (PAGE_SIZE - 1)
    nlane = lax.broadcasted_iota(jnp.int32, (1, PAGE_SIZE), 1)
    nwin = win_ref.shape[1]
    wpos = past - nwin + lax.broadcasted_iota(jnp.int32, (1, nwin), 1)
    groups = range(N_KV)

    def krows(g):
        return pl.ds(g * HEAD_DIM, HEAD_DIM)

    def vrows(g):
        return pl.ds((N_KV + g) * HEAD_DIM, HEAD_DIM)

    q = [q_ref[g] for g in groups]
    jv2 = _dot(idx2_ref[...], expand_ref[...])
    s_sel = [_dot(q[g], kvbuf[slot, 0, g].astype(BF16)) for g in groups]
    s_new_sel = [_dot(q[g], newblk_ref[krows(g), :].astype(BF16)) for g in groups]
    s_win = [_dot(q[g], win_ref[krows(g), :].astype(BF16)) for g in groups]
    s_new_win = [_dot(q[g], kvw_new_ref[krows(g), :].astype(BF16)) for g in groups]

    def two_piece_softmax(sa, va, sb, vb):
        m = jnp.maximum(jnp.max(sa, axis=-1, keepdims=True), jnp.max(sb, axis=-1, keepdims=True))
        ea, eb = jnp.exp(sa - m), jnp.exp(sb - m)
        l = jnp.sum(ea, axis=-1, keepdims=True) + jnp.sum(eb, axis=-1, keepdims=True)
        return (_dot_nt(ea.astype(BF16), va) + _dot_nt(eb.astype(BF16), vb)) / l

    for g in groups:
        jv = jv2[g:g + 1, :].astype(jnp.int32)
        kpos = jv * BLK + (row_in_page & (BLK - 1))
        valid = (own & (jv < n_past_blk) & ((row_in_page >> BLK_SHIFT) == (jv & (BPP - 1)))
                 & (kpos <= qpos))
        has_new = jnp.max(jnp.where(own & (jv >= n_past_blk), 1.0, 0.0), axis=-1, keepdims=True)
        valid_n = (has_new > 0.5) & (nlane < BLK) & (n_past_blk * BLK + nlane <= qpos)
        osel_ref[g] = two_piece_softmax(
            jnp.where(valid, s_sel[g], NEG), kvbuf[slot, 1, g].astype(BF16),
            jnp.where(valid_n, s_new_sel[g], NEG), newblk_ref[vrows(g), :].astype(BF16))

        dw = qpos - wpos
        dn = qpos - (past + nlane)
        valid_w = (dw >= 0) & (dw < WINDOW) & (wpos >= 0)
        valid_n = (dn >= 0) & (dn < WINDOW) & (nlane < s_new)
        owin_ref[g] = two_piece_softmax(
            jnp.where(valid_w, s_win[g], NEG), win_ref[vrows(g), :].astype(BF16),
            jnp.where(valid_n, s_new_win[g], NEG), kvw_new_ref[vrows(g), :].astype(BF16))


def _dec_sel_win(page_flat, q_g, idx2, pool_t, newblk_t, win_t, kvw_new_t, past, s_new):
    nseq = q_g.shape[0]
    rows = s_new * GQA_R
    nwin = win_t.shape[2]
    nkeys = s_new * N_SEL * PAGE_SIZE
    n_phys = pool_t.shape[0]
    expand = (jnp.arange(nkeys)[None, :] // PAGE_SIZE == jnp.arange(s_new * N_SEL)[:, None]).astype(BF16)
    per_seq = lambda a, pg: (a, 0, 0, 0)
    per_seq3 = lambda a, pg: (a, 0, 0)
    grid_spec = pltpu.PrefetchScalarGridSpec(
        num_scalar_prefetch=1,
        grid=(nseq,),
        in_specs=[
            pl.BlockSpec((None, N_KV, rows, HEAD_DIM), per_seq),
            pl.BlockSpec((None, rows, s_new * N_SEL), per_seq3),
            pl.BlockSpec((s_new * N_SEL, nkeys), lambda a, pg: (0, 0)),
            pl.BlockSpec(memory_space=pl.ANY),
            pl.BlockSpec((None, KV_W, PAGE_SIZE), per_seq3),
            pl.BlockSpec((None, KV_W, nwin), per_seq3),
            pl.BlockSpec((None, KV_W, PAGE_SIZE), per_seq3),
        ],
        out_specs=[
            pl.BlockSpec((None, N_KV, rows, HEAD_DIM), per_seq),
            pl.BlockSpec((None, N_KV, rows, HEAD_DIM), per_seq),
            pl.BlockSpec((None, KV_W, nwin), per_seq3),
        ],
        scratch_shapes=[pltpu.VMEM((2, 2, N_KV, HEAD_DIM, nkeys), F32),
                        pltpu.SemaphoreType.DMA((2,))],
    )
    shape = jax.ShapeDtypeStruct((nseq, N_KV, rows, HEAD_DIM), F32)
    return pl.pallas_call(
        functools.partial(_dec_sel_win_body, past=past, s_new=s_new),
        grid_spec=grid_spec,
        out_shape=[shape, shape, jax.ShapeDtypeStruct(win_t.shape, F32)],
        compiler_params=_cparams(("arbitrary",)),
        name="dec_sel_win",
    )(page_flat, q_g, idx2, expand, pool_t.reshape(n_phys, 2, N_KV, HEAD_DIM, PAGE_SIZE),
      newblk_t, win_t, kvw_new_t)


def _out_ln_body(*refs, gated):
    if gated:
        ocmp_ref, osel_ref, owin_ref, gates_ref, yconv_ref, h_ref, wo_ref, g_ref, b_ref, y_ref = refs
        gates = gates_ref[...]
        pieces = []
        for h in range(N_HEADS):
            col = slice(h * HEAD_DIM, (h + 1) * HEAD_DIM)
            pieces.append(gates[:, h:h + 1] * ocmp_ref[:, col]
                          + gates[:, N_HEADS + h:N_HEADS + h + 1] * osel_ref[:, col]
                          + gates[:, 2 * N_HEADS + h:2 * N_HEADS + h + 1] * owin_ref[:, col])
        o = jnp.concatenate(pieces, axis=-1)
    else:
        o_ref, yconv_ref, h_ref, wo_ref, g_ref, b_ref, y_ref = refs
        o = o_ref[...]
    mixed = jnp.concatenate([o, yconv_ref[...]], axis=-1).astype(BF16)
    y_ref[...] = _layer_norm(ALPHA * h_ref[...] + _dot(mixed, wo_ref[...]), g_ref[...], b_ref[...])


def _out_ln(branches, yconv, h2d, w_out, g, b, tm):
    rows = h2d.shape[0]
    gated = len(branches) > 1
    row = lambda i: (i, 0)
    const = lambda i: (0, 0)
    widths = [D_ATTN, D_ATTN, D_ATTN, LANES] if gated else [D_ATTN]
    return pl.pallas_call(
        functools.partial(_out_ln_body, gated=gated),
        grid=(rows // tm,),
        in_specs=[pl.BlockSpec((tm, w), row) for w in widths] + [
            pl.BlockSpec((tm, D_CONV), row),
            pl.BlockSpec((tm, D_MODEL), row),
            pl.BlockSpec((D_MODEL, D_MODEL), const),
            pl.BlockSpec((1, D_MODEL), const),
            pl.BlockSpec((1, D_MODEL), const),
        ],
        out_specs=pl.BlockSpec((tm, D_MODEL), row),
        out_shape=jax.ShapeDtypeStruct((rows, D_MODEL), F32),
        compiler_params=_cparams(("arbitrary",)),
        name="out_ln",
    )(*branches, yconv, h2d, w_out, g, b)


def _block_diag_groups(w):
    z = jnp.zeros_like(w)
    return jnp.concatenate([jnp.concatenate([w, z], axis=-1),
                            jnp.concatenate([z, w], axis=-1)], axis=-2).astype(BF16)


def _to_group_layout(x, nseq, s_new):
    x = x.reshape(nseq, s_new, N_KV, GQA_R, HEAD_DIM).transpose(0, 2, 1, 3, 4)
    return x.reshape(nseq, N_KV, s_new * GQA_R, HEAD_DIM)


def _from_group_layout(x, nseq, s_new):
    x = x.reshape(nseq, N_KV, s_new, GQA_R, HEAD_DIM).transpose(0, 2, 1, 3, 4)
    return x.reshape(nseq * s_new, D_ATTN)


def kernel(x_prompt, x_sample, cache_cmp_kv, cache_sel_kv, state_win_kv, state_conv, page_table,
           ffa_gate, ffa_up, ffa_down, ln1_g, ln1_b, w_in, b_in, w_conv, cmp_w1, cmp_w2, cmp_pe,
           w_out, ln2_g, ln2_b, ffb_gate, ffb_up, ffb_down, ln3_g, ln3_b):
    depth = ffa_gate.shape[0]
    assert depth == 1
    l = 0
    n, t, d = x_prompt.shape
    nseq, s_new, _ = x_sample.shape
    n_pages = page_table.shape[1]
    past = n_pages * PAGE_SIZE
    rows_p, rows_s = n * t, nseq * s_new
    kvshape = (2, N_KV, HEAD_DIM)
    assert d == D_MODEL and rows_p % FFN_ROW_TILE == 0 and t % PROJ_ROW_TILE == 0 and t % ATT_TQ == 0
    assert t // BLK <= AUX_BLOCKS and n % CMP_SEQS_PER_STEP == 0 and nseq % DEC_SEQ_PER_STEP == 0
    assert s_new * GQA_R == BF16_SUBLANES and s_new <= BLK and past % BLK == 0
    assert state_win_kv.shape[2] == WINDOW

    bf = lambda w: w.astype(BF16)
    vec = lambda v: v.reshape(1, -1)
    w_main = bf(w_in[l][:, :D_MAIN])
    b_main = vec(b_in[l][:D_MAIN])
    w_gate = bf(jnp.pad(w_in[l][:, D_MAIN:], ((0, 0), (0, LANES - N_GATE))))
    b_gate = vec(jnp.pad(b_in[l][D_MAIN:], (0, LANES - N_GATE)))
    pe_t = jnp.concatenate([cmp_pe[l].transpose(1, 2, 0)] * BPP, axis=-1)
    w1_t = _block_diag_groups(cmp_w1[l].transpose(0, 2, 1, 3)).reshape(
        2, D_PAIRS, 2 * PAGE_SIZE, PAGE_SIZE)
    w2_t = _block_diag_groups(cmp_w2[l])
    w_out_b = bf(w_out[l])
    ffa = (ffa_gate[l], ffa_up[l], ffa_down[l], vec(ln1_g[l]), vec(ln1_b[l]))
    ffb = (ffb_gate[l], ffb_up[l], ffb_down[l], vec(ln3_g[l]), vec(ln3_b[l]))

    def prompt_cols(x):
        return jnp.concatenate([x[..., :_C_Q], x[..., _C_KVS:_C_KVS + K_HALF],
                                x[..., _C_KVW:_C_KVW + K_HALF]], axis=-1)

    def prompt_rows(x):
        return jnp.pad(x[..., _C_Q:], [(0, 0)] * (x.ndim - 1) + [(0, GATE_ROWS - N_GATE)])

    w_rows, b_rows = bf(prompt_cols(w_in[l])), vec(prompt_cols(b_in[l]))
    w_tr, b_tr = bf(prompt_rows(w_in[l]).T), prompt_rows(b_in[l]).reshape(-1, 1)

    pps = t // PAGE_SIZE
    cmp_seqs = CMP_SEQS_PER_STEP
    hp, hs = _ffn_ln(x_prompt.reshape(rows_p, d), x_sample.reshape(rows_s, d), *ffa, tm=FFN_ROW_TILE)
    q, k_sel, k_win, kvc_t, kvs_t, kvw_t, gates_t, yconv, cstate = _proj_prompt(
        hp.reshape(n, t, d), w_rows, b_rows, w_tr, b_tr, w_conv[l], tm=PROJ_ROW_TILE)
    kc = _compress(jnp.zeros((1,), jnp.int32), kvc_t, pe_t, w1_t, w2_t, n_steps=n // cmp_seqs,
                   n_pages=cmp_seqs * pps, pages_per_seq=pps, name="compress_prompt")
    kc = kc.reshape(n // cmp_seqs, 2, N_KV, cmp_seqs, pps, BPP, HEAD_DIM)
    kc = kc.transpose(0, 3, 2, 1, 4, 5, 6).reshape(n, N_KV, 2, pps * BPP, HEAD_DIM)
    zc = jnp.zeros_like(kc[:, 0, 0])
    kc_pad = bf(jnp.stack([jnp.concatenate([kc[:, 0, 0], zc], axis=-1),
                           jnp.concatenate([zc, kc[:, 1, 0]], axis=-1)], axis=1))
    vc_t = bf(jnp.swapaxes(kc[:, :, 1], -1, -2))
    o = _attn_prompt(q, k_sel, k_win, kvs_t, kvw_t, kc_pad, vc_t, gates_t)
    yp = _out_ln([o.reshape(rows_p, D_ATTN)], yconv.reshape(rows_p, D_CONV), hp, w_out_b,
                 vec(ln2_g[l]), vec(ln2_b[l]), tm=OUT_ROW_TILE)

    def from_feature_major(x_t):
        return jnp.moveaxis(x_t.reshape(x_t.shape[0], *kvshape, x_t.shape[-1]), -1, 1)[None]

    p_cmp = from_feature_major(kvc_t)
    p_sel = from_feature_major(kvs_t)
    p_win = from_feature_major(kvw_t[:, :, t - min(WINDOW, t):])
    p_conv = cstate[:, SUBLANES - (CONV_W - 1):].reshape(1, n, CONV_W - 1, D_CONV)

    qs, kv3, gates_s, cb_s, u_s = _proj_sample(hs, w_main, b_main, w_gate, b_gate)
    kvc_s, kvs_s, kvw_s = kv3[:, :KV_W], kv3[:, KV_W:2 * KV_W], kv3[:, 2 * KV_W:]
    u3 = u_s.reshape(nseq, s_new, D_CONV)
    yconv_s = _conv_sample(cb_s.reshape(nseq, s_new, D_CONV), u3, state_conv[l], w_conv[l])
    s_conv = jnp.concatenate([state_conv[l], u3], axis=1)[:, -(CONV_W - 1):]

    def feature_major(x):
        return jnp.moveaxis(x, -4, -1).reshape(*x.shape[:-4], KV_W, x.shape[-4])

    def new_rows_t(x2d):
        xt = x2d.reshape(nseq, s_new, KV_W).transpose(0, 2, 1)
        return jnp.pad(xt, ((0, 0), (0, 0), (0, PAGE_SIZE - s_new)))

    kc_s = _compress(page_table.reshape(-1), feature_major(cache_cmp_kv[l]), pe_t, w1_t, w2_t,
                     n_steps=nseq, n_pages=n_pages, pages_per_seq=None, name="compress_decode")
    q_g = _to_group_layout(qs, nseq, s_new)
    zq = jnp.zeros_like(q_g)
    q2 = jnp.concatenate([jnp.concatenate([q_g, zq], axis=-1), jnp.concatenate([zq, q_g], axis=-1)], axis=-2)
    ocmp_c, idx_pad, page_pad = _dec_cmp(q2, kc_s, page_table, past, s_new)
    idx = idx_pad[:, :N_SEL]
    idx2 = jnp.pad(bf(idx.reshape(nseq, N_KV, s_new * N_SEL)), ((0, 0), (0, s_new * GQA_R - N_KV), (0, 0)))
    win_t = feature_major(state_win_kv[l])
    kvw_new_t = new_rows_t(kvw_s)
    osel_c, owin_c, win_next_t = _dec_sel_win(
        page_pad[:, :N_SEL].reshape(-1), q_g, idx2, feature_major(cache_sel_kv[l]),
        new_rows_t(kvs_s), win_t, kvw_new_t, past, s_new)
    branches = [_from_group_layout(x, nseq, s_new) for x in (ocmp_c, osel_c, owin_c)]
    ys = _out_ln(branches + [gates_s], yconv_s.reshape(rows_s, D_CONV), hs, w_out_b,
                 vec(ln2_g[l]), vec(ln2_b[l]), tm=rows_s)
    yp, ys = _ffn_ln(yp, ys, *ffb, tm=FFN_ROW_TILE)
    yp, ys = yp.reshape(n, t, d), ys.reshape(nseq, s_new, d)
    s_cmp = kvc_s.reshape(1, nseq, s_new, *kvshape)
    s_sel = kvs_s.reshape(1, nseq, s_new, *kvshape)
    s_win = from_feature_major(win_next_t)
    return (yp, ys, p_cmp, p_sel, p_win, p_conv, s_cmp, s_sel, s_win,
            s_conv.reshape(1, nseq, CONV_W - 1, D_CONV))
```
